```python
import math
import jax
import jax.numpy as jnp
from jax import lax
import numpy as np

D_MODEL = 1024
BATCH = 2
SEQ = 8192
DEPTH = 1
DEC_BATCH = 128
DEC_SEQ = 4
PAST_LEN = 8192
PAGE_SIZE = 128

HEAD_DIM = 64
N_HEADS_A = D_MODEL // HEAD_DIM
D_ATTN = N_HEADS_A * HEAD_DIM
DILATED_BRANCHES = ((128, 1), (512, 4), (2048, 16))
MAX_WINDOW = max(w for w, _ in DILATED_BRANCHES)
BLK = 128
ATTN_SCALE = HEAD_DIM ** -0.5
N_BUCKETS = 32
MAX_EXACT = N_BUCKETS // 2
MAX_DISTANCE = MAX_WINDOW
SSM_HEAD_DIM = 64
N_HEADS_S = D_MODEL // SSM_HEAD_DIM
D_SSM = N_HEADS_S * SSM_HEAD_DIM
SSM_GROUPS = 2
D_STATE = 128
CONV_WIDTH = 4
CONV_DIM = D_SSM + 2 * SSM_GROUPS * D_STATE
SSD_CHUNK = 128
D_MIX = D_ATTN + D_SSM
D_IN_PROJ = 3 * D_ATTN + D_SSM + CONV_DIM + N_HEADS_S
D_FF = 4 * D_MODEL
EPS = 1e-6

kernel_name = 'hymba_dilated_ssd_decoder_step'


def rmsnorm(x, g):
    xf = x.astype(jnp.float32)
    y = xf * lax.rsqrt(jnp.mean(xf * xf, axis=-1, keepdims=True) + EPS)
    return (y * g.astype(jnp.float32)).astype(x.dtype)


def t5_bucket(dist):
    small = dist < MAX_EXACT
    df = jnp.maximum(dist, 1).astype(jnp.float32)
    large = MAX_EXACT + (jnp.log(df / MAX_EXACT) / math.log(MAX_DISTANCE / MAX_EXACT)
                         * (N_BUCKETS - MAX_EXACT)).astype(jnp.int32)
    return jnp.where(small, dist, jnp.minimum(large, N_BUCKETS - 1))


def project(x, norm_mix, w_in, q_norm, k_norm):
    b, t = x.shape[:2]
    p = rmsnorm(x, norm_mix) @ w_in
    q, k, v, z, xbc, dt_raw = jnp.split(
        p, [D_ATTN, 2 * D_ATTN, 3 * D_ATTN, 3 * D_ATTN + D_SSM, 3 * D_ATTN + D_SSM + CONV_DIM], axis=-1)
    q = rmsnorm(q.reshape(b, t, N_HEADS_A, HEAD_DIM), q_norm)
    k = rmsnorm(k.reshape(b, t, N_HEADS_A, HEAD_DIM), k_norm)
    v = v.reshape(b, t, N_HEADS_A, HEAD_DIM)
    return q, k, v, z, xbc, dt_raw


def dilated_branch_prompt(q, k, v, rel_bias, window, dil):
    b, s, h, dh = q.shape
    span = window // dil
    L = s // dil
    nb = -(-L // BLK)
    pad = nb * BLK - L

    def to_phase(t):
        return t.reshape(b, L, dil, h, dh).transpose(0, 2, 1, 3, 4).reshape(b * dil, L, h, dh)

    def key_blocks(t):
        tp = jnp.pad(to_phase(t), ((0, 0), (BLK, pad), (0, 0), (0, 0))).reshape(b * dil, nb + 1, BLK, h, dh)
        return jnp.concatenate([tp[:, :-1], tp[:, 1:]], axis=2)

    qb = jnp.pad(to_phase(q), ((0, 0), (0, pad), (0, 0), (0, 0))).reshape(b * dil, nb, BLK, h, dh)
    kb, vb = key_blocks(k), key_blocks(v)
    rel = jnp.arange(BLK)[:, None] + BLK - jnp.arange(2 * BLK)[None, :]
    band = (rel >= 0) & (rel <= span)
    not_pad = (jnp.arange(nb)[:, None, None] > 0) | (jnp.arange(2 * BLK)[None, None, :] >= BLK)
    valid = band[None] & not_pad
    bias = rel_bias[t5_bucket(jnp.clip(rel, 0, span) * dil)].astype(jnp.float32).transpose(2, 0, 1)
    logits = jnp.einsum('gnqhd,gnkhd->gnhqk', qb, kb, preferred_element_type=jnp.float32) * ATTN_SCALE
    logits = jnp.where(valid[None, :, None], logits + bias[None, None], -jnp.inf)
    m = jnp.max(logits, axis=-1, keepdims=True)
    pr = jnp.exp(logits - m)
    den = jnp.sum(pr, axis=-1)
    o = jnp.einsum('gnhqk,gnkhd->gnqhd', pr, vb.astype(jnp.float32)) / jnp.swapaxes(den, 2, 3)[..., None]
    lse = jnp.swapaxes(m[..., 0] + jnp.log(den), 2, 3)

    def from_phase(t):
        t = t.reshape(b * dil, nb * BLK, *t.shape[3:])[:, :L]
        return t.reshape(b, dil, L, *t.shape[2:]).swapaxes(1, 2).reshape(b, s, *t.shape[2:])

    return from_phase(o), from_phase(lse)


def dilated_branch_sample(q, k_all, v_all, rel_bias, window, dil):
    t = q.shape[1]
    n_past = k_all.shape[1] - t
    span = window // dil
    j = jnp.arange(span + 1)
    idx = n_past + jnp.arange(t)[:, None] - j[None, :] * dil
    valid = idx >= 0
    idx = jnp.maximum(idx, 0)
    kg = k_all[:, idx]
    vg = v_all[:, idx]
    bias = rel_bias[t5_bucket(j * dil)].astype(jnp.float32).T
    logits = jnp.einsum('bthd,btjhd->bthj', q, kg, preferred_element_type=jnp.float32) * ATTN_SCALE
    logits = jnp.where(valid[None, :, None, :], logits + bias[None, None], -jnp.inf)
    m = jnp.max(logits, axis=-1, keepdims=True)
    pr = jnp.exp(logits - m)
    den = jnp.sum(pr, axis=-1)
    o = jnp.einsum('bthj,btjhd->bthd', pr, vg.astype(jnp.float32)) / den[..., None]
    return o, m[..., 0] + jnp.log(den)


def merge_branches(branches):
    outs = jnp.stack([o for o, _ in branches])
    lse = jnp.stack([s for _, s in branches])
    w = jax.nn.softmax(lse, axis=0)
    return jnp.einsum('gbth,gbthd->bthd', w, outs)


def causal_conv(xbc, prior, conv_w, conv_b):
    t = xbc.shape[1]
    xpad = jnp.concatenate([prior, xbc], axis=1)
    out = sum(xpad[:, i:i + t] * conv_w[i] for i in range(CONV_WIDTH)) + conv_b
    return jax.nn.silu(out), xpad[:, t:]


def ssd(xh, dt, a, bm, cm, h0):
    b, s, nh, hp = xh.shape
    q = SSD_CHUNK if s % SSD_CHUNK == 0 else s
    nc = s // q
    g, r = SSM_GROUPS, nh // SSM_GROUPS
    xc = xh.astype(jnp.float32).reshape(b, nc, q, g, r, hp)
    dtc = dt.reshape(b, nc, q, g, r)
    bc = bm.astype(jnp.float32).reshape(b, nc, q, g, D_STATE)
    cc = cm.astype(jnp.float32).reshape(b, nc, q, g, D_STATE)
    cum = jnp.cumsum(dtc * a.reshape(g, r), axis=2)
    causal = jnp.tril(jnp.ones((q, q), bool))[None, None, :, :, None, None]
    seg = cum[:, :, :, None] - cum[:, :, None, :]
    decay = jnp.exp(jnp.where(causal, seg, -jnp.inf))
    cb = jnp.einsum('bcign,bcjgn->bcijg', cc, bc)
    y_diag = jnp.einsum('bcijgr,bcjgrp->bcigrp', cb[..., None] * decay * dtc[:, :, None], xc)
    to_end = jnp.exp(cum[:, :, -1:] - cum) * dtc
    states = jnp.einsum('bcjgn,bcjgr,bcjgrp->bcgrpn', bc, to_end, xc)
    chunk_decay = jnp.exp(cum[:, :, -1])

    def step(hc, inp):
        dec, st = inp
        return dec[..., None, None] * hc + st, hc

    h_last, h_prev = lax.scan(step, h0.astype(jnp.float32).reshape(b, g, r, hp, D_STATE),
                              (jnp.moveaxis(chunk_decay, 1, 0), jnp.moveaxis(states, 1, 0)))
    h_prev = jnp.moveaxis(h_prev, 0, 1)
    y_off = jnp.einsum('bcign,bcigr,bcgrpn->bcigrp', cc, jnp.exp(cum), h_prev)
    return (y_diag + y_off).reshape(b, s, nh, hp), h_last.reshape(b, nh, hp, D_STATE)


def ssm_mixer(z, xbc, dt_raw, conv_prior, h0, conv_w, conv_b, dt_bias, a_log, d_skip, ssm_norm):
    b, t = z.shape[:2]
    xc, conv_state = causal_conv(xbc, conv_prior, conv_w, conv_b)
    xs, bm, cm = jnp.split(xc, [D_SSM, D_SSM + SSM_GROUPS * D_STATE], axis=-1)
    xs = xs.reshape(b, t, N_HEADS_S, SSM_HEAD_DIM)
    bm = bm.reshape(b, t, SSM_GROUPS, D_STATE)
    cm = cm.reshape(b, t, SSM_GROUPS, D_STATE)
    dt = jax.nn.softplus(dt_raw.astype(jnp.float32) + dt_bias.astype(jnp.float32))
    a = -jnp.exp(a_log.astype(jnp.float32))
    y, h_last = ssd(xs, dt, a, bm, cm, h0)
    y = y + xs.astype(jnp.float32) * d_skip.astype(jnp.float32)[:, None]
    y = y.reshape(b, t, D_SSM) * jax.nn.silu(z.astype(jnp.float32))
    yg = y.reshape(b, t, SSM_GROUPS, D_SSM // SSM_GROUPS)
    yg = yg * lax.rsqrt(jnp.mean(yg * yg, axis=-1, keepdims=True) + EPS)
    y = yg.reshape(b, t, D_SSM) * ssm_norm.astype(jnp.float32)
    return y.astype(z.dtype), conv_state, h_last


def residual_out(x, attn_o, ssm_o, w_out, norm_ffn, w_up, w_down):
    h = x + jnp.concatenate([attn_o, ssm_o], axis=-1) @ w_out
    u = rmsnorm(h, norm_ffn) @ w_up
    return h + jnp.square(jax.nn.relu(u)) @ w_down


def setup_inputs(seed: int = 0) -> dict:
    key = jax.random.key(seed)
    ks = jax.random.split(key, 24)
    nrm = jax.random.normal
    win_buf = min(MAX_WINDOW, PAST_LEN)

    def gain(k, n):
        return 1.0 + 0.05 * nrm(k, (DEPTH, n), jnp.float32)

    dt0 = jnp.exp(jax.random.uniform(ks[12], (DEPTH, N_HEADS_S), jnp.float32, math.log(1e-3), math.log(1e-1)))
    return {
        'x_prompt': nrm(ks[0], (BATCH, SEQ, D_MODEL), jnp.float32),
        'x_sample': nrm(ks[1], (DEC_BATCH, DEC_SEQ, D_MODEL), jnp.float32),
        'cache_attn_k': nrm(ks[2], (DEPTH, DEC_BATCH, win_buf, N_HEADS_A, HEAD_DIM), jnp.float32),
        'cache_attn_v': nrm(ks[3], (DEPTH, DEC_BATCH, win_buf, N_HEADS_A, HEAD_DIM), jnp.float32),
        'state_conv': nrm(ks[4], (DEPTH, DEC_BATCH, CONV_WIDTH - 1, CONV_DIM), jnp.float32),
        'state_ssm': 0.1 * nrm(ks[5], (DEPTH, DEC_BATCH, N_HEADS_S, SSM_HEAD_DIM, D_STATE), jnp.float32),
        'norm_mix': gain(ks[6], D_MODEL),
        'w_in': nrm(ks[7], (DEPTH, D_MODEL, D_IN_PROJ), jnp.float32) * D_MODEL ** -0.5,
        'q_norm': gain(ks[8], HEAD_DIM),
        'k_norm': gain(ks[9], HEAD_DIM),
        'rel_bias': 0.5 * nrm(ks[10], (N_BUCKETS, N_HEADS_A), jnp.float32),
        'conv_w': nrm(ks[11], (DEPTH, CONV_WIDTH, CONV_DIM), jnp.float32) * CONV_WIDTH ** -0.5,
        'conv_b': 0.01 * nrm(ks[13], (DEPTH, CONV_DIM), jnp.float32),
        'dt_bias': dt0 + jnp.log(-jnp.expm1(-dt0)),
        'a_log': jnp.log(jax.random.uniform(ks[14], (DEPTH, N_HEADS_S), jnp.float32, 1.0, 16.0)),
        'd_skip': 1.0 + 0.1 * nrm(ks[15], (DEPTH, N_HEADS_S), jnp.float32),
        'ssm_norm': gain(ks[16], D_SSM),
        'w_out': nrm(ks[17], (DEPTH, D_MIX, D_MODEL), jnp.float32) * D_MIX ** -0.5,
        'norm_ffn': gain(ks[18], D_MODEL),
        'w_up': nrm(ks[19], (DEPTH, D_MODEL, D_FF), jnp.float32) * D_MODEL ** -0.5,
        'w_down': nrm(ks[20], (DEPTH, D_FF, D_MODEL), jnp.float32) * D_FF ** -0.5,
    }


def reference(x_prompt, x_sample, cache_attn_k, cache_attn_v, state_conv, state_ssm,
              norm_mix, w_in, q_norm, k_norm, rel_bias, conv_w, conv_b, dt_bias, a_log,
              d_skip, ssm_norm, w_out, norm_ffn, w_up, w_down):
    yp, ys = x_prompt, x_sample
    bp, sp = yp.shape[:2]
    bs, ts = ys.shape[:2]
    keep = min(MAX_WINDOW, sp)
    kp_rows, vp_rows, conv_p, ssm_p = [], [], [], []
    ks_rows, vs_rows, conv_s, ssm_s = [], [], [], []
    for l in range(DEPTH):
        ssm_params = (conv_w[l], conv_b[l], dt_bias[l], a_log[l], d_skip[l], ssm_norm[l])
        ffn_params = (w_out[l], norm_ffn[l], w_up[l], w_down[l])
        q, k, v, z, xbc, dt_raw = project(yp, norm_mix[l], w_in[l], q_norm[l], k_norm[l])
        attn = merge_branches([dilated_branch_prompt(q, k, v, rel_bias, w, d) for w, d in DILATED_BRANCHES])
        attn = attn.reshape(bp, sp, D_ATTN).astype(yp.dtype)
        conv0 = jnp.zeros((bp, CONV_WIDTH - 1, CONV_DIM), yp.dtype)
        h0 = jnp.zeros((bp, N_HEADS_S, SSM_HEAD_DIM, D_STATE), jnp.float32)
        ssm_o, cst, hst = ssm_mixer(z, xbc, dt_raw, conv0, h0, *ssm_params)
        kp_rows.append(k[:, sp - keep:])
        vp_rows.append(v[:, sp - keep:])
        conv_p.append(cst)
        ssm_p.append(hst.astype(yp.dtype))
        yp = residual_out(yp, attn, ssm_o, *ffn_params)
        qs, kn, vn, zs, xbcs, dts = project(ys, norm_mix[l], w_in[l], q_norm[l], k_norm[l])
        k_all = jnp.concatenate([cache_attn_k[l].astype(kn.dtype), kn], axis=1)
        v_all = jnp.concatenate([cache_attn_v[l].astype(vn.dtype), vn], axis=1)
        attn_s = merge_branches([dilated_branch_sample(qs, k_all, v_all, rel_bias, w, d) for w, d in DILATED_BRANCHES])
        attn_s = attn_s.reshape(bs, ts, D_ATTN).astype(ys.dtype)
        ssm_os, csts, hsts = ssm_mixer(zs, xbcs, dts, state_conv[l].astype(ys.dtype), state_ssm[l], *ssm_params)
        ks_rows.append(kn)
        vs_rows.append(vn)
        conv_s.append(csts)
        ssm_s.append(hsts.astype(state_ssm.dtype))
        ys = residual_out(ys, attn_s, ssm_os, *ffn_params)
    new_k_prompt = jnp.stack(kp_rows)
    new_v_prompt = jnp.stack(vp_rows)
    new_conv_prompt = jnp.stack(conv_p)
    new_ssm_prompt = jnp.stack(ssm_p)
    new_k_sample = jnp.stack(ks_rows)
    new_v_sample = jnp.stack(vs_rows)
    new_conv_sample = jnp.stack(conv_s)
    new_ssm_sample = jnp.stack(ssm_s)
    return (yp, ys, new_k_prompt, new_v_prompt, new_conv_prompt, new_ssm_prompt,
            new_k_sample, new_v_sample, new_conv_sample, new_ssm_sample)
```

```python
import functools
import math

import jax
import jax.numpy as jnp
import numpy as np
from jax import lax
from jax.experimental import pallas as pl
from jax.experimental.pallas import tpu as pltpu

F32 = jnp.float32
BF16 = jnp.bfloat16

D_MODEL = 1024
HEAD_DIM = 64
N_HEADS_A = 16
D_ATTN = 1024
SPAN = 128
DILATIONS = (1, 4, 16)
MAX_WINDOW = 2048
N_BUCKETS = 32
MAX_EXACT = 16
SSM_HEAD_DIM = 64
N_HEADS_S = 16
D_SSM = 1024
SSM_GROUPS = 2
D_STATE = 128
CONV_WIDTH = 4
CONV_DIM = D_SSM + 2 * SSM_GROUPS * D_STATE
SSD_CHUNK = 128
D_FF = 4096
EPS = 1e-6
ATTN_SCALE = HEAD_DIM ** -0.5

LANES = 128
NEG = -1e30
VMEM_LIMIT = 56 * 1024 * 1024


def _resident(shape):
    nd = len(shape)
    return pl.BlockSpec(shape, lambda *_: (0,) * nd, pipeline_mode=pl.Buffered(1))


def _split3(x):
    hi = x.astype(BF16)
    r1 = x - hi.astype(F32)
    lo = r1.astype(BF16)
    lo2 = (r1 - lo.astype(F32)).astype(BF16)
    return hi, lo, lo2


def _dot(a, b):
    return jnp.dot(a, b, preferred_element_type=F32)


def _dot_nt(a, b):
    return lax.dot_general(a, b, (((1,), (1,)), ((), ())), preferred_element_type=F32)


IN_TN = 512


def _inproj_kernel(x_ref, gmix_ref, w_ref, wdt_ref, gq_ref, gk_ref,
                   qkv_ref, z_ref, xbc_ref, dt_ref):
    x = x_ref[...]
    ms = jnp.mean(x * x, axis=-1, keepdims=True)
    xn = ((x * lax.rsqrt(ms + EPS)) * gmix_ref[...]).astype(BF16)
    first_head = lax.broadcasted_iota(jnp.int32, (1, LANES), 1) < HEAD_DIM

    n_main = 3 * D_ATTN + D_SSM + CONV_DIM
    for j in range(n_main // IN_TN):
        c0 = j * IN_TN
        t = _dot(xn, w_ref[:, c0:c0 + IN_TN])
        if c0 < 2 * D_ATTN:
            g_ref = gq_ref if c0 < D_ATTN else gk_ref
            g0 = c0 % D_ATTN
            for c in range(IN_TN // LANES):
                tc = t[:, c * LANES:(c + 1) * LANES]
                s = tc * tc
                sa = jnp.sum(jnp.where(first_head, s, 0.0), axis=-1, keepdims=True)
                sb = jnp.sum(jnp.where(first_head, 0.0, s), axis=-1, keepdims=True)
                inv = lax.rsqrt(jnp.where(first_head, sa, sb) * (1.0 / HEAD_DIM) + EPS)
                qkv_ref[:, c0 + c * LANES:c0 + (c + 1) * LANES] = (
                    (tc * inv) * g_ref[:, g0 + c * LANES:g0 + (c + 1) * LANES])
        elif c0 < 3 * D_ATTN:
            qkv_ref[:, c0:c0 + IN_TN] = t
        elif c0 < 3 * D_ATTN + D_SSM:
            z_ref[:, c0 - 3 * D_ATTN:c0 - 3 * D_ATTN + IN_TN] = t
        else:
            o0 = c0 - 3 * D_ATTN - D_SSM
            xbc_ref[:, o0:o0 + IN_TN] = t
    dt_ref[...] = _dot(xn, wdt_ref[...])


def _inproj(x2d, gmix, w_main, w_dt, gq, gk, tm):
    m = x2d.shape[0]
    n_main = w_main.shape[1]
    row = lambda w: pl.BlockSpec((tm, w), lambda i: (i, 0))
    return pl.pallas_call(
        _inproj_kernel,
        grid=(m // tm,),
        in_specs=[row(D_MODEL), _resident((1, D_MODEL)), _resident((D_MODEL, n_main)),
                  _resident((D_MODEL, LANES)), _resident((1, D_ATTN)), _resident((1, D_ATTN))],
        out_specs=[row(3 * D_ATTN), row(D_SSM), row(CONV_DIM), row(LANES)],
        out_shape=[jax.ShapeDtypeStruct((m, 3 * D_ATTN), F32),
                   jax.ShapeDtypeStruct((m, D_SSM), F32),
                   jax.ShapeDtypeStruct((m, CONV_DIM), F32),
                   jax.ShapeDtypeStruct((m, LANES), F32)],
        compiler_params=pltpu.CompilerParams(
            dimension_semantics=("arbitrary",), vmem_limit_bytes=VMEM_LIMIT),
        name="inproj",
    )(x2d, gmix, w_main, w_dt, gq, gk)


FFN_TF = 1024


def _outffn_kernel(x_ref, attn_ref, ssm_ref, wout_ref, gffn_ref, wup_ref, wdown_ref, y_ref):
    a = attn_ref[...].astype(BF16)
    s = ssm_ref[...].astype(BF16)
    h = x_ref[...] + _dot(a, wout_ref[0:D_ATTN, :]) + _dot(s, wout_ref[D_ATTN:D_ATTN + D_SSM, :])
    ms = jnp.mean(h * h, axis=-1, keepdims=True)
    hn = ((h * lax.rsqrt(ms + EPS)) * gffn_ref[...]).astype(BF16)
    acc = None
    for f in range(D_FF // FFN_TF):
        u = _dot(hn, wup_ref[:, f * FFN_TF:(f + 1) * FFN_TF])
        u = jnp.maximum(u, 0.0)
        u = (u * u).astype(BF16)
        d = _dot(u, wdown_ref[f * FFN_TF:(f + 1) * FFN_TF, :])
        acc = d if acc is None else acc + d
    y_ref[...] = h + acc


def _outffn(x2d, attn, ssm_o, w_out, gffn, w_up, w_down, tm):
    m = x2d.shape[0]
    row = pl.BlockSpec((tm, D_MODEL), lambda i: (i, 0))
    return pl.pallas_call(
        _outffn_kernel,
        grid=(m // tm,),
        in_specs=[row, row, row, _resident((D_ATTN + D_SSM, D_MODEL)), _resident((1, D_MODEL)),
                  _resident((D_MODEL, D_FF)), _resident((D_FF, D_MODEL))],
        out_specs=row,
        out_shape=jax.ShapeDtypeStruct((m, D_MODEL), F32),
        compiler_params=pltpu.CompilerParams(
            dimension_semantics=("arbitrary",), vmem_limit_bytes=VMEM_LIMIT),
        name="outffn",
    )(x2d, attn, ssm_o, w_out, gffn, w_up, w_down)


def _sigmoid(x):
    return 1.0 / (1.0 + jnp.exp(-x))


def _softplus(x):
    return jnp.maximum(x, 0.0) + jnp.log1p(jnp.exp(-jnp.abs(x)))


def _expand_heads(v, e_ref):
    hi, lo, lo2 = _split3(v)
    e = e_ref[...]
    return _dot(hi, e) + _dot(lo, e) + _dot(lo2, e)


def _ssd_chunk(z_ref, xbc_ref, dt_ref, y_ref, xpad_ref, state_ref,
               convw_ref, convb_ref, dtb_ref, alog_ref, dskip_ref, gnorm_ref, e_ref, valid_rows):
    q = SSD_CHUNK
    pad = 8
    tail = CONV_WIDTH - 1

    xbc = xbc_ref[...]
    xpad_ref[pad:pad + q, :] = xbc
    conv = convb_ref[...] + convw_ref[tail:tail + 1, :] * xbc
    for i in range(tail):
        conv = conv + convw_ref[i:i + 1, :] * xpad_ref[pad - tail + i:pad - tail + i + q, :]
    xpad_ref[pad - tail:pad, :] = xbc[q - tail:q, :]
    xc = conv * _sigmoid(conv)
    xs = xc[:, 0:D_SSM]

    lane = lax.broadcasted_iota(jnp.int32, (1, LANES), 1)
    live = lane < N_HEADS_S
    if valid_rows < q:
        live = jnp.logical_and(live, lax.broadcasted_iota(jnp.int32, (q, 1), 0) < valid_rows)
    dt = jnp.where(live, _softplus(dt_ref[...] + dtb_ref[...]), 0.0)
    da = dt * (-jnp.exp(alog_ref[...]))

    ii = lax.broadcasted_iota(jnp.int32, (q, q), 0)
    jj = lax.broadcasted_iota(jnp.int32, (q, q), 1)
    causal = ii >= jj
    tri = jnp.where(causal, 1.0, 0.0).astype(BF16)
    hi, lo, lo2 = _split3(da)
    cum = _dot(tri, hi) + _dot(tri, lo) + _dot(tri, lo2)
    cum_t = cum.T

    cum_e = _expand_heads(cum, e_ref)
    last_e = cum_e[q - 1:q, :]
    xdt = xs * _expand_heads(dt, e_ref)
    xw_t = (xdt * jnp.exp(last_e - cum_e)).T.astype(BF16)
    state_decay = jnp.exp(jnp.broadcast_to(last_e, (q, D_SSM)).T)
    off_scale = jnp.exp(cum_e)

    first_head = lane < SSM_HEAD_DIM
    heads_per_group = N_HEADS_S // SSM_GROUPS
    gw = heads_per_group * SSM_HEAD_DIM
    state = state_ref[...]
    state_bf = state.astype(BF16)
    for g in range(SSM_GROUPS):
        bg = xc[:, D_SSM + g * D_STATE:D_SSM + (g + 1) * D_STATE].astype(BF16)
        cg = xc[:, D_SSM + (SSM_GROUPS + g) * D_STATE:D_SSM + (SSM_GROUPS + g + 1) * D_STATE].astype(BF16)
        cb = _dot_nt(cg, bg)
        y_off = _dot_nt(cg, state_bf[g * gw:(g + 1) * gw, :]) * off_scale[:, g * gw:(g + 1) * gw]
        y_ref[:, g * gw:(g + 1) * gw] = y_off
        for hp in range(heads_per_group // 2):
            pair = g * (heads_per_group // 2) + hp
            xp = xdt[:, pair * LANES:(pair + 1) * LANES]
            yp = None
            for s in range(2):
                h = 2 * pair + s
                seg = cum[:, h:h + 1] - cum_t[h:h + 1, :]
                dec = jnp.exp(jnp.where(causal, seg, NEG))
                m = (cb * dec).astype(BF16)
                xm = jnp.where(first_head if s == 0 else jnp.logical_not(first_head), xp, 0.0).astype(BF16)
                d = _dot(m, xm)
                yp = d if yp is None else yp + d
            y_ref[:, pair * LANES:(pair + 1) * LANES] += yp
        state_ref[g * gw:(g + 1) * gw, :] = (
            state_decay[g * gw:(g + 1) * gw, :] * state[g * gw:(g + 1) * gw, :]
            + _dot(xw_t[g * gw:(g + 1) * gw, :], bg))

    z = z_ref[...]
    y = (y_ref[...] + xs * dskip_ref[...]) * (z * _sigmoid(z))
    for g in range(SSM_GROUPS):
        yg = y[:, g * gw:(g + 1) * gw]
        ms = jnp.mean(yg * yg, axis=-1, keepdims=True)
        y_ref[:, g * gw:(g + 1) * gw] = (yg * lax.rsqrt(ms + EPS)) * gnorm_ref[:, g * gw:(g + 1) * gw]


def _ssd_prompt_kernel(z_ref, xbc_ref, dt_ref, conv0_ref, h0_ref, convw_ref, convb_ref, dtb_ref,
                       alog_ref, dskip_ref, gnorm_ref, e_ref, y_ref, hout_ref, xpad_ref, state_ref):
    c = pl.program_id(1)

    @pl.when(c == 0)
    def _():
        state_ref[...] = h0_ref[0]
        xpad_ref[8 - (CONV_WIDTH - 1):8, :] = conv0_ref[0]

    _ssd_chunk(z_ref, xbc_ref, dt_ref, y_ref, xpad_ref, state_ref, convw_ref, convb_ref, dtb_ref,
               alog_ref, dskip_ref, gnorm_ref, e_ref, SSD_CHUNK)

    @pl.when(c == pl.num_programs(1) - 1)
    def _():
        hout_ref[0] = state_ref[...]


def _ssd_sample_kernel(z_ref, xbc_ref, dt_ref, conv0_ref, h0_ref, convw_ref, convb_ref, dtb_ref,
                       alog_ref, dskip_ref, gnorm_ref, e_ref, y_ref, hout_ref,
                       xpad_ref, state_ref, zp_ref, xbcp_ref, dtp_ref, yp_ref, *, rows):
    state_ref[...] = h0_ref[0]
    xpad_ref[8 - (CONV_WIDTH - 1):8, :] = conv0_ref[0]
    for src, dst in ((z_ref, zp_ref), (xbc_ref, xbcp_ref), (dt_ref, dtp_ref)):
        dst[...] = jnp.zeros(dst.shape, F32)
        dst[0:rows, :] = src[0]
    _ssd_chunk(zp_ref, xbcp_ref, dtp_ref, yp_ref, xpad_ref, state_ref, convw_ref, convb_ref, dtb_ref,
               alog_ref, dskip_ref, gnorm_ref, e_ref, rows)
    y_ref[0] = yp_ref[0:rows, :]
    hout_ref[0] = state_ref[...]


def _ssd_param_specs():
    return [_resident((CONV_WIDTH, CONV_DIM)), _resident((1, CONV_DIM)), _resident((1, LANES)),
            _resident((1, LANES)), _resident((1, D_SSM)), _resident((1, D_SSM)),
            _resident((LANES, D_SSM))]


def _ssd_prompt(z, xbc, dt_raw, conv0, h0, ssd_params, batch):
    m = z.shape[0]
    nc = m // batch // SSD_CHUNK
    row = lambda w: pl.BlockSpec((SSD_CHUNK, w), lambda b, c: (b * nc + c, 0))
    per_b = lambda r, w: pl.BlockSpec((1, r, w), lambda b, c: (b, 0, 0))
    return pl.pallas_call(
        _ssd_prompt_kernel,
        grid=(batch, nc),
        in_specs=[row(D_SSM), row(CONV_DIM), row(LANES), per_b(CONV_WIDTH - 1, CONV_DIM),
                  per_b(D_SSM, D_STATE)] + _ssd_param_specs(),
        out_specs=[row(D_SSM), per_b(D_SSM, D_STATE)],
        out_shape=[jax.ShapeDtypeStruct((m, D_SSM), F32),
                   jax.ShapeDtypeStruct((batch, D_SSM, D_STATE), F32)],
        scratch_shapes=[pltpu.VMEM((8 + SSD_CHUNK, CONV_DIM), F32), pltpu.VMEM((D_SSM, D_STATE), F32)],
        compiler_params=pltpu.CompilerParams(
            dimension_semantics=("arbitrary", "arbitrary"), vmem_limit_bytes=VMEM_LIMIT),
        name="ssd_prompt",
    )(z, xbc, dt_raw, conv0, h0, *ssd_params)


def _ssd_sample(z, xbc, dt_raw, conv0, h0, ssd_params):
    batch, rows = z.shape[0], z.shape[1]
    per_b = lambda r, w: pl.BlockSpec((1, r, w), lambda b: (b, 0, 0))
    q = SSD_CHUNK
    return pl.pallas_call(
        functools.partial(_ssd_sample_kernel, rows=rows),
        grid=(batch,),
        in_specs=[per_b(rows, D_SSM), per_b(rows, CONV_DIM), per_b(rows, LANES),
                  per_b(CONV_WIDTH - 1, CONV_DIM), per_b(D_SSM, D_STATE)] + _ssd_param_specs(),
        out_specs=[per_b(rows, D_SSM), per_b(D_SSM, D_STATE)],
        out_shape=[jax.ShapeDtypeStruct((batch, rows, D_SSM), F32),
                   jax.ShapeDtypeStruct((batch, D_SSM, D_STATE), F32)],
        scratch_shapes=[pltpu.VMEM((8 + q, CONV_DIM), F32), pltpu.VMEM((D_SSM, D_STATE), F32),
                        pltpu.VMEM((q, D_SSM), F32), pltpu.VMEM((q, CONV_DIM), F32),
                        pltpu.VMEM((q, LANES), F32), pltpu.VMEM((q, D_SSM), F32)],
        compiler_params=pltpu.CompilerParams(
            dimension_semantics=("arbitrary",), vmem_limit_bytes=VMEM_LIMIT),
        name="ssd_sample",
    )(z, xbc, dt_raw, conv0, h0, *ssd_params)


def _t5_bucket_np(dist):
    dist = np.asarray(dist, np.int64)
    df = np.maximum(dist, 1).astype(np.float32)
    large = MAX_EXACT + (np.log(df / np.float32(MAX_EXACT)) / np.float32(math.log(MAX_WINDOW / MAX_EXACT))
                         * np.float32(N_BUCKETS - MAX_EXACT)).astype(np.int32)
    return np.where(dist < MAX_EXACT, dist, np.minimum(large, N_BUCKETS - 1)).astype(np.int32)


def _band_bucket_tables():
    qi = np.arange(SPAN)[:, None]
    ki = np.arange(SPAN)[None, :]
    out = np.empty((len(DILATIONS), 2, SPAN, SPAN), np.int32)
    for bi, d in enumerate(DILATIONS):
        rel_prev = qi + SPAN - ki
        rel_cur = qi - ki
        out[bi, 0] = np.where(rel_prev <= SPAN, _t5_bucket_np(np.clip(rel_prev, 0, SPAN) * d), -1)
        out[bi, 1] = np.where(rel_cur >= 0, _t5_bucket_np(np.clip(rel_cur, 0, SPAN) * d), -1)
    return out


def _bias_table_kernel(tab_ref, rb_ref, out_ref, *, buckets):
    pair = pl.program_id(0)
    for bi in range(len(DILATIONS)):
        for half in range(2):
            tab = tab_ref[bi, half]
            acc = [jnp.full((SPAN, SPAN), NEG, F32) for _ in range(2)]
            for bkt in buckets[bi][half]:
                hit = tab == bkt
                for s in range(2):
                    acc[s] = jnp.where(hit, rb_ref[bkt, 2 * pair + s], acc[s])
            for s in range(2):
                out_ref[0, bi, s, half] = acc[s]


def _bias_tables(rel_bias):
    tabs = _band_bucket_tables()
    buckets = [[sorted(int(b) for b in np.unique(tabs[bi, half]) if b >= 0) for half in range(2)]
               for bi in range(len(DILATIONS))]
    nd = len(DILATIONS)
    return pl.pallas_call(
        functools.partial(_bias_table_kernel, buckets=buckets),
        grid=(N_HEADS_A // 2,),
        in_specs=[_resident((nd, 2, SPAN, SPAN)),
                  pl.BlockSpec(memory_space=pltpu.SMEM)],
        out_specs=pl.BlockSpec((1, nd, 2, 2, SPAN, SPAN), lambda p: (p, 0, 0, 0, 0, 0)),
        out_shape=jax.ShapeDtypeStruct((N_HEADS_A // 2, nd, 2, 2, SPAN, SPAN), F32),
        compiler_params=pltpu.CompilerParams(dimension_semantics=("arbitrary",)),
        name="bias_tables",
    )(jnp.asarray(tabs), rel_bias)


ATT_T = SPAN * DILATIONS[-1]


def _attn_kernel(q_ref, k_ref, v_ref, bias_ref, o_ref, qm_ref, kph_ref, vm_ref, acc_ref, m_ref, l_ref):
    t = pl.program_id(2)
    lane = lax.broadcasted_iota(jnp.int32, (1, LANES), 1)
    first_head = lane < HEAD_DIM
    head_mask = (first_head, jnp.logical_not(first_head))
    nblk = ATT_T // SPAN

    @pl.when(t == 0)
    def _():
        for bi, d in enumerate(DILATIONS):
            ph = ATT_T // d + SPAN
            for r in range(d):
                kph_ref[bi, r * ph:r * ph + SPAN, :] = jnp.zeros((SPAN, LANES), BF16)
                for s in range(2):
                    vm_ref[bi, s, r * ph:r * ph + SPAN, :] = jnp.zeros((SPAN, LANES), BF16)

    for bi, d in enumerate(DILATIONS):
        ln = ATT_T // d
        ph = ln + SPAN
        for r in range(d):
            rows = pl.ds(r, ln, stride=d) if d > 1 else pl.ds(0, ln)
            qv = q_ref[rows, :] * ATTN_SCALE
            kv = k_ref[rows, :]
            vv = v_ref[rows, :]
            kph_ref[bi, r * ph + SPAN:(r + 1) * ph, :] = kv.astype(BF16)
            for s in range(2):
                qm_ref[bi, s, r * ln:(r + 1) * ln, :] = jnp.where(head_mask[s], qv, 0.0).astype(BF16)
                vm_ref[bi, s, r * ph + SPAN:(r + 1) * ph, :] = jnp.where(head_mask[s], vv, 0.0).astype(BF16)

    for bi, d in enumerate(DILATIONS):
        ln = ATT_T // d
        ph = ln + SPAN
        nb = ln // SPAN
        first_branch = bi == 0
        last_branch = bi == len(DILATIONS) - 1

        def block(blk, carry, bi=bi, d=d, ln=ln, ph=ph, nb=nb, first_branch=first_branch,
                  last_branch=last_branch):
            r = blk // nb
            n = blk % nb
            qrow = pl.multiple_of(r * ln + n * SPAN, SPAN)
            crow = pl.multiple_of(r * ph + SPAN + n * SPAN, SPAN)
            prow = pl.multiple_of(r * ph + n * SPAN, SPAN)
            has_prev = jnp.logical_or(t > 0, n > 0)
            if d > 1:
                pos = pl.ds(r + d * SPAN * n, SPAN, stride=d)
            else:
                pos = pl.ds(pl.multiple_of(n * SPAN, SPAN), SPAN)
            kc = kph_ref[bi, pl.ds(crow, SPAN), :]
            kp = kph_ref[bi, pl.ds(prow, SPAN), :]
            sp, sc, mb = [], [], []
            for s in range(2):
                qs = qm_ref[bi, s, pl.ds(qrow, SPAN), :]
                a = jnp.where(has_prev, _dot_nt(qs, kp) + bias_ref[0, bi, s, 0], NEG)
                c = _dot_nt(qs, kc) + bias_ref[0, bi, s, 1]
                sp.append(a)
                sc.append(c)
                mb.append(jnp.maximum(jnp.max(a, axis=-1, keepdims=True),
                                      jnp.max(c, axis=-1, keepdims=True)))
            m_new = jnp.where(first_head, mb[0], mb[1])
            if not first_branch:
                m_old = m_ref[pos, :]
                m_new = jnp.maximum(m_old, m_new)
                alpha = jnp.exp(m_old - m_new)
            m_sw = pltpu.roll(m_new, HEAD_DIM, 1)
            m_rep = (jnp.where(first_head, m_new, m_sw), jnp.where(first_head, m_sw, m_new))
            pv = None
            rs = []
            for s in range(2):
                pp = jnp.exp(sp[s] - m_rep[s])
                pc = jnp.exp(sc[s] - m_rep[s])
                rs.append(jnp.sum(pp, axis=-1, keepdims=True) + jnp.sum(pc, axis=-1, keepdims=True))
                contrib = (_dot(pp.astype(BF16), vm_ref[bi, s, pl.ds(prow, SPAN), :])
                           + _dot(pc.astype(BF16), vm_ref[bi, s, pl.ds(crow, SPAN), :]))
                pv = contrib if pv is None else pv + contrib
            l_new = jnp.where(first_head, rs[0], rs[1])
            if not first_branch:
                l_new = alpha * l_ref[pos, :] + l_new
                pv = alpha * acc_ref[pos, :] + pv
            if last_branch:
                o_ref[pos, :] = pv / l_new
            else:
                m_ref[pos, :] = m_new
                l_ref[pos, :] = l_new
                acc_ref[pos, :] = pv
            return carry

        lax.fori_loop(0, nblk, block, 0)

    for bi, d in enumerate(DILATIONS):
        ln = ATT_T // d
        ph = ln + SPAN
        for r in range(d):
            kph_ref[bi, r * ph:r * ph + SPAN, :] = kph_ref[bi, (r + 1) * ph - SPAN:(r + 1) * ph, :]
            for s in range(2):
                vm_ref[bi, s, r * ph:r * ph + SPAN, :] = vm_ref[bi, s, (r + 1) * ph - SPAN:(r + 1) * ph, :]


def _attn_prompt(qkv, bias, batch):
    m = qkv.shape[0]
    nt = m // batch // ATT_T
    npair = N_HEADS_A // 2
    nd = len(DILATIONS)
    blk = lambda off: pl.BlockSpec((ATT_T, LANES), lambda b, p, t: (b * nt + t, off + p))
    halo_rows = ATT_T + DILATIONS[-1] * SPAN
    return pl.pallas_call(
        _attn_kernel,
        grid=(batch, npair, nt),
        in_specs=[blk(0), blk(npair), blk(2 * npair),
                  pl.BlockSpec((1, nd, 2, 2, SPAN, SPAN), lambda b, p, t: (p, 0, 0, 0, 0, 0))],
        out_specs=blk(0),
        out_shape=jax.ShapeDtypeStruct((m, D_ATTN), F32),
        scratch_shapes=[pltpu.VMEM((nd, 2, ATT_T, LANES), BF16),
                        pltpu.VMEM((nd, halo_rows, LANES), BF16),
                        pltpu.VMEM((nd, 2, halo_rows, LANES), BF16),
                        pltpu.VMEM((ATT_T, LANES), F32),
                        pltpu.VMEM((ATT_T, LANES), F32),
                        pltpu.VMEM((ATT_T, LANES), F32)],
        compiler_params=pltpu.CompilerParams(
            dimension_semantics=("arbitrary", "arbitrary", "arbitrary"), vmem_limit_bytes=VMEM_LIMIT),
        name="attn_prompt",
    )(qkv, qkv, qkv, bias)


SAMPLE_ROWS = 4


def _sample_tables():
    pos = np.arange(MAX_WINDOW)[None, :]
    t = (np.arange(8) % SAMPLE_ROWS)[:, None]
    dist = MAX_WINDOW + t - pos
    mult = np.zeros(dist.shape, np.float32)
    for d in DILATIONS:
        mult += ((dist % d == 0) & (dist <= SPAN * d)).astype(np.float32)
    bucket = np.where(mult > 0, _t5_bucket_np(dist), -1).astype(np.int32)
    return bucket, np.concatenate([mult, mult], axis=0)


def _sample_bias_kernel(tab_ref, rb_ref, rbv_ref, bmain_ref, bnew_ref, *, buckets):
    pair = pl.program_id(0)
    tab = tab_ref[...]
    acc = [jnp.full(tab.shape, NEG, F32) for _ in range(2)]
    for bkt in buckets:
        hit = tab == bkt
        for s in range(2):
            acc[s] = jnp.where(hit, rb_ref[bkt, 2 * pair + s], acc[s])
    for s in range(2):
        bmain_ref[0, s * 8:(s + 1) * 8, :] = acc[s]
    t = lax.broadcasted_iota(jnp.int32, (8, 1), 0) % SAMPLE_ROWS
    for tp in range(SAMPLE_ROWS):
        b = jnp.full((8, LANES), NEG, F32)
        for dist in range(SAMPLE_ROWS):
            b = jnp.where(t - tp == dist, rbv_ref[dist:dist + 1, :], b)
        bnew_ref[tp] = b


def _sample_bias_tables(rel_bias):
    bucket, mult = _sample_tables()
    buckets = sorted(int(b) for b in np.unique(bucket) if b >= 0)
    npair = N_HEADS_A // 2
    rb_lanes = jnp.pad(rel_bias, ((0, 0), (0, LANES - N_HEADS_A)))
    bmain, bnew = pl.pallas_call(
        functools.partial(_sample_bias_kernel, buckets=buckets),
        grid=(npair,),
        in_specs=[_resident((8, MAX_WINDOW)), pl.BlockSpec(memory_space=pltpu.SMEM),
                  _resident((N_BUCKETS, LANES))],
        out_specs=[pl.BlockSpec((1, 16, MAX_WINDOW), lambda p: (p, 0, 0)),
                   pl.BlockSpec((SAMPLE_ROWS, 8, LANES), lambda p: (0, 0, 0))],
        out_shape=[jax.ShapeDtypeStruct((npair, 16, MAX_WINDOW), F32),
                   jax.ShapeDtypeStruct((SAMPLE_ROWS, 8, LANES), F32)],
        compiler_params=pltpu.CompilerParams(dimension_semantics=("arbitrary",)),
        name="sample_bias_tables",
    )(jnp.asarray(bucket), rel_bias, rb_lanes)
    return bmain, bnew, jnp.asarray(mult)


def _attn_sample_kernel(qkv_ref, kt_ref, vt_ref, bmain_ref, mult_ref, bnew_ref, e_ref, o_ref,
                        q8_ref, om_ref):
    rows = SAMPLE_ROWS
    npair = N_HEADS_A // 2
    blk = qkv_ref[0]
    q8_ref[0:rows, :] = blk
    q8_ref[rows:2 * rows, :] = blk
    q8 = q8_ref[:, 0:D_ATTN] * ATTN_SCALE
    kn = q8_ref[:, D_ATTN:2 * D_ATTN]
    vn = q8_ref[:, 2 * D_ATTN:3 * D_ATTN]
    lane = lax.broadcasted_iota(jnp.int32, (1, LANES), 1)
    first_head = lane < HEAD_DIM
    e = e_ref[...]

    s_new = []
    m_tot = jnp.full((8, LANES), NEG, F32)
    for tp in range(rows):
        hi, lo, lo2 = _split3(q8 * kn[tp:tp + 1, :])
        s = _dot_nt(hi, e) + _dot_nt(lo, e) + _dot_nt(lo2, e) + bnew_ref[tp]
        s_new.append(s)
        m_tot = jnp.maximum(m_tot, s)

    mult = mult_ref[...]
    l_tot = jnp.zeros((8, LANES), F32)
    for p in range(npair):
        qp = q8[:, p * LANES:(p + 1) * LANES]
        qbd = jnp.concatenate([jnp.where(first_head, qp, 0.0), jnp.where(first_head, 0.0, qp)],
                              axis=0).astype(BF16)
        s16 = _dot(qbd, kt_ref[0, p].astype(BF16)) + bmain_ref[p]
        m_new = jnp.concatenate([m_tot[:, 2 * p:2 * p + 1], m_tot[:, 2 * p + 1:2 * p + 2]], axis=0)
        m = jnp.maximum(jnp.max(s16, axis=-1, keepdims=True), m_new)
        pr = mult * jnp.exp(s16 - m)
        ls = jnp.sum(pr, axis=-1, keepdims=True)
        o16 = _dot_nt(pr.astype(BF16), vt_ref[0, p].astype(BF16))
        om_ref[:, p * LANES:(p + 1) * LANES] = jnp.where(first_head, o16[0:8], o16[8:16])
        m_tot = jnp.where(lane == 2 * p, m[0:8], jnp.where(lane == 2 * p + 1, m[8:16], m_tot))
        l_tot = jnp.where(lane == 2 * p, ls[0:8], jnp.where(lane == 2 * p + 1, ls[8:16], l_tot))

    t = lax.broadcasted_iota(jnp.int32, (8, 1), 0) % rows
    o = om_ref[...]
    for tp in range(rows):
        c = jnp.where(t == tp, float(len(DILATIONS)), jnp.where(t > tp, 1.0, 0.0))
        pn = c * jnp.exp(s_new[tp] - m_tot)
        l_tot = l_tot + pn
        o = o + _expand_heads(pn, e_ref) * vn[tp:tp + 1, :]
    o = o / _expand_heads(l_tot, e_ref)
    o_ref[0] = o[0:rows, :]


def _attn_sample(qkv3, kt, vt, bmain, mult, bnew, expand):
    batch, rows = qkv3.shape[0], qkv3.shape[1]
    npair = N_HEADS_A // 2
    cache = pl.BlockSpec((1, npair, LANES, MAX_WINDOW), lambda b: (b, 0, 0, 0))
    return pl.pallas_call(
        _attn_sample_kernel,
        grid=(batch,),
        in_specs=[pl.BlockSpec((1, rows, 3 * D_ATTN), lambda b: (b, 0, 0)), cache, cache,
                  _resident((npair, 16, MAX_WINDOW)), _resident((16, MAX_WINDOW)),
                  _resident((rows, 8, LANES)), _resident((LANES, D_ATTN))],
        out_specs=pl.BlockSpec((1, rows, D_ATTN), lambda b: (b, 0, 0)),
        out_shape=jax.ShapeDtypeStruct((batch, rows, D_ATTN), F32),
        scratch_shapes=[pltpu.VMEM((8, 3 * D_ATTN), F32), pltpu.VMEM((8, D_ATTN), F32)],
        compiler_params=pltpu.CompilerParams(
            dimension_semantics=("arbitrary",), vmem_limit_bytes=VMEM_LIMIT),
        name="attn_sample",
    )(qkv3, kt, vt, bmain, mult, bnew, expand)


def _head_expand_matrix():
    e = np.zeros((LANES, D_SSM), np.float32)
    for h in range(N_HEADS_S):
        e[h, h * SSM_HEAD_DIM:(h + 1) * SSM_HEAD_DIM] = 1.0
    return jnp.asarray(e, BF16)


def _prep_params(norm_mix, w_in, q_norm, k_norm, conv_w, conv_b, dt_bias, a_log, d_skip, ssm_norm,
                 w_out, norm_ffn, w_up, w_down):
    n_main = 3 * D_ATTN + D_SSM + CONV_DIM
    pad_heads = lambda v: jnp.pad(v.reshape(1, N_HEADS_S), ((0, 0), (0, LANES - N_HEADS_S)))
    return dict(
        gmix=norm_mix.reshape(1, D_MODEL),
        w_main=w_in[:, :n_main].astype(BF16),
        w_dt=jnp.pad(w_in[:, n_main:], ((0, 0), (0, LANES - N_HEADS_S))).astype(BF16),
        gq=jnp.tile(q_norm, N_HEADS_A).reshape(1, D_ATTN),
        gk=jnp.tile(k_norm, N_HEADS_A).reshape(1, D_ATTN),
        conv_w=conv_w,
        conv_b=conv_b.reshape(1, CONV_DIM),
        dt_bias=pad_heads(dt_bias),
        a_log=pad_heads(a_log),
        d_skip_e=jnp.repeat(d_skip, SSM_HEAD_DIM).reshape(1, D_SSM),
        gnorm=ssm_norm.reshape(1, D_SSM),
        expand=_head_expand_matrix(),
        w_out=w_out.astype(BF16),
        gffn=norm_ffn.reshape(1, D_MODEL),
        w_up=w_up.astype(BF16),
        w_down=w_down.astype(BF16),
    )


def _ssd_param_list(p):
    return (p["conv_w"], p["conv_b"], p["dt_bias"], p["a_log"], p["d_skip_e"], p["gnorm"], p["expand"])


def kernel(x_prompt, x_sample, cache_attn_k, cache_attn_v, state_conv, state_ssm, norm_mix, w_in, q_norm,
           k_norm, rel_bias, conv_w, conv_b, dt_bias, a_log, d_skip, ssm_norm, w_out, norm_ffn, w_up, w_down):
    depth = w_in.shape[0]
    assert depth == 1, "single-layer decoder"
    l = 0
    p = _prep_params(norm_mix[l], w_in[l], q_norm[l], k_norm[l], conv_w[l], conv_b[l], dt_bias[l], a_log[l],
                     d_skip[l], ssm_norm[l], w_out[l], norm_ffn[l], w_up[l], w_down[l])
    ssd_params = _ssd_param_list(p)
    tm = 512

    bp, sp = x_prompt.shape[:2]
    keep = min(MAX_WINDOW, sp)
    xp = x_prompt.reshape(bp * sp, D_MODEL)
    qkv, z, xbc, dt_raw = _inproj(xp, p["gmix"], p["w_main"], p["w_dt"], p["gq"], p["gk"], tm)
    attn = _attn_prompt(qkv, _bias_tables(rel_bias), bp)
    conv0 = jnp.zeros((bp, CONV_WIDTH - 1, CONV_DIM), F32)
    h0 = jnp.zeros((bp, D_SSM, D_STATE), F32)
    ssm_o, hst = _ssd_prompt(z, xbc, dt_raw, conv0, h0, ssd_params, bp)
    y_prompt = _outffn(xp, attn, ssm_o, p["w_out"], p["gffn"], p["w_up"], p["w_down"], tm)
    qkv3 = qkv.reshape(bp, sp, 3 * D_ATTN)
    new_k_prompt = qkv3[:, sp - keep:, D_ATTN:2 * D_ATTN].reshape(depth, bp, keep, N_HEADS_A, HEAD_DIM)
    new_v_prompt = qkv3[:, sp - keep:, 2 * D_ATTN:].reshape(depth, bp, keep, N_HEADS_A, HEAD_DIM)
    new_conv_prompt = xbc.reshape(bp, sp, CONV_DIM)[:, sp - (CONV_WIDTH - 1):].reshape(
        depth, bp, CONV_WIDTH - 1, CONV_DIM)
    new_ssm_prompt = hst.reshape(depth, bp, N_HEADS_S, SSM_HEAD_DIM, D_STATE)

    bs, ts = x_sample.shape[:2]
    assert cache_attn_k.shape[2] == MAX_WINDOW and ts == SAMPLE_ROWS
    xs = x_sample.reshape(bs * ts, D_MODEL)
    qkv_s, z_s, xbc_s, dt_s = _inproj(xs, p["gmix"], p["w_main"], p["w_dt"], p["gq"], p["gk"], tm)
    npair = N_HEADS_A // 2
    kt = jnp.transpose(cache_attn_k[l], (0, 2, 3, 1)).reshape(bs, npair, LANES, MAX_WINDOW)
    vt = jnp.transpose(cache_attn_v[l], (0, 2, 3, 1)).reshape(bs, npair, LANES, MAX_WINDOW)
    bmain, bnew, mult = _sample_bias_tables(rel_bias)
    attn_s = _attn_sample(qkv_s.reshape(bs, ts, 3 * D_ATTN), kt, vt, bmain, mult, bnew, p["expand"])
    ssm_os, hsts = _ssd_sample(z_s.reshape(bs, ts, D_SSM), xbc_s.reshape(bs, ts, CONV_DIM),
                               dt_s.reshape(bs, ts, LANES), state_conv[l],
                               state_ssm[l].reshape(bs, D_SSM, D_STATE), ssd_params)
    y_sample = _outffn(xs, attn_s.reshape(bs * ts, D_ATTN), ssm_os.reshape(bs * ts, D_SSM),
                       p["w_out"], p["gffn"], p["w_up"], p["w_down"], tm)
    new_k_sample = qkv_s[:, D_ATTN:2 * D_ATTN].reshape(depth, bs, ts, N_HEADS_A, HEAD_DIM)
    new_v_sample = qkv_s[:, 2 * D_ATTN:].reshape(depth, bs, ts, N_HEADS_A, HEAD_DIM)
    new_conv_sample = xbc_s.reshape(bs, ts, CONV_DIM)[:, ts - (CONV_WIDTH - 1):].reshape(
        depth, bs, CONV_WIDTH - 1, CONV_DIM)
    new_ssm_sample = hsts.reshape(depth, bs, N_HEADS_S, SSM_HEAD_DIM, D_STATE)

    return (y_prompt.reshape(bp, sp, D_MODEL), y_sample.reshape(bs, ts, D_MODEL),
            new_k_prompt, new_v_prompt, new_conv_prompt, new_ssm_prompt,
            new_k_sample, new_v_sample, new_conv_sample, new_ssm_sample)
```

```python
import functools
import math

import jax
import jax.numpy as jnp
import numpy as np
from jax import lax
from jax.experimental import pallas as pl
from jax.experimental.pallas import tpu as pltpu

F32 = jnp.float32
BF16 = jnp.bfloat16

D_MODEL = 1024
HEAD_DIM = 64
N_HEADS_A = 16
D_ATTN = 1024
SPAN = 128
DILATIONS = (1, 4, 16)
MAX_WINDOW = 2048
N_BUCKETS = 32
MAX_EXACT = 16
SSM_HEAD_DIM = 64
N_HEADS_S = 16
D_SSM = 1024
SSM_GROUPS = 2
D_STATE = 128
CONV_WIDTH = 4
CONV_DIM = D_SSM + 2 * SSM_GROUPS * D_STATE
SSD_CHUNK = 128
D_FF = 4096
EPS = 1e-6
ATTN_SCALE = HEAD_DIM ** -0.5
LOG2E = 1.4426950408889634

LANES = 128
NEG = -1e30
VMEM_LIMIT = 56 * 1024 * 1024


def _resident(shape):
    nd = len(shape)
    return pl.BlockSpec(shape, lambda *_: (0,) * nd, pipeline_mode=pl.Buffered(1))


def _split3(x):
    hi = x.astype(BF16)
    r1 = x - hi.astype(F32)
    lo = r1.astype(BF16)
    lo2 = (r1 - lo.astype(F32)).astype(BF16)
    return hi, lo, lo2


def _dot(a, b):
    return jnp.dot(a, b, preferred_element_type=F32)


def _dot_nt(a, b):
    return lax.dot_general(a, b, (((1,), (1,)), ((), ())), preferred_element_type=F32)


IN_TN = 512


def _inproj_kernel(x_ref, gmix_ref, w_ref, wdt_ref, gq_ref, gk_ref,
                   qkv_ref, z_ref, xbc_ref, dt_ref):
    x = x_ref[...]
    ms = jnp.mean(x * x, axis=-1, keepdims=True)
    xn = ((x * lax.rsqrt(ms + EPS)) * gmix_ref[...]).astype(BF16)
    first_head = lax.broadcasted_iota(jnp.int32, (1, LANES), 1) < HEAD_DIM

    n_main = 3 * D_ATTN + D_SSM + CONV_DIM
    for j in range(n_main // IN_TN):
        c0 = j * IN_TN
        t = _dot(xn, w_ref[:, c0:c0 + IN_TN])
        if c0 < 2 * D_ATTN:
            g_ref = gq_ref if c0 < D_ATTN else gk_ref
            g0 = c0 % D_ATTN
            for c in range(IN_TN // LANES):
                tc = t[:, c * LANES:(c + 1) * LANES]
                s = tc * tc
                sa = jnp.sum(jnp.where(first_head, s, 0.0), axis=-1, keepdims=True)
                sb = jnp.sum(jnp.where(first_head, 0.0, s), axis=-1, keepdims=True)
                inv = lax.rsqrt(jnp.where(first_head, sa, sb) * (1.0 / HEAD_DIM) + EPS)
                qkv_ref[:, c0 + c * LANES:c0 + (c + 1) * LANES] = (
                    (tc * inv) * g_ref[:, g0 + c * LANES:g0 + (c + 1) * LANES])
        elif c0 < 3 * D_ATTN:
            qkv_ref[:, c0:c0 + IN_TN] = t
        elif c0 < 3 * D_ATTN + D_SSM:
            z_ref[:, c0 - 3 * D_ATTN:c0 - 3 * D_ATTN + IN_TN] = t
        else:
            o0 = c0 - 3 * D_ATTN - D_SSM
            xbc_ref[:, o0:o0 + IN_TN] = t
    dt_ref[...] = _dot(xn, wdt_ref[...])


def _inproj(x2d, gmix, w_main, w_dt, gq, gk, tm):
    m = x2d.shape[0]
    n_main = w_main.shape[1]
    row = lambda w: pl.BlockSpec((tm, w), lambda i: (i, 0))
    return pl.pallas_call(
        _inproj_kernel,
        grid=(m // tm,),
        in_specs=[row(D_MODEL), _resident((1, D_MODEL)), _resident((D_MODEL, n_main)),
                  _resident((D_MODEL, LANES)), _resident((1, D_ATTN)), _resident((1, D_ATTN))],
        out_specs=[row(3 * D_ATTN), row(D_SSM), row(CONV_DIM), row(LANES)],
        out_shape=[jax.ShapeDtypeStruct((m, 3 * D_ATTN), F32),
                   jax.ShapeDtypeStruct((m, D_SSM), F32),
                   jax.ShapeDtypeStruct((m, CONV_DIM), F32),
                   jax.ShapeDtypeStruct((m, LANES), F32)],
        compiler_params=pltpu.CompilerParams(
            dimension_semantics=("arbitrary",), vmem_limit_bytes=VMEM_LIMIT),
        name="inproj",
    )(x2d, gmix, w_main, w_dt, gq, gk)


FFN_TF = 1024


def _outffn_kernel(x_ref, attn_ref, ssm_ref, wout_ref, gffn_ref, wup_ref, wdown_ref, y_ref):
    a = attn_ref[...].astype(BF16)
    s = ssm_ref[...].astype(BF16)
    h = x_ref[...] + _dot(a, wout_ref[0:D_ATTN, :]) + _dot(s, wout_ref[D_ATTN:D_ATTN + D_SSM, :])
    ms = jnp.mean(h * h, axis=-1, keepdims=True)
    hn = ((h * lax.rsqrt(ms + EPS)) * gffn_ref[...]).astype(BF16)
    acc = None
    for f in range(D_FF // FFN_TF):
        u = _dot(hn, wup_ref[:, f * FFN_TF:(f + 1) * FFN_TF])
        u = jnp.maximum(u, 0.0)
        u = (u * u).astype(BF16)
        d = _dot(u, wdown_ref[f * FFN_TF:(f + 1) * FFN_TF, :])
        acc = d if acc is None else acc + d
    y_ref[...] = h + acc


def _outffn(x2d, attn, ssm_o, w_out, gffn, w_up, w_down, tm):
    m = x2d.shape[0]
    row = pl.BlockSpec((tm, D_MODEL), lambda i: (i, 0))
    return pl.pallas_call(
        _outffn_kernel,
        grid=(m // tm,),
        in_specs=[row, row, row, _resident((D_ATTN + D_SSM, D_MODEL)), _resident((1, D_MODEL)),
                  _resident((D_MODEL, D_FF)), _resident((D_FF, D_MODEL))],
        out_specs=row,
        out_shape=jax.ShapeDtypeStruct((m, D_MODEL), F32),
        compiler_params=pltpu.CompilerParams(
            dimension_semantics=("arbitrary",), vmem_limit_bytes=VMEM_LIMIT),
        name="outffn",
    )(x2d, attn, ssm_o, w_out, gffn, w_up, w_down)


def _sigmoid(x):
    return 1.0 / (1.0 + jnp.exp(-x))


def _softplus(x):
    return jnp.maximum(x, 0.0) + jnp.log1p(jnp.exp(-jnp.abs(x)))


def _expand_heads(v, e_ref):
    hi, lo, lo2 = _split3(v)
    e = e_ref[...]
    return _dot(hi, e) + _dot(lo, e) + _dot(lo2, e)


def _ssd_chunk(z_ref, xbc_ref, dt_ref, y_ref, xpad_ref, state_ref,
               convw_ref, convb_ref, dtb_ref, alog_ref, dskip_ref, gnorm_ref, e_ref, valid_rows):
    q = SSD_CHUNK
    pad = 8
    tail = CONV_WIDTH - 1

    xbc = xbc_ref[...]
    xpad_ref[pad:pad + q, :] = xbc
    conv = convb_ref[...] + convw_ref[tail:tail + 1, :] * xbc
    for i in range(tail):
        conv = conv + convw_ref[i:i + 1, :] * xpad_ref[pad - tail + i:pad - tail + i + q, :]
    xpad_ref[pad - tail:pad, :] = xbc[q - tail:q, :]
    xc = conv * _sigmoid(conv)
    xs = xc[:, 0:D_SSM]

    lane = lax.broadcasted_iota(jnp.int32, (1, LANES), 1)
    live = lane < N_HEADS_S
    if valid_rows < q:
        live = jnp.logical_and(live, lax.broadcasted_iota(jnp.int32, (q, 1), 0) < valid_rows)
    dt = jnp.where(live, _softplus(dt_ref[...] + dtb_ref[...]), 0.0)
    da = dt * (-jnp.exp(alog_ref[...]))

    ii = lax.broadcasted_iota(jnp.int32, (q, q), 0)
    jj = lax.broadcasted_iota(jnp.int32, (q, q), 1)
    causal = ii >= jj
    tri = jnp.where(causal, 1.0, 0.0).astype(BF16)
    hi, lo, lo2 = _split3(da)
    cum = _dot(tri, hi) + _dot(tri, lo) + _dot(tri, lo2)
    cum_t = cum.T

    cum_e = _expand_heads(cum, e_ref)
    last_e = cum_e[q - 1:q, :]
    xdt = xs * _expand_heads(dt, e_ref)
    xw_t = (xdt * jnp.exp(last_e - cum_e)).T.astype(BF16)
    state_decay = jnp.exp(jnp.broadcast_to(last_e, (q, D_SSM)).T)
    off_scale = jnp.exp(cum_e)

    first_head = lane < SSM_HEAD_DIM
    heads_per_group = N_HEADS_S // SSM_GROUPS
    gw = heads_per_group * SSM_HEAD_DIM
    state = state_ref[...]
    state_bf = state.astype(BF16)
    for g in range(SSM_GROUPS):
        bg = xc[:, D_SSM + g * D_STATE:D_SSM + (g + 1) * D_STATE].astype(BF16)
        cg = xc[:, D_SSM + (SSM_GROUPS + g) * D_STATE:D_SSM + (SSM_GROUPS + g + 1) * D_STATE].astype(BF16)
        cb = _dot_nt(cg, bg)
        y_off = _dot_nt(cg, state_bf[g * gw:(g + 1) * gw, :]) * off_scale[:, g * gw:(g + 1) * gw]
        y_ref[:, g * gw:(g + 1) * gw] = y_off
        for hp in range(heads_per_group // 2):
            pair = g * (heads_per_group // 2) + hp
            xp = xdt[:, pair * LANES:(pair + 1) * LANES]
            yp = None
            for s in range(2):
                h = 2 * pair + s
                seg = cum[:, h:h + 1] - cum_t[h:h + 1, :]
                dec = jnp.exp(jnp.where(causal, seg, NEG))
                m = (cb * dec).astype(BF16)
                xm = jnp.where(first_head if s == 0 else jnp.logical_not(first_head), xp, 0.0).astype(BF16)
                d = _dot(m, xm)
                yp = d if yp is None else yp + d
            y_ref[:, pair * LANES:(pair + 1) * LANES] += yp
        state_ref[g * gw:(g + 1) * gw, :] = (
            state_decay[g * gw:(g + 1) * gw, :] * state[g * gw:(g + 1) * gw, :]
            + _dot(xw_t[g * gw:(g + 1) * gw, :], bg))

    z = z_ref[...]
    y = (y_ref[...] + xs * dskip_ref[...]) * (z * _sigmoid(z))
    for g in range(SSM_GROUPS):
        yg = y[:, g * gw:(g + 1) * gw]
        ms = jnp.mean(yg * yg, axis=-1, keepdims=True)
        y_ref[:, g * gw:(g + 1) * gw] = (yg * lax.rsqrt(ms + EPS)) * gnorm_ref[:, g * gw:(g + 1) * gw]


def _ssd_prompt_kernel(z_ref, xbc_ref, dt_ref, conv0_ref, h0_ref, convw_ref, convb_ref, dtb_ref,
                       alog_ref, dskip_ref, gnorm_ref, e_ref, y_ref, hout_ref, xpad_ref, state_ref):
    c = pl.program_id(1)

    @pl.when(c == 0)
    def _():
        state_ref[...] = h0_ref[0]
        xpad_ref[8 - (CONV_WIDTH - 1):8, :] = conv0_ref[0]

    _ssd_chunk(z_ref, xbc_ref, dt_ref, y_ref, xpad_ref, state_ref, convw_ref, convb_ref, dtb_ref,
               alog_ref, dskip_ref, gnorm_ref, e_ref, SSD_CHUNK)

    @pl.when(c == pl.num_programs(1) - 1)
    def _():
        hout_ref[0] = state_ref[...]


def _ssd_sample_kernel(z_ref, xbc_ref, dt_ref, conv0_ref, h0_ref, convw_ref, convb_ref, dtb_ref,
                       alog_ref, dskip_ref, gnorm_ref, e_ref, y_ref, hout_ref,
                       xpad_ref, state_ref, zp_ref, xbcp_ref, dtp_ref, yp_ref, *, rows):
    state_ref[...] = h0_ref[0]
    xpad_ref[8 - (CONV_WIDTH - 1):8, :] = conv0_ref[0]
    for src, dst in ((z_ref, zp_ref), (xbc_ref, xbcp_ref), (dt_ref, dtp_ref)):
        dst[...] = jnp.zeros(dst.shape, F32)
        dst[0:rows, :] = src[0]
    _ssd_chunk(zp_ref, xbcp_ref, dtp_ref, yp_ref, xpad_ref, state_ref, convw_ref, convb_ref, dtb_ref,
               alog_ref, dskip_ref, gnorm_ref, e_ref, rows)
    y_ref[0] = yp_ref[0:rows, :]
    hout_ref[0] = state_ref[...]


def _ssd_param_specs():
    return [_resident((CONV_WIDTH, CONV_DIM)), _resident((1, CONV_DIM)), _resident((1, LANES)),
            _resident((1, LANES)), _resident((1, D_SSM)), _resident((1, D_SSM)),
            _resident((LANES, D_SSM))]


def _ssd_prompt(z, xbc, dt_raw, conv0, h0, ssd_params, batch):
    m = z.shape[0]
    nc = m // batch // SSD_CHUNK
    row = lambda w: pl.BlockSpec((SSD_CHUNK, w), lambda b, c: (b * nc + c, 0))
    per_b = lambda r, w: pl.BlockSpec((1, r, w), lambda b, c: (b, 0, 0))
    return pl.pallas_call(
        _ssd_prompt_kernel,
        grid=(batch, nc),
        in_specs=[row(D_SSM), row(CONV_DIM), row(LANES), per_b(CONV_WIDTH - 1, CONV_DIM),
                  per_b(D_SSM, D_STATE)] + _ssd_param_specs(),
        out_specs=[row(D_SSM), per_b(D_SSM, D_STATE)],
        out_shape=[jax.ShapeDtypeStruct((m, D_SSM), F32),
                   jax.ShapeDtypeStruct((batch, D_SSM, D_STATE), F32)],
        scratch_shapes=[pltpu.VMEM((8 + SSD_CHUNK, CONV_DIM), F32), pltpu.VMEM((D_SSM, D_STATE), F32)],
        compiler_params=pltpu.CompilerParams(
            dimension_semantics=("arbitrary", "arbitrary"), vmem_limit_bytes=VMEM_LIMIT),
        name="ssd_prompt",
    )(z, xbc, dt_raw, conv0, h0, *ssd_params)


def _ssd_sample(z, xbc, dt_raw, conv0, h0, ssd_params):
    batch, rows = z.shape[0], z.shape[1]
    per_b = lambda r, w: pl.BlockSpec((1, r, w), lambda b: (b, 0, 0))
    q = SSD_CHUNK
    return pl.pallas_call(
        functools.partial(_ssd_sample_kernel, rows=rows),
        grid=(batch,),
        in_specs=[per_b(rows, D_SSM), per_b(rows, CONV_DIM), per_b(rows, LANES),
                  per_b(CONV_WIDTH - 1, CONV_DIM), per_b(D_SSM, D_STATE)] + _ssd_param_specs(),
        out_specs=[per_b(rows, D_SSM), per_b(D_SSM, D_STATE)],
        out_shape=[jax.ShapeDtypeStruct((batch, rows, D_SSM), F32),
                   jax.ShapeDtypeStruct((batch, D_SSM, D_STATE), F32)],
        scratch_shapes=[pltpu.VMEM((8 + q, CONV_DIM), F32), pltpu.VMEM((D_SSM, D_STATE), F32),
                        pltpu.VMEM((q, D_SSM), F32), pltpu.VMEM((q, CONV_DIM), F32),
                        pltpu.VMEM((q, LANES), F32), pltpu.VMEM((q, D_SSM), F32)],
        compiler_params=pltpu.CompilerParams(
            dimension_semantics=("arbitrary",), vmem_limit_bytes=VMEM_LIMIT),
        name="ssd_sample",
    )(z, xbc, dt_raw, conv0, h0, *ssd_params)


def _t5_bucket_np(dist):
    dist = np.asarray(dist, np.int64)
    df = np.maximum(dist, 1).astype(np.float32)
    large = MAX_EXACT + (np.log(df / np.float32(MAX_EXACT)) / np.float32(math.log(MAX_WINDOW / MAX_EXACT))
                         * np.float32(N_BUCKETS - MAX_EXACT)).astype(np.int32)
    return np.where(dist < MAX_EXACT, dist, np.minimum(large, N_BUCKETS - 1)).astype(np.int32)


def _band_bucket_tables():
    ki = np.arange(SPAN)[None, :]
    out = np.empty((len(DILATIONS), 2, SPAN, SPAN), np.int32)
    for bi, d in enumerate(DILATIONS):
        nrun = DILATIONS[-1] // d
        row = np.arange(SPAN)
        qi = (nrun * (row % (SPAN // nrun)) + row // (SPAN // nrun))[:, None]
        rel_prev = qi + SPAN - ki
        rel_cur = qi - ki
        out[bi, 0] = np.where(rel_prev <= SPAN, _t5_bucket_np(np.clip(rel_prev, 0, SPAN) * d), -1)
        out[bi, 1] = np.where(rel_cur >= 0, _t5_bucket_np(np.clip(rel_cur, 0, SPAN) * d), -1)
    return out


def _bias_table_kernel(tab_ref, rb_ref, out_ref, *, buckets):
    pair = pl.program_id(0)
    for bi in range(len(DILATIONS)):
        for half in range(2):
            tab = tab_ref[bi, half]
            acc = [jnp.full((SPAN, SPAN), NEG, F32) for _ in range(2)]
            for bkt in buckets[bi][half]:
                hit = tab == bkt
                for s in range(2):
                    acc[s] = jnp.where(hit, rb_ref[bkt, 2 * pair + s] * LOG2E, acc[s])
            for s in range(2):
                if half == 0:
                    out_ref[0, bi, s, 0, :, 0:SPAN] = jnp.full((SPAN, SPAN), NEG, F32)
                    out_ref[0, bi, s, 1, :, 0:SPAN] = acc[s]
                else:
                    out_ref[0, bi, s, 0, :, SPAN:2 * SPAN] = acc[s]
                    out_ref[0, bi, s, 1, :, SPAN:2 * SPAN] = acc[s]


def _bias_tables(rel_bias):
    tabs = _band_bucket_tables()
    buckets = [[sorted(int(b) for b in np.unique(tabs[bi, half]) if b >= 0) for half in range(2)]
               for bi in range(len(DILATIONS))]
    nd = len(DILATIONS)
    return pl.pallas_call(
        functools.partial(_bias_table_kernel, buckets=buckets),
        grid=(N_HEADS_A // 2,),
        in_specs=[_resident((nd, 2, SPAN, SPAN)),
                  pl.BlockSpec(memory_space=pltpu.SMEM)],
        out_specs=pl.BlockSpec((1, nd, 2, 2, SPAN, 2 * SPAN), lambda p: (p, 0, 0, 0, 0, 0)),
        out_shape=jax.ShapeDtypeStruct((N_HEADS_A // 2, nd, 2, 2, SPAN, 2 * SPAN), F32),
        compiler_params=pltpu.CompilerParams(dimension_semantics=("arbitrary",)),
        name="bias_tables",
    )(jnp.asarray(tabs), rel_bias)


ATT_T = SPAN * DILATIONS[-1]
ATT_UNROLL = 4


def _query_runs(d, r, n):
    run = SPAN * d // DILATIONS[-1]
    return [pl.ds(pl.multiple_of((r + d * c) * SPAN + run * n, 8), run) for c in range(DILATIONS[-1] // d)]


def _load_runs(ref, runs, *lead):
    parts = [ref[(*lead, rows, slice(None))] for rows in runs]
    return parts[0] if len(parts) == 1 else jnp.concatenate(parts, axis=0)


def _store_runs(ref, runs, value, *lead):
    run = value.shape[0] // len(runs)
    for c, rows in enumerate(runs):
        ref[(*lead, rows, slice(None))] = value[c * run:(c + 1) * run, :]


def _attn_kernel(q_ref, k_ref, v_ref, bias_ref, o_ref, qp_ref, kph_ref, vm_ref, acc_ref, m_ref, l_ref):
    t = pl.program_id(2)
    lane = lax.broadcasted_iota(jnp.int32, (1, LANES), 1)
    first_head = lane < HEAD_DIM
    head_mask = (first_head, jnp.logical_not(first_head))
    nblk = ATT_T // SPAN
    dmax = DILATIONS[-1]

    @pl.when(t == 0)
    def _():
        for bi, d in enumerate(DILATIONS):
            ph = ATT_T // d + SPAN
            for r in range(d):
                kph_ref[bi, r * ph:r * ph + SPAN, :] = jnp.zeros((SPAN, LANES), BF16)
                for s in range(2):
                    vm_ref[bi, s, r * ph:r * ph + SPAN, :] = jnp.zeros((SPAN, LANES), BF16)

    for r in range(dmax):
        qp_ref[r * SPAN:(r + 1) * SPAN, :] = q_ref[pl.ds(r, SPAN, stride=dmax), :] * (ATTN_SCALE * LOG2E)
    for bi, d in enumerate(DILATIONS):
        ln = ATT_T // d
        ph = ln + SPAN
        for r in range(d):
            rows = pl.ds(r, ln, stride=d) if d > 1 else pl.ds(0, ln)
            kv = k_ref[rows, :]
            vv = v_ref[rows, :]
            kph_ref[bi, r * ph + SPAN:(r + 1) * ph, :] = kv.astype(BF16)
            for s in range(2):
                vm_ref[bi, s, r * ph + SPAN:(r + 1) * ph, :] = jnp.where(head_mask[s], vv, 0.0).astype(BF16)

    order = tuple(reversed(range(len(DILATIONS))))
    for bi in order:
        d = DILATIONS[bi]
        ln = ATT_T // d
        ph = ln + SPAN
        nb = ln // SPAN
        first_branch = bi == order[0]
        last_branch = bi == order[-1]

        def group(it, carry, bi=bi, d=d, ph=ph, nb=nb, first_branch=first_branch, last_branch=last_branch):
            loaded = []
            for g in range(ATT_UNROLL):
                blk = it * ATT_UNROLL + g
                r = blk // nb
                n = blk % nb
                runs = _query_runs(d, r, n)
                krow = pl.multiple_of(r * ph + n * SPAN, SPAN)
                variant = jnp.logical_or(t > 0, n > 0).astype(jnp.int32)
                kk = kph_ref[bi, pl.ds(krow, 2 * SPAN), :]
                qf = _load_runs(qp_ref, runs)
                qs = [jnp.where(head_mask[s], qf, 0.0).astype(BF16) for s in range(2)]
                vs = [vm_ref[bi, s, pl.ds(krow, 2 * SPAN), :] for s in range(2)]
                bs = [bias_ref[0, bi, s, variant] for s in range(2)]
                old = None
                if not first_branch:
                    old = ([_load_runs(m_ref, runs, s) for s in range(2)],
                           [_load_runs(l_ref, runs, s) for s in range(2)], _load_runs(acc_ref, runs))
                loaded.append((runs, kk, qs, vs, bs, old))
            logits = [[_dot_nt(qs[s], kk) + bs[s] for s in range(2)]
                      for _, kk, qs, _, bs, _ in loaded]
            soft = []
            for (runs, kk, qs, vs, bs, old), sc2 in zip(loaded, logits):
                m_new, l_new, alpha, ps = [], [], [], []
                for s in range(2):
                    halves = (sc2[s][:, 0:SPAN], sc2[s][:, SPAN:2 * SPAN])
                    mb = jnp.max(jnp.maximum(halves[0], halves[1]), axis=-1, keepdims=True)
                    if first_branch:
                        m = jnp.broadcast_to(mb, (SPAN, LANES))
                    else:
                        m = jnp.maximum(old[0][s], mb)
                        alpha.append(jnp.exp2(old[0][s] - m))
                    p = [jnp.exp2(h - m) for h in halves]
                    rs = jnp.sum(p[0] + p[1], axis=-1, keepdims=True)
                    ps.append(jnp.concatenate([p[0].astype(BF16), p[1].astype(BF16)], axis=1))
                    m_new.append(m)
                    l_new.append(jnp.broadcast_to(rs, (SPAN, LANES)) if first_branch
                                 else alpha[s] * old[1][s] + rs)
                soft.append((m_new, l_new, alpha, ps))
            results = []
            for (runs, kk, qs, vs, bs, old), (m_new, l_new, alpha, ps) in zip(loaded, soft):
                pv = _dot(ps[0], vs[0]) + _dot(ps[1], vs[1])
                if not first_branch:
                    pv = jnp.where(first_head, alpha[0], alpha[1]) * old[2] + pv
                results.append((runs, m_new, l_new, pv))
            for runs, m_new, l_new, pv in results:
                if last_branch:
                    _store_runs(acc_ref, runs, pv / jnp.where(first_head, l_new[0], l_new[1]))
                else:
                    for s in range(2):
                        _store_runs(m_ref, runs, m_new[s], s)
                        _store_runs(l_ref, runs, l_new[s], s)
                    _store_runs(acc_ref, runs, pv)
            return carry

        lax.fori_loop(0, nblk // ATT_UNROLL, group, 0)

    for r in range(dmax):
        o_ref[pl.ds(r, SPAN, stride=dmax), :] = acc_ref[r * SPAN:(r + 1) * SPAN, :]

    for bi, d in enumerate(DILATIONS):
        ln = ATT_T // d
        ph = ln + SPAN
        for r in range(d):
            kph_ref[bi, r * ph:r * ph + SPAN, :] = kph_ref[bi, (r + 1) * ph - SPAN:(r + 1) * ph, :]
            for s in range(2):
                vm_ref[bi, s, r * ph:r * ph + SPAN, :] = vm_ref[bi, s, (r + 1) * ph - SPAN:(r + 1) * ph, :]


def _attn_prompt(qkv, bias, batch):
    m = qkv.shape[0]
    nt = m // batch // ATT_T
    npair = N_HEADS_A // 2
    nd = len(DILATIONS)
    blk = lambda off: pl.BlockSpec((ATT_T, LANES), lambda b, p, t: (b * nt + t, off + p))
    halo_rows = ATT_T + DILATIONS[-1] * SPAN
    return pl.pallas_call(
        _attn_kernel,
        grid=(batch, npair, nt),
        in_specs=[blk(0), blk(npair), blk(2 * npair),
                  pl.BlockSpec((1, nd, 2, 2, SPAN, 2 * SPAN), lambda b, p, t: (p, 0, 0, 0, 0, 0))],
        out_specs=blk(0),
        out_shape=jax.ShapeDtypeStruct((m, D_ATTN), F32),
        scratch_shapes=[pltpu.VMEM((ATT_T, LANES), F32),
                        pltpu.VMEM((nd, halo_rows, LANES), BF16),
                        pltpu.VMEM((nd, 2, halo_rows, LANES), BF16),
                        pltpu.VMEM((ATT_T, LANES), F32),
                        pltpu.VMEM((2, ATT_T, LANES), F32),
                        pltpu.VMEM((2, ATT_T, LANES), F32)],
        compiler_params=pltpu.CompilerParams(
            dimension_semantics=("arbitrary", "arbitrary", "arbitrary"), vmem_limit_bytes=VMEM_LIMIT),
        name="attn_prompt",
    )(qkv, qkv, qkv, bias)


SAMPLE_ROWS = 4


def _sample_tables():
    pos = np.arange(MAX_WINDOW)[None, :]
    t = (np.arange(8) % SAMPLE_ROWS)[:, None]
    dist = MAX_WINDOW + t - pos
    mult = np.zeros(dist.shape, np.float32)
    for d in DILATIONS:
        mult += ((dist % d == 0) & (dist <= SPAN * d)).astype(np.float32)
    bucket = np.where(mult > 0, _t5_bucket_np(dist), -1).astype(np.int32)
    return bucket, np.concatenate([mult, mult], axis=0)


def _sample_bias_kernel(tab_ref, rb_ref, rbv_ref, bmain_ref, bnew_ref, *, buckets):
    pair = pl.program_id(0)
    tab = tab_ref[...]
    acc = [jnp.full(tab.shape, NEG, F32) for _ in range(2)]
    for bkt in buckets:
        hit = tab == bkt
        for s in range(2):
            acc[s] = jnp.where(hit, rb_ref[bkt, 2 * pair + s], acc[s])
    for s in range(2):
        bmain_ref[0, s * 8:(s + 1) * 8, :] = acc[s]
    t = lax.broadcasted_iota(jnp.int32, (8, 1), 0) % SAMPLE_ROWS
    for tp in range(SAMPLE_ROWS):
        b = jnp.full((8, LANES), NEG, F32)
        for dist in range(SAMPLE_ROWS):
            b = jnp.where(t - tp == dist, rbv_ref[dist:dist + 1, :], b)
        bnew_ref[tp] = b


def _sample_bias_tables(rel_bias):
    bucket, mult = _sample_tables()
    buckets = sorted(int(b) for b in np.unique(bucket) if b >= 0)
    npair = N_HEADS_A // 2
    rb_lanes = jnp.pad(rel_bias, ((0, 0), (0, LANES - N_HEADS_A)))
    bmain, bnew = pl.pallas_call(
        functools.partial(_sample_bias_kernel, buckets=buckets),
        grid=(npair,),
        in_specs=[_resident((8, MAX_WINDOW)), pl.BlockSpec(memory_space=pltpu.SMEM),
                  _resident((N_BUCKETS, LANES))],
        out_specs=[pl.BlockSpec((1, 16, MAX_WINDOW), lambda p: (p, 0, 0)),
                   pl.BlockSpec((SAMPLE_ROWS, 8, LANES), lambda p: (0, 0, 0))],
        out_shape=[jax.ShapeDtypeStruct((npair, 16, MAX_WINDOW), F32),
                   jax.ShapeDtypeStruct((SAMPLE_ROWS, 8, LANES), F32)],
        compiler_params=pltpu.CompilerParams(dimension_semantics=("arbitrary",)),
        name="sample_bias_tables",
    )(jnp.asarray(bucket), rel_bias, rb_lanes)
    return bmain, bnew, jnp.asarray(mult)


def _attn_sample_kernel(qkv_ref, kt_ref, vt_ref, bmain_ref, mult_ref, bnew_ref, e_ref, o_ref,
                        q8_ref, om_ref):
    rows = SAMPLE_ROWS
    npair = N_HEADS_A // 2
    blk = qkv_ref[0]
    q8_ref[0:rows, :] = blk
    q8_ref[rows:2 * rows, :] = blk
    q8 = q8_ref[:, 0:D_ATTN] * ATTN_SCALE
    kn = q8_ref[:, D_ATTN:2 * D_ATTN]
    vn = q8_ref[:, 2 * D_ATTN:3 * D_ATTN]
    lane = lax.broadcasted_iota(jnp.int32, (1, LANES), 1)
    first_head = lane < HEAD_DIM
    e = e_ref[...]

    s_new = []
    m_tot = jnp.full((8, LANES), NEG, F32)
    for tp in range(rows):
        hi, lo, lo2 = _split3(q8 * kn[tp:tp + 1, :])
        s = _dot_nt(hi, e) + _dot_nt(lo, e) + _dot_nt(lo2, e) + bnew_ref[tp]
        s_new.append(s)
        m_tot = jnp.maximum(m_tot, s)

    mult = mult_ref[...]
    l_tot = jnp.zeros((8, LANES), F32)
    for p in range(npair):
        qp = q8[:, p * LANES:(p + 1) * LANES]
        qbd = jnp.concatenate([jnp.where(first_head, qp, 0.0), jnp.where(first_head, 0.0, qp)],
                              axis=0).astype(BF16)
        s16 = _dot(qbd, kt_ref[0, p].astype(BF16)) + bmain_ref[p]
        m_new = jnp.concatenate([m_tot[:, 2 * p:2 * p + 1], m_tot[:, 2 * p + 1:2 * p + 2]], axis=0)
        m = jnp.maximum(jnp.max(s16, axis=-1, keepdims=True), m_new)
        pr = mult * jnp.exp(s16 - m)
        ls = jnp.sum(pr, axis=-1, keepdims=True)
        o16 = _dot_nt(pr.astype(BF16), vt_ref[0, p].astype(BF16))
        om_ref[:, p * LANES:(p + 1) * LANES] = jnp.where(first_head, o16[0:8], o16[8:16])
        m_tot = jnp.where(lane == 2 * p, m[0:8], jnp.where(lane == 2 * p + 1, m[8:16], m_tot))
        l_tot = jnp.where(lane == 2 * p, ls[0:8], jnp.where(lane == 2 * p + 1, ls[8:16], l_tot))

    t = lax.broadcasted_iota(jnp.int32, (8, 1), 0) % rows
    o = om_ref[...]
    for tp in range(rows):
        c = jnp.where(t == tp, float(len(DILATIONS)), jnp.where(t > tp, 1.0, 0.0))
        pn = c * jnp.exp(s_new[tp] - m_tot)
        l_tot = l_tot + pn
        o = o + _expand_heads(pn, e_ref) * vn[tp:tp + 1, :]
    o = o / _expand_heads(l_tot, e_ref)
    o_ref[0] = o[0:rows, :]


def _attn_sample(qkv3, kt, vt, bmain, mult, bnew, expand):
    batch, rows = qkv3.shape[0], qkv3.shape[1]
    npair = N_HEADS_A // 2
    cache = pl.BlockSpec((1, npair, LANES, MAX_WINDOW), lambda b: (b, 0, 0, 0))
    return pl.pallas_call(
        _attn_sample_kernel,
        grid=(batch,),
        in_specs=[pl.BlockSpec((1, rows, 3 * D_ATTN), lambda b: (b, 0, 0)), cache, cache,
                  _resident((npair, 16, MAX_WINDOW)), _resident((16, MAX_WINDOW)),
                  _resident((rows, 8, LANES)), _resident((LANES, D_ATTN))],
        out_specs=pl.BlockSpec((1, rows, D_ATTN), lambda b: (b, 0, 0)),
        out_shape=jax.ShapeDtypeStruct((batch, rows, D_ATTN), F32),
        scratch_shapes=[pltpu.VMEM((8, 3 * D_ATTN), F32), pltpu.VMEM((8, D_ATTN), F32)],
        compiler_params=pltpu.CompilerParams(
            dimension_semantics=("arbitrary",), vmem_limit_bytes=VMEM_LIMIT),
        name="attn_sample",
    )(qkv3, kt, vt, bmain, mult, bnew, expand)


def _head_expand_matrix():
    e = np.zeros((LANES, D_SSM), np.float32)
    for h in range(N_HEADS_S):
        e[h, h * SSM_HEAD_DIM:(h + 1) * SSM_HEAD_DIM] = 1.0
    return jnp.asarray(e, BF16)


def _prep_params(norm_mix, w_in, q_norm, k_norm, conv_w, conv_b, dt_bias, a_log, d_skip, ssm_norm,
                 w_out, norm_ffn, w_up, w_down):
    n_main = 3 * D_ATTN + D_SSM + CONV_DIM
    pad_heads = lambda v: jnp.pad(v.reshape(1, N_HEADS_S), ((0, 0), (0, LANES - N_HEADS_S)))
    return dict(
        gmix=norm_mix.reshape(1, D_MODEL),
        w_main=w_in[:, :n_main].astype(BF16),
        w_dt=jnp.pad(w_in[:, n_main:], ((0, 0), (0, LANES - N_HEADS_S))).astype(BF16),
        gq=jnp.tile(q_norm, N_HEADS_A).reshape(1, D_ATTN),
        gk=jnp.tile(k_norm, N_HEADS_A).reshape(1, D_ATTN),
        conv_w=conv_w,
        conv_b=conv_b.reshape(1, CONV_DIM),
        dt_bias=pad_heads(dt_bias),
        a_log=pad_heads(a_log),
        d_skip_e=jnp.repeat(d_skip, SSM_HEAD_DIM).reshape(1, D_SSM),
        gnorm=ssm_norm.reshape(1, D_SSM),
        expand=_head_expand_matrix(),
        w_out=w_out.astype(BF16),
        gffn=norm_ffn.reshape(1, D_MODEL),
        w_up=w_up.astype(BF16),
        w_down=w_down.astype(BF16),
    )


def _ssd_param_list(p):
    return (p["conv_w"], p["conv_b"], p["dt_bias"], p["a_log"], p["d_skip_e"], p["gnorm"], p["expand"])


def kernel(x_prompt, x_sample, cache_attn_k, cache_attn_v, state_conv, state_ssm, norm_mix, w_in, q_norm,
           k_norm, rel_bias, conv_w, conv_b, dt_bias, a_log, d_skip, ssm_norm, w_out, norm_ffn, w_up, w_down):
    depth = w_in.shape[0]
    assert depth == 1, "single-layer decoder"
    l = 0
    p = _prep_params(norm_mix[l], w_in[l], q_norm[l], k_norm[l], conv_w[l], conv_b[l], dt_bias[l], a_log[l],
                     d_skip[l], ssm_norm[l], w_out[l], norm_ffn[l], w_up[l], w_down[l])
    ssd_params = _ssd_param_list(p)
    tm = 512

    bp, sp = x_prompt.shape[:2]
    keep = min(MAX_WINDOW, sp)
    xp = x_prompt.reshape(bp * sp, D_MODEL)
    qkv, z, xbc, dt_raw = _inproj(xp, p["gmix"], p["w_main"], p["w_dt"], p["gq"], p["gk"], tm)
    attn = _attn_prompt(qkv, _bias_tables(rel_bias), bp)
    conv0 = jnp.zeros((bp, CONV_WIDTH - 1, CONV_DIM), F32)
    h0 = jnp.zeros((bp, D_SSM, D_STATE), F32)
    ssm_o, hst = _ssd_prompt(z, xbc, dt_raw, conv0, h0, ssd_params, bp)
    y_prompt = _outffn(xp, attn, ssm_o, p["w_out"], p["gffn"], p["w_up"], p["w_down"], tm)
    qkv3 = qkv.reshape(bp, sp, 3 * D_ATTN)
    new_k_prompt = qkv3[:, sp - keep:, D_ATTN:2 * D_ATTN].reshape(depth, bp, keep, N_HEADS_A, HEAD_DIM)
    new_v_prompt = qkv3[:, sp - keep:, 2 * D_ATTN:].reshape(depth, bp, keep, N_HEADS_A, HEAD_DIM)
    new_conv_prompt = xbc.reshape(bp, sp, CONV_DIM)[:, sp - (CONV_WIDTH - 1):].reshape(
        depth, bp, CONV_WIDTH - 1, CONV_DIM)
    new_ssm_prompt = hst.reshape(depth, bp, N_HEADS_S, SSM_HEAD_DIM, D_STATE)

    bs, ts = x_sample.shape[:2]
    assert cache_attn_k.shape[2] == MAX_WINDOW and ts == SAMPLE_ROWS
    xs = x_sample.reshape(bs * ts, D_MODEL)
    qkv_s, z_s, xbc_s, dt_s = _inproj(xs, p["gmix"], p["w_main"], p["w_dt"], p["gq"], p["gk"], tm)
    npair = N_HEADS_A // 2
    kt = jnp.transpose(cache_attn_k[l], (0, 2, 3, 1)).reshape(bs, npair, LANES, MAX_WINDOW)
    vt = jnp.transpose(cache_attn_v[l], (0, 2, 3, 1)).reshape(bs, npair, LANES, MAX_WINDOW)
    bmain, bnew, mult = _sample_bias_tables(rel_bias)
    attn_s = _attn_sample(qkv_s.reshape(bs, ts, 3 * D_ATTN), kt, vt, bmain, mult, bnew, p["expand"])
    ssm_os, hsts = _ssd_sample(z_s.reshape(bs, ts, D_SSM), xbc_s.reshape(bs, ts, CONV_DIM),
                               dt_s.reshape(bs, ts, LANES), state_conv[l],
                               state_ssm[l].reshape(bs, D_SSM, D_STATE), ssd_params)
    y_sample = _outffn(xs, attn_s.reshape(bs * ts, D_ATTN), ssm_os.reshape(bs * ts, D_SSM),
                       p["w_out"], p["gffn"], p["w_up"], p["w_down"], tm)
    new_k_sample = qkv_s[:, D_ATTN:2 * D_ATTN].reshape(depth, bs, ts, N_HEADS_A, HEAD_DIM)
    new_v_sample = qkv_s[:, 2 * D_ATTN:].reshape(depth, bs, ts, N_HEADS_A, HEAD_DIM)
    new_conv_sample = xbc_s.reshape(bs, ts, CONV_DIM)[:, ts - (CONV_WIDTH - 1):].reshape(
        depth, bs, CONV_WIDTH - 1, CONV_DIM)
    new_ssm_sample = hsts.reshape(depth, bs, N_HEADS_S, SSM_HEAD_DIM, D_STATE)

    return (y_prompt.reshape(bp, sp, D_MODEL), y_sample.reshape(bs, ts, D_MODEL),
            new_k_prompt, new_v_prompt, new_conv_prompt, new_ssm_prompt,
            new_k_sample, new_v_sample, new_conv_sample, new_ssm_sample)
```

```python
import functools
import math

import jax
import jax.numpy as jnp
import numpy as np
from jax import lax
from jax.experimental import pallas as pl
from jax.experimental.pallas import tpu as pltpu

F32 = jnp.float32
BF16 = jnp.bfloat16

D_MODEL = 1024
HEAD_DIM = 64
N_HEADS_A = 16
D_ATTN = 1024
SPAN = 128
DILATIONS = (1, 4, 16)
MAX_WINDOW = 2048
N_BUCKETS = 32
MAX_EXACT = 16
SSM_HEAD_DIM = 64
N_HEADS_S = 16
D_SSM = 1024
SSM_GROUPS = 2
D_STATE = 128
CONV_WIDTH = 4
CONV_DIM = D_SSM + 2 * SSM_GROUPS * D_STATE
SSD_CHUNK = 128
D_FF = 4096
EPS = 1e-6
ATTN_SCALE = HEAD_DIM ** -0.5
LOG2E = 1.4426950408889634

LANES = 128
NEG = -1e30
VMEM_LIMIT = 56 * 1024 * 1024


def _resident(shape):
    nd = len(shape)
    return pl.BlockSpec(shape, lambda *_: (0,) * nd, pipeline_mode=pl.Buffered(1))


def _split3(x):
    hi = x.astype(BF16)
    r1 = x - hi.astype(F32)
    lo = r1.astype(BF16)
    lo2 = (r1 - lo.astype(F32)).astype(BF16)
    return hi, lo, lo2


def _dot(a, b):
    return jnp.dot(a, b, preferred_element_type=F32)


def _dot_nt(a, b):
    return lax.dot_general(a, b, (((1,), (1,)), ((), ())), preferred_element_type=F32)


IN_TN = 512


def _inproj_kernel(x_ref, gmix_ref, w_ref, wdt_ref, gq_ref, gk_ref,
                   qkv_ref, z_ref, xbc_ref, dt_ref):
    x = x_ref[...]
    ms = jnp.mean(x * x, axis=-1, keepdims=True)
    xn = ((x * lax.rsqrt(ms + EPS)) * gmix_ref[...]).astype(BF16)
    first_head = lax.broadcasted_iota(jnp.int32, (1, LANES), 1) < HEAD_DIM

    n_main = 3 * D_ATTN + D_SSM + CONV_DIM
    for j in range(n_main // IN_TN):
        c0 = j * IN_TN
        t = _dot(xn, w_ref[:, c0:c0 + IN_TN])
        if c0 < 2 * D_ATTN:
            g_ref = gq_ref if c0 < D_ATTN else gk_ref
            g0 = c0 % D_ATTN
            for c in range(IN_TN // LANES):
                tc = t[:, c * LANES:(c + 1) * LANES]
                s = tc * tc
                sa = jnp.sum(jnp.where(first_head, s, 0.0), axis=-1, keepdims=True)
                sb = jnp.sum(jnp.where(first_head, 0.0, s), axis=-1, keepdims=True)
                inv = lax.rsqrt(jnp.where(first_head, sa, sb) * (1.0 / HEAD_DIM) + EPS)
                qkv_ref[:, c0 + c * LANES:c0 + (c + 1) * LANES] = (
                    (tc * inv) * g_ref[:, g0 + c * LANES:g0 + (c + 1) * LANES])
        elif c0 < 3 * D_ATTN:
            qkv_ref[:, c0:c0 + IN_TN] = t
        elif c0 < 3 * D_ATTN + D_SSM:
            z_ref[:, c0 - 3 * D_ATTN:c0 - 3 * D_ATTN + IN_TN] = t
        else:
            o0 = c0 - 3 * D_ATTN - D_SSM
            xbc_ref[:, o0:o0 + IN_TN] = t
    dt_ref[...] = _dot(xn, wdt_ref[...])


def _inproj(x2d, gmix, w_main, w_dt, gq, gk, tm):
    m = x2d.shape[0]
    n_main = w_main.shape[1]
    row = lambda w: pl.BlockSpec((tm, w), lambda i: (i, 0))
    return pl.pallas_call(
        _inproj_kernel,
        grid=(m // tm,),
        in_specs=[row(D_MODEL), _resident((1, D_MODEL)), _resident((D_MODEL, n_main)),
                  _resident((D_MODEL, LANES)), _resident((1, D_ATTN)), _resident((1, D_ATTN))],
        out_specs=[row(3 * D_ATTN), row(D_SSM), row(CONV_DIM), row(LANES)],
        out_shape=[jax.ShapeDtypeStruct((m, 3 * D_ATTN), F32),
                   jax.ShapeDtypeStruct((m, D_SSM), F32),
                   jax.ShapeDtypeStruct((m, CONV_DIM), F32),
                   jax.ShapeDtypeStruct((m, LANES), F32)],
        compiler_params=pltpu.CompilerParams(
            dimension_semantics=("arbitrary",), vmem_limit_bytes=VMEM_LIMIT),
        name="inproj",
    )(x2d, gmix, w_main, w_dt, gq, gk)


FFN_TF = 1024


def _outffn_kernel(x_ref, attn_ref, ssm_ref, wout_ref, gffn_ref, wup_ref, wdown_ref, y_ref):
    a = attn_ref[...].astype(BF16)
    s = ssm_ref[...].astype(BF16)
    h = x_ref[...] + _dot(a, wout_ref[0:D_ATTN, :]) + _dot(s, wout_ref[D_ATTN:D_ATTN + D_SSM, :])
    ms = jnp.mean(h * h, axis=-1, keepdims=True)
    hn = ((h * lax.rsqrt(ms + EPS)) * gffn_ref[...]).astype(BF16)
    acc = None
    for f in range(D_FF // FFN_TF):
        u = _dot(hn, wup_ref[:, f * FFN_TF:(f + 1) * FFN_TF])
        u = jnp.maximum(u, 0.0)
        u = (u * u).astype(BF16)
        d = _dot(u, wdown_ref[f * FFN_TF:(f + 1) * FFN_TF, :])
        acc = d if acc is None else acc + d
    y_ref[...] = h + acc


def _outffn(x2d, attn, ssm_o, w_out, gffn, w_up, w_down, tm):
    m = x2d.shape[0]
    row = pl.BlockSpec((tm, D_MODEL), lambda i: (i, 0))
    return pl.pallas_call(
        _outffn_kernel,
        grid=(m // tm,),
        in_specs=[row, row, row, _resident((D_ATTN + D_SSM, D_MODEL)), _resident((1, D_MODEL)),
                  _resident((D_MODEL, D_FF)), _resident((D_FF, D_MODEL))],
        out_specs=row,
        out_shape=jax.ShapeDtypeStruct((m, D_MODEL), F32),
        compiler_params=pltpu.CompilerParams(
            dimension_semantics=("arbitrary",), vmem_limit_bytes=VMEM_LIMIT),
        name="outffn",
    )(x2d, attn, ssm_o, w_out, gffn, w_up, w_down)


def _sigmoid(x):
    return 1.0 / (1.0 + jnp.exp(-x))


def _softplus(x):
    return jnp.maximum(x, 0.0) + jnp.log1p(jnp.exp(-jnp.abs(x)))


def _expand_heads(v, e_ref):
    hi, lo, lo2 = _split3(v)
    e = e_ref[...]
    return _dot(hi, e) + _dot(lo, e) + _dot(lo2, e)


def _ssd_chunk(z_ref, xbc_ref, dt_ref, y_ref, xpad_ref, state_ref,
               convw_ref, convb_ref, dtb_ref, alog_ref, dskip_ref, gnorm_ref, e_ref, valid_rows):
    q = SSD_CHUNK
    pad = 8
    tail = CONV_WIDTH - 1

    xbc = xbc_ref[...]
    xpad_ref[pad:pad + q, :] = xbc
    conv = convb_ref[...] + convw_ref[tail:tail + 1, :] * xbc
    for i in range(tail):
        conv = conv + convw_ref[i:i + 1, :] * xpad_ref[pad - tail + i:pad - tail + i + q, :]
    xpad_ref[pad - tail:pad, :] = xbc[q - tail:q, :]
    xc = conv * _sigmoid(conv)
    xs = xc[:, 0:D_SSM]

    lane = lax.broadcasted_iota(jnp.int32, (1, LANES), 1)
    live = lane < N_HEADS_S
    if valid_rows < q:
        live = jnp.logical_and(live, lax.broadcasted_iota(jnp.int32, (q, 1), 0) < valid_rows)
    dt = jnp.where(live, _softplus(dt_ref[...] + dtb_ref[...]), 0.0)
    da = dt * (-jnp.exp(alog_ref[...]))

    ii = lax.broadcasted_iota(jnp.int32, (q, q), 0)
    jj = lax.broadcasted_iota(jnp.int32, (q, q), 1)
    causal = ii >= jj
    tri = jnp.where(causal, 1.0, 0.0).astype(BF16)
    hi, lo, lo2 = _split3(da)
    cum = _dot(tri, hi) + _dot(tri, lo) + _dot(tri, lo2)
    cum_t = cum.T

    cum_e = _expand_heads(cum, e_ref)
    last_e = cum_e[q - 1:q, :]
    xdt = xs * _expand_heads(dt, e_ref)
    xw_t = (xdt * jnp.exp(last_e - cum_e)).T.astype(BF16)
    state_decay = jnp.exp(jnp.broadcast_to(last_e, (q, D_SSM)).T)
    off_scale = jnp.exp(cum_e)

    first_head = lane < SSM_HEAD_DIM
    heads_per_group = N_HEADS_S // SSM_GROUPS
    gw = heads_per_group * SSM_HEAD_DIM
    state = state_ref[...]
    state_bf = state.astype(BF16)
    for g in range(SSM_GROUPS):
        bg = xc[:, D_SSM + g * D_STATE:D_SSM + (g + 1) * D_STATE].astype(BF16)
        cg = xc[:, D_SSM + (SSM_GROUPS + g) * D_STATE:D_SSM + (SSM_GROUPS + g + 1) * D_STATE].astype(BF16)
        cb = _dot_nt(cg, bg)
        y_off = _dot_nt(cg, state_bf[g * gw:(g + 1) * gw, :]) * off_scale[:, g * gw:(g + 1) * gw]
        y_ref[:, g * gw:(g + 1) * gw] = y_off
        for hp in range(heads_per_group // 2):
            pair = g * (heads_per_group // 2) + hp
            xp = xdt[:, pair * LANES:(pair + 1) * LANES]
            yp = None
            for s in range(2):
                h = 2 * pair + s
                seg = cum[:, h:h + 1] - cum_t[h:h + 1, :]
                dec = jnp.exp(jnp.where(causal, seg, NEG))
                m = (cb * dec).astype(BF16)
                xm = jnp.where(first_head if s == 0 else jnp.logical_not(first_head), xp, 0.0).astype(BF16)
                d = _dot(m, xm)
                yp = d if yp is None else yp + d
            y_ref[:, pair * LANES:(pair + 1) * LANES] += yp
        state_ref[g * gw:(g + 1) * gw, :] = (
            state_decay[g * gw:(g + 1) * gw, :] * state[g * gw:(g + 1) * gw, :]
            + _dot(xw_t[g * gw:(g + 1) * gw, :], bg))

    z = z_ref[...]
    y = (y_ref[...] + xs * dskip_ref[...]) * (z * _sigmoid(z))
    for g in range(SSM_GROUPS):
        yg = y[:, g * gw:(g + 1) * gw]
        ms = jnp.mean(yg * yg, axis=-1, keepdims=True)
        y_ref[:, g * gw:(g + 1) * gw] = (yg * lax.rsqrt(ms + EPS)) * gnorm_ref[:, g * gw:(g + 1) * gw]


def _ssd_prompt_kernel(z_ref, xbc_ref, dt_ref, conv0_ref, h0_ref, convw_ref, convb_ref, dtb_ref,
                       alog_ref, dskip_ref, gnorm_ref, e_ref, y_ref, hout_ref, xpad_ref, state_ref):
    c = pl.program_id(1)

    @pl.when(c == 0)
    def _():
        state_ref[...] = h0_ref[0]
        xpad_ref[8 - (CONV_WIDTH - 1):8, :] = conv0_ref[0]

    _ssd_chunk(z_ref, xbc_ref, dt_ref, y_ref, xpad_ref, state_ref, convw_ref, convb_ref, dtb_ref,
               alog_ref, dskip_ref, gnorm_ref, e_ref, SSD_CHUNK)

    @pl.when(c == pl.num_programs(1) - 1)
    def _():
        hout_ref[0] = state_ref[...]


def _ssd_sample_kernel(z_ref, xbc_ref, dt_ref, conv0_ref, h0_ref, convw_ref, convb_ref, dtb_ref,
                       alog_ref, dskip_ref, gnorm_ref, e_ref, y_ref, hout_ref, xs_ref, z8_ref, dt8_ref,
                       *, rows):
    tail = CONV_WIDTH - 1
    r8 = 8
    row = lax.broadcasted_iota(jnp.int32, (r8, 1), 0)
    lane = lax.broadcasted_iota(jnp.int32, (1, LANES), 1)

    xs_ref[r8:2 * r8, :] = jnp.zeros((r8, CONV_DIM), F32)
    xs_ref[r8:r8 + rows, :] = xbc_ref[0]
    xs_ref[r8 - tail:r8, :] = conv0_ref[0]
    z8_ref[...] = jnp.zeros((r8, D_SSM), F32)
    z8_ref[0:rows, :] = z_ref[0]
    dt8_ref[...] = jnp.zeros((r8, LANES), F32)
    dt8_ref[0:rows, :] = dt_ref[0]

    conv = convb_ref[...] + convw_ref[tail:tail + 1, :] * xs_ref[r8:2 * r8, :]
    for i in range(tail):
        conv = conv + convw_ref[i:i + 1, :] * xs_ref[r8 - tail + i:2 * r8 - tail + i, :]
    xc = conv * _sigmoid(conv)
    xs = xc[:, 0:D_SSM]

    live = jnp.logical_and(lane < N_HEADS_S, row < rows)
    dt = jnp.where(live, _softplus(dt8_ref[...] + dtb_ref[...]), 0.0)
    da = dt * (-jnp.exp(alog_ref[...]))
    cum = da
    for k in range(1, rows):
        cum = cum + pltpu.roll(da, k, 0)
    last = rows - 1

    heads_per_group = N_HEADS_S // SSM_GROUPS
    gw = heads_per_group * SSM_HEAD_DIM
    bg = [xc[:, D_SSM + g * D_STATE:D_SSM + (g + 1) * D_STATE] for g in range(SSM_GROUPS)]
    cg = [xc[:, D_SSM + (SSM_GROUPS + g) * D_STATE:D_SSM + (SSM_GROUPS + g + 1) * D_STATE]
          for g in range(SSM_GROUPS)]
    group0 = lane < heads_per_group
    w = []
    for j in range(rows):
        cb = [jnp.sum(cg[g] * bg[g][j:j + 1, :], axis=-1, keepdims=True) for g in range(SSM_GROUPS)]
        dec = jnp.exp(jnp.where(row >= j, cum - cum[j:j + 1, :], NEG))
        w.append(jnp.where(group0, cb[0], cb[1]) * dec)

    stack = jnp.concatenate([cum, dt] + w, axis=0)
    hi, lo, lo2 = _split3(stack)
    ex = _dot(jnp.concatenate([hi, lo, lo2], axis=0), e_ref[...])
    ns = stack.shape[0]
    ex = ex[0:ns] + ex[ns:2 * ns] + ex[2 * ns:3 * ns]
    cum_e = ex[0:r8]
    xdt = xs * ex[r8:2 * r8]
    y = None
    for j in range(rows):
        term = ex[(2 + j) * r8:(3 + j) * r8] * xdt[j:j + 1, :]
        y = term if y is None else y + term
    last_e = cum_e[last:last + 1, :]

    state = h0_ref[0]
    state_bf = state.astype(BF16)
    zeros8 = jnp.zeros((r8, D_STATE), F32)
    y_off = [_dot_nt(jnp.concatenate([cg[g], zeros8], axis=0).astype(BF16),
                     state_bf[g * gw:(g + 1) * gw, :])[0:r8, :] for g in range(SSM_GROUPS)]
    y = y + jnp.concatenate(y_off, axis=1) * jnp.exp(cum_e)

    z = z8_ref[...]
    y = (y + xs * dskip_ref[...]) * (z * _sigmoid(z))
    outs = []
    for g in range(SSM_GROUPS):
        yg = y[:, g * gw:(g + 1) * gw]
        ms = jnp.mean(yg * yg, axis=-1, keepdims=True)
        outs.append((yg * lax.rsqrt(ms + EPS)) * gnorm_ref[:, g * gw:(g + 1) * gw])
    y_ref[0] = jnp.concatenate(outs, axis=1)[0:rows, :]

    pad = jnp.zeros((SSD_CHUNK - r8, D_SSM), F32)
    xw_t = jnp.concatenate([xdt * jnp.exp(last_e - cum_e), pad], axis=0).T.astype(BF16)
    state_decay = jnp.exp(jnp.broadcast_to(last_e, (SSD_CHUNK, D_SSM)).T)
    bpad = jnp.zeros((SSD_CHUNK - r8, D_STATE), F32)
    for g in range(SSM_GROUPS):
        b128 = jnp.concatenate([bg[g], bpad], axis=0).astype(BF16)
        hout_ref[0, g * gw:(g + 1) * gw, :] = (
            state_decay[g * gw:(g + 1) * gw, :] * state[g * gw:(g + 1) * gw, :]
            + _dot(xw_t[g * gw:(g + 1) * gw, :], b128))


def _ssd_param_specs():
    return [_resident((CONV_WIDTH, CONV_DIM)), _resident((1, CONV_DIM)), _resident((1, LANES)),
            _resident((1, LANES)), _resident((1, D_SSM)), _resident((1, D_SSM)),
            _resident((LANES, D_SSM))]


def _ssd_prompt(z, xbc, dt_raw, conv0, h0, ssd_params, batch):
    m = z.shape[0]
    nc = m // batch // SSD_CHUNK
    row = lambda w: pl.BlockSpec((SSD_CHUNK, w), lambda b, c: (b * nc + c, 0))
    per_b = lambda r, w: pl.BlockSpec((1, r, w), lambda b, c: (b, 0, 0))
    return pl.pallas_call(
        _ssd_prompt_kernel,
        grid=(batch, nc),
        in_specs=[row(D_SSM), row(CONV_DIM), row(LANES), per_b(CONV_WIDTH - 1, CONV_DIM),
                  per_b(D_SSM, D_STATE)] + _ssd_param_specs(),
        out_specs=[row(D_SSM), per_b(D_SSM, D_STATE)],
        out_shape=[jax.ShapeDtypeStruct((m, D_SSM), F32),
                   jax.ShapeDtypeStruct((batch, D_SSM, D_STATE), F32)],
        scratch_shapes=[pltpu.VMEM((8 + SSD_CHUNK, CONV_DIM), F32), pltpu.VMEM((D_SSM, D_STATE), F32)],
        compiler_params=pltpu.CompilerParams(
            dimension_semantics=("arbitrary", "arbitrary"), vmem_limit_bytes=VMEM_LIMIT),
        name="ssd_prompt",
    )(z, xbc, dt_raw, conv0, h0, *ssd_params)


def _ssd_sample(z, xbc, dt_raw, conv0, h0, ssd_params):
    batch, rows = z.shape[0], z.shape[1]
    per_b = lambda r, w: pl.BlockSpec((1, r, w), lambda b: (b, 0, 0))
    q = SSD_CHUNK
    return pl.pallas_call(
        functools.partial(_ssd_sample_kernel, rows=rows),
        grid=(batch,),
        in_specs=[per_b(rows, D_SSM), per_b(rows, CONV_DIM), per_b(rows, LANES),
                  per_b(CONV_WIDTH - 1, CONV_DIM), per_b(D_SSM, D_STATE)] + _ssd_param_specs(),
        out_specs=[per_b(rows, D_SSM), per_b(D_SSM, D_STATE)],
        out_shape=[jax.ShapeDtypeStruct((batch, rows, D_SSM), F32),
                   jax.ShapeDtypeStruct((batch, D_SSM, D_STATE), F32)],
        scratch_shapes=[pltpu.VMEM((16, CONV_DIM), F32), pltpu.VMEM((8, D_SSM), F32),
                        pltpu.VMEM((8, LANES), F32)],
        compiler_params=pltpu.CompilerParams(
            dimension_semantics=("arbitrary",), vmem_limit_bytes=VMEM_LIMIT),
        name="ssd_sample",
    )(z, xbc, dt_raw, conv0, h0, *ssd_params)


def _t5_bucket_np(dist):
    dist = np.asarray(dist, np.int64)
    df = np.maximum(dist, 1).astype(np.float32)
    large = MAX_EXACT + (np.log(df / np.float32(MAX_EXACT)) / np.float32(math.log(MAX_WINDOW / MAX_EXACT))
                         * np.float32(N_BUCKETS - MAX_EXACT)).astype(np.int32)
    return np.where(dist < MAX_EXACT, dist, np.minimum(large, N_BUCKETS - 1)).astype(np.int32)


def _band_bucket_tables():
    ki = np.arange(SPAN)[None, :]
    out = np.empty((len(DILATIONS), 2, SPAN, SPAN), np.int32)
    for bi, d in enumerate(DILATIONS):
        nrun = DILATIONS[-1] // d
        row = np.arange(SPAN)
        qi = (nrun * (row % (SPAN // nrun)) + row // (SPAN // nrun))[:, None]
        rel_prev = qi + SPAN - ki
        rel_cur = qi - ki
        out[bi, 0] = np.where(rel_prev <= SPAN, _t5_bucket_np(np.clip(rel_prev, 0, SPAN) * d), -1)
        out[bi, 1] = np.where(rel_cur >= 0, _t5_bucket_np(np.clip(rel_cur, 0, SPAN) * d), -1)
    return out


def _bias_table_kernel(tab_ref, rb_ref, out_ref, *, buckets):
    pair = pl.program_id(0)
    for bi in range(len(DILATIONS)):
        for half in range(2):
            tab = tab_ref[bi, half]
            acc = [jnp.full((SPAN, SPAN), NEG, F32) for _ in range(2)]
            for bkt in buckets[bi][half]:
                hit = tab == bkt
                for s in range(2):
                    acc[s] = jnp.where(hit, rb_ref[bkt, 2 * pair + s] * LOG2E, acc[s])
            for s in range(2):
                rows = slice(s * SPAN, (s + 1) * SPAN)
                if half == 0:
                    out_ref[0, bi, 0, rows, 0:SPAN] = jnp.full((SPAN, SPAN), NEG, F32)
                    out_ref[0, bi, 1, rows, 0:SPAN] = acc[s]
                else:
                    out_ref[0, bi, 0, rows, SPAN:2 * SPAN] = acc[s]
                    out_ref[0, bi, 1, rows, SPAN:2 * SPAN] = acc[s]


def _bias_tables(rel_bias):
    tabs = _band_bucket_tables()
    buckets = [[sorted(int(b) for b in np.unique(tabs[bi, half]) if b >= 0) for half in range(2)]
               for bi in range(len(DILATIONS))]
    nd = len(DILATIONS)
    return pl.pallas_call(
        functools.partial(_bias_table_kernel, buckets=buckets),
        grid=(N_HEADS_A // 2,),
        in_specs=[_resident((nd, 2, SPAN, SPAN)),
                  pl.BlockSpec(memory_space=pltpu.SMEM)],
        out_specs=pl.BlockSpec((1, nd, 2, 2 * SPAN, 2 * SPAN), lambda p: (p, 0, 0, 0, 0)),
        out_shape=jax.ShapeDtypeStruct((N_HEADS_A // 2, nd, 2, 2 * SPAN, 2 * SPAN), F32),
        compiler_params=pltpu.CompilerParams(dimension_semantics=("arbitrary",)),
        name="bias_tables",
    )(jnp.asarray(tabs), rel_bias)


ATT_T = SPAN * DILATIONS[-1]
ATT_UNROLL = 8
ATT_SUB = 1


def _query_runs(d, r, n):
    run = SPAN * d // DILATIONS[-1]
    return [pl.ds(pl.multiple_of((r + d * c) * SPAN + run * n, 8), run) for c in range(DILATIONS[-1] // d)]


def _load_runs(ref, runs, *lead):
    parts = [ref[(*lead, rows, slice(None))] for rows in runs]
    return parts[0] if len(parts) == 1 else jnp.concatenate(parts, axis=0)


def _store_runs(ref, runs, value, *lead):
    run = value.shape[0] // len(runs)
    for c, rows in enumerate(runs):
        ref[(*lead, rows, slice(None))] = value[c * run:(c + 1) * run, :]


def _attn_kernel(q_ref, k_ref, v_ref, bias_ref, o_ref, qp_ref, tmp_ref, kph_ref, vph_ref, acc_ref, m_ref,
                 l_ref):
    t = pl.program_id(2)
    lane = lax.broadcasted_iota(jnp.int32, (1, LANES), 1)
    first_head = lane < HEAD_DIM
    head_mask = (first_head, jnp.logical_not(first_head))
    nblk = ATT_T // SPAN
    dmax = DILATIONS[-1]

    @pl.when(t == 0)
    def _():
        for bi, d in enumerate(DILATIONS):
            ph = ATT_T // d + SPAN
            for r in range(d):
                kph_ref[bi, r * ph:r * ph + SPAN, :] = jnp.zeros((SPAN, LANES), BF16)
                vph_ref[bi, r * ph:r * ph + SPAN, :] = jnp.zeros((SPAN, LANES), BF16)

    d4, d16 = DILATIONS[1], DILATIONS[2]
    assert DILATIONS == (1, d4, d4 * d4)
    l4, l16 = ATT_T // d4, ATT_T // d16
    for r4 in range(d4):
        tmp_ref[r4 * l4:(r4 + 1) * l4, :] = q_ref[pl.ds(r4, l4, stride=d4), :] * (ATTN_SCALE * LOG2E)
    for r16 in range(d16):
        qp_ref[r16 * l16:(r16 + 1) * l16, :] = tmp_ref[pl.ds((r16 % d4) * l4 + r16 // d4, l16, stride=d4), :]
    for src_ref, dst_ref in ((k_ref, kph_ref), (v_ref, vph_ref)):
        dst_ref[0, SPAN:SPAN + ATT_T, :] = src_ref[...].astype(BF16)
        ph4, ph16 = l4 + SPAN, l16 + SPAN
        for r4 in range(d4):
            x4 = src_ref[pl.ds(r4, l4, stride=d4), :]
            tmp_ref[r4 * l4:(r4 + 1) * l4, :] = x4
            dst_ref[1, r4 * ph4 + SPAN:(r4 + 1) * ph4, :] = x4.astype(BF16)
        for r16 in range(d16):
            x16 = tmp_ref[pl.ds((r16 % d4) * l4 + r16 // d4, l16, stride=d4), :]
            dst_ref[2, r16 * ph16 + SPAN:(r16 + 1) * ph16, :] = x16.astype(BF16)

    order = tuple(reversed(range(len(DILATIONS))))
    for bi in order:
        d = DILATIONS[bi]
        ln = ATT_T // d
        ph = ln + SPAN
        nb = ln // SPAN
        first_branch = bi == order[0]
        last_branch = bi == order[-1]

        def group(it, carry, bi=bi, d=d, ph=ph, nb=nb, first_branch=first_branch, last_branch=last_branch):
            def fetch(j):
                sub = []
                for g in range(j * ATT_SUB, (j + 1) * ATT_SUB):
                    blk = it * ATT_UNROLL + g
                    r = blk // nb
                    n = blk % nb
                    runs = _query_runs(d, r, n)
                    krow = pl.multiple_of(r * ph + n * SPAN, SPAN)
                    variant = jnp.logical_or(t > 0, n > 0).astype(jnp.int32)
                    qf = _load_runs(qp_ref, runs)
                    qs = jnp.concatenate([jnp.where(head_mask[s], qf, 0.0).astype(BF16) for s in range(2)],
                                         axis=0)
                    logits = _dot_nt(qs, kph_ref[bi, pl.ds(krow, 2 * SPAN), :]) + bias_ref[0, bi, variant]
                    old = None
                    if not first_branch:
                        old = ([_load_runs(m_ref, runs, s) for s in range(2)],
                               [_load_runs(l_ref, runs, s) for s in range(2)], _load_runs(acc_ref, runs))
                    sub.append((runs, krow, old, logits))
                return sub

            nsub = ATT_UNROLL // ATT_SUB
            nxt = fetch(0)
            for j in range(nsub):
                cur = nxt
                if j + 1 < nsub:
                    nxt = fetch(j + 1)
                for runs, m_new, l_new, pv in softmax_pv(cur):
                    if last_branch:
                        _store_runs(acc_ref, runs, pv / jnp.where(first_head, l_new[0], l_new[1]))
                    else:
                        for s in range(2):
                            _store_runs(m_ref, runs, m_new[s], s)
                            _store_runs(l_ref, runs, l_new[s], s)
                        _store_runs(acc_ref, runs, pv)
            return carry

        def softmax_pv(sub, bi=bi, first_branch=first_branch):
            soft = []
            for runs, krow, old, sc2 in sub:
                m_new, l_new, alpha, ps = [], [], [], []
                for s in range(2):
                    sc = sc2[s * SPAN:(s + 1) * SPAN, :]
                    halves = (sc[:, 0:SPAN], sc[:, SPAN:2 * SPAN])
                    mb = jnp.max(jnp.maximum(halves[0], halves[1]), axis=-1, keepdims=True)
                    if first_branch:
                        m = jnp.broadcast_to(mb, (SPAN, LANES))
                    else:
                        m = jnp.maximum(old[0][s], mb)
                        alpha.append(jnp.exp2(old[0][s] - m))
                    p = [jnp.exp2(h - m) for h in halves]
                    rs = jnp.sum(p[0] + p[1], axis=-1, keepdims=True)
                    ps.append(jnp.concatenate([p[0].astype(BF16), p[1].astype(BF16)], axis=1))
                    m_new.append(m)
                    l_new.append(jnp.broadcast_to(rs, (SPAN, LANES)) if first_branch
                                 else alpha[s] * old[1][s] + rs)
                soft.append((m_new, l_new, alpha, ps))
            out = []
            for (runs, krow, old, _), (m_new, l_new, alpha, ps) in zip(sub, soft):
                pv2 = _dot(jnp.concatenate(ps, axis=0), vph_ref[bi, pl.ds(krow, 2 * SPAN), :])
                pv = jnp.where(first_head, pv2[0:SPAN, :], pv2[SPAN:2 * SPAN, :])
                if not first_branch:
                    pv = jnp.where(first_head, alpha[0], alpha[1]) * old[2] + pv
                out.append((runs, m_new, l_new, pv))
            return out

        lax.fori_loop(0, nblk // ATT_UNROLL, group, 0)

    for r in range(dmax):
        o_ref[pl.ds(r, SPAN, stride=dmax), :] = acc_ref[r * SPAN:(r + 1) * SPAN, :]

    for bi, d in enumerate(DILATIONS):
        ln = ATT_T // d
        ph = ln + SPAN
        for r in range(d):
            kph_ref[bi, r * ph:r * ph + SPAN, :] = kph_ref[bi, (r + 1) * ph - SPAN:(r + 1) * ph, :]
            vph_ref[bi, r * ph:r * ph + SPAN, :] = vph_ref[bi, (r + 1) * ph - SPAN:(r + 1) * ph, :]


def _attn_prompt(qkv, bias, batch):
    m = qkv.shape[0]
    nt = m // batch // ATT_T
    npair = N_HEADS_A // 2
    nd = len(DILATIONS)
    blk = lambda off: pl.BlockSpec((ATT_T, LANES), lambda b, p, t: (b * nt + t, off + p))
    halo_rows = ATT_T + DILATIONS[-1] * SPAN
    return pl.pallas_call(
        _attn_kernel,
        grid=(batch, npair, nt),
        in_specs=[blk(0), blk(npair), blk(2 * npair),
                  pl.BlockSpec((1, nd, 2, 2 * SPAN, 2 * SPAN), lambda b, p, t: (p, 0, 0, 0, 0))],
        out_specs=blk(0),
        out_shape=jax.ShapeDtypeStruct((m, D_ATTN), F32),
        scratch_shapes=[pltpu.VMEM((ATT_T, LANES), F32),
                        pltpu.VMEM((ATT_T, LANES), F32),
                        pltpu.VMEM((nd, halo_rows, LANES), BF16),
                        pltpu.VMEM((nd, halo_rows, LANES), BF16),
                        pltpu.VMEM((ATT_T, LANES), F32),
                        pltpu.VMEM((2, ATT_T, LANES), F32),
                        pltpu.VMEM((2, ATT_T, LANES), F32)],
        compiler_params=pltpu.CompilerParams(
            dimension_semantics=("arbitrary", "arbitrary", "arbitrary"), vmem_limit_bytes=VMEM_LIMIT),
        name="attn_prompt",
    )(qkv, qkv, qkv, bias)


SAMPLE_ROWS = 4


def _sample_tables():
    pos = np.arange(MAX_WINDOW)[None, :]
    t = (np.arange(8) % SAMPLE_ROWS)[:, None]
    dist = MAX_WINDOW + t - pos
    mult = np.zeros(dist.shape, np.float32)
    for d in DILATIONS:
        mult += ((dist % d == 0) & (dist <= SPAN * d)).astype(np.float32)
    bucket = np.where(mult > 0, _t5_bucket_np(dist), -1).astype(np.int32)
    return bucket, np.concatenate([mult, mult], axis=0)


def _sample_bias_kernel(tab_ref, rb_ref, rbv_ref, bmain_ref, bnew_ref, *, buckets):
    pair = pl.program_id(0)
    tab = tab_ref[...]
    acc = [jnp.full(tab.shape, NEG, F32) for _ in range(2)]
    for bkt in buckets:
        hit = tab == bkt
        for s in range(2):
            acc[s] = jnp.where(hit, rb_ref[bkt, 2 * pair + s], acc[s])
    for s in range(2):
        bmain_ref[0, s * 8:(s + 1) * 8, :] = acc[s]
    t = lax.broadcasted_iota(jnp.int32, (8, 1), 0) % SAMPLE_ROWS
    for tp in range(SAMPLE_ROWS):
        b = jnp.full((8, LANES), NEG, F32)
        for dist in range(SAMPLE_ROWS):
            b = jnp.where(t - tp == dist, rbv_ref[dist:dist + 1, :], b)
        bnew_ref[tp] = b


def _sample_bias_tables(rel_bias):
    bucket, mult = _sample_tables()
    buckets = sorted(int(b) for b in np.unique(bucket) if b >= 0)
    npair = N_HEADS_A // 2
    rb_lanes = jnp.pad(rel_bias, ((0, 0), (0, LANES - N_HEADS_A)))
    bmain, bnew = pl.pallas_call(
        functools.partial(_sample_bias_kernel, buckets=buckets),
        grid=(npair,),
        in_specs=[_resident((8, MAX_WINDOW)), pl.BlockSpec(memory_space=pltpu.SMEM),
                  _resident((N_BUCKETS, LANES))],
        out_specs=[pl.BlockSpec((1, 16, MAX_WINDOW), lambda p: (p, 0, 0)),
                   pl.BlockSpec((SAMPLE_ROWS, 8, LANES), lambda p: (0, 0, 0))],
        out_shape=[jax.ShapeDtypeStruct((npair, 16, MAX_WINDOW), F32),
                   jax.ShapeDtypeStruct((SAMPLE_ROWS, 8, LANES), F32)],
        compiler_params=pltpu.CompilerParams(dimension_semantics=("arbitrary",)),
        name="sample_bias_tables",
    )(jnp.asarray(bucket), rel_bias, rb_lanes)
    return bmain, bnew, jnp.asarray(mult)


def _attn_sample_kernel(qkv_ref, kt_ref, vt_ref, bmain_ref, mult_ref, bnew_ref, e_ref, o_ref,
                        q8_ref, om_ref):
    rows = SAMPLE_ROWS
    npair = N_HEADS_A // 2
    blk = qkv_ref[0]
    q8_ref[0:rows, :] = blk
    q8_ref[rows:2 * rows, :] = blk
    q8 = q8_ref[:, 0:D_ATTN] * ATTN_SCALE
    kn = q8_ref[:, D_ATTN:2 * D_ATTN]
    vn = q8_ref[:, 2 * D_ATTN:3 * D_ATTN]
    lane = lax.broadcasted_iota(jnp.int32, (1, LANES), 1)
    first_head = lane < HEAD_DIM
    e = e_ref[...]

    s_new = []
    m_tot = jnp.full((8, LANES), NEG, F32)
    for tp in range(rows):
        hi, lo, lo2 = _split3(q8 * kn[tp:tp + 1, :])
        s = _dot_nt(hi, e) + _dot_nt(lo, e) + _dot_nt(lo2, e) + bnew_ref[tp]
        s_new.append(s)
        m_tot = jnp.maximum(m_tot, s)

    mult = mult_ref[...]
    l_tot = jnp.zeros((8, LANES), F32)
    for p in range(npair):
        qp = q8[:, p * LANES:(p + 1) * LANES]
        qbd = jnp.concatenate([jnp.where(first_head, qp, 0.0), jnp.where(first_head, 0.0, qp)],
                              axis=0).astype(BF16)
        s16 = _dot(qbd, kt_ref[0, p].astype(BF16)) + bmain_ref[p]
        m_new = jnp.concatenate([m_tot[:, 2 * p:2 * p + 1], m_tot[:, 2 * p + 1:2 * p + 2]], axis=0)
        m = jnp.maximum(jnp.max(s16, axis=-1, keepdims=True), m_new)
        pr = mult * jnp.exp(s16 - m)
        ls = jnp.sum(pr, axis=-1, keepdims=True)
        o16 = _dot_nt(pr.astype(BF16), vt_ref[0, p].astype(BF16))
        om_ref[:, p * LANES:(p + 1) * LANES] = jnp.where(first_head, o16[0:8], o16[8:16])
        m_tot = jnp.where(lane == 2 * p, m[0:8], jnp.where(lane == 2 * p + 1, m[8:16], m_tot))
        l_tot = jnp.where(lane == 2 * p, ls[0:8], jnp.where(lane == 2 * p + 1, ls[8:16], l_tot))

    t = lax.broadcasted_iota(jnp.int32, (8, 1), 0) % rows
    o = om_ref[...]
    for tp in range(rows):
        c = jnp.where(t == tp, float(len(DILATIONS)), jnp.where(t > tp, 1.0, 0.0))
        pn = c * jnp.exp(s_new[tp] - m_tot)
        l_tot = l_tot + pn
        o = o + _expand_heads(pn, e_ref) * vn[tp:tp + 1, :]
    o = o / _expand_heads(l_tot, e_ref)
    o_ref[0] = o[0:rows, :]


def _attn_sample(qkv3, kt, vt, bmain, mult, bnew, expand):
    batch, rows = qkv3.shape[0], qkv3.shape[1]
    npair = N_HEADS_A // 2
    cache = pl.BlockSpec((1, npair, LANES, MAX_WINDOW), lambda b: (b, 0, 0, 0))
    return pl.pallas_call(
        _attn_sample_kernel,
        grid=(batch,),
        in_specs=[pl.BlockSpec((1, rows, 3 * D_ATTN), lambda b: (b, 0, 0)), cache, cache,
                  _resident((npair, 16, MAX_WINDOW)), _resident((16, MAX_WINDOW)),
                  _resident((rows, 8, LANES)), _resident((LANES, D_ATTN))],
        out_specs=pl.BlockSpec((1, rows, D_ATTN), lambda b: (b, 0, 0)),
        out_shape=jax.ShapeDtypeStruct((batch, rows, D_ATTN), F32),
        scratch_shapes=[pltpu.VMEM((8, 3 * D_ATTN), F32), pltpu.VMEM((8, D_ATTN), F32)],
        compiler_params=pltpu.CompilerParams(
            dimension_semantics=("arbitrary",), vmem_limit_bytes=VMEM_LIMIT),
        name="attn_sample",
    )(qkv3, kt, vt, bmain, mult, bnew, expand)


def _head_expand_matrix():
    e = np.zeros((LANES, D_SSM), np.float32)
    for h in range(N_HEADS_S):
        e[h, h * SSM_HEAD_DIM:(h + 1) * SSM_HEAD_DIM] = 1.0
    return jnp.asarray(e, BF16)


def _prep_params(norm_mix, w_in, q_norm, k_norm, conv_w, conv_b, dt_bias, a_log, d_skip, ssm_norm,
                 w_out, norm_ffn, w_up, w_down):
    n_main = 3 * D_ATTN + D_SSM + CONV_DIM
    pad_heads = lambda v: jnp.pad(v.reshape(1, N_HEADS_S), ((0, 0), (0, LANES - N_HEADS_S)))
    return dict(
        gmix=norm_mix.reshape(1, D_MODEL),
        w_main=w_in[:, :n_main].astype(BF16),
        w_dt=jnp.pad(w_in[:, n_main:], ((0, 0), (0, LANES - N_HEADS_S))).astype(BF16),
        gq=jnp.tile(q_norm, N_HEADS_A).reshape(1, D_ATTN),
        gk=jnp.tile(k_norm, N_HEADS_A).reshape(1, D_ATTN),
        conv_w=conv_w,
        conv_b=conv_b.reshape(1, CONV_DIM),
        dt_bias=pad_heads(dt_bias),
        a_log=pad_heads(a_log),
        d_skip_e=jnp.repeat(d_skip, SSM_HEAD_DIM).reshape(1, D_SSM),
        gnorm=ssm_norm.reshape(1, D_SSM),
        expand=_head_expand_matrix(),
        w_out=w_out.astype(BF16),
        gffn=norm_ffn.reshape(1, D_MODEL),
        w_up=w_up.astype(BF16),
        w_down=w_down.astype(BF16),
    )


def _ssd_param_list(p):
    return (p["conv_w"], p["conv_b"], p["dt_bias"], p["a_log"], p["d_skip_e"], p["gnorm"], p["expand"])


def kernel(x_prompt, x_sample, cache_attn_k, cache_attn_v, state_conv, state_ssm, norm_mix, w_in, q_norm,
           k_norm, rel_bias, conv_w, conv_b, dt_bias, a_log, d_skip, ssm_norm, w_out, norm_ffn, w_up, w_down):
    depth = w_in.shape[0]
    assert depth == 1, "single-layer decoder"
    l = 0
    p = _prep_params(norm_mix[l], w_in[l], q_norm[l], k_norm[l], conv_w[l], conv_b[l], dt_bias[l], a_log[l],
                     d_skip[l], ssm_norm[l], w_out[l], norm_ffn[l], w_up[l], w_down[l])
    ssd_params = _ssd_param_list(p)
    tm = 512

    bp, sp = x_prompt.shape[:2]
    keep = min(MAX_WINDOW, sp)
    xp = x_prompt.reshape(bp * sp, D_MODEL)
    qkv, z, xbc, dt_raw = _inproj(xp, p["gmix"], p["w_main"], p["w_dt"], p["gq"], p["gk"], tm)
    attn = _attn_prompt(qkv, _bias_tables(rel_bias), bp)
    conv0 = jnp.zeros((bp, CONV_WIDTH - 1, CONV_DIM), F32)
    h0 = jnp.zeros((bp, D_SSM, D_STATE), F32)
    ssm_o, hst = _ssd_prompt(z, xbc, dt_raw, conv0, h0, ssd_params, bp)
    y_prompt = _outffn(xp, attn, ssm_o, p["w_out"], p["gffn"], p["w_up"], p["w_down"], tm)
    qkv3 = qkv.reshape(bp, sp, 3 * D_ATTN)
    new_k_prompt = qkv3[:, sp - keep:, D_ATTN:2 * D_ATTN].reshape(depth, bp, keep, N_HEADS_A, HEAD_DIM)
    new_v_prompt = qkv3[:, sp - keep:, 2 * D_ATTN:].reshape(depth, bp, keep, N_HEADS_A, HEAD_DIM)
    new_conv_prompt = xbc.reshape(bp, sp, CONV_DIM)[:, sp - (CONV_WIDTH - 1):].reshape(
        depth, bp, CONV_WIDTH - 1, CONV_DIM)
    new_ssm_prompt = hst.reshape(depth, bp, N_HEADS_S, SSM_HEAD_DIM, D_STATE)

    bs, ts = x_sample.shape[:2]
    assert cache_attn_k.shape[2] == MAX_WINDOW and ts == SAMPLE_ROWS
    xs = x_sample.reshape(bs * ts, D_MODEL)
    qkv_s, z_s, xbc_s, dt_s = _inproj(xs, p["gmix"], p["w_main"], p["w_dt"], p["gq"], p["gk"], tm)
    npair = N_HEADS_A // 2
    kt = jnp.transpose(cache_attn_k[l], (0, 2, 3, 1)).reshape(bs, npair, LANES, MAX_WINDOW)
    vt = jnp.transpose(cache_attn_v[l], (0, 2, 3, 1)).reshape(bs, npair, LANES, MAX_WINDOW)
    bmain, bnew, mult = _sample_bias_tables(rel_bias)
    attn_s = _attn_sample(qkv_s.reshape(bs, ts, 3 * D_ATTN), kt, vt, bmain, mult, bnew, p["expand"])
    ssm_os, hsts = _ssd_sample(z_s.reshape(bs, ts, D_SSM), xbc_s.reshape(bs, ts, CONV_DIM),
                               dt_s.reshape(bs, ts, LANES), state_conv[l],
                               state_ssm[l].reshape(bs, D_SSM, D_STATE), ssd_params)
    y_sample = _outffn(xs, attn_s.reshape(bs * ts, D_ATTN), ssm_os.reshape(bs * ts, D_SSM),
                       p["w_out"], p["gffn"], p["w_up"], p["w_down"], tm)
    new_k_sample = qkv_s[:, D_ATTN:2 * D_ATTN].reshape(depth, bs, ts, N_HEADS_A, HEAD_DIM)
    new_v_sample = qkv_s[:, 2 * D_ATTN:].reshape(depth, bs, ts, N_HEADS_A, HEAD_DIM)
    new_conv_sample = xbc_s.reshape(bs, ts, CONV_DIM)[:, ts - (CONV_WIDTH - 1):].reshape(
        depth, bs, CONV_WIDTH - 1, CONV_DIM)
    new_ssm_sample = hsts.reshape(depth, bs, N_HEADS_S, SSM_HEAD_DIM, D_STATE)

    return (y_prompt.reshape(bp, sp, D_MODEL), y_sample.reshape(bs, ts, D_MODEL),
            new_k_prompt, new_v_prompt, new_conv_prompt, new_ssm_prompt,
            new_k_sample, new_v_sample, new_conv_sample, new_ssm_sample)
```

```python
import functools
import math

import jax
import jax.numpy as jnp
import numpy as np
from jax import lax
from jax.experimental import pallas as pl
from jax.experimental.pallas import tpu as pltpu

F32 = jnp.float32
BF16 = jnp.bfloat16

D_MODEL = 1024
HEAD_DIM = 64
N_HEADS_A = 16
D_ATTN = 1024
SPAN = 128
DILATIONS = (1, 4, 16)
MAX_WINDOW = 2048
N_BUCKETS = 32
MAX_EXACT = 16
SSM_HEAD_DIM = 64
N_HEADS_S = 16
D_SSM = 1024
SSM_GROUPS = 2
D_STATE = 128
CONV_WIDTH = 4
CONV_DIM = D_SSM + 2 * SSM_GROUPS * D_STATE
SSD_CHUNK = 128
D_FF = 4096
EPS = 1e-6
ATTN_SCALE = HEAD_DIM ** -0.5
LOG2E = 1.4426950408889634

LANES = 128
NEG = -1e30
VMEM_LIMIT = 56 * 1024 * 1024


def _resident(shape):
    nd = len(shape)
    return pl.BlockSpec(shape, lambda *_: (0,) * nd, pipeline_mode=pl.Buffered(1))


def _split3(x):
    hi = x.astype(BF16)
    r1 = x - hi.astype(F32)
    lo = r1.astype(BF16)
    lo2 = (r1 - lo.astype(F32)).astype(BF16)
    return hi, lo, lo2


def _dot(a, b):
    return jnp.dot(a, b, preferred_element_type=F32)


def _dot_nt(a, b):
    return lax.dot_general(a, b, (((1,), (1,)), ((), ())), preferred_element_type=F32)


IN_TN = 512


def _inproj_kernel(x_ref, gmix_ref, w_ref, wdt_ref, gq_ref, gk_ref,
                   qkv_ref, z_ref, xbc_ref, dt_ref):
    x = x_ref[...]
    ms = jnp.mean(x * x, axis=-1, keepdims=True)
    xn = ((x * lax.rsqrt(ms + EPS)) * gmix_ref[...]).astype(BF16)
    first_head = lax.broadcasted_iota(jnp.int32, (1, LANES), 1) < HEAD_DIM

    n_main = 3 * D_ATTN + D_SSM + CONV_DIM
    for j in range(n_main // IN_TN):
        c0 = j * IN_TN
        t = _dot(xn, w_ref[:, c0:c0 + IN_TN])
        if c0 < 2 * D_ATTN:
            g_ref = gq_ref if c0 < D_ATTN else gk_ref
            g0 = c0 % D_ATTN
            for c in range(IN_TN // LANES):
                tc = t[:, c * LANES:(c + 1) * LANES]
                s = tc * tc
                sa = jnp.sum(jnp.where(first_head, s, 0.0), axis=-1, keepdims=True)
                sb = jnp.sum(jnp.where(first_head, 0.0, s), axis=-1, keepdims=True)
                inv = lax.rsqrt(jnp.where(first_head, sa, sb) * (1.0 / HEAD_DIM) + EPS)
                qkv_ref[:, c0 + c * LANES:c0 + (c + 1) * LANES] = (
                    (tc * inv) * g_ref[:, g0 + c * LANES:g0 + (c + 1) * LANES])
        elif c0 < 3 * D_ATTN:
            qkv_ref[:, c0:c0 + IN_TN] = t
        elif c0 < 3 * D_ATTN + D_SSM:
            z_ref[:, c0 - 3 * D_ATTN:c0 - 3 * D_ATTN + IN_TN] = t
        else:
            o0 = c0 - 3 * D_ATTN - D_SSM
            xbc_ref[:, o0:o0 + IN_TN] = t
    dt_ref[...] = _dot(xn, wdt_ref[...])


def _inproj(x2d, gmix, w_main, w_dt, gq, gk, tm):
    m = x2d.shape[0]
    n_main = w_main.shape[1]
    row = lambda w: pl.BlockSpec((tm, w), lambda i: (i, 0))
    return pl.pallas_call(
        _inproj_kernel,
        grid=(m // tm,),
        in_specs=[row(D_MODEL), _resident((1, D_MODEL)), _resident((D_MODEL, n_main)),
                  _resident((D_MODEL, LANES)), _resident((1, D_ATTN)), _resident((1, D_ATTN))],
        out_specs=[row(3 * D_ATTN), row(D_SSM), row(CONV_DIM), row(LANES)],
        out_shape=[jax.ShapeDtypeStruct((m, 3 * D_ATTN), F32),
                   jax.ShapeDtypeStruct((m, D_SSM), F32),
                   jax.ShapeDtypeStruct((m, CONV_DIM), F32),
                   jax.ShapeDtypeStruct((m, LANES), F32)],
        compiler_params=pltpu.CompilerParams(
            dimension_semantics=("arbitrary",), vmem_limit_bytes=VMEM_LIMIT),
        name="inproj",
    )(x2d, gmix, w_main, w_dt, gq, gk)


FFN_TF = 1024


def _outffn_kernel(x_ref, attn_ref, ssm_ref, wout_ref, gffn_ref, wup_ref, wdown_ref, y_ref):
    a = attn_ref[...].astype(BF16)
    s = ssm_ref[...].astype(BF16)
    h = x_ref[...] + _dot(a, wout_ref[0:D_ATTN, :]) + _dot(s, wout_ref[D_ATTN:D_ATTN + D_SSM, :])
    ms = jnp.mean(h * h, axis=-1, keepdims=True)
    hn = ((h * lax.rsqrt(ms + EPS)) * gffn_ref[...]).astype(BF16)
    acc = None
    for f in range(D_FF // FFN_TF):
        u = _dot(hn, wup_ref[:, f * FFN_TF:(f + 1) * FFN_TF])
        u = jnp.maximum(u, 0.0)
        u = (u * u).astype(BF16)
        d = _dot(u, wdown_ref[f * FFN_TF:(f + 1) * FFN_TF, :])
        acc = d if acc is None else acc + d
    y_ref[...] = h + acc


def _outffn(x2d, attn, ssm_o, w_out, gffn, w_up, w_down, tm):
    m = x2d.shape[0]
    row = pl.BlockSpec((tm, D_MODEL), lambda i: (i, 0))
    return pl.pallas_call(
        _outffn_kernel,
        grid=(m // tm,),
        in_specs=[row, row, row, _resident((D_ATTN + D_SSM, D_MODEL)), _resident((1, D_MODEL)),
                  _resident((D_MODEL, D_FF)), _resident((D_FF, D_MODEL))],
        out_specs=row,
        out_shape=jax.ShapeDtypeStruct((m, D_MODEL), F32),
        compiler_params=pltpu.CompilerParams(
            dimension_semantics=("arbitrary",), vmem_limit_bytes=VMEM_LIMIT),
        name="outffn",
    )(x2d, attn, ssm_o, w_out, gffn, w_up, w_down)


def _sigmoid(x):
    return 0.5 * jnp.tanh(0.5 * x) + 0.5


def _softplus(x):
    return jnp.maximum(x, 0.0) + jnp.log1p(jnp.exp(-jnp.abs(x)))


def _expand_heads(v, e_ref):
    hi, lo, lo2 = _split3(v)
    e = e_ref[...]
    return _dot(hi, e) + _dot(lo, e) + _dot(lo2, e)


def _ssd_chunk(z_ref, xbc_ref, dt_ref, y_ref, xpad_ref, state_ref,
               convw_ref, convb_ref, dtb_ref, alog_ref, dskip_ref, gnorm_ref, e_ref, valid_rows):
    q = SSD_CHUNK
    pad = 8
    tail = CONV_WIDTH - 1

    xbc = xbc_ref[...]
    xpad_ref[pad:pad + q, :] = xbc
    conv = convb_ref[...] + convw_ref[tail:tail + 1, :] * xbc
    for i in range(tail):
        conv = conv + convw_ref[i:i + 1, :] * xpad_ref[pad - tail + i:pad - tail + i + q, :]
    xpad_ref[pad - tail:pad, :] = xbc[q - tail:q, :]
    xc = conv * _sigmoid(conv)
    xs = xc[:, 0:D_SSM]

    lane = lax.broadcasted_iota(jnp.int32, (1, LANES), 1)
    live = lane < N_HEADS_S
    if valid_rows < q:
        live = jnp.logical_and(live, lax.broadcasted_iota(jnp.int32, (q, 1), 0) < valid_rows)
    dt = jnp.where(live, _softplus(dt_ref[...] + dtb_ref[...]), 0.0)
    da = dt * (-jnp.exp(alog_ref[...]))

    ii = lax.broadcasted_iota(jnp.int32, (q, q), 0)
    jj = lax.broadcasted_iota(jnp.int32, (q, q), 1)
    causal = ii >= jj
    tri = jnp.where(causal, 1.0, 0.0).astype(BF16)
    hi, lo, lo2 = _split3(da)
    cum = _dot(tri, hi) + _dot(tri, lo) + _dot(tri, lo2)
    cum_t = cum.T

    cum_e = _expand_heads(cum, e_ref)
    last_e = cum_e[q - 1:q, :]
    xdt = xs * _expand_heads(dt, e_ref)
    xw_t = (xdt * jnp.exp(last_e - cum_e)).T.astype(BF16)
    state_decay = jnp.exp(jnp.broadcast_to(last_e, (q, D_SSM)).T)
    off_scale = jnp.exp(cum_e)

    first_head = lane < SSM_HEAD_DIM
    heads_per_group = N_HEADS_S // SSM_GROUPS
    gw = heads_per_group * SSM_HEAD_DIM
    state = state_ref[...]
    state_bf = state.astype(BF16)
    for g in range(SSM_GROUPS):
        bg = xc[:, D_SSM + g * D_STATE:D_SSM + (g + 1) * D_STATE].astype(BF16)
        cg = xc[:, D_SSM + (SSM_GROUPS + g) * D_STATE:D_SSM + (SSM_GROUPS + g + 1) * D_STATE].astype(BF16)
        cb = _dot_nt(cg, bg)
        y_off = _dot_nt(cg, state_bf[g * gw:(g + 1) * gw, :]) * off_scale[:, g * gw:(g + 1) * gw]
        y_ref[:, g * gw:(g + 1) * gw] = y_off
        for hp in range(heads_per_group // 2):
            pair = g * (heads_per_group // 2) + hp
            xp = xdt[:, pair * LANES:(pair + 1) * LANES]
            yp = None
            for s in range(2):
                h = 2 * pair + s
                seg = cum[:, h:h + 1] - cum_t[h:h + 1, :]
                dec = jnp.exp(jnp.where(causal, seg, NEG))
                m = (cb * dec).astype(BF16)
                xm = jnp.where(first_head if s == 0 else jnp.logical_not(first_head), xp, 0.0).astype(BF16)
                d = _dot(m, xm)
                yp = d if yp is None else yp + d
            y_ref[:, pair * LANES:(pair + 1) * LANES] += yp
        state_ref[g * gw:(g + 1) * gw, :] = (
            state_decay[g * gw:(g + 1) * gw, :] * state[g * gw:(g + 1) * gw, :]
            + _dot(xw_t[g * gw:(g + 1) * gw, :], bg))

    z = z_ref[...]
    y = (y_ref[...] + xs * dskip_ref[...]) * (z * _sigmoid(z))
    for g in range(SSM_GROUPS):
        yg = y[:, g * gw:(g + 1) * gw]
        ms = jnp.mean(yg * yg, axis=-1, keepdims=True)
        y_ref[:, g * gw:(g + 1) * gw] = (yg * lax.rsqrt(ms + EPS)) * gnorm_ref[:, g * gw:(g + 1) * gw]


def _ssd_prompt_kernel(z_ref, xbc_ref, dt_ref, conv0_ref, h0_ref, convw_ref, convb_ref, dtb_ref,
                       alog_ref, dskip_ref, gnorm_ref, e_ref, y_ref, hout_ref, xpad_ref, state_ref):
    c = pl.program_id(0)
    nseq = z_ref.shape[0]

    @pl.when(c == 0)
    def _():
        state_ref[...] = h0_ref[...]
        xpad_ref[:, 8 - (CONV_WIDTH - 1):8, :] = conv0_ref[...]

    for b in range(nseq):
        _ssd_chunk(z_ref.at[b], xbc_ref.at[b], dt_ref.at[b], y_ref.at[b], xpad_ref.at[b], state_ref.at[b],
                   convw_ref, convb_ref, dtb_ref, alog_ref, dskip_ref, gnorm_ref, e_ref, SSD_CHUNK)

    @pl.when(c == pl.num_programs(0) - 1)
    def _():
        hout_ref[...] = state_ref[...]


def _ssd_sample_kernel(z_ref, xbc_ref, dt_ref, conv0_ref, h0_ref, convw_ref, convb_ref, dtb_ref,
                       alog_ref, dskip_ref, gnorm_ref, e_ref, y_ref, hout_ref, xs_ref, z8_ref, dt8_ref,
                       *, rows):
    for b in range(z_ref.shape[0]):
        _ssd_sample_one(z_ref.at[b], xbc_ref.at[b], dt_ref.at[b], conv0_ref.at[b], h0_ref.at[b], convw_ref,
                        convb_ref, dtb_ref, alog_ref, dskip_ref, gnorm_ref, e_ref, y_ref.at[b], hout_ref.at[b],
                        xs_ref.at[b], z8_ref.at[b], dt8_ref.at[b], rows)


def _ssd_sample_one(z_ref, xbc_ref, dt_ref, conv0_ref, h0_ref, convw_ref, convb_ref, dtb_ref,
                    alog_ref, dskip_ref, gnorm_ref, e_ref, y_ref, hout_ref, xs_ref, z8_ref, dt8_ref, rows):
    tail = CONV_WIDTH - 1
    r8 = 8
    row = lax.broadcasted_iota(jnp.int32, (r8, 1), 0)
    lane = lax.broadcasted_iota(jnp.int32, (1, LANES), 1)

    xs_ref[r8:2 * r8, :] = jnp.zeros((r8, CONV_DIM), F32)
    xs_ref[r8:r8 + rows, :] = xbc_ref[...]
    xs_ref[r8 - tail:r8, :] = conv0_ref[...]
    z8_ref[...] = jnp.zeros((r8, D_SSM), F32)
    z8_ref[0:rows, :] = z_ref[...]
    dt8_ref[...] = jnp.zeros((r8, LANES), F32)
    dt8_ref[0:rows, :] = dt_ref[...]

    conv = convb_ref[...] + convw_ref[tail:tail + 1, :] * xs_ref[r8:2 * r8, :]
    for i in range(tail):
        conv = conv + convw_ref[i:i + 1, :] * xs_ref[r8 - tail + i:2 * r8 - tail + i, :]
    xc = conv * _sigmoid(conv)
    xs = xc[:, 0:D_SSM]

    live = jnp.logical_and(lane < N_HEADS_S, row < rows)
    dt = jnp.where(live, _softplus(dt8_ref[...] + dtb_ref[...]), 0.0)
    da = dt * (-jnp.exp(alog_ref[...]))
    cum = da
    for k in range(1, rows):
        cum = cum + pltpu.roll(da, k, 0)
    last = rows - 1

    heads_per_group = N_HEADS_S // SSM_GROUPS
    gw = heads_per_group * SSM_HEAD_DIM
    bg = [xc[:, D_SSM + g * D_STATE:D_SSM + (g + 1) * D_STATE] for g in range(SSM_GROUPS)]
    cg = [xc[:, D_SSM + (SSM_GROUPS + g) * D_STATE:D_SSM + (SSM_GROUPS + g + 1) * D_STATE]
          for g in range(SSM_GROUPS)]
    group0 = lane < heads_per_group
    w = []
    for j in range(rows):
        cb = [jnp.sum(cg[g] * bg[g][j:j + 1, :], axis=-1, keepdims=True) for g in range(SSM_GROUPS)]
        dec = jnp.exp(jnp.where(row >= j, cum - cum[j:j + 1, :], NEG))
        w.append(jnp.where(group0, cb[0], cb[1]) * dec)

    stack = jnp.concatenate([cum, dt] + w, axis=0)
    hi, lo, lo2 = _split3(stack)
    ex = _dot(jnp.concatenate([hi, lo, lo2], axis=0), e_ref[...])
    ns = stack.shape[0]
    ex = ex[0:ns] + ex[ns:2 * ns] + ex[2 * ns:3 * ns]
    cum_e = ex[0:r8]
    xdt = xs * ex[r8:2 * r8]
    y = None
    for j in range(rows):
        term = ex[(2 + j) * r8:(3 + j) * r8] * xdt[j:j + 1, :]
        y = term if y is None else y + term
    last_e = cum_e[last:last + 1, :]

    state = h0_ref[...]
    state_bf = state.astype(BF16)
    zeros8 = jnp.zeros((r8, D_STATE), F32)
    y_off = [_dot_nt(jnp.concatenate([cg[g], zeros8], axis=0).astype(BF16),
                     state_bf[g * gw:(g + 1) * gw, :])[0:r8, :] for g in range(SSM_GROUPS)]
    y = y + jnp.concatenate(y_off, axis=1) * jnp.exp(cum_e)

    z = z8_ref[...]
    y = (y + xs * dskip_ref[...]) * (z * _sigmoid(z))
    outs = []
    for g in range(SSM_GROUPS):
        yg = y[:, g * gw:(g + 1) * gw]
        ms = jnp.mean(yg * yg, axis=-1, keepdims=True)
        outs.append((yg * lax.rsqrt(ms + EPS)) * gnorm_ref[:, g * gw:(g + 1) * gw])
    y_ref[...] = jnp.concatenate(outs, axis=1)[0:rows, :]

    pad = jnp.zeros((SSD_CHUNK - r8, D_SSM), F32)
    xw_t = jnp.concatenate([xdt * jnp.exp(last_e - cum_e), pad], axis=0).T.astype(BF16)
    state_decay = jnp.exp(jnp.broadcast_to(last_e, (SSD_CHUNK, D_SSM)).T)
    bpad = jnp.zeros((SSD_CHUNK - r8, D_STATE), F32)
    for g in range(SSM_GROUPS):
        b128 = jnp.concatenate([bg[g], bpad], axis=0).astype(BF16)
        hout_ref[g * gw:(g + 1) * gw, :] = (
            state_decay[g * gw:(g + 1) * gw, :] * state[g * gw:(g + 1) * gw, :]
            + _dot(xw_t[g * gw:(g + 1) * gw, :], b128))


def _ssd_param_specs():
    return [_resident((CONV_WIDTH, CONV_DIM)), _resident((1, CONV_DIM)), _resident((1, LANES)),
            _resident((1, LANES)), _resident((1, D_SSM)), _resident((1, D_SSM)),
            _resident((LANES, D_SSM))]


def _ssd_prompt(z, xbc, dt_raw, conv0, h0, ssd_params):
    batch, seq = z.shape[0], z.shape[1]
    row = lambda w: pl.BlockSpec((batch, SSD_CHUNK, w), lambda c: (0, c, 0))
    whole = lambda r, w: pl.BlockSpec((batch, r, w), lambda c: (0, 0, 0))
    return pl.pallas_call(
        _ssd_prompt_kernel,
        grid=(seq // SSD_CHUNK,),
        in_specs=[row(D_SSM), row(CONV_DIM), row(LANES), whole(CONV_WIDTH - 1, CONV_DIM),
                  whole(D_SSM, D_STATE)] + _ssd_param_specs(),
        out_specs=[row(D_SSM), whole(D_SSM, D_STATE)],
        out_shape=[jax.ShapeDtypeStruct((batch, seq, D_SSM), F32),
                   jax.ShapeDtypeStruct((batch, D_SSM, D_STATE), F32)],
        scratch_shapes=[pltpu.VMEM((batch, 8 + SSD_CHUNK, CONV_DIM), F32),
                        pltpu.VMEM((batch, D_SSM, D_STATE), F32)],
        compiler_params=pltpu.CompilerParams(
            dimension_semantics=("arbitrary",), vmem_limit_bytes=VMEM_LIMIT),
        name="ssd_prompt",
    )(z, xbc, dt_raw, conv0, h0, *ssd_params)


SSD_SAMPLE_SEQS = 2


def _ssd_sample(z, xbc, dt_raw, conv0, h0, ssd_params):
    batch, rows = z.shape[0], z.shape[1]
    nb = SSD_SAMPLE_SEQS
    per_b = lambda r, w: pl.BlockSpec((nb, r, w), lambda b: (b, 0, 0))
    return pl.pallas_call(
        functools.partial(_ssd_sample_kernel, rows=rows),
        grid=(batch // nb,),
        in_specs=[per_b(rows, D_SSM), per_b(rows, CONV_DIM), per_b(rows, LANES),
                  per_b(CONV_WIDTH - 1, CONV_DIM), per_b(D_SSM, D_STATE)] + _ssd_param_specs(),
        out_specs=[per_b(rows, D_SSM), per_b(D_SSM, D_STATE)],
        out_shape=[jax.ShapeDtypeStruct((batch, rows, D_SSM), F32),
                   jax.ShapeDtypeStruct((batch, D_SSM, D_STATE), F32)],
        scratch_shapes=[pltpu.VMEM((nb, 16, CONV_DIM), F32), pltpu.VMEM((nb, 8, D_SSM), F32),
                        pltpu.VMEM((nb, 8, LANES), F32)],
        compiler_params=pltpu.CompilerParams(
            dimension_semantics=("arbitrary",), vmem_limit_bytes=VMEM_LIMIT),
        name="ssd_sample",
    )(z, xbc, dt_raw, conv0, h0, *ssd_params)


def _t5_bucket_np(dist):
    dist = np.asarray(dist, np.int64)
    df = np.maximum(dist, 1).astype(np.float32)
    large = MAX_EXACT + (np.log(df / np.float32(MAX_EXACT)) / np.float32(math.log(MAX_WINDOW / MAX_EXACT))
                         * np.float32(N_BUCKETS - MAX_EXACT)).astype(np.int32)
    return np.where(dist < MAX_EXACT, dist, np.minimum(large, N_BUCKETS - 1)).astype(np.int32)


def _band_bucket_tables():
    ki = np.arange(SPAN)[None, :]
    out = np.empty((len(DILATIONS), 2, SPAN, SPAN), np.int32)
    for bi, d in enumerate(DILATIONS):
        nrun = DILATIONS[-1] // d
        row = np.arange(SPAN)
        qi = (nrun * (row % (SPAN // nrun)) + row // (SPAN // nrun))[:, None]
        rel_prev = qi + SPAN - ki
        rel_cur = qi - ki
        out[bi, 0] = np.where(rel_prev <= SPAN, _t5_bucket_np(np.clip(rel_prev, 0, SPAN) * d), -1)
        out[bi, 1] = np.where(rel_cur >= 0, _t5_bucket_np(np.clip(rel_cur, 0, SPAN) * d), -1)
    return out


def _bias_table_kernel(tab_ref, rb_ref, out_ref, *, buckets):
    pair = pl.program_id(0)
    for bi in range(len(DILATIONS)):
        for half in range(2):
            tab = tab_ref[bi, half]
            acc = [jnp.full((SPAN, SPAN), NEG, F32) for _ in range(2)]
            for bkt in buckets[bi][half]:
                hit = tab == bkt
                for s in range(2):
                    acc[s] = jnp.where(hit, rb_ref[bkt, 2 * pair + s] * LOG2E, acc[s])
            for s in range(2):
                rows = slice(s * SPAN, (s + 1) * SPAN)
                if half == 0:
                    out_ref[0, bi, 0, rows, 0:SPAN] = jnp.full((SPAN, SPAN), NEG, F32)
                    out_ref[0, bi, 1, rows, 0:SPAN] = acc[s]
                else:
                    out_ref[0, bi, 0, rows, SPAN:2 * SPAN] = acc[s]
                    out_ref[0, bi, 1, rows, SPAN:2 * SPAN] = acc[s]


def _bias_tables(rel_bias):
    tabs = _band_bucket_tables()
    buckets = [[sorted(int(b) for b in np.unique(tabs[bi, half]) if b >= 0) for half in range(2)]
               for bi in range(len(DILATIONS))]
    nd = len(DILATIONS)
    return pl.pallas_call(
        functools.partial(_bias_table_kernel, buckets=buckets),
        grid=(N_HEADS_A // 2,),
        in_specs=[_resident((nd, 2, SPAN, SPAN)),
                  pl.BlockSpec(memory_space=pltpu.SMEM)],
        out_specs=pl.BlockSpec((1, nd, 2, 2 * SPAN, 2 * SPAN), lambda p: (p, 0, 0, 0, 0)),
        out_shape=jax.ShapeDtypeStruct((N_HEADS_A // 2, nd, 2, 2 * SPAN, 2 * SPAN), F32),
        compiler_params=pltpu.CompilerParams(dimension_semantics=("arbitrary",)),
        name="bias_tables",
    )(jnp.asarray(tabs), rel_bias)


ATT_T = SPAN * DILATIONS[-1]
ATT_UNROLL = 8
ATT_SUB = 1


def _query_runs(d, r, n):
    run = SPAN * d // DILATIONS[-1]
    return [pl.ds(pl.multiple_of((r + d * c) * SPAN + run * n, 8), run) for c in range(DILATIONS[-1] // d)]


def _load_runs(ref, runs, *lead):
    parts = [ref[(*lead, rows, slice(None))] for rows in runs]
    return parts[0] if len(parts) == 1 else jnp.concatenate(parts, axis=0)


def _store_runs(ref, runs, value, *lead):
    run = value.shape[0] // len(runs)
    for c, rows in enumerate(runs):
        ref[(*lead, rows, slice(None))] = value[c * run:(c + 1) * run, :]


def _attn_kernel(q_ref, k_ref, v_ref, bias_ref, o_ref, qp_ref, tmp_ref, kph_ref, vph_ref, acc_ref, m_ref,
                 l_ref):
    t = pl.program_id(2)
    lane = lax.broadcasted_iota(jnp.int32, (1, LANES), 1)
    first_head = lane < HEAD_DIM
    head_mask = (first_head, jnp.logical_not(first_head))
    nblk = ATT_T // SPAN
    dmax = DILATIONS[-1]

    @pl.when(t == 0)
    def _():
        for bi, d in enumerate(DILATIONS):
            ph = ATT_T // d + SPAN
            for r in range(d):
                kph_ref[bi, r * ph:r * ph + SPAN, :] = jnp.zeros((SPAN, LANES), BF16)
                vph_ref[bi, r * ph:r * ph + SPAN, :] = jnp.zeros((SPAN, LANES), BF16)

    d4, d16 = DILATIONS[1], DILATIONS[2]
    assert DILATIONS == (1, d4, d4 * d4)
    l4, l16 = ATT_T // d4, ATT_T // d16
    for r4 in range(d4):
        tmp_ref[r4 * l4:(r4 + 1) * l4, :] = q_ref[pl.ds(r4, l4, stride=d4), :] * (ATTN_SCALE * LOG2E)
    for r16 in range(d16):
        qp_ref[r16 * l16:(r16 + 1) * l16, :] = tmp_ref[pl.ds((r16 % d4) * l4 + r16 // d4, l16, stride=d4), :]
    for src_ref, dst_ref in ((k_ref, kph_ref), (v_ref, vph_ref)):
        dst_ref[0, SPAN:SPAN + ATT_T, :] = src_ref[...].astype(BF16)
        ph4, ph16 = l4 + SPAN, l16 + SPAN
        for r4 in range(d4):
            x4 = src_ref[pl.ds(r4, l4, stride=d4), :]
            tmp_ref[r4 * l4:(r4 + 1) * l4, :] = x4
            dst_ref[1, r4 * ph4 + SPAN:(r4 + 1) * ph4, :] = x4.astype(BF16)
        for r16 in range(d16):
            x16 = tmp_ref[pl.ds((r16 % d4) * l4 + r16 // d4, l16, stride=d4), :]
            dst_ref[2, r16 * ph16 + SPAN:(r16 + 1) * ph16, :] = x16.astype(BF16)

    order = tuple(reversed(range(len(DILATIONS))))
    for bi in order:
        d = DILATIONS[bi]
        ln = ATT_T // d
        ph = ln + SPAN
        nb = ln // SPAN
        first_branch = bi == order[0]
        last_branch = bi == order[-1]

        def group(it, carry, bi=bi, d=d, ph=ph, nb=nb, first_branch=first_branch, last_branch=last_branch):
            def fetch(j):
                sub = []
                for g in range(j * ATT_SUB, (j + 1) * ATT_SUB):
                    blk = it * ATT_UNROLL + g
                    r = blk // nb
                    n = blk % nb
                    runs = _query_runs(d, r, n)
                    krow = pl.multiple_of(r * ph + n * SPAN, SPAN)
                    variant = jnp.logical_or(t > 0, n > 0).astype(jnp.int32)
                    qf = _load_runs(qp_ref, runs)
                    qs = jnp.concatenate([jnp.where(head_mask[s], qf, 0.0).astype(BF16) for s in range(2)],
                                         axis=0)
                    logits = _dot_nt(qs, kph_ref[bi, pl.ds(krow, 2 * SPAN), :]) + bias_ref[0, bi, variant]
                    old = None
                    if not first_branch:
                        old = ([_load_runs(m_ref, runs, s) for s in range(2)],
                               [_load_runs(l_ref, runs, s) for s in range(2)], _load_runs(acc_ref, runs))
                    sub.append((runs, krow, old, logits))
                return sub

            nsub = ATT_UNROLL // ATT_SUB
            nxt = fetch(0)
            for j in range(nsub):
                cur = nxt
                if j + 1 < nsub:
                    nxt = fetch(j + 1)
                for runs, m_new, l_new, pv in softmax_pv(cur):
                    if last_branch:
                        _store_runs(acc_ref, runs, pv / jnp.where(first_head, l_new[0], l_new[1]))
                    else:
                        for s in range(2):
                            _store_runs(m_ref, runs, m_new[s], s)
                            _store_runs(l_ref, runs, l_new[s], s)
                        _store_runs(acc_ref, runs, pv)
            return carry

        def softmax_pv(sub, bi=bi, first_branch=first_branch):
            soft = []
            for runs, krow, old, sc2 in sub:
                m_new, l_new, alpha, ps = [], [], [], []
                for s in range(2):
                    sc = sc2[s * SPAN:(s + 1) * SPAN, :]
                    halves = (sc[:, 0:SPAN], sc[:, SPAN:2 * SPAN])
                    mb = jnp.max(jnp.maximum(halves[0], halves[1]), axis=-1, keepdims=True)
                    if first_branch:
                        m = jnp.broadcast_to(mb, (SPAN, LANES))
                    else:
                        m = jnp.maximum(old[0][s], mb)
                        alpha.append(jnp.exp2(old[0][s] - m))
                    p = [jnp.exp2(h - m) for h in halves]
                    rs = jnp.sum(p[0] + p[1], axis=-1, keepdims=True)
                    ps.append(jnp.concatenate([p[0].astype(BF16), p[1].astype(BF16)], axis=1))
                    m_new.append(m)
                    l_new.append(jnp.broadcast_to(rs, (SPAN, LANES)) if first_branch
                                 else alpha[s] * old[1][s] + rs)
                soft.append((m_new, l_new, alpha, ps))
            out = []
            for (runs, krow, old, _), (m_new, l_new, alpha, ps) in zip(sub, soft):
                pv2 = _dot(jnp.concatenate(ps, axis=0), vph_ref[bi, pl.ds(krow, 2 * SPAN), :])
                pv = jnp.where(first_head, pv2[0:SPAN, :], pv2[SPAN:2 * SPAN, :])
                if not first_branch:
                    pv = jnp.where(first_head, alpha[0], alpha[1]) * old[2] + pv
                out.append((runs, m_new, l_new, pv))
            return out

        lax.fori_loop(0, nblk // ATT_UNROLL, group, 0)

    for r in range(dmax):
        o_ref[pl.ds(r, SPAN, stride=dmax), :] = acc_ref[r * SPAN:(r + 1) * SPAN, :]

    for bi, d in enumerate(DILATIONS):
        ln = ATT_T // d
        ph = ln + SPAN
        for r in range(d):
            kph_ref[bi, r * ph:r * ph + SPAN, :] = kph_ref[bi, (r + 1) * ph - SPAN:(r + 1) * ph, :]
            vph_ref[bi, r * ph:r * ph + SPAN, :] = vph_ref[bi, (r + 1) * ph - SPAN:(r + 1) * ph, :]


def _attn_prompt(qkv, bias, batch):
    m = qkv.shape[0]
    nt = m // batch // ATT_T
    npair = N_HEADS_A // 2
    nd = len(DILATIONS)
    blk = lambda off: pl.BlockSpec((ATT_T, LANES), lambda b, p, t: (b * nt + t, off + p))
    halo_rows = ATT_T + DILATIONS[-1] * SPAN
    return pl.pallas_call(
        _attn_kernel,
        grid=(batch, npair, nt),
        in_specs=[blk(0), blk(npair), blk(2 * npair),
                  pl.BlockSpec((1, nd, 2, 2 * SPAN, 2 * SPAN), lambda b, p, t: (p, 0, 0, 0, 0))],
        out_specs=blk(0),
        out_shape=jax.ShapeDtypeStruct((m, D_ATTN), F32),
        scratch_shapes=[pltpu.VMEM((ATT_T, LANES), F32),
                        pltpu.VMEM((ATT_T, LANES), F32),
                        pltpu.VMEM((nd, halo_rows, LANES), BF16),
                        pltpu.VMEM((nd, halo_rows, LANES), BF16),
                        pltpu.VMEM((ATT_T, LANES), F32),
                        pltpu.VMEM((2, ATT_T, LANES), F32),
                        pltpu.VMEM((2, ATT_T, LANES), F32)],
        compiler_params=pltpu.CompilerParams(
            dimension_semantics=("arbitrary", "arbitrary", "arbitrary"), vmem_limit_bytes=VMEM_LIMIT),
        name="attn_prompt",
    )(qkv, qkv, qkv, bias)


SAMPLE_ROWS = 4


def _sample_tables():
    pos = np.arange(MAX_WINDOW)[None, :]
    t = (np.arange(8) % SAMPLE_ROWS)[:, None]
    dist = MAX_WINDOW + t - pos
    mult = np.zeros(dist.shape, np.float32)
    for d in DILATIONS:
        mult += ((dist % d == 0) & (dist <= SPAN * d)).astype(np.float32)
    bucket = np.where(mult > 0, _t5_bucket_np(dist), -1).astype(np.int32)
    return bucket, np.concatenate([mult, mult], axis=0)


def _sample_bias_kernel(tab_ref, rb_ref, rbv_ref, bmain_ref, bnew_ref, *, buckets):
    pair = pl.program_id(0)
    tab = tab_ref[...]
    acc = [jnp.full(tab.shape, NEG, F32) for _ in range(2)]
    for bkt in buckets:
        hit = tab == bkt
        for s in range(2):
            acc[s] = jnp.where(hit, rb_ref[bkt, 2 * pair + s], acc[s])
    for s in range(2):
        bmain_ref[0, s * 8:(s + 1) * 8, :] = acc[s]
    t = lax.broadcasted_iota(jnp.int32, (8, 1), 0) % SAMPLE_ROWS
    for tp in range(SAMPLE_ROWS):
        b = jnp.full((8, LANES), NEG, F32)
        for dist in range(SAMPLE_ROWS):
            b = jnp.where(t - tp == dist, rbv_ref[dist:dist + 1, :], b)
        bnew_ref[tp] = b


def _sample_bias_tables(rel_bias):
    bucket, mult = _sample_tables()
    buckets = sorted(int(b) for b in np.unique(bucket) if b >= 0)
    npair = N_HEADS_A // 2
    rb_lanes = jnp.pad(rel_bias, ((0, 0), (0, LANES - N_HEADS_A)))
    bmain, bnew = pl.pallas_call(
        functools.partial(_sample_bias_kernel, buckets=buckets),
        grid=(npair,),
        in_specs=[_resident((8, MAX_WINDOW)), pl.BlockSpec(memory_space=pltpu.SMEM),
                  _resident((N_BUCKETS, LANES))],
        out_specs=[pl.BlockSpec((1, 16, MAX_WINDOW), lambda p: (p, 0, 0)),
                   pl.BlockSpec((SAMPLE_ROWS, 8, LANES), lambda p: (0, 0, 0))],
        out_shape=[jax.ShapeDtypeStruct((npair, 16, MAX_WINDOW), F32),
                   jax.ShapeDtypeStruct((SAMPLE_ROWS, 8, LANES), F32)],
        compiler_params=pltpu.CompilerParams(dimension_semantics=("arbitrary",)),
        name="sample_bias_tables",
    )(jnp.asarray(bucket), rel_bias, rb_lanes)
    return bmain, bnew, jnp.asarray(mult)


def _attn_sample_kernel(qkv_ref, kt_ref, vt_ref, bmain_ref, mult_ref, bnew_ref, e_ref, o_ref,
                        q8_ref, om_ref):
    rows = SAMPLE_ROWS
    npair = N_HEADS_A // 2
    blk = qkv_ref[0]
    q8_ref[0:rows, :] = blk
    q8_ref[rows:2 * rows, :] = blk
    q8 = q8_ref[:, 0:D_ATTN] * ATTN_SCALE
    kn = q8_ref[:, D_ATTN:2 * D_ATTN]
    vn = q8_ref[:, 2 * D_ATTN:3 * D_ATTN]
    lane = lax.broadcasted_iota(jnp.int32, (1, LANES), 1)
    first_head = lane < HEAD_DIM
    e = e_ref[...]

    s_new = []
    m_tot = jnp.full((8, LANES), NEG, F32)
    for tp in range(rows):
        hi, lo, lo2 = _split3(q8 * kn[tp:tp + 1, :])
        s = _dot_nt(hi, e) + _dot_nt(lo, e) + _dot_nt(lo2, e) + bnew_ref[tp]
        s_new.append(s)
        m_tot = jnp.maximum(m_tot, s)

    mult = mult_ref[...]
    l_tot = jnp.zeros((8, LANES), F32)

    def window_logits(p):
        qp = q8[:, p * LANES:(p + 1) * LANES]
        qbd = jnp.concatenate([jnp.where(first_head, qp, 0.0), jnp.where(first_head, 0.0, qp)],
                              axis=0).astype(BF16)
        return _dot(qbd, kt_ref[0, p].astype(BF16)) + bmain_ref[p]

    s_next = window_logits(0)
    for p in range(npair):
        s16 = s_next
        if p + 1 < npair:
            s_next = window_logits(p + 1)
        m_new =jnp.concatenate([m_tot[:, 2 * p:2 * p + 1], m_tot[:, 2 * p + 1:2 * p + 2]], axis=0)
        m = jnp.maximum(jnp.max(s16, axis=-1, keepdims=True), m_new)
        pr = mult * jnp.exp(s16 - m)
        ls = jnp.sum(pr, axis=-1, keepdims=True)
        o16 = _dot_nt(pr.astype(BF16), vt_ref[0, p].astype(BF16))
        om_ref[:, p * LANES:(p + 1) * LANES] = jnp.where(first_head, o16[0:8], o16[8:16])
        m_tot = jnp.where(lane == 2 * p, m[0:8], jnp.where(lane == 2 * p + 1, m[8:16], m_tot))
        l_tot = jnp.where(lane == 2 * p, ls[0:8], jnp.where(lane == 2 * p + 1, ls[8:16], l_tot))

    t = lax.broadcasted_iota(jnp.int32, (8, 1), 0) % rows
    o = om_ref[...]
    for tp in range(rows):
        c = jnp.where(t == tp, float(len(DILATIONS)), jnp.where(t > tp, 1.0, 0.0))
        pn = c * jnp.exp(s_new[tp] - m_tot)
        l_tot = l_tot + pn
        o = o + _expand_heads(pn, e_ref) * vn[tp:tp + 1, :]
    o = o / _expand_heads(l_tot, e_ref)
    o_ref[0] = o[0:rows, :]


def _attn_sample(qkv3, kt, vt, bmain, mult, bnew, expand):
    batch, rows = qkv3.shape[0], qkv3.shape[1]
    npair = N_HEADS_A // 2
    cache = pl.BlockSpec((1, npair, LANES, MAX_WINDOW), lambda b: (b, 0, 0, 0))
    return pl.pallas_call(
        _attn_sample_kernel,
        grid=(batch,),
        in_specs=[pl.BlockSpec((1, rows, 3 * D_ATTN), lambda b: (b, 0, 0)), cache, cache,
                  _resident((npair, 16, MAX_WINDOW)), _resident((16, MAX_WINDOW)),
                  _resident((rows, 8, LANES)), _resident((LANES, D_ATTN))],
        out_specs=pl.BlockSpec((1, rows, D_ATTN), lambda b: (b, 0, 0)),
        out_shape=jax.ShapeDtypeStruct((batch, rows, D_ATTN), F32),
        scratch_shapes=[pltpu.VMEM((8, 3 * D_ATTN), F32), pltpu.VMEM((8, D_ATTN), F32)],
        compiler_params=pltpu.CompilerParams(
            dimension_semantics=("arbitrary",), vmem_limit_bytes=VMEM_LIMIT),
        name="attn_sample",
    )(qkv3, kt, vt, bmain, mult, bnew, expand)


def _head_expand_matrix():
    e = np.zeros((LANES, D_SSM), np.float32)
    for h in range(N_HEADS_S):
        e[h, h * SSM_HEAD_DIM:(h + 1) * SSM_HEAD_DIM] = 1.0
    return jnp.asarray(e, BF16)


def _prep_params(norm_mix, w_in, q_norm, k_norm, conv_w, conv_b, dt_bias, a_log, d_skip, ssm_norm,
                 w_out, norm_ffn, w_up, w_down):
    n_main = 3 * D_ATTN + D_SSM + CONV_DIM
    pad_heads = lambda v: jnp.pad(v.reshape(1, N_HEADS_S), ((0, 0), (0, LANES - N_HEADS_S)))
    return dict(
        gmix=norm_mix.reshape(1, D_MODEL),
        w_main=w_in[:, :n_main].astype(BF16),
        w_dt=jnp.pad(w_in[:, n_main:], ((0, 0), (0, LANES - N_HEADS_S))).astype(BF16),
        gq=jnp.tile(q_norm, N_HEADS_A).reshape(1, D_ATTN),
        gk=jnp.tile(k_norm, N_HEADS_A).reshape(1, D_ATTN),
        conv_w=conv_w,
        conv_b=conv_b.reshape(1, CONV_DIM),
        dt_bias=pad_heads(dt_bias),
        a_log=pad_heads(a_log),
        d_skip_e=jnp.repeat(d_skip, SSM_HEAD_DIM).reshape(1, D_SSM),
        gnorm=ssm_norm.reshape(1, D_SSM),
        expand=_head_expand_matrix(),
        w_out=w_out.astype(BF16),
        gffn=norm_ffn.reshape(1, D_MODEL),
        w_up=w_up.astype(BF16),
        w_down=w_down.astype(BF16),
    )


def _ssd_param_list(p):
    return (p["conv_w"], p["conv_b"], p["dt_bias"], p["a_log"], p["d_skip_e"], p["gnorm"], p["expand"])


def kernel(x_prompt, x_sample, cache_attn_k, cache_attn_v, state_conv, state_ssm, norm_mix, w_in, q_norm,
           k_norm, rel_bias, conv_w, conv_b, dt_bias, a_log, d_skip, ssm_norm, w_out, norm_ffn, w_up, w_down):
    depth = w_in.shape[0]
    assert depth == 1, "single-layer decoder"
    l = 0
    p = _prep_params(norm_mix[l], w_in[l], q_norm[l], k_norm[l], conv_w[l], conv_b[l], dt_bias[l], a_log[l],
                     d_skip[l], ssm_norm[l], w_out[l], norm_ffn[l], w_up[l], w_down[l])
    ssd_params = _ssd_param_list(p)
    tm = 512

    bp, sp = x_prompt.shape[:2]
    keep = min(MAX_WINDOW, sp)
    xp = x_prompt.reshape(bp * sp, D_MODEL)
    qkv, z, xbc, dt_raw = _inproj(xp, p["gmix"], p["w_main"], p["w_dt"], p["gq"], p["gk"], tm)
    attn = _attn_prompt(qkv, _bias_tables(rel_bias), bp)
    conv0 = jnp.zeros((bp, CONV_WIDTH - 1, CONV_DIM), F32)
    h0 = jnp.zeros((bp, D_SSM, D_STATE), F32)
    ssm_o, hst = _ssd_prompt(z.reshape(bp, sp, D_SSM), xbc.reshape(bp, sp, CONV_DIM),
                             dt_raw.reshape(bp, sp, LANES), conv0, h0, ssd_params)
    y_prompt = _outffn(xp, attn, ssm_o.reshape(bp * sp, D_SSM), p["w_out"], p["gffn"], p["w_up"], p["w_down"], tm)
    qkv3 = qkv.reshape(bp, sp, 3 * D_ATTN)
    new_k_prompt = qkv3[:, sp - keep:, D_ATTN:2 * D_ATTN].reshape(depth, bp, keep, N_HEADS_A, HEAD_DIM)
    new_v_prompt = qkv3[:, sp - keep:, 2 * D_ATTN:].reshape(depth, bp, keep, N_HEADS_A, HEAD_DIM)
    new_conv_prompt = xbc.reshape(bp, sp, CONV_DIM)[:, sp - (CONV_WIDTH - 1):].reshape(
        depth, bp, CONV_WIDTH - 1, CONV_DIM)
    new_ssm_prompt = hst.reshape(depth, bp, N_HEADS_S, SSM_HEAD_DIM, D_STATE)

    bs, ts = x_sample.shape[:2]
    assert cache_attn_k.shape[2] == MAX_WINDOW and ts == SAMPLE_ROWS
    xs = x_sample.reshape(bs * ts, D_MODEL)
    qkv_s, z_s, xbc_s, dt_s = _inproj(xs, p["gmix"], p["w_main"], p["w_dt"], p["gq"], p["gk"], tm)
    npair = N_HEADS_A // 2
    kt = jnp.transpose(cache_attn_k[l], (0, 2, 3, 1)).reshape(bs, npair, LANES, MAX_WINDOW)
    vt = jnp.transpose(cache_attn_v[l], (0, 2, 3, 1)).reshape(bs, npair, LANES, MAX_WINDOW)
    bmain, bnew, mult = _sample_bias_tables(rel_bias)
    attn_s = _attn_sample(qkv_s.reshape(bs, ts, 3 * D_ATTN), kt, vt, bmain, mult, bnew, p["expand"])
    ssm_os, hsts = _ssd_sample(z_s.reshape(bs, ts, D_SSM), xbc_s.reshape(bs, ts, CONV_DIM),
                               dt_s.reshape(bs, ts, LANES), state_conv[l],
                               state_ssm[l].reshape(bs, D_SSM, D_STATE), ssd_params)
    y_sample = _outffn(xs, attn_s.reshape(bs * ts, D_ATTN), ssm_os.reshape(bs * ts, D_SSM),
                       p["w_out"], p["gffn"], p["w_up"], p["w_down"], tm)
    new_k_sample = qkv_s[:, D_ATTN:2 * D_ATTN].reshape(depth, bs, ts, N_HEADS_A, HEAD_DIM)
    new_v_sample = qkv_s[:, 2 * D_ATTN:].reshape(depth, bs, ts, N_HEADS_A, HEAD_DIM)
    new_conv_sample = xbc_s.reshape(bs, ts, CONV_DIM)[:, ts - (CONV_WIDTH - 1):].reshape(
        depth, bs, CONV_WIDTH - 1, CONV_DIM)
    new_ssm_sample = hsts.reshape(depth, bs, N_HEADS_S, SSM_HEAD_DIM, D_STATE)

    return (y_prompt.reshape(bp, sp, D_MODEL), y_sample.reshape(bs, ts, D_MODEL),
            new_k_prompt, new_v_prompt, new_conv_prompt, new_ssm_prompt,
            new_k_sample, new_v_sample, new_conv_sample, new_ssm_sample)
```

```python
import functools
import math

import jax
import jax.numpy as jnp
import numpy as np
from jax import lax
from jax.experimental import pallas as pl
from jax.experimental.pallas import tpu as pltpu

F32 = jnp.float32
BF16 = jnp.bfloat16

D_MODEL = 1024
HEAD_DIM = 64
N_HEADS_A = 16
D_ATTN = 1024
SPAN = 128
DILATIONS = (1, 4, 16)
MAX_WINDOW = 2048
N_BUCKETS = 32
MAX_EXACT = 16
SSM_HEAD_DIM = 64
N_HEADS_S = 16
D_SSM = 1024
SSM_GROUPS = 2
D_STATE = 128
CONV_WIDTH = 4
CONV_DIM = D_SSM + 2 * SSM_GROUPS * D_STATE
SSD_CHUNK = 128
D_FF = 4096
EPS = 1e-6
ATTN_SCALE = HEAD_DIM ** -0.5
LOG2E = 1.4426950408889634

LANES = 128
NEG = -1e30
VMEM_LIMIT = 56 * 1024 * 1024


def _resident(shape):
    nd = len(shape)
    return pl.BlockSpec(shape, lambda *_: (0,) * nd, pipeline_mode=pl.Buffered(1))


def _split3(x):
    hi = x.astype(BF16)
    r1 = x - hi.astype(F32)
    lo = r1.astype(BF16)
    lo2 = (r1 - lo.astype(F32)).astype(BF16)
    return hi, lo, lo2


def _dot(a, b):
    return jnp.dot(a, b, preferred_element_type=F32)


def _dot_nt(a, b):
    return lax.dot_general(a, b, (((1,), (1,)), ((), ())), preferred_element_type=F32)


IN_TN = 512


def _inproj_kernel(x_ref, gmix_ref, w_ref, wdt_ref, gq_ref, gk_ref,
                   qkv_ref, z_ref, xbc_ref, dt_ref):
    x = x_ref[...]
    ms = jnp.mean(x * x, axis=-1, keepdims=True)
    xn = ((x * lax.rsqrt(ms + EPS)) * gmix_ref[...]).astype(BF16)
    first_head = lax.broadcasted_iota(jnp.int32, (1, LANES), 1) < HEAD_DIM

    n_main = 3 * D_ATTN + D_SSM + CONV_DIM
    for j in range(n_main // IN_TN):
        c0 = j * IN_TN
        t = _dot(xn, w_ref[:, c0:c0 + IN_TN])
        if c0 < 2 * D_ATTN:
            g_ref = gq_ref if c0 < D_ATTN else gk_ref
            g0 = c0 % D_ATTN
            for c in range(IN_TN // LANES):
                tc = t[:, c * LANES:(c + 1) * LANES]
                s = tc * tc
                sa = jnp.sum(jnp.where(first_head, s, 0.0), axis=-1, keepdims=True)
                sb = jnp.sum(jnp.where(first_head, 0.0, s), axis=-1, keepdims=True)
                inv = lax.rsqrt(jnp.where(first_head, sa, sb) * (1.0 / HEAD_DIM) + EPS)
                qkv_ref[:, c0 + c * LANES:c0 + (c + 1) * LANES] = (
                    (tc * inv) * g_ref[:, g0 + c * LANES:g0 + (c + 1) * LANES])
        elif c0 < 3 * D_ATTN:
            qkv_ref[:, c0:c0 + IN_TN] = t
        elif c0 < 3 * D_ATTN + D_SSM:
            z_ref[:, c0 - 3 * D_ATTN:c0 - 3 * D_ATTN + IN_TN] = t
        else:
            o0 = c0 - 3 * D_ATTN - D_SSM
            xbc_ref[:, o0:o0 + IN_TN] = t
    dt_ref[...] = _dot(xn, wdt_ref[...])


def _inproj(x2d, gmix, w_main, w_dt, gq, gk, tm):
    m = x2d.shape[0]
    row = lambda w: pl.BlockSpec((tm, w), lambda i: (i, 0))
    return pl.pallas_call(
        _inproj_kernel,
        grid=(m // tm,),
        in_specs=[row(D_MODEL), _resident((1, D_MODEL)), _resident(w_main.shape),
                  _resident((D_MODEL, LANES)), _resident((1, D_ATTN)), _resident((1, D_ATTN))],
        out_specs=[row(3 * D_ATTN), row(D_SSM), row(CONV_DIM), row(LANES)],
        out_shape=[jax.ShapeDtypeStruct((m, 3 * D_ATTN), F32),
                   jax.ShapeDtypeStruct((m, D_SSM), F32),
                   jax.ShapeDtypeStruct((m, CONV_DIM), F32),
                   jax.ShapeDtypeStruct((m, LANES), F32)],
        compiler_params=pltpu.CompilerParams(
            dimension_semantics=("arbitrary",), vmem_limit_bytes=VMEM_LIMIT),
        name="inproj",
    )(x2d, gmix, w_main, w_dt, gq, gk)


FFN_TF = 1024


def _outffn_kernel(x_ref, attn_ref, ssm_ref, wout_ref, gffn_ref, wup_ref, wdown_ref, y_ref):
    a = attn_ref[...].astype(BF16)
    s = ssm_ref[...].astype(BF16)
    h = x_ref[...] + _dot(a, wout_ref[0:D_ATTN, :]) + _dot(s, wout_ref[D_ATTN:D_ATTN + D_SSM, :])
    ms = jnp.mean(h * h, axis=-1, keepdims=True)
    hn = ((h * lax.rsqrt(ms + EPS)) * gffn_ref[...]).astype(BF16)
    acc = None
    for f in range(D_FF // FFN_TF):
        u = _dot(hn, wup_ref[:, f * FFN_TF:(f + 1) * FFN_TF])
        u = jnp.maximum(u, 0.0)
        u = (u * u).astype(BF16)
        d = _dot(u, wdown_ref[f * FFN_TF:(f + 1) * FFN_TF, :])
        acc = d if acc is None else acc + d
    y_ref[...] = h + acc


def _outffn(x2d, attn, ssm_o, w_out, gffn, w_up, w_down, tm):
    m = x2d.shape[0]
    row = pl.BlockSpec((tm, D_MODEL), lambda i: (i, 0))
    return pl.pallas_call(
        _outffn_kernel,
        grid=(m // tm,),
        in_specs=[row, row, row, _resident((D_ATTN + D_SSM, D_MODEL)), _resident((1, D_MODEL)),
                  _resident((D_MODEL, D_FF)), _resident((D_FF, D_MODEL))],
        out_specs=row,
        out_shape=jax.ShapeDtypeStruct((m, D_MODEL), F32),
        compiler_params=pltpu.CompilerParams(
            dimension_semantics=("arbitrary",), vmem_limit_bytes=VMEM_LIMIT),
        name="outffn",
    )(x2d, attn, ssm_o, w_out, gffn, w_up, w_down)


def _sigmoid(x):
    return 0.5 * jnp.tanh(0.5 * x) + 0.5


def _softplus(x):
    return jnp.maximum(x, 0.0) + jnp.log1p(jnp.exp(-jnp.abs(x)))


def _expand_heads(v, e_ref):
    hi, lo, lo2 = _split3(v)
    e = e_ref[...]
    return _dot(hi, e) + _dot(lo, e) + _dot(lo2, e)


def _ssd_chunk(z_ref, xbc_ref, dt_ref, y_ref, xpad_ref, state_ref,
               convw_ref, convb_ref, dtb_ref, alog_ref, dskip_ref, gnorm_ref, e_ref, valid_rows):
    q = SSD_CHUNK
    pad = 8
    tail = CONV_WIDTH - 1

    xbc = xbc_ref[...]
    xpad_ref[pad:pad + q, :] = xbc
    conv = convb_ref[...] + convw_ref[tail:tail + 1, :] * xbc
    for i in range(tail):
        conv = conv + convw_ref[i:i + 1, :] * xpad_ref[pad - tail + i:pad - tail + i + q, :]
    xpad_ref[pad - tail:pad, :] = xbc[q - tail:q, :]
    xc = conv * _sigmoid(conv)
    xs = xc[:, 0:D_SSM]

    lane = lax.broadcasted_iota(jnp.int32, (1, LANES), 1)
    live = lane < N_HEADS_S
    if valid_rows < q:
        live = jnp.logical_and(live, lax.broadcasted_iota(jnp.int32, (q, 1), 0) < valid_rows)
    dt = jnp.where(live, _softplus(dt_ref[...] + dtb_ref[...]), 0.0)
    da = dt * (-jnp.exp(alog_ref[...]))

    ii = lax.broadcasted_iota(jnp.int32, (q, q), 0)
    jj = lax.broadcasted_iota(jnp.int32, (q, q), 1)
    causal = ii >= jj
    tri = jnp.where(causal, 1.0, 0.0).astype(BF16)
    hi, lo, lo2 = _split3(da)
    cum = _dot(tri, hi) + _dot(tri, lo) + _dot(tri, lo2)
    cum_t = cum.T

    cum_e = _expand_heads(cum, e_ref)
    last_e = cum_e[q - 1:q, :]
    xdt = xs * _expand_heads(dt, e_ref)
    xw_t = (xdt * jnp.exp(last_e - cum_e)).T.astype(BF16)
    state_decay = jnp.exp(jnp.broadcast_to(last_e, (q, D_SSM)).T)
    off_scale = jnp.exp(cum_e)

    first_head = lane < SSM_HEAD_DIM
    heads_per_group = N_HEADS_S // SSM_GROUPS
    gw = heads_per_group * SSM_HEAD_DIM
    state = state_ref[...]
    state_bf = state.astype(BF16)
    for g in range(SSM_GROUPS):
        bg = xc[:, D_SSM + g * D_STATE:D_SSM + (g + 1) * D_STATE].astype(BF16)
        cg = xc[:, D_SSM + (SSM_GROUPS + g) * D_STATE:D_SSM + (SSM_GROUPS + g + 1) * D_STATE].astype(BF16)
        cb = _dot_nt(cg, bg)
        y_off = _dot_nt(cg, state_bf[g * gw:(g + 1) * gw, :]) * off_scale[:, g * gw:(g + 1) * gw]
        y_ref[:, g * gw:(g + 1) * gw] = y_off
        for hp in range(heads_per_group // 2):
            pair = g * (heads_per_group // 2) + hp
            xp = xdt[:, pair * LANES:(pair + 1) * LANES]
            yp = None
            for s in range(2):
                h = 2 * pair + s
                seg = cum[:, h:h + 1] - cum_t[h:h + 1, :]
                dec = jnp.exp(jnp.where(causal, seg, NEG))
                m = (cb * dec).astype(BF16)
                xm = jnp.where(first_head if s == 0 else jnp.logical_not(first_head), xp, 0.0).astype(BF16)
                d = _dot(m, xm)
                yp = d if yp is None else yp + d
            y_ref[:, pair * LANES:(pair + 1) * LANES] += yp
        state_ref[g * gw:(g + 1) * gw, :] = (
            state_decay[g * gw:(g + 1) * gw, :] * state[g * gw:(g + 1) * gw, :]
            + _dot(xw_t[g * gw:(g + 1) * gw, :], bg))

    z = z_ref[...]
    y = (y_ref[...] + xs * dskip_ref[...]) * (z * _sigmoid(z))
    for g in range(SSM_GROUPS):
        yg = y[:, g * gw:(g + 1) * gw]
        ms = jnp.mean(yg * yg, axis=-1, keepdims=True)
        y_ref[:, g * gw:(g + 1) * gw] = (yg * lax.rsqrt(ms + EPS)) * gnorm_ref[:, g * gw:(g + 1) * gw]


def _ssd_prompt_kernel(z_ref, xbc_ref, dt_ref, conv0_ref, h0_ref, convw_ref, convb_ref, dtb_ref,
                       alog_ref, dskip_ref, gnorm_ref, e_ref, y_ref, hout_ref, xpad_ref, state_ref):
    c = pl.program_id(0)
    nseq = z_ref.shape[0]

    @pl.when(c == 0)
    def _():
        state_ref[...] = h0_ref[...]
        xpad_ref[:, 8 - (CONV_WIDTH - 1):8, :] = conv0_ref[...]

    for b in range(nseq):
        _ssd_chunk(z_ref.at[b], xbc_ref.at[b], dt_ref.at[b], y_ref.at[b], xpad_ref.at[b], state_ref.at[b],
                   convw_ref, convb_ref, dtb_ref, alog_ref, dskip_ref, gnorm_ref, e_ref, SSD_CHUNK)

    @pl.when(c == pl.num_programs(0) - 1)
    def _():
        hout_ref[...] = state_ref[...]


def _ssd_sample_kernel(z_ref, xbc_ref, dt_ref, conv0_ref, h0_ref, convw_ref, convb_ref, dtb_ref,
                       alog_ref, dskip_ref, gnorm_ref, e_ref, y_ref, hout_ref, xs_ref, z8_ref, dt8_ref,
                       *, rows):
    for b in range(h0_ref.shape[0]):
        tok = pl.ds(b * rows, rows)
        _ssd_sample_one(z_ref.at[tok], xbc_ref.at[tok], dt_ref.at[tok], conv0_ref.at[:, b], h0_ref.at[b],
                        convw_ref, convb_ref, dtb_ref, alog_ref, dskip_ref, gnorm_ref, e_ref, y_ref.at[tok],
                        hout_ref.at[b], xs_ref.at[b], z8_ref.at[b], dt8_ref.at[b], rows)


def _ssd_sample_one(z_ref, xbc_ref, dt_ref, conv0_ref, h0_ref, convw_ref, convb_ref, dtb_ref,
                    alog_ref, dskip_ref, gnorm_ref, e_ref, y_ref, hout_ref, xs_ref, z8_ref, dt8_ref, rows):
    tail = CONV_WIDTH - 1
    r8 = 8
    row = lax.broadcasted_iota(jnp.int32, (r8, 1), 0)
    lane = lax.broadcasted_iota(jnp.int32, (1, LANES), 1)

    xs_ref[r8:2 * r8, :] = jnp.zeros((r8, CONV_DIM), F32)
    xs_ref[r8:r8 + rows, :] = xbc_ref[...]
    xs_ref[r8 - tail:r8, :] = conv0_ref[...]
    z8_ref[...] = jnp.zeros((r8, D_SSM), F32)
    z8_ref[0:rows, :] = z_ref[...]
    dt8_ref[...] = jnp.zeros((r8, LANES), F32)
    dt8_ref[0:rows, :] = dt_ref[...]

    conv = convb_ref[...] + convw_ref[tail:tail + 1, :] * xs_ref[r8:2 * r8, :]
    for i in range(tail):
        conv = conv + convw_ref[i:i + 1, :] * xs_ref[r8 - tail + i:2 * r8 - tail + i, :]
    xc = conv * _sigmoid(conv)
    xs = xc[:, 0:D_SSM]

    live = jnp.logical_and(lane < N_HEADS_S, row < rows)
    dt = jnp.where(live, _softplus(dt8_ref[...] + dtb_ref[...]), 0.0)
    da = dt * (-jnp.exp(alog_ref[...]))
    cum = da
    for k in range(1, rows):
        cum = cum + pltpu.roll(da, k, 0)
    last = rows - 1

    heads_per_group = N_HEADS_S // SSM_GROUPS
    gw = heads_per_group * SSM_HEAD_DIM
    bg = [xc[:, D_SSM + g * D_STATE:D_SSM + (g + 1) * D_STATE] for g in range(SSM_GROUPS)]
    cg = [xc[:, D_SSM + (SSM_GROUPS + g) * D_STATE:D_SSM + (SSM_GROUPS + g + 1) * D_STATE]
          for g in range(SSM_GROUPS)]
    group0 = lane < heads_per_group
    w = []
    for j in range(rows):
        cb = [jnp.sum(cg[g] * bg[g][j:j + 1, :], axis=-1, keepdims=True) for g in range(SSM_GROUPS)]
        dec = jnp.exp(jnp.where(row >= j, cum - cum[j:j + 1, :], NEG))
        w.append(jnp.where(group0, cb[0], cb[1]) * dec)

    stack = jnp.concatenate([cum, dt] + w, axis=0)
    hi, lo, lo2 = _split3(stack)
    ex = _dot(jnp.concatenate([hi, lo, lo2], axis=0), e_ref[...])
    ns = stack.shape[0]
    ex = ex[0:ns] + ex[ns:2 * ns] + ex[2 * ns:3 * ns]
    cum_e = ex[0:r8]
    xdt = xs * ex[r8:2 * r8]
    y = None
    for j in range(rows):
        term = ex[(2 + j) * r8:(3 + j) * r8] * xdt[j:j + 1, :]
        y = term if y is None else y + term
    last_e = cum_e[last:last + 1, :]

    state = h0_ref[...]
    state_bf = state.astype(BF16)
    zeros8 = jnp.zeros((r8, D_STATE), F32)
    y_off = [_dot_nt(jnp.concatenate([cg[g], zeros8], axis=0).astype(BF16),
                     state_bf[g * gw:(g + 1) * gw, :])[0:r8, :] for g in range(SSM_GROUPS)]
    y = y + jnp.concatenate(y_off, axis=1) * jnp.exp(cum_e)

    z = z8_ref[...]
    y = (y + xs * dskip_ref[...]) * (z * _sigmoid(z))
    outs = []
    for g in range(SSM_GROUPS):
        yg = y[:, g * gw:(g + 1) * gw]
        ms = jnp.mean(yg * yg, axis=-1, keepdims=True)
        outs.append((yg * lax.rsqrt(ms + EPS)) * gnorm_ref[:, g * gw:(g + 1) * gw])
    y_ref[...] = jnp.concatenate(outs, axis=1)[0:rows, :]

    pad = jnp.zeros((SSD_CHUNK - r8, D_SSM), F32)
    xw_t = jnp.concatenate([xdt * jnp.exp(last_e - cum_e), pad], axis=0).T.astype(BF16)
    state_decay = jnp.exp(jnp.broadcast_to(last_e, (SSD_CHUNK, D_SSM)).T)
    bpad = jnp.zeros((SSD_CHUNK - r8, D_STATE), F32)
    for g in range(SSM_GROUPS):
        b128 = jnp.concatenate([bg[g], bpad], axis=0).astype(BF16)
        hout_ref[g * gw:(g + 1) * gw, :] = (
            state_decay[g * gw:(g + 1) * gw, :] * state[g * gw:(g + 1) * gw, :]
            + _dot(xw_t[g * gw:(g + 1) * gw, :], b128))


def _ssd_param_specs():
    return [_resident((CONV_WIDTH, CONV_DIM)), _resident((1, CONV_DIM)), _resident((1, LANES)),
            _resident((1, LANES)), _resident((1, D_SSM)), _resident((1, D_SSM)),
            _resident((LANES, D_SSM))]


def _ssd_prompt(z, xbc, dt_raw, conv0, h0, ssd_params):
    batch, seq = z.shape[0], z.shape[1]
    row = lambda w: pl.BlockSpec((batch, SSD_CHUNK, w), lambda c: (0, c, 0))
    whole = lambda r, w: pl.BlockSpec((batch, r, w), lambda c: (0, 0, 0))
    return pl.pallas_call(
        _ssd_prompt_kernel,
        grid=(seq // SSD_CHUNK,),
        in_specs=[row(D_SSM), row(CONV_DIM), row(LANES), whole(CONV_WIDTH - 1, CONV_DIM),
                  whole(D_SSM, D_STATE)] + _ssd_param_specs(),
        out_specs=[row(D_SSM), whole(D_SSM, D_STATE)],
        out_shape=[jax.ShapeDtypeStruct((batch, seq, D_SSM), F32),
                   jax.ShapeDtypeStruct((batch, D_SSM, D_STATE), F32)],
        scratch_shapes=[pltpu.VMEM((batch, 8 + SSD_CHUNK, CONV_DIM), F32),
                        pltpu.VMEM((batch, D_SSM, D_STATE), F32)],
        compiler_params=pltpu.CompilerParams(
            dimension_semantics=("arbitrary",), vmem_limit_bytes=VMEM_LIMIT),
        name="ssd_prompt",
    )(z, xbc, dt_raw, conv0, h0, *ssd_params)


SSD_SAMPLE_SEQS = 8


def _ssd_sample(z, xbc, dt_raw, conv0, h0, ssd_params, rows):
    batch = h0.shape[0]
    nb = SSD_SAMPLE_SEQS
    tok = lambda w: pl.BlockSpec((nb * rows, w), lambda i: (i, 0))
    state = pl.BlockSpec((nb, D_SSM, D_STATE), lambda i: (i, 0, 0))
    return pl.pallas_call(
        functools.partial(_ssd_sample_kernel, rows=rows),
        grid=(batch // nb,),
        in_specs=[tok(D_SSM), tok(CONV_DIM), tok(LANES),
                  pl.BlockSpec((CONV_WIDTH - 1, nb, CONV_DIM), lambda i: (0, i, 0)), state] + _ssd_param_specs(),
        out_specs=[tok(D_SSM), state],
        out_shape=[jax.ShapeDtypeStruct((batch * rows, D_SSM), F32),
                   jax.ShapeDtypeStruct((batch, D_SSM, D_STATE), F32)],
        scratch_shapes=[pltpu.VMEM((nb, 16, CONV_DIM), F32), pltpu.VMEM((nb, 8, D_SSM), F32),
                        pltpu.VMEM((nb, 8, LANES), F32)],
        compiler_params=pltpu.CompilerParams(
            dimension_semantics=("arbitrary",), vmem_limit_bytes=VMEM_LIMIT),
        name="ssd_sample",
    )(z, xbc, dt_raw, conv0, h0, *ssd_params)


def _t5_bucket_np(dist):
    dist = np.asarray(dist, np.int64)
    df = np.maximum(dist, 1).astype(np.float32)
    large = MAX_EXACT + (np.log(df / np.float32(MAX_EXACT)) / np.float32(math.log(MAX_WINDOW / MAX_EXACT))
                         * np.float32(N_BUCKETS - MAX_EXACT)).astype(np.int32)
    return np.where(dist < MAX_EXACT, dist, np.minimum(large, N_BUCKETS - 1)).astype(np.int32)


def _band_bucket_tables():
    ki = np.arange(SPAN)[None, :]
    out = np.empty((len(DILATIONS), 2, SPAN, SPAN), np.int32)
    for bi, d in enumerate(DILATIONS):
        nrun = DILATIONS[-1] // d
        row = np.arange(SPAN)
        qi = (nrun * (row % (SPAN // nrun)) + row // (SPAN // nrun))[:, None]
        rel_prev = qi + SPAN - ki
        rel_cur = qi - ki
        out[bi, 0] = np.where(rel_prev <= SPAN, _t5_bucket_np(np.clip(rel_prev, 0, SPAN) * d), -1)
        out[bi, 1] = np.where(rel_cur >= 0, _t5_bucket_np(np.clip(rel_cur, 0, SPAN) * d), -1)
    return out


def _bias_table_kernel(tab_ref, rb_ref, out_ref, *, buckets):
    pair = pl.program_id(0)
    for bi in range(len(DILATIONS)):
        for half in range(2):
            tab = tab_ref[bi, half]
            acc = [jnp.full((SPAN, SPAN), NEG, F32) for _ in range(2)]
            for bkt in buckets[bi][half]:
                hit = tab == bkt
                for s in range(2):
                    acc[s] = jnp.where(hit, rb_ref[bkt, 2 * pair + s] * LOG2E, acc[s])
            for s in range(2):
                rows = slice(s * SPAN, (s + 1) * SPAN)
                if half == 0:
                    out_ref[0, bi, 0, rows, 0:SPAN] = jnp.full((SPAN, SPAN), NEG, F32)
                    out_ref[0, bi, 1, rows, 0:SPAN] = acc[s]
                else:
                    out_ref[0, bi, 0, rows, SPAN:2 * SPAN] = acc[s]
                    out_ref[0, bi, 1, rows, SPAN:2 * SPAN] = acc[s]


def _bias_tables(rel_bias):
    tabs = _band_bucket_tables()
    buckets = [[sorted(int(b) for b in np.unique(tabs[bi, half]) if b >= 0) for half in range(2)]
               for bi in range(len(DILATIONS))]
    nd = len(DILATIONS)
    return pl.pallas_call(
        functools.partial(_bias_table_kernel, buckets=buckets),
        grid=(N_HEADS_A // 2,),
        in_specs=[_resident((nd, 2, SPAN, SPAN)),
                  pl.BlockSpec(memory_space=pltpu.SMEM)],
        out_specs=pl.BlockSpec((1, nd, 2, 2 * SPAN, 2 * SPAN), lambda p: (p, 0, 0, 0, 0)),
        out_shape=jax.ShapeDtypeStruct((N_HEADS_A // 2, nd, 2, 2 * SPAN, 2 * SPAN), F32),
        compiler_params=pltpu.CompilerParams(dimension_semantics=("arbitrary",)),
        name="bias_tables",
    )(jnp.asarray(tabs), rel_bias)


ATT_T = SPAN * DILATIONS[-1]
ATT_UNROLL = 8
ATT_SUB = 1


def _query_runs(d, r, n):
    run = SPAN * d // DILATIONS[-1]
    return [pl.ds(pl.multiple_of((r + d * c) * SPAN + run * n, 8), run) for c in range(DILATIONS[-1] // d)]


def _load_runs(ref, runs, *lead):
    parts = [ref[(*lead, rows, slice(None))] for rows in runs]
    return parts[0] if len(parts) == 1 else jnp.concatenate(parts, axis=0)


def _store_runs(ref, runs, value, *lead):
    run = value.shape[0] // len(runs)
    for c, rows in enumerate(runs):
        ref[(*lead, rows, slice(None))] = value[c * run:(c + 1) * run, :]


def _attn_kernel(q_ref, k_ref, v_ref, bias_ref, o_ref, qp_ref, tmp_ref, kph_ref, vph_ref, acc_ref, m_ref,
                 l_ref):
    t = pl.program_id(2)
    lane = lax.broadcasted_iota(jnp.int32, (1, LANES), 1)
    first_head = lane < HEAD_DIM
    head_mask = (first_head, jnp.logical_not(first_head))
    nblk = ATT_T // SPAN
    dmax = DILATIONS[-1]

    @pl.when(t == 0)
    def _():
        for bi, d in enumerate(DILATIONS):
            ph = ATT_T // d + SPAN
            for r in range(d):
                kph_ref[bi, r * ph:r * ph + SPAN, :] = jnp.zeros((SPAN, LANES), BF16)
                vph_ref[bi, r * ph:r * ph + SPAN, :] = jnp.zeros((SPAN, LANES), BF16)

    d4, d16 = DILATIONS[1], DILATIONS[2]
    assert DILATIONS == (1, d4, d4 * d4)
    l4, l16 = ATT_T // d4, ATT_T // d16
    for r4 in range(d4):
        tmp_ref[r4 * l4:(r4 + 1) * l4, :] = q_ref[pl.ds(r4, l4, stride=d4), :] * (ATTN_SCALE * LOG2E)
    for r16 in range(d16):
        qp_ref[r16 * l16:(r16 + 1) * l16, :] = tmp_ref[pl.ds((r16 % d4) * l4 + r16 // d4, l16, stride=d4), :]
    for src_ref, dst_ref in ((k_ref, kph_ref), (v_ref, vph_ref)):
        dst_ref[0, SPAN:SPAN + ATT_T, :] = src_ref[...].astype(BF16)
        ph4, ph16 = l4 + SPAN, l16 + SPAN
        for r4 in range(d4):
            x4 = src_ref[pl.ds(r4, l4, stride=d4), :]
            tmp_ref[r4 * l4:(r4 + 1) * l4, :] = x4
            dst_ref[1, r4 * ph4 + SPAN:(r4 + 1) * ph4, :] = x4.astype(BF16)
        for r16 in range(d16):
            x16 = tmp_ref[pl.ds((r16 % d4) * l4 + r16 // d4, l16, stride=d4), :]
            dst_ref[2, r16 * ph16 + SPAN:(r16 + 1) * ph16, :] = x16.astype(BF16)

    order = tuple(reversed(range(len(DILATIONS))))
    for bi in order:
        d = DILATIONS[bi]
        ln = ATT_T // d
        ph = ln + SPAN
        nb = ln // SPAN
        first_branch = bi == order[0]
        last_branch = bi == order[-1]

        def group(it, carry, bi=bi, d=d, ph=ph, nb=nb, first_branch=first_branch, last_branch=last_branch):
            def fetch(j):
                sub = []
                for g in range(j * ATT_SUB, (j + 1) * ATT_SUB):
                    blk = it * ATT_UNROLL + g
                    r = blk // nb
                    n = blk % nb
                    runs = _query_runs(d, r, n)
                    krow = pl.multiple_of(r * ph + n * SPAN, SPAN)
                    variant = jnp.logical_or(t > 0, n > 0).astype(jnp.int32)
                    qf = _load_runs(qp_ref, runs)
                    qs = jnp.concatenate([jnp.where(head_mask[s], qf, 0.0).astype(BF16) for s in range(2)],
                                         axis=0)
                    logits = _dot_nt(qs, kph_ref[bi, pl.ds(krow, 2 * SPAN), :]) + bias_ref[0, bi, variant]
                    old = None
                    if not first_branch:
                        old = ([_load_runs(m_ref, runs, s) for s in range(2)],
                               [_load_runs(l_ref, runs, s) for s in range(2)], _load_runs(acc_ref, runs))
                    sub.append((runs, krow, old, logits))
                return sub

            nsub = ATT_UNROLL // ATT_SUB
            nxt = fetch(0)
            for j in range(nsub):
                cur = nxt
                if j + 1 < nsub:
                    nxt = fetch(j + 1)
                for runs, m_new, l_new, pv in softmax_pv(cur):
                    if last_branch:
                        _store_runs(acc_ref, runs, pv / jnp.where(first_head, l_new[0], l_new[1]))
                    else:
                        for s in range(2):
                            _store_runs(m_ref, runs, m_new[s], s)
                            _store_runs(l_ref, runs, l_new[s], s)
                        _store_runs(acc_ref, runs, pv)
            return carry

        def softmax_pv(sub, bi=bi, first_branch=first_branch):
            soft = []
            for runs, krow, old, sc2 in sub:
                m_new, l_new, alpha, ps = [], [], [], []
                for s in range(2):
                    sc = sc2[s * SPAN:(s + 1) * SPAN, :]
                    halves = (sc[:, 0:SPAN], sc[:, SPAN:2 * SPAN])
                    mb = jnp.max(jnp.maximum(halves[0], halves[1]), axis=-1, keepdims=True)
                    if first_branch:
                        m = jnp.broadcast_to(mb, (SPAN, LANES))
                    else:
                        m = jnp.maximum(old[0][s], mb)
                        alpha.append(jnp.exp2(old[0][s] - m))
                    p = [jnp.exp2(h - m) for h in halves]
                    rs = jnp.sum(p[0] + p[1], axis=-1, keepdims=True)
                    ps.append(jnp.concatenate([p[0].astype(BF16), p[1].astype(BF16)], axis=1))
                    m_new.append(m)
                    l_new.append(jnp.broadcast_to(rs, (SPAN, LANES)) if first_branch
                                 else alpha[s] * old[1][s] + rs)
                soft.append((m_new, l_new, alpha, ps))
            out = []
            for (runs, krow, old, _), (m_new, l_new, alpha, ps) in zip(sub, soft):
                pv2 = _dot(jnp.concatenate(ps, axis=0), vph_ref[bi, pl.ds(krow, 2 * SPAN), :])
                pv = jnp.where(first_head, pv2[0:SPAN, :], pv2[SPAN:2 * SPAN, :])
                if not first_branch:
                    pv = jnp.where(first_head, alpha[0], alpha[1]) * old[2] + pv
                out.append((runs, m_new, l_new, pv))
            return out

        lax.fori_loop(0, nblk // ATT_UNROLL, group, 0)

    for r in range(dmax):
        o_ref[pl.ds(r, SPAN, stride=dmax), :] = acc_ref[r * SPAN:(r + 1) * SPAN, :]

    for bi, d in enumerate(DILATIONS):
        ln = ATT_T // d
        ph = ln + SPAN
        for r in range(d):
            kph_ref[bi, r * ph:r * ph + SPAN, :] = kph_ref[bi, (r + 1) * ph - SPAN:(r + 1) * ph, :]
            vph_ref[bi, r * ph:r * ph + SPAN, :] = vph_ref[bi, (r + 1) * ph - SPAN:(r + 1) * ph, :]


def _attn_prompt(qkv, bias, batch):
    m = qkv.shape[0]
    nt = m // batch // ATT_T
    npair = N_HEADS_A // 2
    nd = len(DILATIONS)
    blk = lambda off: pl.BlockSpec((ATT_T, LANES), lambda b, p, t: (b * nt + t, off + p))
    halo_rows = ATT_T + DILATIONS[-1] * SPAN
    return pl.pallas_call(
        _attn_kernel,
        grid=(batch, npair, nt),
        in_specs=[blk(0), blk(npair), blk(2 * npair),
                  pl.BlockSpec((1, nd, 2, 2 * SPAN, 2 * SPAN), lambda b, p, t: (p, 0, 0, 0, 0))],
        out_specs=blk(0),
        out_shape=jax.ShapeDtypeStruct((m, D_ATTN), F32),
        scratch_shapes=[pltpu.VMEM((ATT_T, LANES), F32),
                        pltpu.VMEM((ATT_T, LANES), F32),
                        pltpu.VMEM((nd, halo_rows, LANES), BF16),
                        pltpu.VMEM((nd, halo_rows, LANES), BF16),
                        pltpu.VMEM((ATT_T, LANES), F32),
                        pltpu.VMEM((2, ATT_T, LANES), F32),
                        pltpu.VMEM((2, ATT_T, LANES), F32)],
        compiler_params=pltpu.CompilerParams(
            dimension_semantics=("arbitrary", "arbitrary", "arbitrary"), vmem_limit_bytes=VMEM_LIMIT),
        name="attn_prompt",
    )(qkv, qkv, qkv, bias)


SAMPLE_ROWS = 4


def _sample_tables():
    pos = np.arange(MAX_WINDOW)[None, :]
    t = (np.arange(8) % SAMPLE_ROWS)[:, None]
    dist = MAX_WINDOW + t - pos
    mult = np.zeros(dist.shape, np.float32)
    for d in DILATIONS:
        mult += ((dist % d == 0) & (dist <= SPAN * d)).astype(np.float32)
    bucket = np.where(mult > 0, _t5_bucket_np(dist), -1).astype(np.int32)
    return bucket, np.concatenate([mult, mult], axis=0)


def _sample_bias_kernel(tab_ref, rb_ref, rbv_ref, bmain_ref, bnew_ref, *, buckets):
    pair = pl.program_id(0)
    tab = tab_ref[...]
    acc = [jnp.full(tab.shape, NEG, F32) for _ in range(2)]
    for bkt in buckets:
        hit = tab == bkt
        for s in range(2):
            acc[s] = jnp.where(hit, rb_ref[bkt, 2 * pair + s], acc[s])
    for s in range(2):
        bmain_ref[0, s * 8:(s + 1) * 8, :] = acc[s]
    t = lax.broadcasted_iota(jnp.int32, (8, 1), 0) % SAMPLE_ROWS
    for tp in range(SAMPLE_ROWS):
        b = jnp.full((8, LANES), NEG, F32)
        for dist in range(SAMPLE_ROWS):
            b = jnp.where(t - tp == dist, rbv_ref[dist:dist + 1, :], b)
        bnew_ref[tp] = b


def _sample_bias_tables(rel_bias):
    bucket, mult = _sample_tables()
    buckets = sorted(int(b) for b in np.unique(bucket) if b >= 0)
    npair = N_HEADS_A // 2
    rb_lanes = jnp.pad(rel_bias, ((0, 0), (0, LANES - N_HEADS_A)))
    bmain, bnew = pl.pallas_call(
        functools.partial(_sample_bias_kernel, buckets=buckets),
        grid=(npair,),
        in_specs=[_resident((8, MAX_WINDOW)), pl.BlockSpec(memory_space=pltpu.SMEM),
                  _resident((N_BUCKETS, LANES))],
        out_specs=[pl.BlockSpec((1, 16, MAX_WINDOW), lambda p: (p, 0, 0)),
                   pl.BlockSpec((SAMPLE_ROWS, 8, LANES), lambda p: (0, 0, 0))],
        out_shape=[jax.ShapeDtypeStruct((npair, 16, MAX_WINDOW), F32),
                   jax.ShapeDtypeStruct((SAMPLE_ROWS, 8, LANES), F32)],
        compiler_params=pltpu.CompilerParams(dimension_semantics=("arbitrary",)),
        name="sample_bias_tables",
    )(jnp.asarray(bucket), rel_bias, rb_lanes)
    return bmain, bnew, jnp.asarray(mult)


def _attn_sample_kernel(qkv_ref, kt_ref, vt_ref, bmain_ref, mult_ref, bnew_ref, e_ref, o_ref, om_ref):
    rows = SAMPLE_ROWS
    npair = N_HEADS_A // 2
    second = pl.program_id(0) % 2 == 1
    lower = lax.broadcasted_iota(jnp.int32, (2 * rows, 1), 0) < rows
    mine = jnp.logical_xor(lower, second)
    blk = qkv_ref[...]
    q8_all = jnp.where(mine, blk, pltpu.roll(blk, rows, 0))
    q8 = q8_all[:, 0:D_ATTN] * ATTN_SCALE
    kn = q8_all[:, D_ATTN:2 * D_ATTN]
    vn = q8_all[:, 2 * D_ATTN:3 * D_ATTN]
    lane = lax.broadcasted_iota(jnp.int32, (1, LANES), 1)
    first_head = lane < HEAD_DIM
    e = e_ref[...]

    s_new = []
    m_tot = jnp.full((8, LANES), NEG, F32)
    for tp in range(rows):
        hi, lo, lo2 = _split3(q8 * kn[tp:tp + 1, :])
        s = _dot_nt(hi, e) + _dot_nt(lo, e) + _dot_nt(lo2, e) + bnew_ref[tp]
        s_new.append(s)
        m_tot = jnp.maximum(m_tot, s)

    mult = mult_ref[...]
    l_tot = jnp.zeros((8, LANES), F32)

    def window_logits(p):
        qp = q8[:, p * LANES:(p + 1) * LANES]
        qbd = jnp.concatenate([jnp.where(first_head, qp, 0.0), jnp.where(first_head, 0.0, qp)],
                              axis=0).astype(BF16)
        return _dot(qbd, kt_ref[0, p].astype(BF16)) + bmain_ref[p]

    s_next = window_logits(0)
    for p in range(npair):
        s16 = s_next
        if p + 1 < npair:
            s_next = window_logits(p + 1)
        m_new =jnp.concatenate([m_tot[:, 2 * p:2 * p + 1], m_tot[:, 2 * p + 1:2 * p + 2]], axis=0)
        m = jnp.maximum(jnp.max(s16, axis=-1, keepdims=True), m_new)
        pr = mult * jnp.exp(s16 - m)
        ls = jnp.sum(pr, axis=-1, keepdims=True)
        o16 = _dot_nt(pr.astype(BF16), vt_ref[0, p].astype(BF16))
        om_ref[:, p * LANES:(p + 1) * LANES] = jnp.where(first_head, o16[0:8], o16[8:16])
        m_tot = jnp.where(lane == 2 * p, m[0:8], jnp.where(lane == 2 * p + 1, m[8:16], m_tot))
        l_tot = jnp.where(lane == 2 * p, ls[0:8], jnp.where(lane == 2 * p + 1, ls[8:16], l_tot))

    t = lax.broadcasted_iota(jnp.int32, (8, 1), 0) % rows
    o = om_ref[...]
    for tp in range(rows):
        c = jnp.where(t == tp, float(len(DILATIONS)), jnp.where(t > tp, 1.0, 0.0))
        pn = c * jnp.exp(s_new[tp] - m_tot)
        l_tot = l_tot + pn
        o = o + _expand_heads(pn, e_ref) * vn[tp:tp + 1, :]
    o = o / _expand_heads(l_tot, e_ref)
    @pl.when(jnp.logical_not(second))
    def _():
        o_ref[...] = o

    @pl.when(second)
    def _():
        o_ref[...] = jnp.where(lower, o_ref[...], o)


def _attn_sample(qkv, kt, vt, bmain, mult, bnew, expand):
    rows = SAMPLE_ROWS
    batch = qkv.shape[0] // rows
    npair = N_HEADS_A // 2
    cache = pl.BlockSpec((1, npair, LANES, MAX_WINDOW), lambda b: (b, 0, 0, 0))
    return pl.pallas_call(
        _attn_sample_kernel,
        grid=(batch,),
        in_specs=[pl.BlockSpec((2 * rows, 3 * D_ATTN), lambda b: (b // 2, 0)), cache, cache,
                  _resident((npair, 16, MAX_WINDOW)), _resident((16, MAX_WINDOW)),
                  _resident((rows, 8, LANES)), _resident((LANES, D_ATTN))],
        out_specs=pl.BlockSpec((2 * rows, D_ATTN), lambda b: (b // 2, 0)),
        out_shape=jax.ShapeDtypeStruct((batch * rows, D_ATTN), F32),
        scratch_shapes=[pltpu.VMEM((8, D_ATTN), F32)],
        compiler_params=pltpu.CompilerParams(
            dimension_semantics=("arbitrary",), vmem_limit_bytes=VMEM_LIMIT),
        name="attn_sample",
    )(qkv, kt, vt, bmain, mult, bnew, expand)


def _head_expand_matrix():
    e = np.zeros((LANES, D_SSM), np.float32)
    for h in range(N_HEADS_S):
        e[h, h * SSM_HEAD_DIM:(h + 1) * SSM_HEAD_DIM] = 1.0
    return jnp.asarray(e, BF16)


def _prep_params(norm_mix, w_in, q_norm, k_norm, conv_w, conv_b, dt_bias, a_log, d_skip, ssm_norm,
                 w_out, norm_ffn, w_up, w_down):
    n_main = 3 * D_ATTN + D_SSM + CONV_DIM
    pad_heads = lambda v: jnp.pad(v.reshape(1, N_HEADS_S), ((0, 0), (0, LANES - N_HEADS_S)))
    return dict(
        gmix=norm_mix.reshape(1, D_MODEL),
        w_main=w_in.astype(BF16),
        w_dt=jnp.pad(w_in[:, n_main:], ((0, 0), (0, LANES - N_HEADS_S))).astype(BF16),
        gq=jnp.tile(q_norm, N_HEADS_A).reshape(1, D_ATTN),
        gk=jnp.tile(k_norm, N_HEADS_A).reshape(1, D_ATTN),
        conv_w=conv_w,
        conv_b=conv_b.reshape(1, CONV_DIM),
        dt_bias=pad_heads(dt_bias),
        a_log=pad_heads(a_log),
        d_skip_e=jnp.repeat(d_skip, SSM_HEAD_DIM).reshape(1, D_SSM),
        gnorm=ssm_norm.reshape(1, D_SSM),
        expand=_head_expand_matrix(),
        w_out=w_out.astype(BF16),
        gffn=norm_ffn.reshape(1, D_MODEL),
        w_up=w_up.astype(BF16),
        w_down=w_down.astype(BF16),
    )


def _ssd_param_list(p):
    return (p["conv_w"], p["conv_b"], p["dt_bias"], p["a_log"], p["d_skip_e"], p["gnorm"], p["expand"])


def kernel(x_prompt, x_sample, cache_attn_k, cache_attn_v, state_conv, state_ssm, norm_mix, w_in, q_norm,
           k_norm, rel_bias, conv_w, conv_b, dt_bias, a_log, d_skip, ssm_norm, w_out, norm_ffn, w_up, w_down):
    depth = w_in.shape[0]
    assert depth == 1, "single-layer decoder"
    l = 0
    p = _prep_params(norm_mix[l], w_in[l], q_norm[l], k_norm[l], conv_w[l], conv_b[l], dt_bias[l], a_log[l],
                     d_skip[l], ssm_norm[l], w_out[l], norm_ffn[l], w_up[l], w_down[l])
    ssd_params = _ssd_param_list(p)
    tm = 512

    bp, sp = x_prompt.shape[:2]
    keep = min(MAX_WINDOW, sp)
    xp = x_prompt.reshape(bp * sp, D_MODEL)
    qkv, z, xbc, dt_raw = _inproj(xp, p["gmix"], p["w_main"], p["w_dt"], p["gq"], p["gk"], tm)
    attn = _attn_prompt(qkv, _bias_tables(rel_bias), bp)
    conv0 = jnp.zeros((bp, CONV_WIDTH - 1, CONV_DIM), F32)
    h0 = jnp.zeros((bp, D_SSM, D_STATE), F32)
    ssm_o, hst = _ssd_prompt(z.reshape(bp, sp, D_SSM), xbc.reshape(bp, sp, CONV_DIM),
                             dt_raw.reshape(bp, sp, LANES), conv0, h0, ssd_params)
    y_prompt = _outffn(xp, attn, ssm_o.reshape(bp * sp, D_SSM), p["w_out"], p["gffn"], p["w_up"], p["w_down"], tm)
    qkv3 = qkv.reshape(bp, sp, 3 * D_ATTN)
    new_k_prompt = qkv3[:, sp - keep:, D_ATTN:2 * D_ATTN].reshape(depth, bp, keep, N_HEADS_A, HEAD_DIM)
    new_v_prompt = qkv3[:, sp - keep:, 2 * D_ATTN:].reshape(depth, bp, keep, N_HEADS_A, HEAD_DIM)
    new_conv_prompt = xbc.reshape(bp, sp, CONV_DIM)[:, sp - (CONV_WIDTH - 1):].reshape(
        depth, bp, CONV_WIDTH - 1, CONV_DIM)
    new_ssm_prompt = hst.reshape(depth, bp, N_HEADS_S, SSM_HEAD_DIM, D_STATE)

    bs, ts = x_sample.shape[:2]
    assert cache_attn_k.shape[2] == MAX_WINDOW and ts == SAMPLE_ROWS
    xs = x_sample.reshape(bs * ts, D_MODEL)
    qkv_s, z_s, xbc_s, dt_s = _inproj(xs, p["gmix"], p["w_main"], p["w_dt"], p["gq"], p["gk"], tm)
    npair = N_HEADS_A // 2
    kt = jnp.transpose(cache_attn_k[l], (0, 2, 3, 1)).reshape(bs, npair, LANES, MAX_WINDOW)
    vt = jnp.transpose(cache_attn_v[l], (0, 2, 3, 1)).reshape(bs, npair, LANES, MAX_WINDOW)
    bmain, bnew, mult = _sample_bias_tables(rel_bias)
    attn_s = _attn_sample(qkv_s, kt, vt, bmain, mult, bnew, p["expand"])
    ssm_os, hsts = _ssd_sample(z_s, xbc_s, dt_s, jnp.transpose(state_conv[l], (1, 0, 2)),
                               state_ssm[l].reshape(bs, D_SSM, D_STATE), ssd_params, ts)
    y_sample = _outffn(xs, attn_s, ssm_os, p["w_out"], p["gffn"], p["w_up"], p["w_down"], tm)
    new_k_sample = qkv_s[:, D_ATTN:2 * D_ATTN].reshape(depth, bs, ts, N_HEADS_A, HEAD_DIM)
    new_v_sample = qkv_s[:, 2 * D_ATTN:].reshape(depth, bs, ts, N_HEADS_A, HEAD_DIM)
    new_conv_sample = xbc_s.reshape(bs, ts, CONV_DIM)[:, ts - (CONV_WIDTH - 1):].reshape(
        depth, bs, CONV_WIDTH - 1, CONV_DIM)
    new_ssm_sample = hsts.reshape(depth, bs, N_HEADS_S, SSM_HEAD_DIM, D_STATE)

    return (y_prompt.reshape(bp, sp, D_MODEL), y_sample.reshape(bs, ts, D_MODEL),
            new_k_prompt, new_v_prompt, new_conv_prompt, new_ssm_prompt,
            new_k_sample, new_v_sample, new_conv_sample, new_ssm_sample)
```

```python
import functools
import math

import jax
import jax.numpy as jnp
import numpy as np
from jax import lax
from jax.experimental import pallas as pl
from jax.experimental.pallas import tpu as pltpu

F32 = jnp.float32
BF16 = jnp.bfloat16

D_MODEL = 1024
HEAD_DIM = 64
N_HEADS_A = 16
D_ATTN = 1024
SPAN = 128
DILATIONS = (1, 4, 16)
MAX_WINDOW = 2048
N_BUCKETS = 32
MAX_EXACT = 16
SSM_HEAD_DIM = 64
N_HEADS_S = 16
D_SSM = 1024
SSM_GROUPS = 2
D_STATE = 128
CONV_WIDTH = 4
CONV_DIM = D_SSM + 2 * SSM_GROUPS * D_STATE
SSD_CHUNK = 128
D_FF = 4096
EPS = 1e-6
ATTN_SCALE = HEAD_DIM ** -0.5
LOG2E = 1.4426950408889634

LANES = 128
NEG = -1e30
VMEM_LIMIT = 56 * 1024 * 1024


def _resident(shape):
    nd = len(shape)
    return pl.BlockSpec(shape, lambda *_: (0,) * nd, pipeline_mode=pl.Buffered(1))


def _split3(x):
    hi = x.astype(BF16)
    r1 = x - hi.astype(F32)
    lo = r1.astype(BF16)
    lo2 = (r1 - lo.astype(F32)).astype(BF16)
    return hi, lo, lo2


def _dot(a, b):
    return jnp.dot(a, b, preferred_element_type=F32)


def _dot_nt(a, b):
    return lax.dot_general(a, b, (((1,), (1,)), ((), ())), preferred_element_type=F32)


def _sigmoid(x):
    return 0.5 * jnp.tanh(0.5 * x) + 0.5


def _softplus(x):
    return jnp.maximum(x, 0.0) + jnp.log1p(jnp.exp(-jnp.abs(x)))


IN_TN = 512


def _inproj_kernel(x_ref, gmix_ref, w_ref, wdt_ref, gq_ref, gk_ref,
                   qkv_ref, z_ref, xbc_ref, dt_ref):
    x = x_ref[...]
    ms = jnp.mean(x * x, axis=-1, keepdims=True)
    xn = ((x * lax.rsqrt(ms + EPS)) * gmix_ref[...]).astype(BF16)
    first_head = lax.broadcasted_iota(jnp.int32, (1, LANES), 1) < HEAD_DIM

    n_main = 3 * D_ATTN + D_SSM + CONV_DIM
    for j in range(n_main // IN_TN):
        c0 = j * IN_TN
        t = _dot(xn, w_ref[:, c0:c0 + IN_TN])
        if c0 < 2 * D_ATTN:
            g_ref = gq_ref if c0 < D_ATTN else gk_ref
            g0 = c0 % D_ATTN
            for c in range(IN_TN // LANES):
                tc = t[:, c * LANES:(c + 1) * LANES]
                s = tc * tc
                sa = jnp.sum(jnp.where(first_head, s, 0.0), axis=-1, keepdims=True)
                sb = jnp.sum(jnp.where(first_head, 0.0, s), axis=-1, keepdims=True)
                inv = lax.rsqrt(jnp.where(first_head, sa, sb) * (1.0 / HEAD_DIM) + EPS)
                qkv_ref[:, c0 + c * LANES:c0 + (c + 1) * LANES] = (
                    (tc * inv) * g_ref[:, g0 + c * LANES:g0 + (c + 1) * LANES])
        elif c0 < 3 * D_ATTN:
            qkv_ref[:, c0:c0 + IN_TN] = t
        elif c0 < 3 * D_ATTN + D_SSM:
            z_ref[:, c0 - 3 * D_ATTN:c0 - 3 * D_ATTN + IN_TN] = t
        else:
            o0 = c0 - 3 * D_ATTN - D_SSM
            xbc_ref[:, o0:o0 + IN_TN] = t
    dt_ref[...] = _dot(xn, wdt_ref[...])


def _inproj(x2d, gmix, w_main, w_dt, gq, gk, tm):
    m = x2d.shape[0]
    row = lambda w: pl.BlockSpec((tm, w), lambda i: (i, 0))
    return pl.pallas_call(
        _inproj_kernel,
        grid=(m // tm,),
        in_specs=[row(D_MODEL), _resident((1, D_MODEL)), _resident(w_main.shape),
                  _resident((D_MODEL, LANES)), _resident((1, D_ATTN)), _resident((1, D_ATTN))],
        out_specs=[row(3 * D_ATTN), row(D_SSM), row(CONV_DIM), row(LANES)],
        out_shape=[jax.ShapeDtypeStruct((m, 3 * D_ATTN), F32),
                   jax.ShapeDtypeStruct((m, D_SSM), F32),
                   jax.ShapeDtypeStruct((m, CONV_DIM), F32),
                   jax.ShapeDtypeStruct((m, LANES), F32)],
        compiler_params=pltpu.CompilerParams(
            dimension_semantics=("arbitrary",), vmem_limit_bytes=VMEM_LIMIT),
        name="inproj",
    )(x2d, gmix, w_main, w_dt, gq, gk)


FFN_TF = 1024


def _outffn_kernel(x_ref, attn_ref, ssm_ref, wout_ref, gffn_ref, wup_ref, wdown_ref, y_ref):
    a = attn_ref[...].astype(BF16)
    s = ssm_ref[...].astype(BF16)
    h = x_ref[...] + _dot(a, wout_ref[0:D_ATTN, :]) + _dot(s, wout_ref[D_ATTN:D_ATTN + D_SSM, :])
    ms = jnp.mean(h * h, axis=-1, keepdims=True)
    hn = ((h * lax.rsqrt(ms + EPS)) * gffn_ref[...]).astype(BF16)
    acc = None
    for f in range(D_FF // FFN_TF):
        u = _dot(hn, wup_ref[:, f * FFN_TF:(f + 1) * FFN_TF])
        u = jnp.maximum(u, 0.0)
        u = (u * u).astype(BF16)
        d = _dot(u, wdown_ref[f * FFN_TF:(f + 1) * FFN_TF, :])
        acc = d if acc is None else acc + d
    y_ref[...] = h + acc


def _outffn(x2d, attn, ssm_o, w_out, gffn, w_up, w_down, tm):
    m = x2d.shape[0]
    row = pl.BlockSpec((tm, D_MODEL), lambda i: (i, 0))
    return pl.pallas_call(
        _outffn_kernel,
        grid=(m // tm,),
        in_specs=[row, row, row, _resident((D_ATTN + D_SSM, D_MODEL)), _resident((1, D_MODEL)),
                  _resident((D_MODEL, D_FF)), _resident((D_FF, D_MODEL))],
        out_specs=row,
        out_shape=jax.ShapeDtypeStruct((m, D_MODEL), F32),
        compiler_params=pltpu.CompilerParams(
            dimension_semantics=("arbitrary",), vmem_limit_bytes=VMEM_LIMIT),
        name="outffn",
    )(x2d, attn, ssm_o, w_out, gffn, w_up, w_down)


def _expand_heads(v, e_ref):
    hi, lo, lo2 = _split3(v)
    e = e_ref[...]
    return _dot(hi, e) + _dot(lo, e) + _dot(lo2, e)


def _ssd_chunk(z_ref, xbc_ref, dt_ref, y_ref, xpad_ref, state_ref,
               convw_ref, convb_ref, dtb_ref, alog_ref, dskip_ref, gnorm_ref, e_ref):
    q = SSD_CHUNK
    pad = 8
    tail = CONV_WIDTH - 1

    xbc = xbc_ref[...]
    xpad_ref[pad:pad + q, :] = xbc
    conv = convb_ref[...] + convw_ref[tail:tail + 1, :] * xbc
    for i in range(tail):
        conv = conv + convw_ref[i:i + 1, :] * xpad_ref[pad - tail + i:pad - tail + i + q, :]
    xpad_ref[pad - tail:pad, :] = xbc[q - tail:q, :]
    xc = conv * _sigmoid(conv)
    xs = xc[:, 0:D_SSM]

    lane = lax.broadcasted_iota(jnp.int32, (1, LANES), 1)
    dt = jnp.where(lane < N_HEADS_S, _softplus(dt_ref[...] + dtb_ref[...]), 0.0)
    da = dt * (-jnp.exp(alog_ref[...]))

    ii = lax.broadcasted_iota(jnp.int32, (q, q), 0)
    jj = lax.broadcasted_iota(jnp.int32, (q, q), 1)
    causal = ii >= jj
    tri = jnp.where(causal, 1.0, 0.0).astype(BF16)
    hi, lo, lo2 = _split3(da)
    cum = _dot(tri, hi) + _dot(tri, lo) + _dot(tri, lo2)
    cum_t = cum.T

    cum_e = _expand_heads(cum, e_ref)
    last_e = cum_e[q - 1:q, :]
    xdt = xs * _expand_heads(dt, e_ref)
    xw_t = (xdt * jnp.exp(last_e - cum_e)).T.astype(BF16)
    state_decay = jnp.exp(jnp.broadcast_to(last_e, (q, D_SSM)).T)
    off_scale = jnp.exp(cum_e)

    first_head = lane < SSM_HEAD_DIM
    heads_per_group = N_HEADS_S // SSM_GROUPS
    gw = heads_per_group * SSM_HEAD_DIM
    state = state_ref[...]
    state_bf = state.astype(BF16)
    yield
    for g in range(SSM_GROUPS):
        bg = xc[:, D_SSM + g * D_STATE:D_SSM + (g + 1) * D_STATE].astype(BF16)
        cg = xc[:, D_SSM + (SSM_GROUPS + g) * D_STATE:D_SSM + (SSM_GROUPS + g + 1) * D_STATE].astype(BF16)
        cb = _dot_nt(cg, bg)
        y_off = _dot_nt(cg, state_bf[g * gw:(g + 1) * gw, :]) * off_scale[:, g * gw:(g + 1) * gw]
        y_ref[:, g * gw:(g + 1) * gw] = y_off
        for hp in range(heads_per_group // 2):
            pair = g * (heads_per_group // 2) + hp
            xp = xdt[:, pair * LANES:(pair + 1) * LANES]
            yp = None
            for s in range(2):
                h = 2 * pair + s
                seg = cum[:, h:h + 1] - cum_t[h:h + 1, :]
                dec = jnp.exp(jnp.where(causal, seg, NEG))
                m = (cb * dec).astype(BF16)
                xm = jnp.where(first_head if s == 0 else jnp.logical_not(first_head), xp, 0.0).astype(BF16)
                d = _dot(m, xm)
                yp = d if yp is None else yp + d
            y_ref[:, pair * LANES:(pair + 1) * LANES] += yp
            yield
        state_ref[g * gw:(g + 1) * gw, :] = (
            state_decay[g * gw:(g + 1) * gw, :] * state[g * gw:(g + 1) * gw, :]
            + _dot(xw_t[g * gw:(g + 1) * gw, :], bg))

    z = z_ref[...]
    y = (y_ref[...] + xs * dskip_ref[...]) * (z * _sigmoid(z))
    for g in range(SSM_GROUPS):
        yg = y[:, g * gw:(g + 1) * gw]
        ms = jnp.mean(yg * yg, axis=-1, keepdims=True)
        y_ref[:, g * gw:(g + 1) * gw] = (yg * lax.rsqrt(ms + EPS)) * gnorm_ref[:, g * gw:(g + 1) * gw]


def _ssd_sample_kernel(z_ref, xbc_ref, dt_ref, conv0_ref, h0_ref, convw_ref, convb_ref, dtb_ref,
                       alog_ref, dskip_ref, gnorm_ref, e_ref, y_ref, hout_ref, xs_ref, z8_ref, dt8_ref,
                       *, rows):
    for b in range(h0_ref.shape[0]):
        tok = pl.ds(b * rows, rows)
        _ssd_sample_one(z_ref.at[tok], xbc_ref.at[tok], dt_ref.at[tok], conv0_ref.at[:, b], h0_ref.at[b],
                        convw_ref, convb_ref, dtb_ref, alog_ref, dskip_ref, gnorm_ref, e_ref, y_ref.at[tok],
                        hout_ref.at[b], xs_ref.at[b], z8_ref.at[b], dt8_ref.at[b], rows)


def _ssd_sample_one(z_ref, xbc_ref, dt_ref, conv0_ref, h0_ref, convw_ref, convb_ref, dtb_ref,
                    alog_ref, dskip_ref, gnorm_ref, e_ref, y_ref, hout_ref, xs_ref, z8_ref, dt8_ref, rows):
    tail = CONV_WIDTH - 1
    r8 = 8
    row = lax.broadcasted_iota(jnp.int32, (r8, 1), 0)
    lane = lax.broadcasted_iota(jnp.int32, (1, LANES), 1)

    xs_ref[r8:2 * r8, :] = jnp.zeros((r8, CONV_DIM), F32)
    xs_ref[r8:r8 + rows, :] = xbc_ref[...]
    xs_ref[r8 - tail:r8, :] = conv0_ref[...]
    z8_ref[...] = jnp.zeros((r8, D_SSM), F32)
    z8_ref[0:rows, :] = z_ref[...]
    dt8_ref[...] = jnp.zeros((r8, LANES), F32)
    dt8_ref[0:rows, :] = dt_ref[...]

    conv = convb_ref[...] + convw_ref[tail:tail + 1, :] * xs_ref[r8:2 * r8, :]
    for i in range(tail):
        conv = conv + convw_ref[i:i + 1, :] * xs_ref[r8 - tail + i:2 * r8 - tail + i, :]
    xc = conv * _sigmoid(conv)
    xs = xc[:, 0:D_SSM]

    live = jnp.logical_and(lane < N_HEADS_S, row < rows)
    dt = jnp.where(live, _softplus(dt8_ref[...] + dtb_ref[...]), 0.0)
    da = dt * (-jnp.exp(alog_ref[...]))
    cum = da
    for k in range(1, rows):
        cum = cum + pltpu.roll(da, k, 0)
    last = rows - 1

    heads_per_group = N_HEADS_S // SSM_GROUPS
    gw = heads_per_group * SSM_HEAD_DIM
    bg = [xc[:, D_SSM + g * D_STATE:D_SSM + (g + 1) * D_STATE] for g in range(SSM_GROUPS)]
    cg = [xc[:, D_SSM + (SSM_GROUPS + g) * D_STATE:D_SSM + (SSM_GROUPS + g + 1) * D_STATE]
          for g in range(SSM_GROUPS)]
    group0 = lane < heads_per_group
    w = []
    for j in range(rows):
        cb = [jnp.sum(cg[g] * bg[g][j:j + 1, :], axis=-1, keepdims=True) for g in range(SSM_GROUPS)]
        dec = jnp.exp(jnp.where(row >= j, cum - cum[j:j + 1, :], NEG))
        w.append(jnp.where(group0, cb[0], cb[1]) * dec)

    stack = jnp.concatenate([cum, dt] + w, axis=0)
    hi, lo, lo2 = _split3(stack)
    ex = _dot(jnp.concatenate([hi, lo, lo2], axis=0), e_ref[...])
    ns = stack.shape[0]
    ex = ex[0:ns] + ex[ns:2 * ns] + ex[2 * ns:3 * ns]
    cum_e = ex[0:r8]
    xdt = xs * ex[r8:2 * r8]
    y = None
    for j in range(rows):
        term = ex[(2 + j) * r8:(3 + j) * r8] * xdt[j:j + 1, :]
        y = term if y is None else y + term
    last_e = cum_e[last:last + 1, :]

    state = h0_ref[...]
    state_bf = state.astype(BF16)
    zeros8 = jnp.zeros((r8, D_STATE), F32)
    y_off = [_dot_nt(jnp.concatenate([cg[g], zeros8], axis=0).astype(BF16),
                     state_bf[g * gw:(g + 1) * gw, :])[0:r8, :] for g in range(SSM_GROUPS)]
    y = y + jnp.concatenate(y_off, axis=1) * jnp.exp(cum_e)

    z = z8_ref[...]
    y = (y + xs * dskip_ref[...]) * (z * _sigmoid(z))
    outs = []
    for g in range(SSM_GROUPS):
        yg = y[:, g * gw:(g + 1) * gw]
        ms = jnp.mean(yg * yg, axis=-1, keepdims=True)
        outs.append((yg * lax.rsqrt(ms + EPS)) * gnorm_ref[:, g * gw:(g + 1) * gw])
    y_ref[...] = jnp.concatenate(outs, axis=1)[0:rows, :]

    pad = jnp.zeros((SSD_CHUNK - r8, D_SSM), F32)
    xw_t = jnp.concatenate([xdt * jnp.exp(last_e - cum_e), pad], axis=0).T.astype(BF16)
    state_decay = jnp.exp(jnp.broadcast_to(last_e, (SSD_CHUNK, D_SSM)).T)
    bpad = jnp.zeros((SSD_CHUNK - r8, D_STATE), F32)
    for g in range(SSM_GROUPS):
        b128 = jnp.concatenate([bg[g], bpad], axis=0).astype(BF16)
        hout_ref[g * gw:(g + 1) * gw, :] = (
            state_decay[g * gw:(g + 1) * gw, :] * state[g * gw:(g + 1) * gw, :]
            + _dot(xw_t[g * gw:(g + 1) * gw, :], b128))


def _ssd_param_specs():
    return [_resident((CONV_WIDTH, CONV_DIM)), _resident((1, CONV_DIM)), _resident((1, LANES)),
            _resident((1, LANES)), _resident((1, D_SSM)), _resident((1, D_SSM)),
            _resident((LANES, D_SSM))]


SSD_SAMPLE_SEQS = 8


def _ssd_sample(z, xbc, dt_raw, conv0, h0, ssd_params, rows):
    batch = h0.shape[0]
    nb = SSD_SAMPLE_SEQS
    tok = lambda w: pl.BlockSpec((nb * rows, w), lambda i: (i, 0))
    state = pl.BlockSpec((nb, D_SSM, D_STATE), lambda i: (i, 0, 0))
    return pl.pallas_call(
        functools.partial(_ssd_sample_kernel, rows=rows),
        grid=(batch // nb,),
        in_specs=[tok(D_SSM), tok(CONV_DIM), tok(LANES),
                  pl.BlockSpec((CONV_WIDTH - 1, nb, CONV_DIM), lambda i: (0, i, 0)), state] + _ssd_param_specs(),
        out_specs=[tok(D_SSM), state],
        out_shape=[jax.ShapeDtypeStruct((batch * rows, D_SSM), F32),
                   jax.ShapeDtypeStruct((batch, D_SSM, D_STATE), F32)],
        scratch_shapes=[pltpu.VMEM((nb, 16, CONV_DIM), F32), pltpu.VMEM((nb, 8, D_SSM), F32),
                        pltpu.VMEM((nb, 8, LANES), F32)],
        compiler_params=pltpu.CompilerParams(
            dimension_semantics=("arbitrary",), vmem_limit_bytes=VMEM_LIMIT),
        name="ssd_sample",
    )(z, xbc, dt_raw, conv0, h0, *ssd_params)


def _t5_bucket_np(dist):
    dist = np.asarray(dist, np.int64)
    df = np.maximum(dist, 1).astype(np.float32)
    large = MAX_EXACT + (np.log(df / np.float32(MAX_EXACT)) / np.float32(math.log(MAX_WINDOW / MAX_EXACT))
                         * np.float32(N_BUCKETS - MAX_EXACT)).astype(np.int32)
    return np.where(dist < MAX_EXACT, dist, np.minimum(large, N_BUCKETS - 1)).astype(np.int32)


def _band_bucket_tables():
    ki = np.arange(SPAN)[None, :]
    out = np.empty((len(DILATIONS), 2, SPAN, SPAN), np.int32)
    for bi, d in enumerate(DILATIONS):
        nrun = DILATIONS[-1] // d
        row = np.arange(SPAN)
        qi = (nrun * (row % (SPAN // nrun)) + row // (SPAN // nrun))[:, None]
        rel_prev = qi + SPAN - ki
        rel_cur = qi - ki
        out[bi, 0] = np.where(rel_prev <= SPAN, _t5_bucket_np(np.clip(rel_prev, 0, SPAN) * d), -1)
        out[bi, 1] = np.where(rel_cur >= 0, _t5_bucket_np(np.clip(rel_cur, 0, SPAN) * d), -1)
    return out


def _bias_table_kernel(tab_ref, rb_ref, out_ref, *, buckets):
    pair = pl.program_id(0)
    for bi in range(len(DILATIONS)):
        for half in range(2):
            tab = tab_ref[bi, half]
            acc = [jnp.full((SPAN, SPAN), NEG, F32) for _ in range(2)]
            for bkt in buckets[bi][half]:
                hit = tab == bkt
                for s in range(2):
                    acc[s] = jnp.where(hit, rb_ref[bkt, 2 * pair + s] * LOG2E, acc[s])
            for s in range(2):
                rows = slice(s * SPAN, (s + 1) * SPAN)
                if half == 0:
                    out_ref[0, bi, 0, rows, 0:SPAN] = jnp.full((SPAN, SPAN), NEG, F32)
                    out_ref[0, bi, 1, rows, 0:SPAN] = acc[s]
                else:
                    out_ref[0, bi, 0, rows, SPAN:2 * SPAN] = acc[s]
                    out_ref[0, bi, 1, rows, SPAN:2 * SPAN] = acc[s]


def _bias_tables(rel_bias):
    tabs = _band_bucket_tables()
    buckets = [[sorted(int(b) for b in np.unique(tabs[bi, half]) if b >= 0) for half in range(2)]
               for bi in range(len(DILATIONS))]
    nd = len(DILATIONS)
    return pl.pallas_call(
        functools.partial(_bias_table_kernel, buckets=buckets),
        grid=(N_HEADS_A // 2,),
        in_specs=[_resident((nd, 2, SPAN, SPAN)),
                  pl.BlockSpec(memory_space=pltpu.SMEM)],
        out_specs=pl.BlockSpec((1, nd, 2, 2 * SPAN, 2 * SPAN), lambda p: (p, 0, 0, 0, 0)),
        out_shape=jax.ShapeDtypeStruct((N_HEADS_A // 2, nd, 2, 2 * SPAN, 2 * SPAN), F32),
        compiler_params=pltpu.CompilerParams(dimension_semantics=("arbitrary",)),
        name="bias_tables",
    )(jnp.asarray(tabs), rel_bias)


ATT_T = SPAN * DILATIONS[-1]
ATT_UNROLL = 8
ATT_SUB = 1


def _query_runs(d, r, n):
    run = SPAN * d // DILATIONS[-1]
    return [pl.ds(pl.multiple_of((r + d * c) * SPAN + run * n, 8), run) for c in range(DILATIONS[-1] // d)]


def _load_runs(ref, runs, *lead):
    parts = [ref[(*lead, rows, slice(None))] for rows in runs]
    return parts[0] if len(parts) == 1 else jnp.concatenate(parts, axis=0)


def _store_runs(ref, runs, value, *lead):
    run = value.shape[0] // len(runs)
    for c, rows in enumerate(runs):
        ref[(*lead, rows, slice(None))] = value[c * run:(c + 1) * run, :]


def _attn_kernel(q_ref, k_ref, v_ref, bias_ref, o_ref, qp_ref, tmp_ref, kph_ref, vph_ref, acc_ref, m_ref,
                 l_ref):
    t = pl.program_id(2)
    lane = lax.broadcasted_iota(jnp.int32, (1, LANES), 1)
    first_head = lane < HEAD_DIM
    head_mask = (first_head, jnp.logical_not(first_head))
    nblk = ATT_T // SPAN
    dmax = DILATIONS[-1]

    @pl.when(t == 0)
    def _():
        for bi, d in enumerate(DILATIONS):
            ph = ATT_T // d + SPAN
            for r in range(d):
                kph_ref[bi, r * ph:r * ph + SPAN, :] = jnp.zeros((SPAN, LANES), BF16)
                vph_ref[bi, r * ph:r * ph + SPAN, :] = jnp.zeros((SPAN, LANES), BF16)

    d4, d16 = DILATIONS[1], DILATIONS[2]
    assert DILATIONS == (1, d4, d4 * d4)
    l4, l16 = ATT_T // d4, ATT_T // d16
    for r4 in range(d4):
        tmp_ref[r4 * l4:(r4 + 1) * l4, :] = q_ref[pl.ds(r4, l4, stride=d4), :] * (ATTN_SCALE * LOG2E)
    for r16 in range(d16):
        qp_ref[r16 * l16:(r16 + 1) * l16, :] = tmp_ref[pl.ds((r16 % d4) * l4 + r16 // d4, l16, stride=d4), :]
    for src_ref, dst_ref in ((k_ref, kph_ref), (v_ref, vph_ref)):
        dst_ref[0, SPAN:SPAN + ATT_T, :] = src_ref[...].astype(BF16)
        ph4, ph16 = l4 + SPAN, l16 + SPAN
        for r4 in range(d4):
            x4 = src_ref[pl.ds(r4, l4, stride=d4), :]
            tmp_ref[r4 * l4:(r4 + 1) * l4, :] = x4
            dst_ref[1, r4 * ph4 + SPAN:(r4 + 1) * ph4, :] = x4.astype(BF16)
        for r16 in range(d16):
            x16 = tmp_ref[pl.ds((r16 % d4) * l4 + r16 // d4, l16, stride=d4), :]
            dst_ref[2, r16 * ph16 + SPAN:(r16 + 1) * ph16, :] = x16.astype(BF16)

    order = tuple(reversed(range(len(DILATIONS))))
    for bi in order:
        d = DILATIONS[bi]
        ln = ATT_T // d
        ph = ln + SPAN
        nb = ln // SPAN
        first_branch = bi == order[0]
        last_branch = bi == order[-1]

        def group(it, carry, bi=bi, d=d, ph=ph, nb=nb, first_branch=first_branch, last_branch=last_branch):
            def fetch(j):
                sub = []
                for g in range(j * ATT_SUB, (j + 1) * ATT_SUB):
                    blk = it * ATT_UNROLL + g
                    r = blk // nb
                    n = blk % nb
                    runs = _query_runs(d, r, n)
                    krow = pl.multiple_of(r * ph + n * SPAN, SPAN)
                    variant = jnp.logical_or(t > 0, n > 0).astype(jnp.int32)
                    qf = _load_runs(qp_ref, runs)
                    qs = jnp.concatenate([jnp.where(head_mask[s], qf, 0.0).astype(BF16) for s in range(2)],
                                         axis=0)
                    logits = _dot_nt(qs, kph_ref[bi, pl.ds(krow, 2 * SPAN), :]) + bias_ref[0, bi, variant]
                    old = None
                    if not first_branch:
                        old = ([_load_runs(m_ref, runs, s) for s in range(2)],
                               [_load_runs(l_ref, runs, s) for s in range(2)], _load_runs(acc_ref, runs))
                    sub.append((runs, krow, old, logits))
                return sub

            nsub = ATT_UNROLL // ATT_SUB
            nxt = fetch(0)
            for j in range(nsub):
                cur = nxt
                if j + 1 < nsub:
                    nxt = fetch(j + 1)
                for runs, m_new, l_new, pv in softmax_pv(cur):
                    if last_branch:
                        _store_runs(acc_ref, runs, pv / jnp.where(first_head, l_new[0], l_new[1]))
                    else:
                        for s in range(2):
                            _store_runs(m_ref, runs, m_new[s], s)
                            _store_runs(l_ref, runs, l_new[s], s)
                        _store_runs(acc_ref, runs, pv)
            return carry

        def softmax_pv(sub, bi=bi, first_branch=first_branch):
            soft = []
            for runs, krow, old, sc2 in sub:
                m_new, l_new, alpha, ps = [], [], [], []
                for s in range(2):
                    sc = sc2[s * SPAN:(s + 1) * SPAN, :]
                    halves = (sc[:, 0:SPAN], sc[:, SPAN:2 * SPAN])
                    mb = jnp.max(jnp.maximum(halves[0], halves[1]), axis=-1, keepdims=True)
                    if first_branch:
                        m = jnp.broadcast_to(mb, (SPAN, LANES))
                    else:
                        m = jnp.maximum(old[0][s], mb)
                        alpha.append(jnp.exp2(old[0][s] - m))
                    p = [jnp.exp2(h - m) for h in halves]
                    rs = jnp.sum(p[0] + p[1], axis=-1, keepdims=True)
                    ps.append(jnp.concatenate([p[0].astype(BF16), p[1].astype(BF16)], axis=1))
                    m_new.append(m)
                    l_new.append(jnp.broadcast_to(rs, (SPAN, LANES)) if first_branch
                                 else alpha[s] * old[1][s] + rs)
                soft.append((m_new, l_new, alpha, ps))
            out = []
            for (runs, krow, old, _), (m_new, l_new, alpha, ps) in zip(sub, soft):
                pv2 = _dot(jnp.concatenate(ps, axis=0), vph_ref[bi, pl.ds(krow, 2 * SPAN), :])
                pv = jnp.where(first_head, pv2[0:SPAN, :], pv2[SPAN:2 * SPAN, :])
                if not first_branch:
                    pv = jnp.where(first_head, alpha[0], alpha[1]) * old[2] + pv
                out.append((runs, m_new, l_new, pv))
            return out

        lax.fori_loop(0, nblk // ATT_UNROLL, group, 0)

    for r in range(dmax):
        o_ref[pl.ds(r, SPAN, stride=dmax), :] = acc_ref[r * SPAN:(r + 1) * SPAN, :]

    for bi, d in enumerate(DILATIONS):
        ln = ATT_T // d
        ph = ln + SPAN
        for r in range(d):
            kph_ref[bi, r * ph:r * ph + SPAN, :] = kph_ref[bi, (r + 1) * ph - SPAN:(r + 1) * ph, :]
            vph_ref[bi, r * ph:r * ph + SPAN, :] = vph_ref[bi, (r + 1) * ph - SPAN:(r + 1) * ph, :]


def _attn_prompt(qkv, bias, batch):
    m = qkv.shape[0]
    nt = m // batch // ATT_T
    npair = N_HEADS_A // 2
    nd = len(DILATIONS)
    blk = lambda off: pl.BlockSpec((ATT_T, LANES), lambda b, p, t: (b * nt + t, off + p))
    halo_rows = ATT_T + DILATIONS[-1] * SPAN
    return pl.pallas_call(
        _attn_kernel,
        grid=(batch, npair, nt),
        in_specs=[blk(0), blk(npair), blk(2 * npair),
                  pl.BlockSpec((1, nd, 2, 2 * SPAN, 2 * SPAN), lambda b, p, t: (p, 0, 0, 0, 0))],
        out_specs=blk(0),
        out_shape=jax.ShapeDtypeStruct((m, D_ATTN), F32),
        scratch_shapes=[pltpu.VMEM((ATT_T, LANES), F32),
                        pltpu.VMEM((ATT_T, LANES), F32),
                        pltpu.VMEM((nd, halo_rows, LANES), BF16),
                        pltpu.VMEM((nd, halo_rows, LANES), BF16),
                        pltpu.VMEM((ATT_T, LANES), F32),
                        pltpu.VMEM((2, ATT_T, LANES), F32),
                        pltpu.VMEM((2, ATT_T, LANES), F32)],
        compiler_params=pltpu.CompilerParams(
            dimension_semantics=("arbitrary", "arbitrary", "arbitrary"), vmem_limit_bytes=VMEM_LIMIT),
        name="attn_prompt",
    )(qkv, qkv, qkv, bias)


SAMPLE_ROWS = 4


def _sample_tables():
    pos = np.arange(MAX_WINDOW)[None, :]
    t = (np.arange(8) % SAMPLE_ROWS)[:, None]
    dist = MAX_WINDOW + t - pos
    mult = np.zeros(dist.shape, np.float32)
    for d in DILATIONS:
        mult += ((dist % d == 0) & (dist <= SPAN * d)).astype(np.float32)
    bucket = np.where(mult > 0, _t5_bucket_np(dist), -1).astype(np.int32)
    return bucket, np.concatenate([mult, mult], axis=0)


def _sample_bias_kernel(tab_ref, rb_ref, rbv_ref, bmain_ref, bnew_ref, *, buckets):
    pair = pl.program_id(0)
    tab = tab_ref[...]
    acc = [jnp.full(tab.shape, NEG, F32) for _ in range(2)]
    for bkt in buckets:
        hit = tab == bkt
        for s in range(2):
            acc[s] = jnp.where(hit, rb_ref[bkt, 2 * pair + s], acc[s])
    for s in range(2):
        bmain_ref[0, s * 8:(s + 1) * 8, :] = acc[s]
    t = lax.broadcasted_iota(jnp.int32, (8, 1), 0) % SAMPLE_ROWS
    for tp in range(SAMPLE_ROWS):
        b = jnp.full((8, LANES), NEG, F32)
        for dist in range(SAMPLE_ROWS):
            b = jnp.where(t - tp == dist, rbv_ref[dist:dist + 1, :], b)
        bnew_ref[tp] = b


def _sample_bias_tables(rel_bias):
    bucket, mult = _sample_tables()
    buckets = sorted(int(b) for b in np.unique(bucket) if b >= 0)
    npair = N_HEADS_A // 2
    rb_lanes = jnp.pad(rel_bias, ((0, 0), (0, LANES - N_HEADS_A)))
    bmain, bnew = pl.pallas_call(
        functools.partial(_sample_bias_kernel, buckets=buckets),
        grid=(npair,),
        in_specs=[_resident((8, MAX_WINDOW)), pl.BlockSpec(memory_space=pltpu.SMEM),
                  _resident((N_BUCKETS, LANES))],
        out_specs=[pl.BlockSpec((1, 16, MAX_WINDOW), lambda p: (p, 0, 0)),
                   pl.BlockSpec((SAMPLE_ROWS, 8, LANES), lambda p: (0, 0, 0))],
        out_shape=[jax.ShapeDtypeStruct((npair, 16, MAX_WINDOW), F32),
                   jax.ShapeDtypeStruct((SAMPLE_ROWS, 8, LANES), F32)],
        compiler_params=pltpu.CompilerParams(dimension_semantics=("arbitrary",)),
        name="sample_bias_tables",
    )(jnp.asarray(bucket), rel_bias, rb_lanes)
    return bmain, bnew, jnp.asarray(mult)


def _attn_sample_body(qkv_ref, kt_ref, vt_ref, bmain_ref, mult_ref, bnew_ref, e_ref, om_ref):
    rows = SAMPLE_ROWS
    npair = N_HEADS_A // 2
    second = pl.program_id(0) % 2 == 1
    lower = lax.broadcasted_iota(jnp.int32, (2 * rows, 1), 0) < rows
    mine = jnp.logical_xor(lower, second)
    blk = qkv_ref[...]
    q8_all = jnp.where(mine, blk, pltpu.roll(blk, rows, 0))
    q8 = q8_all[:, 0:D_ATTN] * ATTN_SCALE
    kn = q8_all[:, D_ATTN:2 * D_ATTN]
    vn = q8_all[:, 2 * D_ATTN:3 * D_ATTN]
    lane = lax.broadcasted_iota(jnp.int32, (1, LANES), 1)
    first_head = lane < HEAD_DIM
    e = e_ref[...]

    s_new = []
    m_tot = jnp.full((8, LANES), NEG, F32)
    for tp in range(rows):
        hi, lo, lo2 = _split3(q8 * kn[tp:tp + 1, :])
        s = _dot_nt(hi, e) + _dot_nt(lo, e) + _dot_nt(lo2, e) + bnew_ref[tp]
        s_new.append(s)
        m_tot = jnp.maximum(m_tot, s)

    mult = mult_ref[...]
    l_tot = jnp.zeros((8, LANES), F32)

    def window_logits(p):
        qp = q8[:, p * LANES:(p + 1) * LANES]
        qbd = jnp.concatenate([jnp.where(first_head, qp, 0.0), jnp.where(first_head, 0.0, qp)],
                              axis=0).astype(BF16)
        return _dot(qbd, kt_ref[0, p].astype(BF16)) + bmain_ref[p]

    s_next = window_logits(0)
    for p in range(npair):
        s16 = s_next
        if p + 1 < npair:
            s_next = window_logits(p + 1)
        m_new =jnp.concatenate([m_tot[:, 2 * p:2 * p + 1], m_tot[:, 2 * p + 1:2 * p + 2]], axis=0)
        m = jnp.maximum(jnp.max(s16, axis=-1, keepdims=True), m_new)
        pr = mult * jnp.exp(s16 - m)
        ls = jnp.sum(pr, axis=-1, keepdims=True)
        o16 = _dot_nt(pr.astype(BF16), vt_ref[0, p].astype(BF16))
        om_ref[:, p * LANES:(p + 1) * LANES] = jnp.where(first_head, o16[0:8], o16[8:16])
        m_tot = jnp.where(lane == 2 * p, m[0:8], jnp.where(lane == 2 * p + 1, m[8:16], m_tot))
        l_tot = jnp.where(lane == 2 * p, ls[0:8], jnp.where(lane == 2 * p + 1, ls[8:16], l_tot))
        yield

    t = lax.broadcasted_iota(jnp.int32, (8, 1), 0) % rows
    o = om_ref[...]
    for tp in range(rows):
        c = jnp.where(t == tp, float(len(DILATIONS)), jnp.where(t > tp, 1.0, 0.0))
        pn = c * jnp.exp(s_new[tp] - m_tot)
        l_tot = l_tot + pn
        o = o + _expand_heads(pn, e_ref) * vn[tp:tp + 1, :]
    o = o / _expand_heads(l_tot, e_ref)
    return o, second, lower


def _interleave(*gens):
    results = [None] * len(gens)
    live = list(range(len(gens)))
    while live:
        for k in list(live):
            try:
                next(gens[k])
            except StopIteration as stop:
                results[k] = stop.value
                live.remove(k)
    return results


def _sample_attn_prompt_ssd_kernel(qkv_ref, kt_ref, vt_ref, bmain_ref, mult_ref, bnew_ref, e_ref,
                                   z_ref, xbc_ref, dt_ref, conv0_ref, h0_ref, convw_ref, convb_ref, dtb_ref,
                                   alog_ref, dskip_ref, gnorm_ref, o_ref, y_ref, hout_ref,
                                   om_ref, xpad_ref, state_ref, *, chunks):
    c = pl.program_id(0) % chunks

    @pl.when(c == 0)
    def _():
        state_ref[...] = h0_ref[0]
        xpad_ref[8 - (CONV_WIDTH - 1):8, :] = conv0_ref[0]

    attn = _attn_sample_body(qkv_ref, kt_ref, vt_ref, bmain_ref, mult_ref, bnew_ref, e_ref, om_ref)
    scan = _ssd_chunk(z_ref.at[0], xbc_ref.at[0], dt_ref.at[0], y_ref.at[0], xpad_ref, state_ref,
                      convw_ref, convb_ref, dtb_ref, alog_ref, dskip_ref, gnorm_ref, e_ref)
    o, second, lower = _interleave(attn, scan)[0]

    @pl.when(jnp.logical_not(second))
    def _():
        o_ref[...] = o

    @pl.when(second)
    def _():
        o_ref[...] = jnp.where(lower, o_ref[...], o)

    @pl.when(c == chunks - 1)
    def _():
        hout_ref[0] = state_ref[...]


def _sample_attn_prompt_ssd(qkv, kt, vt, bmain, mult, bnew, expand, z, xbc, dt_raw, conv0, h0, ssd_params):
    rows = SAMPLE_ROWS
    batch = qkv.shape[0] // rows
    nseq, seq = z.shape[0], z.shape[1]
    chunks = seq // SSD_CHUNK
    assert batch == nseq * chunks, "one prompt chunk per sample sequence"
    npair = N_HEADS_A // 2
    cache = pl.BlockSpec((1, npair, LANES, MAX_WINDOW), lambda i: (i, 0, 0, 0))
    chunk = lambda w: pl.BlockSpec((1, SSD_CHUNK, w), lambda i: (i // chunks, i % chunks, 0))
    per_seq = lambda r, w: pl.BlockSpec((1, r, w), lambda i: (i // chunks, 0, 0))
    return pl.pallas_call(
        functools.partial(_sample_attn_prompt_ssd_kernel, chunks=chunks),
        grid=(batch,),
        in_specs=[pl.BlockSpec((2 * rows, 3 * D_ATTN), lambda i: (i // 2, 0)), cache, cache,
                  _resident((npair, 16, MAX_WINDOW)), _resident((16, MAX_WINDOW)),
                  _resident((rows, 8, LANES)), _resident((LANES, D_ATTN)),
                  chunk(D_SSM), chunk(CONV_DIM), chunk(LANES), per_seq(CONV_WIDTH - 1, CONV_DIM),
                  per_seq(D_SSM, D_STATE)] + _ssd_param_specs()[:-1],
        out_specs=[pl.BlockSpec((2 * rows, D_ATTN), lambda i: (i // 2, 0)), chunk(D_SSM),
                   per_seq(D_SSM, D_STATE)],
        out_shape=[jax.ShapeDtypeStruct((batch * rows, D_ATTN), F32),
                   jax.ShapeDtypeStruct((nseq, seq, D_SSM), F32),
                   jax.ShapeDtypeStruct((nseq, D_SSM, D_STATE), F32)],
        scratch_shapes=[pltpu.VMEM((8, D_ATTN), F32), pltpu.VMEM((8 + SSD_CHUNK, CONV_DIM), F32),
                        pltpu.VMEM((D_SSM, D_STATE), F32)],
        compiler_params=pltpu.CompilerParams(
            dimension_semantics=("arbitrary",), vmem_limit_bytes=VMEM_LIMIT),
        name="sample_attn_prompt_ssd",
    )(qkv, kt, vt, bmain, mult, bnew, expand, z, xbc, dt_raw, conv0, h0, *ssd_params[:-1])


def _head_expand_matrix():
    e = np.zeros((LANES, D_SSM), np.float32)
    for h in range(N_HEADS_S):
        e[h, h * SSM_HEAD_DIM:(h + 1) * SSM_HEAD_DIM] = 1.0
    return jnp.asarray(e, BF16)


def _prep_params(norm_mix, w_in, q_norm, k_norm, conv_w, conv_b, dt_bias, a_log, d_skip, ssm_norm,
                 w_out, norm_ffn, w_up, w_down):
    n_main = 3 * D_ATTN + D_SSM + CONV_DIM
    pad_heads = lambda v: jnp.pad(v.reshape(1, N_HEADS_S), ((0, 0), (0, LANES - N_HEADS_S)))
    return dict(
        gmix=norm_mix.reshape(1, D_MODEL),
        w_main=w_in.astype(BF16),
        w_dt=jnp.pad(w_in[:, n_main:], ((0, 0), (0, LANES - N_HEADS_S))).astype(BF16),
        gq=jnp.tile(q_norm, N_HEADS_A).reshape(1, D_ATTN),
        gk=jnp.tile(k_norm, N_HEADS_A).reshape(1, D_ATTN),
        conv_w=conv_w,
        conv_b=conv_b.reshape(1, CONV_DIM),
        dt_bias=pad_heads(dt_bias),
        a_log=pad_heads(a_log),
        d_skip_e=jnp.repeat(d_skip, SSM_HEAD_DIM).reshape(1, D_SSM),
        gnorm=ssm_norm.reshape(1, D_SSM),
        expand=_head_expand_matrix(),
        w_out=w_out.astype(BF16),
        gffn=norm_ffn.reshape(1, D_MODEL),
        w_up=w_up.astype(BF16),
        w_down=w_down.astype(BF16),
    )


def _ssd_param_list(p):
    return (p["conv_w"], p["conv_b"], p["dt_bias"], p["a_log"], p["d_skip_e"], p["gnorm"], p["expand"])


def kernel(x_prompt, x_sample, cache_attn_k, cache_attn_v, state_conv, state_ssm, norm_mix, w_in, q_norm,
           k_norm, rel_bias, conv_w, conv_b, dt_bias, a_log, d_skip, ssm_norm, w_out, norm_ffn, w_up, w_down):
    depth = w_in.shape[0]
    assert depth == 1, "single-layer decoder"
    l = 0
    p = _prep_params(norm_mix[l], w_in[l], q_norm[l], k_norm[l], conv_w[l], conv_b[l], dt_bias[l], a_log[l],
                     d_skip[l], ssm_norm[l], w_out[l], norm_ffn[l], w_up[l], w_down[l])
    ssd_params = _ssd_param_list(p)
    tm = 512

    bp, sp = x_prompt.shape[:2]
    keep = min(MAX_WINDOW, sp)
    xp = x_prompt.reshape(bp * sp, D_MODEL)
    qkv, z, xbc, dt_raw = _inproj(xp, p["gmix"], p["w_main"], p["w_dt"], p["gq"], p["gk"], tm)
    attn = _attn_prompt(qkv, _bias_tables(rel_bias), bp)
    qkv3 = qkv.reshape(bp, sp, 3 * D_ATTN)
    new_k_prompt = qkv3[:, sp - keep:, D_ATTN:2 * D_ATTN].reshape(depth, bp, keep, N_HEADS_A, HEAD_DIM)
    new_v_prompt = qkv3[:, sp - keep:, 2 * D_ATTN:].reshape(depth, bp, keep, N_HEADS_A, HEAD_DIM)
    new_conv_prompt = xbc.reshape(bp, sp, CONV_DIM)[:, sp - (CONV_WIDTH - 1):].reshape(
        depth, bp, CONV_WIDTH - 1, CONV_DIM)

    bs, ts = x_sample.shape[:2]
    assert cache_attn_k.shape[2] == MAX_WINDOW and ts == SAMPLE_ROWS
    xs = x_sample.reshape(bs * ts, D_MODEL)
    qkv_s, z_s, xbc_s, dt_s = _inproj(xs, p["gmix"], p["w_main"], p["w_dt"], p["gq"], p["gk"], tm)
    npair = N_HEADS_A // 2
    kt = jnp.transpose(cache_attn_k[l], (0, 2, 3, 1)).reshape(bs, npair, LANES, MAX_WINDOW)
    vt = jnp.transpose(cache_attn_v[l], (0, 2, 3, 1)).reshape(bs, npair, LANES, MAX_WINDOW)
    bmain, bnew, mult = _sample_bias_tables(rel_bias)
    conv0 = jnp.zeros((bp, CONV_WIDTH - 1, CONV_DIM), F32)
    h0 = jnp.zeros((bp, D_SSM, D_STATE), F32)
    attn_s, ssm_o, hst = _sample_attn_prompt_ssd(
        qkv_s, kt, vt, bmain, mult, bnew, p["expand"], z.reshape(bp, sp, D_SSM), xbc.reshape(bp, sp, CONV_DIM),
        dt_raw.reshape(bp, sp, LANES), conv0, h0, ssd_params)
    y_prompt = _outffn(xp, attn, ssm_o.reshape(bp * sp, D_SSM), p["w_out"], p["gffn"], p["w_up"], p["w_down"], tm)
    new_ssm_prompt = hst.reshape(depth, bp, N_HEADS_S, SSM_HEAD_DIM, D_STATE)
    ssm_os, hsts = _ssd_sample(z_s, xbc_s, dt_s, jnp.transpose(state_conv[l], (1, 0, 2)),
                               state_ssm[l].reshape(bs, D_SSM, D_STATE), ssd_params, ts)
    y_sample = _outffn(xs, attn_s, ssm_os, p["w_out"], p["gffn"], p["w_up"], p["w_down"], tm)
    new_k_sample = qkv_s[:, D_ATTN:2 * D_ATTN].reshape(depth, bs, ts, N_HEADS_A, HEAD_DIM)
    new_v_sample = qkv_s[:, 2 * D_ATTN:].reshape(depth, bs, ts, N_HEADS_A, HEAD_DIM)
    new_conv_sample = xbc_s.reshape(bs, ts, CONV_DIM)[:, ts - (CONV_WIDTH - 1):].reshape(
        depth, bs, CONV_WIDTH - 1, CONV_DIM)
    new_ssm_sample = hsts.reshape(depth, bs, N_HEADS_S, SSM_HEAD_DIM, D_STATE)

    return (y_prompt.reshape(bp, sp, D_MODEL), y_sample.reshape(bs, ts, D_MODEL),
            new_k_prompt, new_v_prompt, new_conv_prompt, new_ssm_prompt,
            new_k_sample, new_v_sample, new_conv_sample, new_ssm_sample)
```

```python
import functools
import math

import jax
import jax.numpy as jnp
import numpy as np
from jax import lax
from jax.experimental import pallas as pl
from jax.experimental.pallas import tpu as pltpu

F32 = jnp.float32
BF16 = jnp.bfloat16

D_MODEL = 1024
HEAD_DIM = 64
N_HEADS_A = 16
D_ATTN = 1024
SPAN = 128
DILATIONS = (1, 4, 16)
MAX_WINDOW = 2048
N_BUCKETS = 32
MAX_EXACT = 16
SSM_HEAD_DIM = 64
N_HEADS_S = 16
D_SSM = 1024
SSM_GROUPS = 2
D_STATE = 128
CONV_WIDTH = 4
CONV_DIM = D_SSM + 2 * SSM_GROUPS * D_STATE
SSD_CHUNK = 128
D_FF = 4096
EPS = 1e-6
ATTN_SCALE = HEAD_DIM ** -0.5
LOG2E = 1.4426950408889634

LANES = 128
NEG = -1e30
VMEM_LIMIT = 56 * 1024 * 1024


def _resident(shape):
    nd = len(shape)
    return pl.BlockSpec(shape, lambda *_: (0,) * nd, pipeline_mode=pl.Buffered(1))


def _split3(x):
    hi = x.astype(BF16)
    r1 = x - hi.astype(F32)
    lo = r1.astype(BF16)
    lo2 = (r1 - lo.astype(F32)).astype(BF16)
    return hi, lo, lo2


def _dot(a, b):
    return jnp.dot(a, b, preferred_element_type=F32)


def _dot_nt(a, b):
    return lax.dot_general(a, b, (((1,), (1,)), ((), ())), preferred_element_type=F32)


def _sigmoid(x):
    return 0.5 * jnp.tanh(0.5 * x) + 0.5


def _softplus(x):
    return jnp.maximum(x, 0.0) + jnp.log1p(jnp.exp(-jnp.abs(x)))


IN_TN = 512


def _inproj_kernel(x_ref, gmix_ref, w_ref, wdt_ref, gq_ref, gk_ref,
                   qkv_ref, z_ref, xbc_ref, dt_ref):
    x = x_ref[...]
    ms = jnp.mean(x * x, axis=-1, keepdims=True)
    xn = ((x * lax.rsqrt(ms + EPS)) * gmix_ref[...]).astype(BF16)
    first_head = lax.broadcasted_iota(jnp.int32, (1, LANES), 1) < HEAD_DIM

    n_main = 3 * D_ATTN + D_SSM + CONV_DIM
    for j in range(n_main // IN_TN):
        c0 = j * IN_TN
        t = _dot(xn, w_ref[:, c0:c0 + IN_TN])
        if c0 < 2 * D_ATTN:
            g_ref = gq_ref if c0 < D_ATTN else gk_ref
            g0 = c0 % D_ATTN
            for c in range(IN_TN // LANES):
                tc = t[:, c * LANES:(c + 1) * LANES]
                s = tc * tc
                sa = jnp.sum(jnp.where(first_head, s, 0.0), axis=-1, keepdims=True)
                sb = jnp.sum(jnp.where(first_head, 0.0, s), axis=-1, keepdims=True)
                inv = lax.rsqrt(jnp.where(first_head, sa, sb) * (1.0 / HEAD_DIM) + EPS)
                qkv_ref[:, c0 + c * LANES:c0 + (c + 1) * LANES] = (
                    (tc * inv) * g_ref[:, g0 + c * LANES:g0 + (c + 1) * LANES])
        elif c0 < 3 * D_ATTN:
            qkv_ref[:, c0:c0 + IN_TN] = t
        elif c0 < 3 * D_ATTN + D_SSM:
            z_ref[:, c0 - 3 * D_ATTN:c0 - 3 * D_ATTN + IN_TN] = t
        else:
            o0 = c0 - 3 * D_ATTN - D_SSM
            xbc_ref[:, o0:o0 + IN_TN] = t
    dt_ref[...] = _dot(xn, wdt_ref[...])


def _inproj(x2d, gmix, w_main, w_dt, gq, gk, tm):
    m = x2d.shape[0]
    row = lambda w: pl.BlockSpec((tm, w), lambda i: (i, 0))
    return pl.pallas_call(
        _inproj_kernel,
        grid=(m // tm,),
        in_specs=[row(D_MODEL), _resident((1, D_MODEL)), _resident(w_main.shape),
                  _resident((D_MODEL, LANES)), _resident((1, D_ATTN)), _resident((1, D_ATTN))],
        out_specs=[row(3 * D_ATTN), row(D_SSM), row(CONV_DIM), row(LANES)],
        out_shape=[jax.ShapeDtypeStruct((m, 3 * D_ATTN), F32),
                   jax.ShapeDtypeStruct((m, D_SSM), F32),
                   jax.ShapeDtypeStruct((m, CONV_DIM), F32),
                   jax.ShapeDtypeStruct((m, LANES), F32)],
        compiler_params=pltpu.CompilerParams(
            dimension_semantics=("arbitrary",), vmem_limit_bytes=VMEM_LIMIT),
        name="inproj",
    )(x2d, gmix, w_main, w_dt, gq, gk)


FFN_TF = 1024


def _outffn_kernel(x_ref, attn_ref, ssm_ref, wout_ref, gffn_ref, wup_ref, wdown_ref, y_ref):
    a = attn_ref[...].astype(BF16)
    s = ssm_ref[...].astype(BF16)
    h = x_ref[...] + _dot(a, wout_ref[0:D_ATTN, :]) + _dot(s, wout_ref[D_ATTN:D_ATTN + D_SSM, :])
    ms = jnp.mean(h * h, axis=-1, keepdims=True)
    hn = ((h * lax.rsqrt(ms + EPS)) * gffn_ref[...]).astype(BF16)
    acc = None
    for f in range(D_FF // FFN_TF):
        u = _dot(hn, wup_ref[:, f * FFN_TF:(f + 1) * FFN_TF])
        u = jnp.maximum(u, 0.0)
        u = (u * u).astype(BF16)
        d = _dot(u, wdown_ref[f * FFN_TF:(f + 1) * FFN_TF, :])
        acc = d if acc is None else acc + d
    y_ref[...] = h + acc


def _outffn(x2d, attn, ssm_o, w_out, gffn, w_up, w_down, tm):
    m = x2d.shape[0]
    row = pl.BlockSpec((tm, D_MODEL), lambda i: (i, 0))
    return pl.pallas_call(
        _outffn_kernel,
        grid=(m // tm,),
        in_specs=[row, row, row, _resident((D_ATTN + D_SSM, D_MODEL)), _resident((1, D_MODEL)),
                  _resident((D_MODEL, D_FF)), _resident((D_FF, D_MODEL))],
        out_specs=row,
        out_shape=jax.ShapeDtypeStruct((m, D_MODEL), F32),
        compiler_params=pltpu.CompilerParams(
            dimension_semantics=("arbitrary",), vmem_limit_bytes=VMEM_LIMIT),
        name="outffn",
    )(x2d, attn, ssm_o, w_out, gffn, w_up, w_down)


def _expand_heads(v, e_ref):
    hi, lo, lo2 = _split3(v)
    e = e_ref[...]
    return _dot(hi, e) + _dot(lo, e) + _dot(lo2, e)


def _ssd_chunk(z_ref, xbc_ref, dt_ref, y_ref, xpad_ref, state_ref,
               convw_ref, convb_ref, dtb_ref, alog_ref, dskip_ref, gnorm_ref, e_ref):
    q = SSD_CHUNK
    pad = 8
    tail = CONV_WIDTH - 1

    xbc = xbc_ref[...]
    xpad_ref[pad:pad + q, :] = xbc
    conv = convb_ref[...] + convw_ref[tail:tail + 1, :] * xbc
    for i in range(tail):
        conv = conv + convw_ref[i:i + 1, :] * xpad_ref[pad - tail + i:pad - tail + i + q, :]
    xpad_ref[pad - tail:pad, :] = xbc[q - tail:q, :]
    xc = conv * _sigmoid(conv)
    xs = xc[:, 0:D_SSM]

    lane = lax.broadcasted_iota(jnp.int32, (1, LANES), 1)
    dt = jnp.where(lane < N_HEADS_S, _softplus(dt_ref[...] + dtb_ref[...]), 0.0)
    da = dt * (-jnp.exp(alog_ref[...]))

    ii = lax.broadcasted_iota(jnp.int32, (q, q), 0)
    jj = lax.broadcasted_iota(jnp.int32, (q, q), 1)
    causal = ii >= jj
    tri = jnp.where(causal, 1.0, 0.0).astype(BF16)
    hi, lo, lo2 = _split3(da)
    cum = _dot(tri, hi) + _dot(tri, lo) + _dot(tri, lo2)
    cum_t = cum.T

    cum_e = _expand_heads(cum, e_ref)
    last_e = cum_e[q - 1:q, :]
    xdt = xs * _expand_heads(dt, e_ref)
    xw_t = (xdt * jnp.exp(last_e - cum_e)).T.astype(BF16)
    state_decay = jnp.exp(jnp.broadcast_to(last_e, (q, D_SSM)).T)
    off_scale = jnp.exp(cum_e)

    first_head = lane < SSM_HEAD_DIM
    heads_per_group = N_HEADS_S // SSM_GROUPS
    gw = heads_per_group * SSM_HEAD_DIM
    state = state_ref[...]
    state_bf = state.astype(BF16)
    yield
    for g in range(SSM_GROUPS):
        bg = xc[:, D_SSM + g * D_STATE:D_SSM + (g + 1) * D_STATE].astype(BF16)
        cg = xc[:, D_SSM + (SSM_GROUPS + g) * D_STATE:D_SSM + (SSM_GROUPS + g + 1) * D_STATE].astype(BF16)
        cb = _dot_nt(cg, bg)
        y_off = _dot_nt(cg, state_bf[g * gw:(g + 1) * gw, :]) * off_scale[:, g * gw:(g + 1) * gw]
        y_ref[:, g * gw:(g + 1) * gw] = y_off
        for hp in range(heads_per_group // 2):
            pair = g * (heads_per_group // 2) + hp
            xp = xdt[:, pair * LANES:(pair + 1) * LANES]
            yp = None
            for s in range(2):
                h = 2 * pair + s
                seg = cum[:, h:h + 1] - cum_t[h:h + 1, :]
                dec = jnp.exp(jnp.where(causal, seg, NEG))
                m = (cb * dec).astype(BF16)
                xm = jnp.where(first_head if s == 0 else jnp.logical_not(first_head), xp, 0.0).astype(BF16)
                d = _dot(m, xm)
                yp = d if yp is None else yp + d
            y_ref[:, pair * LANES:(pair + 1) * LANES] += yp
            yield
        state_ref[g * gw:(g + 1) * gw, :] = (
            state_decay[g * gw:(g + 1) * gw, :] * state[g * gw:(g + 1) * gw, :]
            + _dot(xw_t[g * gw:(g + 1) * gw, :], bg))

    z = z_ref[...]
    y = (y_ref[...] + xs * dskip_ref[...]) * (z * _sigmoid(z))
    for g in range(SSM_GROUPS):
        yg = y[:, g * gw:(g + 1) * gw]
        ms = jnp.mean(yg * yg, axis=-1, keepdims=True)
        y_ref[:, g * gw:(g + 1) * gw] = (yg * lax.rsqrt(ms + EPS)) * gnorm_ref[:, g * gw:(g + 1) * gw]


def _ssd_sample_kernel(z_ref, xbc_ref, dt_ref, conv0_ref, h0_ref, convw_ref, convb_ref, dtb_ref,
                       alog_ref, dskip_ref, gnorm_ref, e_ref, y_ref, hout_ref, xs_ref, z8_ref, dt8_ref,
                       *, rows):
    for b in range(h0_ref.shape[0]):
        tok = pl.ds(b * rows, rows)
        _ssd_sample_one(z_ref.at[tok], xbc_ref.at[tok], dt_ref.at[tok], conv0_ref.at[:, b], h0_ref.at[b],
                        convw_ref, convb_ref, dtb_ref, alog_ref, dskip_ref, gnorm_ref, e_ref, y_ref.at[tok],
                        hout_ref.at[b], xs_ref.at[b], z8_ref.at[b], dt8_ref.at[b], rows)


def _ssd_sample_one(z_ref, xbc_ref, dt_ref, conv0_ref, h0_ref, convw_ref, convb_ref, dtb_ref,
                    alog_ref, dskip_ref, gnorm_ref, e_ref, y_ref, hout_ref, xs_ref, z8_ref, dt8_ref, rows):
    tail = CONV_WIDTH - 1
    r8 = 8
    row = lax.broadcasted_iota(jnp.int32, (r8, 1), 0)
    lane = lax.broadcasted_iota(jnp.int32, (1, LANES), 1)

    xs_ref[r8:2 * r8, :] = jnp.zeros((r8, CONV_DIM), F32)
    xs_ref[r8:r8 + rows, :] = xbc_ref[...]
    xs_ref[r8 - tail:r8, :] = conv0_ref[...]
    z8_ref[...] = jnp.zeros((r8, D_SSM), F32)
    z8_ref[0:rows, :] = z_ref[...]
    dt8_ref[...] = jnp.zeros((r8, LANES), F32)
    dt8_ref[0:rows, :] = dt_ref[...]

    conv = convb_ref[...] + convw_ref[tail:tail + 1, :] * xs_ref[r8:2 * r8, :]
    for i in range(tail):
        conv = conv + convw_ref[i:i + 1, :] * xs_ref[r8 - tail + i:2 * r8 - tail + i, :]
    xc = conv * _sigmoid(conv)
    xs = xc[:, 0:D_SSM]

    live = jnp.logical_and(lane < N_HEADS_S, row < rows)
    dt = jnp.where(live, _softplus(dt8_ref[...] + dtb_ref[...]), 0.0)
    da = dt * (-jnp.exp(alog_ref[...]))
    cum = da
    for k in range(1, rows):
        cum = cum + pltpu.roll(da, k, 0)
    last = rows - 1

    heads_per_group = N_HEADS_S // SSM_GROUPS
    gw = heads_per_group * SSM_HEAD_DIM
    bg = [xc[:, D_SSM + g * D_STATE:D_SSM + (g + 1) * D_STATE] for g in range(SSM_GROUPS)]
    cg = [xc[:, D_SSM + (SSM_GROUPS + g) * D_STATE:D_SSM + (SSM_GROUPS + g + 1) * D_STATE]
          for g in range(SSM_GROUPS)]
    group0 = lane < heads_per_group
    w = []
    for j in range(rows):
        cb = [jnp.sum(cg[g] * bg[g][j:j + 1, :], axis=-1, keepdims=True) for g in range(SSM_GROUPS)]
        dec = jnp.exp(jnp.where(row >= j, cum - cum[j:j + 1, :], NEG))
        w.append(jnp.where(group0, cb[0], cb[1]) * dec)

    stack = jnp.concatenate([cum, dt] + w, axis=0)
    hi, lo, lo2 = _split3(stack)
    ex = _dot(jnp.concatenate([hi, lo, lo2], axis=0), e_ref[...])
    ns = stack.shape[0]
    ex = ex[0:ns] + ex[ns:2 * ns] + ex[2 * ns:3 * ns]
    cum_e = ex[0:r8]
    xdt = xs * ex[r8:2 * r8]
    y = None
    for j in range(rows):
        term = ex[(2 + j) * r8:(3 + j) * r8] * xdt[j:j + 1, :]
        y = term if y is None else y + term
    last_e = cum_e[last:last + 1, :]

    state = h0_ref[...]
    state_bf = state.astype(BF16)
    zeros8 = jnp.zeros((r8, D_STATE), F32)
    y_off = [_dot_nt(jnp.concatenate([cg[g], zeros8], axis=0).astype(BF16),
                     state_bf[g * gw:(g + 1) * gw, :])[0:r8, :] for g in range(SSM_GROUPS)]
    y = y + jnp.concatenate(y_off, axis=1) * jnp.exp(cum_e)

    z = z8_ref[...]
    y = (y + xs * dskip_ref[...]) * (z * _sigmoid(z))
    outs = []
    for g in range(SSM_GROUPS):
        yg = y[:, g * gw:(g + 1) * gw]
        ms = jnp.mean(yg * yg, axis=-1, keepdims=True)
        outs.append((yg * lax.rsqrt(ms + EPS)) * gnorm_ref[:, g * gw:(g + 1) * gw])
    y_ref[...] = jnp.concatenate(outs, axis=1)[0:rows, :]

    pad = jnp.zeros((SSD_CHUNK - r8, D_SSM), F32)
    xw_t = jnp.concatenate([xdt * jnp.exp(last_e - cum_e), pad], axis=0).T.astype(BF16)
    state_decay = jnp.exp(jnp.broadcast_to(last_e, (SSD_CHUNK, D_SSM)).T)
    bpad = jnp.zeros((SSD_CHUNK - r8, D_STATE), F32)
    for g in range(SSM_GROUPS):
        b128 = jnp.concatenate([bg[g], bpad], axis=0).astype(BF16)
        hout_ref[g * gw:(g + 1) * gw, :] = (
            state_decay[g * gw:(g + 1) * gw, :] * state[g * gw:(g + 1) * gw, :]
            + _dot(xw_t[g * gw:(g + 1) * gw, :], b128))


def _ssd_param_specs():
    return [_resident((CONV_WIDTH, CONV_DIM)), _resident((1, CONV_DIM)), _resident((1, LANES)),
            _resident((1, LANES)), _resident((1, D_SSM)), _resident((1, D_SSM)),
            _resident((LANES, D_SSM))]


SSD_SAMPLE_SEQS = 8


def _ssd_sample(z, xbc, dt_raw, conv0, h0, ssd_params, rows):
    batch = h0.shape[0]
    nb = SSD_SAMPLE_SEQS
    tok = lambda w: pl.BlockSpec((nb * rows, w), lambda i: (i, 0))
    state = pl.BlockSpec((nb, D_SSM, D_STATE), lambda i: (i, 0, 0))
    return pl.pallas_call(
        functools.partial(_ssd_sample_kernel, rows=rows),
        grid=(batch // nb,),
        in_specs=[tok(D_SSM), tok(CONV_DIM), tok(LANES),
                  pl.BlockSpec((CONV_WIDTH - 1, nb, CONV_DIM), lambda i: (0, i, 0)), state] + _ssd_param_specs(),
        out_specs=[tok(D_SSM), state],
        out_shape=[jax.ShapeDtypeStruct((batch * rows, D_SSM), F32),
                   jax.ShapeDtypeStruct((batch, D_SSM, D_STATE), F32)],
        scratch_shapes=[pltpu.VMEM((nb, 16, CONV_DIM), F32), pltpu.VMEM((nb, 8, D_SSM), F32),
                        pltpu.VMEM((nb, 8, LANES), F32)],
        compiler_params=pltpu.CompilerParams(
            dimension_semantics=("arbitrary",), vmem_limit_bytes=VMEM_LIMIT),
        name="ssd_sample",
    )(z, xbc, dt_raw, conv0, h0, *ssd_params)


def _t5_bucket_np(dist):
    dist = np.asarray(dist, np.int64)
    df = np.maximum(dist, 1).astype(np.float32)
    large = MAX_EXACT + (np.log(df / np.float32(MAX_EXACT)) / np.float32(math.log(MAX_WINDOW / MAX_EXACT))
                         * np.float32(N_BUCKETS - MAX_EXACT)).astype(np.int32)
    return np.where(dist < MAX_EXACT, dist, np.minimum(large, N_BUCKETS - 1)).astype(np.int32)


def _band_bucket_tables():
    ki = np.arange(SPAN)[None, :]
    out = np.empty((len(DILATIONS), 2, SPAN, SPAN), np.int32)
    for bi, d in enumerate(DILATIONS):
        nrun = DILATIONS[-1] // d
        row = np.arange(SPAN)
        qi = (nrun * (row % (SPAN // nrun)) + row // (SPAN // nrun))[:, None]
        rel_prev = qi + SPAN - ki
        rel_cur = qi - ki
        out[bi, 0] = np.where(rel_prev <= SPAN, _t5_bucket_np(np.clip(rel_prev, 0, SPAN) * d), -1)
        out[bi, 1] = np.where(rel_cur >= 0, _t5_bucket_np(np.clip(rel_cur, 0, SPAN) * d), -1)
    return out


def _bias_table_kernel(tab_ref, rb_ref, out_ref, *, buckets):
    pair = pl.program_id(0)
    for bi in range(len(DILATIONS)):
        for half in range(2):
            tab = tab_ref[bi, half]
            acc = [jnp.full((SPAN, SPAN), NEG, F32) for _ in range(2)]
            for bkt in buckets[bi][half]:
                hit = tab == bkt
                for s in range(2):
                    acc[s] = jnp.where(hit, rb_ref[bkt, 2 * pair + s] * LOG2E, acc[s])
            for s in range(2):
                rows = slice(s * SPAN, (s + 1) * SPAN)
                if half == 0:
                    out_ref[0, bi, 0, rows, 0:SPAN] = jnp.full((SPAN, SPAN), NEG, F32)
                    out_ref[0, bi, 1, rows, 0:SPAN] = acc[s]
                else:
                    out_ref[0, bi, 0, rows, SPAN:2 * SPAN] = acc[s]
                    out_ref[0, bi, 1, rows, SPAN:2 * SPAN] = acc[s]


def _bias_tables(rel_bias):
    tabs = _band_bucket_tables()
    buckets = [[sorted(int(b) for b in np.unique(tabs[bi, half]) if b >= 0) for half in range(2)]
               for bi in range(len(DILATIONS))]
    nd = len(DILATIONS)
    return pl.pallas_call(
        functools.partial(_bias_table_kernel, buckets=buckets),
        grid=(N_HEADS_A // 2,),
        in_specs=[_resident((nd, 2, SPAN, SPAN)),
                  pl.BlockSpec(memory_space=pltpu.SMEM)],
        out_specs=pl.BlockSpec((1, nd, 2, 2 * SPAN, 2 * SPAN), lambda p: (p, 0, 0, 0, 0)),
        out_shape=jax.ShapeDtypeStruct((N_HEADS_A // 2, nd, 2, 2 * SPAN, 2 * SPAN), F32),
        compiler_params=pltpu.CompilerParams(dimension_semantics=("arbitrary",)),
        name="bias_tables",
    )(jnp.asarray(tabs), rel_bias)


ATT_T = SPAN * DILATIONS[-1]
ATT_UNROLL = 16
ATT_SUB = 1


def _query_runs(d, r, n):
    run = SPAN * d // DILATIONS[-1]
    return [pl.ds(pl.multiple_of((r + d * c) * SPAN + run * n, 8), run) for c in range(DILATIONS[-1] // d)]


def _load_runs(ref, runs, *lead):
    parts = [ref[(*lead, rows, slice(None))] for rows in runs]
    return parts[0] if len(parts) == 1 else jnp.concatenate(parts, axis=0)


def _store_runs(ref, runs, value, *lead):
    run = value.shape[0] // len(runs)
    for c, rows in enumerate(runs):
        ref[(*lead, rows, slice(None))] = value[c * run:(c + 1) * run, :]


def _attn_kernel(q_ref, k_ref, v_ref, bias_ref, o_ref, qp_ref, tmp_ref, kph_ref, vph_ref, acc_ref, m_ref,
                 l_ref):
    t = pl.program_id(2)
    lane = lax.broadcasted_iota(jnp.int32, (1, LANES), 1)
    first_head = lane < HEAD_DIM
    head_mask = (first_head, jnp.logical_not(first_head))
    nblk = ATT_T // SPAN
    dmax = DILATIONS[-1]

    @pl.when(t == 0)
    def _():
        for bi, d in enumerate(DILATIONS):
            ph = ATT_T // d + SPAN
            for r in range(d):
                kph_ref[bi, r * ph:r * ph + SPAN, :] = jnp.zeros((SPAN, LANES), BF16)
                vph_ref[bi, r * ph:r * ph + SPAN, :] = jnp.zeros((SPAN, LANES), BF16)

    d4, d16 = DILATIONS[1], DILATIONS[2]
    assert DILATIONS == (1, d4, d4 * d4)
    l4, l16 = ATT_T // d4, ATT_T // d16
    for r4 in range(d4):
        tmp_ref[r4 * l4:(r4 + 1) * l4, :] = q_ref[pl.ds(r4, l4, stride=d4), :] * (ATTN_SCALE * LOG2E)
    for r16 in range(d16):
        qp_ref[r16 * l16:(r16 + 1) * l16, :] = tmp_ref[pl.ds((r16 % d4) * l4 + r16 // d4, l16, stride=d4), :]
    for src_ref, dst_ref in ((k_ref, kph_ref), (v_ref, vph_ref)):
        dst_ref[0, SPAN:SPAN + ATT_T, :] = src_ref[...].astype(BF16)
        ph4, ph16 = l4 + SPAN, l16 + SPAN
        for r4 in range(d4):
            x4 = src_ref[pl.ds(r4, l4, stride=d4), :]
            tmp_ref[r4 * l4:(r4 + 1) * l4, :] = x4
            dst_ref[1, r4 * ph4 + SPAN:(r4 + 1) * ph4, :] = x4.astype(BF16)
        for r16 in range(d16):
            x16 = tmp_ref[pl.ds((r16 % d4) * l4 + r16 // d4, l16, stride=d4), :]
            dst_ref[2, r16 * ph16 + SPAN:(r16 + 1) * ph16, :] = x16.astype(BF16)

    order = tuple(reversed(range(len(DILATIONS))))
    for bi in order:
        d = DILATIONS[bi]
        ln = ATT_T // d
        ph = ln + SPAN
        nb = ln // SPAN
        first_branch = bi == order[0]
        last_branch = bi == order[-1]

        def group(it, carry, bi=bi, d=d, ph=ph, nb=nb, first_branch=first_branch, last_branch=last_branch):
            def fetch(j):
                sub = []
                for g in range(j * ATT_SUB, (j + 1) * ATT_SUB):
                    blk = it * ATT_UNROLL + g
                    r = blk // nb
                    n = blk % nb
                    runs = _query_runs(d, r, n)
                    krow = pl.multiple_of(r * ph + n * SPAN, SPAN)
                    variant = jnp.logical_or(t > 0, n > 0).astype(jnp.int32)
                    qf = _load_runs(qp_ref, runs)
                    qs = jnp.concatenate([jnp.where(head_mask[s], qf, 0.0).astype(BF16) for s in range(2)],
                                         axis=0)
                    logits = _dot_nt(qs, kph_ref[bi, pl.ds(krow, 2 * SPAN), :]) + bias_ref[0, bi, variant]
                    old = None
                    if not first_branch:
                        old = ([_load_runs(m_ref, runs, s) for s in range(2)],
                               [_load_runs(l_ref, runs, s) for s in range(2)], _load_runs(acc_ref, runs))
                    sub.append((runs, krow, old, logits))
                return sub

            nsub = ATT_UNROLL // ATT_SUB
            nxt = fetch(0)
            for j in range(nsub):
                cur = nxt
                if j + 1 < nsub:
                    nxt = fetch(j + 1)
                for runs, m_new, l_new, pv in softmax_pv(cur):
                    if last_branch:
                        _store_runs(acc_ref, runs, pv / jnp.where(first_head, l_new[0], l_new[1]))
                    else:
                        for s in range(2):
                            _store_runs(m_ref, runs, m_new[s], s)
                            _store_runs(l_ref, runs, l_new[s], s)
                        _store_runs(acc_ref, runs, pv)
            return carry

        def softmax_pv(sub, bi=bi, first_branch=first_branch):
            soft = []
            for runs, krow, old, sc2 in sub:
                m_new, l_new, alpha, ps = [], [], [], []
                for s in range(2):
                    sc = sc2[s * SPAN:(s + 1) * SPAN, :]
                    halves = (sc[:, 0:SPAN], sc[:, SPAN:2 * SPAN])
                    mb = jnp.max(jnp.maximum(halves[0], halves[1]), axis=-1, keepdims=True)
                    if first_branch:
                        m = jnp.broadcast_to(mb, (SPAN, LANES))
                    else:
                        m = jnp.maximum(old[0][s], mb)
                        alpha.append(jnp.exp2(old[0][s] - m))
                    p = [jnp.exp2(h - m) for h in halves]
                    rs = jnp.sum(p[0] + p[1], axis=-1, keepdims=True)
                    ps.append(jnp.concatenate([p[0].astype(BF16), p[1].astype(BF16)], axis=1))
                    m_new.append(m)
                    l_new.append(jnp.broadcast_to(rs, (SPAN, LANES)) if first_branch
                                 else alpha[s] * old[1][s] + rs)
                soft.append((m_new, l_new, alpha, ps))
            out = []
            for (runs, krow, old, _), (m_new, l_new, alpha, ps) in zip(sub, soft):
                pv2 = _dot(jnp.concatenate(ps, axis=0), vph_ref[bi, pl.ds(krow, 2 * SPAN), :])
                pv = jnp.where(first_head, pv2[0:SPAN, :], pv2[SPAN:2 * SPAN, :])
                if not first_branch:
                    pv = jnp.where(first_head, alpha[0], alpha[1]) * old[2] + pv
                out.append((runs, m_new, l_new, pv))
            return out

        lax.fori_loop(0, nblk // ATT_UNROLL, group, 0)

    for r in range(dmax):
        o_ref[pl.ds(r, SPAN, stride=dmax), :] = acc_ref[r * SPAN:(r + 1) * SPAN, :]

    for bi, d in enumerate(DILATIONS):
        ln = ATT_T // d
        ph = ln + SPAN
        for r in range(d):
            kph_ref[bi, r * ph:r * ph + SPAN, :] = kph_ref[bi, (r + 1) * ph - SPAN:(r + 1) * ph, :]
            vph_ref[bi, r * ph:r * ph + SPAN, :] = vph_ref[bi, (r + 1) * ph - SPAN:(r + 1) * ph, :]


def _attn_prompt(qkv, bias, batch):
    m = qkv.shape[0]
    nt = m // batch // ATT_T
    npair = N_HEADS_A // 2
    nd = len(DILATIONS)
    blk = lambda off: pl.BlockSpec((ATT_T, LANES), lambda b, p, t: (b * nt + t, off + p))
    halo_rows = ATT_T + DILATIONS[-1] * SPAN
    return pl.pallas_call(
        _attn_kernel,
        grid=(batch, npair, nt),
        in_specs=[blk(0), blk(npair), blk(2 * npair),
                  pl.BlockSpec((1, nd, 2, 2 * SPAN, 2 * SPAN), lambda b, p, t: (p, 0, 0, 0, 0))],
        out_specs=blk(0),
        out_shape=jax.ShapeDtypeStruct((m, D_ATTN), F32),
        scratch_shapes=[pltpu.VMEM((ATT_T, LANES), F32),
                        pltpu.VMEM((ATT_T, LANES), F32),
                        pltpu.VMEM((nd, halo_rows, LANES), BF16),
                        pltpu.VMEM((nd, halo_rows, LANES), BF16),
                        pltpu.VMEM((ATT_T, LANES), F32),
                        pltpu.VMEM((2, ATT_T, LANES), F32),
                        pltpu.VMEM((2, ATT_T, LANES), F32)],
        compiler_params=pltpu.CompilerParams(
            dimension_semantics=("arbitrary", "arbitrary", "arbitrary"), vmem_limit_bytes=VMEM_LIMIT),
        name="attn_prompt",
    )(qkv, qkv, qkv, bias)


SAMPLE_ROWS = 4


def _sample_tables():
    pos = np.arange(MAX_WINDOW)[None, :]
    t = (np.arange(8) % SAMPLE_ROWS)[:, None]
    dist = MAX_WINDOW + t - pos
    mult = np.zeros(dist.shape, np.float32)
    for d in DILATIONS:
        mult += ((dist % d == 0) & (dist <= SPAN * d)).astype(np.float32)
    bucket = np.where(mult > 0, _t5_bucket_np(dist), -1).astype(np.int32)
    return bucket, np.concatenate([mult, mult], axis=0)


def _sample_bias_kernel(tab_ref, rb_ref, rbv_ref, bmain_ref, bnew_ref, *, buckets):
    pair = pl.program_id(0)
    tab = tab_ref[...]
    acc = [jnp.full(tab.shape, NEG, F32) for _ in range(2)]
    for bkt in buckets:
        hit = tab == bkt
        for s in range(2):
            acc[s] = jnp.where(hit, rb_ref[bkt, 2 * pair + s], acc[s])
    for s in range(2):
        bmain_ref[0, s * 8:(s + 1) * 8, :] = acc[s]
    t = lax.broadcasted_iota(jnp.int32, (8, 1), 0) % SAMPLE_ROWS
    for tp in range(SAMPLE_ROWS):
        b = jnp.full((8, LANES), NEG, F32)
        for dist in range(SAMPLE_ROWS):
            b = jnp.where(t - tp == dist, rbv_ref[dist:dist + 1, :], b)
        bnew_ref[tp] = b


def _sample_bias_tables(rel_bias):
    bucket, mult = _sample_tables()
    buckets = sorted(int(b) for b in np.unique(bucket) if b >= 0)
    npair = N_HEADS_A // 2
    rb_lanes = jnp.pad(rel_bias, ((0, 0), (0, LANES - N_HEADS_A)))
    bmain, bnew = pl.pallas_call(
        functools.partial(_sample_bias_kernel, buckets=buckets),
        grid=(npair,),
        in_specs=[_resident((8, MAX_WINDOW)), pl.BlockSpec(memory_space=pltpu.SMEM),
                  _resident((N_BUCKETS, LANES))],
        out_specs=[pl.BlockSpec((1, 16, MAX_WINDOW), lambda p: (p, 0, 0)),
                   pl.BlockSpec((SAMPLE_ROWS, 8, LANES), lambda p: (0, 0, 0))],
        out_shape=[jax.ShapeDtypeStruct((npair, 16, MAX_WINDOW), F32),
                   jax.ShapeDtypeStruct((SAMPLE_ROWS, 8, LANES), F32)],
        compiler_params=pltpu.CompilerParams(dimension_semantics=("arbitrary",)),
        name="sample_bias_tables",
    )(jnp.asarray(bucket), rel_bias, rb_lanes)
    return bmain, bnew, jnp.asarray(mult)


def _attn_sample_body(qkv_ref, kt_ref, vt_ref, bmain_ref, mult_ref, bnew_ref, e_ref, om_ref):
    rows = SAMPLE_ROWS
    npair = N_HEADS_A // 2
    second = pl.program_id(0) % 2 == 1
    lower = lax.broadcasted_iota(jnp.int32, (2 * rows, 1), 0) < rows
    mine = jnp.logical_xor(lower, second)
    blk = qkv_ref[...]
    q8_all = jnp.where(mine, blk, pltpu.roll(blk, rows, 0))
    q8 = q8_all[:, 0:D_ATTN] * ATTN_SCALE
    kn = q8_all[:, D_ATTN:2 * D_ATTN]
    vn = q8_all[:, 2 * D_ATTN:3 * D_ATTN]
    lane = lax.broadcasted_iota(jnp.int32, (1, LANES), 1)
    first_head = lane < HEAD_DIM
    e = e_ref[...]

    s_new = []
    m_tot = jnp.full((8, LANES), NEG, F32)
    for tp in range(rows):
        hi, lo, lo2 = _split3(q8 * kn[tp:tp + 1, :])
        s = _dot_nt(hi, e) + _dot_nt(lo, e) + _dot_nt(lo2, e) + bnew_ref[tp]
        s_new.append(s)
        m_tot = jnp.maximum(m_tot, s)

    mult = mult_ref[...]
    l_tot = jnp.zeros((8, LANES), F32)

    def window_logits(p):
        qp = q8[:, p * LANES:(p + 1) * LANES]
        qbd = jnp.concatenate([jnp.where(first_head, qp, 0.0), jnp.where(first_head, 0.0, qp)],
                              axis=0).astype(BF16)
        return _dot(qbd, kt_ref[0, p].astype(BF16)) + bmain_ref[p]

    s_next = window_logits(0)
    for p in range(npair):
        s16 = s_next
        if p + 1 < npair:
            s_next = window_logits(p + 1)
        m_new =jnp.concatenate([m_tot[:, 2 * p:2 * p + 1], m_tot[:, 2 * p + 1:2 * p + 2]], axis=0)
        m = jnp.maximum(jnp.max(s16, axis=-1, keepdims=True), m_new)
        pr = mult * jnp.exp(s16 - m)
        ls = jnp.sum(pr, axis=-1, keepdims=True)
        o16 = _dot_nt(pr.astype(BF16), vt_ref[0, p].astype(BF16))
        om_ref[:, p * LANES:(p + 1) * LANES] = jnp.where(first_head, o16[0:8], o16[8:16])
        m_tot = jnp.where(lane == 2 * p, m[0:8], jnp.where(lane == 2 * p + 1, m[8:16], m_tot))
        l_tot = jnp.where(lane == 2 * p, ls[0:8], jnp.where(lane == 2 * p + 1, ls[8:16], l_tot))
        yield

    t = lax.broadcasted_iota(jnp.int32, (8, 1), 0) % rows
    o = om_ref[...]
    for tp in range(rows):
        c = jnp.where(t == tp, float(len(DILATIONS)), jnp.where(t > tp, 1.0, 0.0))
        pn = c * jnp.exp(s_new[tp] - m_tot)
        l_tot = l_tot + pn
        o = o + _expand_heads(pn, e_ref) * vn[tp:tp + 1, :]
    o = o / _expand_heads(l_tot, e_ref)
    return o, second, lower


def _interleave(*gens):
    results = [None] * len(gens)
    live = list(range(len(gens)))
    while live:
        for k in list(live):
            try:
                next(gens[k])
            except StopIteration as stop:
                results[k] = stop.value
                live.remove(k)
    return results


def _sample_attn_prompt_ssd_kernel(qkv_ref, kt_ref, vt_ref, bmain_ref, mult_ref, bnew_ref, e_ref,
                                   z_ref, xbc_ref, dt_ref, conv0_ref, h0_ref, convw_ref, convb_ref, dtb_ref,
                                   alog_ref, dskip_ref, gnorm_ref, o_ref, y_ref, hout_ref,
                                   om_ref, xpad_ref, state_ref, *, chunks):
    c = pl.program_id(0) % chunks

    @pl.when(c == 0)
    def _():
        state_ref[...] = h0_ref[0]
        xpad_ref[8 - (CONV_WIDTH - 1):8, :] = conv0_ref[0]

    attn = _attn_sample_body(qkv_ref, kt_ref, vt_ref, bmain_ref, mult_ref, bnew_ref, e_ref, om_ref)
    scan = _ssd_chunk(z_ref.at[0], xbc_ref.at[0], dt_ref.at[0], y_ref.at[0], xpad_ref, state_ref,
                      convw_ref, convb_ref, dtb_ref, alog_ref, dskip_ref, gnorm_ref, e_ref)
    o, second, lower = _interleave(attn, scan)[0]

    @pl.when(jnp.logical_not(second))
    def _():
        o_ref[...] = o

    @pl.when(second)
    def _():
        o_ref[...] = jnp.where(lower, o_ref[...], o)

    @pl.when(c == chunks - 1)
    def _():
        hout_ref[0] = state_ref[...]


def _sample_attn_prompt_ssd(qkv, kt, vt, bmain, mult, bnew, expand, z, xbc, dt_raw, conv0, h0, ssd_params):
    rows = SAMPLE_ROWS
    batch = qkv.shape[0] // rows
    nseq, seq = z.shape[0], z.shape[1]
    chunks = seq // SSD_CHUNK
    assert batch == nseq * chunks, "one prompt chunk per sample sequence"
    npair = N_HEADS_A // 2
    cache = pl.BlockSpec((1, npair, LANES, MAX_WINDOW), lambda i: (i, 0, 0, 0))
    chunk = lambda w: pl.BlockSpec((1, SSD_CHUNK, w), lambda i: (i // chunks, i % chunks, 0))
    per_seq = lambda r, w: pl.BlockSpec((1, r, w), lambda i: (i // chunks, 0, 0))
    return pl.pallas_call(
        functools.partial(_sample_attn_prompt_ssd_kernel, chunks=chunks),
        grid=(batch,),
        in_specs=[pl.BlockSpec((2 * rows, 3 * D_ATTN), lambda i: (i // 2, 0)), cache, cache,
                  _resident((npair, 16, MAX_WINDOW)), _resident((16, MAX_WINDOW)),
                  _resident((rows, 8, LANES)), _resident((LANES, D_ATTN)),
                  chunk(D_SSM), chunk(CONV_DIM), chunk(LANES), per_seq(CONV_WIDTH - 1, CONV_DIM),
                  per_seq(D_SSM, D_STATE)] + _ssd_param_specs()[:-1],
        out_specs=[pl.BlockSpec((2 * rows, D_ATTN), lambda i: (i // 2, 0)), chunk(D_SSM),
                   per_seq(D_SSM, D_STATE)],
        out_shape=[jax.ShapeDtypeStruct((batch * rows, D_ATTN), F32),
                   jax.ShapeDtypeStruct((nseq, seq, D_SSM), F32),
                   jax.ShapeDtypeStruct((nseq, D_SSM, D_STATE), F32)],
        scratch_shapes=[pltpu.VMEM((8, D_ATTN), F32), pltpu.VMEM((8 + SSD_CHUNK, CONV_DIM), F32),
                        pltpu.VMEM((D_SSM, D_STATE), F32)],
        compiler_params=pltpu.CompilerParams(
            dimension_semantics=("arbitrary",), vmem_limit_bytes=VMEM_LIMIT),
        name="sample_attn_prompt_ssd",
    )(qkv, kt, vt, bmain, mult, bnew, expand, z, xbc, dt_raw, conv0, h0, *ssd_params[:-1])


def _head_expand_matrix():
    e = np.zeros((LANES, D_SSM), np.float32)
    for h in range(N_HEADS_S):
        e[h, h * SSM_HEAD_DIM:(h + 1) * SSM_HEAD_DIM] = 1.0
    return jnp.asarray(e, BF16)


def _prep_params(norm_mix, w_in, q_norm, k_norm, conv_w, conv_b, dt_bias, a_log, d_skip, ssm_norm,
                 w_out, norm_ffn, w_up, w_down):
    n_main = 3 * D_ATTN + D_SSM + CONV_DIM
    pad_heads = lambda v: jnp.pad(v.reshape(1, N_HEADS_S), ((0, 0), (0, LANES - N_HEADS_S)))
    return dict(
        gmix=norm_mix.reshape(1, D_MODEL),
        w_main=w_in.astype(BF16),
        w_dt=jnp.pad(w_in[:, n_main:], ((0, 0), (0, LANES - N_HEADS_S))).astype(BF16),
        gq=jnp.tile(q_norm, N_HEADS_A).reshape(1, D_ATTN),
        gk=jnp.tile(k_norm, N_HEADS_A).reshape(1, D_ATTN),
        conv_w=conv_w,
        conv_b=conv_b.reshape(1, CONV_DIM),
        dt_bias=pad_heads(dt_bias),
        a_log=pad_heads(a_log),
        d_skip_e=jnp.repeat(d_skip, SSM_HEAD_DIM).reshape(1, D_SSM),
        gnorm=ssm_norm.reshape(1, D_SSM),
        expand=_head_expand_matrix(),
        w_out=w_out.astype(BF16),
        gffn=norm_ffn.reshape(1, D_MODEL),
        w_up=w_up.astype(BF16),
        w_down=w_down.astype(BF16),
    )


def _ssd_param_list(p):
    return (p["conv_w"], p["conv_b"], p["dt_bias"], p["a_log"], p["d_skip_e"], p["gnorm"], p["expand"])


def kernel(x_prompt, x_sample, cache_attn_k, cache_attn_v, state_conv, state_ssm, norm_mix, w_in, q_norm,
           k_norm, rel_bias, conv_w, conv_b, dt_bias, a_log, d_skip, ssm_norm, w_out, norm_ffn, w_up, w_down):
    depth = w_in.shape[0]
    assert depth == 1, "single-layer decoder"
    l = 0
    p = _prep_params(norm_mix[l], w_in[l], q_norm[l], k_norm[l], conv_w[l], conv_b[l], dt_bias[l], a_log[l],
                     d_skip[l], ssm_norm[l], w_out[l], norm_ffn[l], w_up[l], w_down[l])
    ssd_params = _ssd_param_list(p)
    tm = 512

    bp, sp = x_prompt.shape[:2]
    keep = min(MAX_WINDOW, sp)
    xp = x_prompt.reshape(bp * sp, D_MODEL)
    qkv, z, xbc, dt_raw = _inproj(xp, p["gmix"], p["w_main"], p["w_dt"], p["gq"], p["gk"], tm)
    attn = _attn_prompt(qkv, _bias_tables(rel_bias), bp)
    qkv3 = qkv.reshape(bp, sp, 3 * D_ATTN)
    new_k_prompt = qkv3[:, sp - keep:, D_ATTN:2 * D_ATTN].reshape(depth, bp, keep, N_HEADS_A, HEAD_DIM)
    new_v_prompt = qkv3[:, sp - keep:, 2 * D_ATTN:].reshape(depth, bp, keep, N_HEADS_A, HEAD_DIM)
    new_conv_prompt = xbc.reshape(bp, sp, CONV_DIM)[:, sp - (CONV_WIDTH - 1):].reshape(
        depth, bp, CONV_WIDTH - 1, CONV_DIM)

    bs, ts = x_sample.shape[:2]
    assert cache_attn_k.shape[2] == MAX_WINDOW and ts == SAMPLE_ROWS
    xs = x_sample.reshape(bs * ts, D_MODEL)
    qkv_s, z_s, xbc_s, dt_s = _inproj(xs, p["gmix"], p["w_main"], p["w_dt"], p["gq"], p["gk"], tm)
    npair = N_HEADS_A // 2
    kt = jnp.transpose(cache_attn_k[l], (0, 2, 3, 1)).reshape(bs, npair, LANES, MAX_WINDOW)
    vt = jnp.transpose(cache_attn_v[l], (0, 2, 3, 1)).reshape(bs, npair, LANES, MAX_WINDOW)
    bmain, bnew, mult = _sample_bias_tables(rel_bias)
    conv0 = jnp.zeros((bp, CONV_WIDTH - 1, CONV_DIM), F32)
    h0 = jnp.zeros((bp, D_SSM, D_STATE), F32)
    attn_s, ssm_o, hst = _sample_attn_prompt_ssd(
        qkv_s, kt, vt, bmain, mult, bnew, p["expand"], z.reshape(bp, sp, D_SSM), xbc.reshape(bp, sp, CONV_DIM),
        dt_raw.reshape(bp, sp, LANES), conv0, h0, ssd_params)
    y_prompt = _outffn(xp, attn, ssm_o.reshape(bp * sp, D_SSM), p["w_out"], p["gffn"], p["w_up"], p["w_down"], tm)
    new_ssm_prompt = hst.reshape(depth, bp, N_HEADS_S, SSM_HEAD_DIM, D_STATE)
    ssm_os, hsts = _ssd_sample(z_s, xbc_s, dt_s, jnp.transpose(state_conv[l], (1, 0, 2)),
                               state_ssm[l].reshape(bs, D_SSM, D_STATE), ssd_params, ts)
    y_sample = _outffn(xs, attn_s, ssm_os, p["w_out"], p["gffn"], p["w_up"], p["w_down"], tm)
    new_k_sample = qkv_s[:, D_ATTN:2 * D_ATTN].reshape(depth, bs, ts, N_HEADS_A, HEAD_DIM)
    new_v_sample = qkv_s[:, 2 * D_ATTN:].reshape(depth, bs, ts, N_HEADS_A, HEAD_DIM)
    new_conv_sample = xbc_s.reshape(bs, ts, CONV_DIM)[:, ts - (CONV_WIDTH - 1):].reshape(
        depth, bs, CONV_WIDTH - 1, CONV_DIM)
    new_ssm_sample = hsts.reshape(depth, bs, N_HEADS_S, SSM_HEAD_DIM, D_STATE)

    return (y_prompt.reshape(bp, sp, D_MODEL), y_sample.reshape(bs, ts, D_MODEL),
            new_k_prompt, new_v_prompt, new_conv_prompt, new_ssm_prompt,
            new_k_sample, new_v_sample, new_conv_sample, new_ssm_sample)
```

```python
import functools
import math

import jax
import jax.numpy as jnp
import numpy as np
from jax import lax
from jax.experimental import pallas as pl
from jax.experimental.pallas import tpu as pltpu

F32 = jnp.float32
BF16 = jnp.bfloat16

D_MODEL = 1024
HEAD_DIM = 64
N_HEADS_A = 16
D_ATTN = 1024
SPAN = 128
DILATIONS = (1, 4, 16)
MAX_WINDOW = 2048
N_BUCKETS = 32
MAX_EXACT = 16
SSM_HEAD_DIM = 64
N_HEADS_S = 16
D_SSM = 1024
SSM_GROUPS = 2
D_STATE = 128
CONV_WIDTH = 4
CONV_DIM = D_SSM + 2 * SSM_GROUPS * D_STATE
SSD_CHUNK = 128
D_FF = 4096
EPS = 1e-6
ATTN_SCALE = HEAD_DIM ** -0.5
LOG2E = 1.4426950408889634

LANES = 128
NEG = -1e30
VMEM_LIMIT = 56 * 1024 * 1024


def _resident(shape):
    nd = len(shape)
    return pl.BlockSpec(shape, lambda *_: (0,) * nd, pipeline_mode=pl.Buffered(1))


def _split3(x):
    hi = x.astype(BF16)
    r1 = x - hi.astype(F32)
    lo = r1.astype(BF16)
    lo2 = (r1 - lo.astype(F32)).astype(BF16)
    return hi, lo, lo2


def _dot(a, b):
    return jnp.dot(a, b, preferred_element_type=F32)


def _dot_nt(a, b):
    return lax.dot_general(a, b, (((1,), (1,)), ((), ())), preferred_element_type=F32)


def _sigmoid(x):
    return 0.5 * jnp.tanh(0.5 * x) + 0.5


def _softplus(x):
    return jnp.maximum(x, 0.0) + jnp.log1p(jnp.exp(-jnp.abs(x)))


IN_TN = 512


def _inproj_kernel(x_ref, gmix_ref, w_ref, wdt_ref, gq_ref, gk_ref,
                   qkv_ref, z_ref, xbc_ref, dt_ref):
    x = x_ref[...]
    ms = jnp.mean(x * x, axis=-1, keepdims=True)
    xn = ((x * lax.rsqrt(ms + EPS)) * gmix_ref[...]).astype(BF16)
    first_head = lax.broadcasted_iota(jnp.int32, (1, LANES), 1) < HEAD_DIM

    n_main = 3 * D_ATTN + D_SSM + CONV_DIM
    for j in range(n_main // IN_TN):
        c0 = j * IN_TN
        t = _dot(xn, w_ref[:, c0:c0 + IN_TN])
        if c0 < 2 * D_ATTN:
            g_ref = gq_ref if c0 < D_ATTN else gk_ref
            g0 = c0 % D_ATTN
            for c in range(IN_TN // LANES):
                tc = t[:, c * LANES:(c + 1) * LANES]
                s = tc * tc
                sa = jnp.sum(jnp.where(first_head, s, 0.0), axis=-1, keepdims=True)
                sb = jnp.sum(jnp.where(first_head, 0.0, s), axis=-1, keepdims=True)
                inv = lax.rsqrt(jnp.where(first_head, sa, sb) * (1.0 / HEAD_DIM) + EPS)
                qkv_ref[:, c0 + c * LANES:c0 + (c + 1) * LANES] = (
                    (tc * inv) * g_ref[:, g0 + c * LANES:g0 + (c + 1) * LANES])
        elif c0 < 3 * D_ATTN:
            qkv_ref[:, c0:c0 + IN_TN] = t
        elif c0 < 3 * D_ATTN + D_SSM:
            z_ref[:, c0 - 3 * D_ATTN:c0 - 3 * D_ATTN + IN_TN] = t
        else:
            o0 = c0 - 3 * D_ATTN - D_SSM
            xbc_ref[:, o0:o0 + IN_TN] = t
    dt_ref[...] = _dot(xn, wdt_ref[...])


def _inproj(x2d, gmix, w_main, w_dt, gq, gk, tm):
    m = x2d.shape[0]
    row = lambda w: pl.BlockSpec((tm, w), lambda i: (i, 0))
    return pl.pallas_call(
        _inproj_kernel,
        grid=(m // tm,),
        in_specs=[row(D_MODEL), _resident((1, D_MODEL)), _resident(w_main.shape),
                  _resident((D_MODEL, LANES)), _resident((1, D_ATTN)), _resident((1, D_ATTN))],
        out_specs=[row(3 * D_ATTN), row(D_SSM), row(CONV_DIM), row(LANES)],
        out_shape=[jax.ShapeDtypeStruct((m, 3 * D_ATTN), F32),
                   jax.ShapeDtypeStruct((m, D_SSM), F32),
                   jax.ShapeDtypeStruct((m, CONV_DIM), F32),
                   jax.ShapeDtypeStruct((m, LANES), F32)],
        compiler_params=pltpu.CompilerParams(
            dimension_semantics=("arbitrary",), vmem_limit_bytes=VMEM_LIMIT),
        name="inproj",
    )(x2d, gmix, w_main, w_dt, gq, gk)


FFN_TF = 1024


def _outffn_kernel(x_ref, attn_ref, ssm_ref, wout_ref, gffn_ref, wup_ref, wdown_ref, y_ref):
    a = attn_ref[...].astype(BF16)
    s = ssm_ref[...].astype(BF16)
    h = x_ref[...] + _dot(a, wout_ref[0:D_ATTN, :]) + _dot(s, wout_ref[D_ATTN:D_ATTN + D_SSM, :])
    ms = jnp.mean(h * h, axis=-1, keepdims=True)
    hn = ((h * lax.rsqrt(ms + EPS)) * gffn_ref[...]).astype(BF16)
    acc = None
    for f in range(D_FF // FFN_TF):
        u = _dot(hn, wup_ref[:, f * FFN_TF:(f + 1) * FFN_TF])
        u = jnp.maximum(u, 0.0)
        u = (u * u).astype(BF16)
        d = _dot(u, wdown_ref[f * FFN_TF:(f + 1) * FFN_TF, :])
        acc = d if acc is None else acc + d
    y_ref[...] = h + acc


def _outffn(x2d, attn, ssm_o, w_out, gffn, w_up, w_down, tm):
    m = x2d.shape[0]
    row = pl.BlockSpec((tm, D_MODEL), lambda i: (i, 0))
    return pl.pallas_call(
        _outffn_kernel,
        grid=(m // tm,),
        in_specs=[row, row, row, _resident((D_ATTN + D_SSM, D_MODEL)), _resident((1, D_MODEL)),
                  _resident((D_MODEL, D_FF)), _resident((D_FF, D_MODEL))],
        out_specs=row,
        out_shape=jax.ShapeDtypeStruct((m, D_MODEL), F32),
        compiler_params=pltpu.CompilerParams(
            dimension_semantics=("arbitrary",), vmem_limit_bytes=VMEM_LIMIT),
        name="outffn",
    )(x2d, attn, ssm_o, w_out, gffn, w_up, w_down)


def _expand_heads(v, e_ref):
    hi, lo, lo2 = _split3(v)
    e = e_ref[...]
    return _dot(hi, e) + _dot(lo, e) + _dot(lo2, e)


def _ssd_chunk(z_ref, xbc_ref, dt_ref, y_ref, xpad_ref, state_ref,
               convw_ref, convb_ref, dtb_ref, alog_ref, dskip_ref, gnorm_ref, e_ref):
    q = SSD_CHUNK
    pad = 8
    tail = CONV_WIDTH - 1

    xbc = xbc_ref[...]
    xpad_ref[pad:pad + q, :] = xbc
    conv = convb_ref[...] + convw_ref[tail:tail + 1, :] * xbc
    for i in range(tail):
        conv = conv + convw_ref[i:i + 1, :] * xpad_ref[pad - tail + i:pad - tail + i + q, :]
    xpad_ref[pad - tail:pad, :] = xbc[q - tail:q, :]
    xc = conv * _sigmoid(conv)
    xs = xc[:, 0:D_SSM]

    lane = lax.broadcasted_iota(jnp.int32, (1, LANES), 1)
    dt = jnp.where(lane < N_HEADS_S, _softplus(dt_ref[...] + dtb_ref[...]), 0.0)
    da = dt * (-jnp.exp(alog_ref[...]))

    ii = lax.broadcasted_iota(jnp.int32, (q, q), 0)
    jj = lax.broadcasted_iota(jnp.int32, (q, q), 1)
    causal = ii >= jj
    tri = jnp.where(causal, 1.0, 0.0).astype(BF16)
    hi, lo, lo2 = _split3(da)
    cum = _dot(tri, hi) + _dot(tri, lo) + _dot(tri, lo2)
    cum_t = cum.T

    cum_e = _expand_heads(cum, e_ref)
    last_e = cum_e[q - 1:q, :]
    xdt = xs * _expand_heads(dt, e_ref)
    xw_t = (xdt * jnp.exp(last_e - cum_e)).T.astype(BF16)
    state_decay = jnp.exp(jnp.broadcast_to(last_e, (q, D_SSM)).T)
    off_scale = jnp.exp(cum_e)

    first_head = lane < SSM_HEAD_DIM
    heads_per_group = N_HEADS_S // SSM_GROUPS
    gw = heads_per_group * SSM_HEAD_DIM
    state = state_ref[...]
    state_bf = state.astype(BF16)
    yield
    for g in range(SSM_GROUPS):
        bg = xc[:, D_SSM + g * D_STATE:D_SSM + (g + 1) * D_STATE].astype(BF16)
        cg = xc[:, D_SSM + (SSM_GROUPS + g) * D_STATE:D_SSM + (SSM_GROUPS + g + 1) * D_STATE].astype(BF16)
        cb = _dot_nt(cg, bg)
        y_off = _dot_nt(cg, state_bf[g * gw:(g + 1) * gw, :]) * off_scale[:, g * gw:(g + 1) * gw]
        y_ref[:, g * gw:(g + 1) * gw] = y_off
        for hp in range(heads_per_group // 2):
            pair = g * (heads_per_group // 2) + hp
            xp = xdt[:, pair * LANES:(pair + 1) * LANES]
            yp = None
            for s in range(2):
                h = 2 * pair + s
                seg = cum[:, h:h + 1] - cum_t[h:h + 1, :]
                dec = jnp.exp(jnp.where(causal, seg, NEG))
                m = (cb * dec).astype(BF16)
                xm = jnp.where(first_head if s == 0 else jnp.logical_not(first_head), xp, 0.0).astype(BF16)
                d = _dot(m, xm)
                yp = d if yp is None else yp + d
            y_ref[:, pair * LANES:(pair + 1) * LANES] += yp
            yield
        state_ref[g * gw:(g + 1) * gw, :] = (
            state_decay[g * gw:(g + 1) * gw, :] * state[g * gw:(g + 1) * gw, :]
            + _dot(xw_t[g * gw:(g + 1) * gw, :], bg))

    z = z_ref[...]
    y = (y_ref[...] + xs * dskip_ref[...]) * (z * _sigmoid(z))
    for g in range(SSM_GROUPS):
        yg = y[:, g * gw:(g + 1) * gw]
        ms = jnp.mean(yg * yg, axis=-1, keepdims=True)
        y_ref[:, g * gw:(g + 1) * gw] = (yg * lax.rsqrt(ms + EPS)) * gnorm_ref[:, g * gw:(g + 1) * gw]


def _ssd_sample_kernel(z_ref, xbc_ref, dt_ref, conv0_ref, h0_ref, convw_ref, convb_ref, dtb_ref,
                       alog_ref, dskip_ref, gnorm_ref, e_ref, y_ref, hout_ref, xs_ref, z8_ref, dt8_ref,
                       *, rows):
    for b in range(h0_ref.shape[0]):
        tok = pl.ds(b * rows, rows)
        _ssd_sample_one(z_ref.at[tok], xbc_ref.at[tok], dt_ref.at[tok], conv0_ref.at[:, b], h0_ref.at[b],
                        convw_ref, convb_ref, dtb_ref, alog_ref, dskip_ref, gnorm_ref, e_ref, y_ref.at[tok],
                        hout_ref.at[b], xs_ref.at[b], z8_ref.at[b], dt8_ref.at[b], rows)


def _ssd_sample_one(z_ref, xbc_ref, dt_ref, conv0_ref, h0_ref, convw_ref, convb_ref, dtb_ref,
                    alog_ref, dskip_ref, gnorm_ref, e_ref, y_ref, hout_ref, xs_ref, z8_ref, dt8_ref, rows):
    tail = CONV_WIDTH - 1
    r8 = 8
    row = lax.broadcasted_iota(jnp.int32, (r8, 1), 0)
    lane = lax.broadcasted_iota(jnp.int32, (1, LANES), 1)

    xs_ref[r8:2 * r8, :] = jnp.zeros((r8, CONV_DIM), F32)
    xs_ref[r8:r8 + rows, :] = xbc_ref[...]
    xs_ref[r8 - tail:r8, :] = conv0_ref[...]
    z8_ref[...] = jnp.zeros((r8, D_SSM), F32)
    z8_ref[0:rows, :] = z_ref[...]
    dt8_ref[...] = jnp.zeros((r8, LANES), F32)
    dt8_ref[0:rows, :] = dt_ref[...]

    conv = convb_ref[...] + convw_ref[tail:tail + 1, :] * xs_ref[r8:2 * r8, :]
    for i in range(tail):
        conv = conv + convw_ref[i:i + 1, :] * xs_ref[r8 - tail + i:2 * r8 - tail + i, :]
    xc = conv * _sigmoid(conv)
    xs = xc[:, 0:D_SSM]

    live = jnp.logical_and(lane < N_HEADS_S, row < rows)
    dt = jnp.where(live, _softplus(dt8_ref[...] + dtb_ref[...]), 0.0)
    da = dt * (-jnp.exp(alog_ref[...]))
    cum = da
    for k in range(1, rows):
        cum = cum + pltpu.roll(da, k, 0)
    last = rows - 1

    heads_per_group = N_HEADS_S // SSM_GROUPS
    gw = heads_per_group * SSM_HEAD_DIM
    bg = [xc[:, D_SSM + g * D_STATE:D_SSM + (g + 1) * D_STATE] for g in range(SSM_GROUPS)]
    cg = [xc[:, D_SSM + (SSM_GROUPS + g) * D_STATE:D_SSM + (SSM_GROUPS + g + 1) * D_STATE]
          for g in range(SSM_GROUPS)]
    group0 = lane < heads_per_group
    w = []
    for j in range(rows):
        cb = [jnp.sum(cg[g] * bg[g][j:j + 1, :], axis=-1, keepdims=True) for g in range(SSM_GROUPS)]
        dec = jnp.exp(jnp.where(row >= j, cum - cum[j:j + 1, :], NEG))
        w.append(jnp.where(group0, cb[0], cb[1]) * dec)

    stack = jnp.concatenate([cum, dt] + w, axis=0)
    hi, lo, lo2 = _split3(stack)
    ex = _dot(jnp.concatenate([hi, lo, lo2], axis=0), e_ref[...])
    ns = stack.shape[0]
    ex = ex[0:ns] + ex[ns:2 * ns] + ex[2 * ns:3 * ns]
    cum_e = ex[0:r8]
    xdt = xs * ex[r8:2 * r8]
    y = None
    for j in range(rows):
        term = ex[(2 + j) * r8:(3 + j) * r8] * xdt[j:j + 1, :]
        y = term if y is None else y + term
    last_e = cum_e[last:last + 1, :]

    state = h0_ref[...]
    state_bf = state.astype(BF16)
    zeros8 = jnp.zeros((r8, D_STATE), F32)
    y_off = [_dot_nt(jnp.concatenate([cg[g], zeros8], axis=0).astype(BF16),
                     state_bf[g * gw:(g + 1) * gw, :])[0:r8, :] for g in range(SSM_GROUPS)]
    y = y + jnp.concatenate(y_off, axis=1) * jnp.exp(cum_e)

    z = z8_ref[...]
    y = (y + xs * dskip_ref[...]) * (z * _sigmoid(z))
    outs = []
    for g in range(SSM_GROUPS):
        yg = y[:, g * gw:(g + 1) * gw]
        ms = jnp.mean(yg * yg, axis=-1, keepdims=True)
        outs.append((yg * lax.rsqrt(ms + EPS)) * gnorm_ref[:, g * gw:(g + 1) * gw])
    y_ref[...] = jnp.concatenate(outs, axis=1)[0:rows, :]

    pad = jnp.zeros((SSD_CHUNK - r8, D_SSM), F32)
    xw_t = jnp.concatenate([xdt * jnp.exp(last_e - cum_e), pad], axis=0).T.astype(BF16)
    state_decay = jnp.exp(jnp.broadcast_to(last_e, (SSD_CHUNK, D_SSM)).T)
    bpad = jnp.zeros((SSD_CHUNK - r8, D_STATE), F32)
    for g in range(SSM_GROUPS):
        b128 = jnp.concatenate([bg[g], bpad], axis=0).astype(BF16)
        hout_ref[g * gw:(g + 1) * gw, :] = (
            state_decay[g * gw:(g + 1) * gw, :] * state[g * gw:(g + 1) * gw, :]
            + _dot(xw_t[g * gw:(g + 1) * gw, :], b128))


def _ssd_param_specs():
    return [_resident((CONV_WIDTH, CONV_DIM)), _resident((1, CONV_DIM)), _resident((1, LANES)),
            _resident((1, LANES)), _resident((1, D_SSM)), _resident((1, D_SSM)),
            _resident((LANES, D_SSM))]


SSD_SAMPLE_SEQS = 8


def _ssd_sample(z, xbc, dt_raw, conv0, h0, ssd_params, rows):
    batch = h0.shape[0]
    nb = SSD_SAMPLE_SEQS
    tok = lambda w: pl.BlockSpec((nb * rows, w), lambda i: (i, 0))
    state = pl.BlockSpec((nb, D_SSM, D_STATE), lambda i: (i, 0, 0))
    return pl.pallas_call(
        functools.partial(_ssd_sample_kernel, rows=rows),
        grid=(batch // nb,),
        in_specs=[tok(D_SSM), tok(CONV_DIM), tok(LANES),
                  pl.BlockSpec((CONV_WIDTH - 1, nb, CONV_DIM), lambda i: (0, i, 0)), state] + _ssd_param_specs(),
        out_specs=[tok(D_SSM), state],
        out_shape=[jax.ShapeDtypeStruct((batch * rows, D_SSM), F32),
                   jax.ShapeDtypeStruct((batch, D_SSM, D_STATE), F32)],
        scratch_shapes=[pltpu.VMEM((nb, 16, CONV_DIM), F32), pltpu.VMEM((nb, 8, D_SSM), F32),
                        pltpu.VMEM((nb, 8, LANES), F32)],
        compiler_params=pltpu.CompilerParams(
            dimension_semantics=("arbitrary",), vmem_limit_bytes=VMEM_LIMIT),
        name="ssd_sample",
    )(z, xbc, dt_raw, conv0, h0, *ssd_params)


def _t5_bucket_np(dist):
    dist = np.asarray(dist, np.int64)
    df = np.maximum(dist, 1).astype(np.float32)
    large = MAX_EXACT + (np.log(df / np.float32(MAX_EXACT)) / np.float32(math.log(MAX_WINDOW / MAX_EXACT))
                         * np.float32(N_BUCKETS - MAX_EXACT)).astype(np.int32)
    return np.where(dist < MAX_EXACT, dist, np.minimum(large, N_BUCKETS - 1)).astype(np.int32)


def _band_bucket_tables():
    ki = np.arange(SPAN)[None, :]
    out = np.empty((len(DILATIONS), 2, SPAN, SPAN), np.int32)
    for bi, d in enumerate(DILATIONS):
        nrun = DILATIONS[-1] // d
        row = np.arange(SPAN)
        qi = (nrun * (row % (SPAN // nrun)) + row // (SPAN // nrun))[:, None]
        rel_prev = qi + SPAN - ki
        rel_cur = qi - ki
        out[bi, 0] = np.where(rel_prev <= SPAN, _t5_bucket_np(np.clip(rel_prev, 0, SPAN) * d), -1)
        out[bi, 1] = np.where(rel_cur >= 0, _t5_bucket_np(np.clip(rel_cur, 0, SPAN) * d), -1)
    return out


def _bias_table_kernel(tab_ref, rb_ref, out_ref, *, buckets):
    pair = pl.program_id(0)
    for bi in range(len(DILATIONS)):
        for half in range(2):
            tab = tab_ref[bi, half]
            acc = [jnp.full((SPAN, SPAN), NEG, F32) for _ in range(2)]
            for bkt in buckets[bi][half]:
                hit = tab == bkt
                for s in range(2):
                    acc[s] = jnp.where(hit, rb_ref[bkt, 2 * pair + s] * LOG2E, acc[s])
            for s in range(2):
                rows = slice(s * SPAN, (s + 1) * SPAN)
                if half == 0:
                    out_ref[0, bi, 0, rows, 0:SPAN] = jnp.full((SPAN, SPAN), NEG, F32)
                    out_ref[0, bi, 1, rows, 0:SPAN] = acc[s]
                else:
                    out_ref[0, bi, 0, rows, SPAN:2 * SPAN] = acc[s]
                    out_ref[0, bi, 1, rows, SPAN:2 * SPAN] = acc[s]


def _bias_tables(rel_bias):
    tabs = _band_bucket_tables()
    buckets = [[sorted(int(b) for b in np.unique(tabs[bi, half]) if b >= 0) for half in range(2)]
               for bi in range(len(DILATIONS))]
    nd = len(DILATIONS)
    return pl.pallas_call(
        functools.partial(_bias_table_kernel, buckets=buckets),
        grid=(N_HEADS_A // 2,),
        in_specs=[_resident((nd, 2, SPAN, SPAN)),
                  pl.BlockSpec(memory_space=pltpu.SMEM)],
        out_specs=pl.BlockSpec((1, nd, 2, 2 * SPAN, 2 * SPAN), lambda p: (p, 0, 0, 0, 0)),
        out_shape=jax.ShapeDtypeStruct((N_HEADS_A // 2, nd, 2, 2 * SPAN, 2 * SPAN), F32),
        compiler_params=pltpu.CompilerParams(dimension_semantics=("arbitrary",)),
        name="bias_tables",
    )(jnp.asarray(tabs), rel_bias)


ATT_T = SPAN * DILATIONS[-1]
ATT_UNROLL = 16
ATT_SUB = 2


def _query_runs(d, r, n):
    run = SPAN * d // DILATIONS[-1]
    return [pl.ds(pl.multiple_of((r + d * c) * SPAN + run * n, 8), run) for c in range(DILATIONS[-1] // d)]


def _load_runs(ref, runs, *lead):
    parts = [ref[(*lead, rows, slice(None))] for rows in runs]
    return parts[0] if len(parts) == 1 else jnp.concatenate(parts, axis=0)


def _store_runs(ref, runs, value, *lead):
    run = value.shape[0] // len(runs)
    for c, rows in enumerate(runs):
        ref[(*lead, rows, slice(None))] = value[c * run:(c + 1) * run, :]


def _attn_kernel(q_ref, k_ref, v_ref, bias_ref, o_ref, qp_ref, tmp_ref, kph_ref, vph_ref, acc_ref, m_ref,
                 l_ref):
    t = pl.program_id(2)
    lane = lax.broadcasted_iota(jnp.int32, (1, LANES), 1)
    first_head = lane < HEAD_DIM
    head_mask = (first_head, jnp.logical_not(first_head))
    nblk = ATT_T // SPAN
    dmax = DILATIONS[-1]

    @pl.when(t == 0)
    def _():
        for bi, d in enumerate(DILATIONS):
            ph = ATT_T // d + SPAN
            for r in range(d):
                kph_ref[bi, r * ph:r * ph + SPAN, :] = jnp.zeros((SPAN, LANES), BF16)
                vph_ref[bi, r * ph:r * ph + SPAN, :] = jnp.zeros((SPAN, LANES), BF16)

    d4, d16 = DILATIONS[1], DILATIONS[2]
    assert DILATIONS == (1, d4, d4 * d4)
    l4, l16 = ATT_T // d4, ATT_T // d16
    for r4 in range(d4):
        tmp_ref[r4 * l4:(r4 + 1) * l4, :] = q_ref[pl.ds(r4, l4, stride=d4), :] * (ATTN_SCALE * LOG2E)
    for r16 in range(d16):
        qp_ref[r16 * l16:(r16 + 1) * l16, :] = tmp_ref[pl.ds((r16 % d4) * l4 + r16 // d4, l16, stride=d4), :]
    for src_ref, dst_ref in ((k_ref, kph_ref), (v_ref, vph_ref)):
        dst_ref[0, SPAN:SPAN + ATT_T, :] = src_ref[...].astype(BF16)
        ph4, ph16 = l4 + SPAN, l16 + SPAN
        for r4 in range(d4):
            x4 = src_ref[pl.ds(r4, l4, stride=d4), :]
            tmp_ref[r4 * l4:(r4 + 1) * l4, :] = x4
            dst_ref[1, r4 * ph4 + SPAN:(r4 + 1) * ph4, :] = x4.astype(BF16)
        for r16 in range(d16):
            x16 = tmp_ref[pl.ds((r16 % d4) * l4 + r16 // d4, l16, stride=d4), :]
            dst_ref[2, r16 * ph16 + SPAN:(r16 + 1) * ph16, :] = x16.astype(BF16)

    order = tuple(reversed(range(len(DILATIONS))))
    for bi in order:
        d = DILATIONS[bi]
        ln = ATT_T // d
        ph = ln + SPAN
        nb = ln // SPAN
        first_branch = bi == order[0]
        last_branch = bi == order[-1]

        def group(it, carry, bi=bi, d=d, ph=ph, nb=nb, first_branch=first_branch, last_branch=last_branch):
            def fetch(j):
                sub = []
                for g in range(j * ATT_SUB, (j + 1) * ATT_SUB):
                    blk = it * ATT_UNROLL + g
                    r = blk // nb
                    n = blk % nb
                    runs = _query_runs(d, r, n)
                    krow = pl.multiple_of(r * ph + n * SPAN, SPAN)
                    variant = jnp.logical_or(t > 0, n > 0).astype(jnp.int32)
                    qf = _load_runs(qp_ref, runs)
                    qs = jnp.concatenate([jnp.where(head_mask[s], qf, 0.0).astype(BF16) for s in range(2)],
                                         axis=0)
                    logits = _dot_nt(qs, kph_ref[bi, pl.ds(krow, 2 * SPAN), :]) + bias_ref[0, bi, variant]
                    old = None
                    if not first_branch:
                        old = ([_load_runs(m_ref, runs, s) for s in range(2)],
                               [_load_runs(l_ref, runs, s) for s in range(2)], _load_runs(acc_ref, runs))
                    sub.append((runs, krow, old, logits))
                return sub

            nsub = ATT_UNROLL // ATT_SUB
            nxt = fetch(0)
            for j in range(nsub):
                cur = nxt
                if j + 1 < nsub:
                    nxt = fetch(j + 1)
                for runs, m_new, l_new, pv in softmax_pv(cur):
                    if last_branch:
                        _store_runs(acc_ref, runs, pv / jnp.where(first_head, l_new[0], l_new[1]))
                    else:
                        for s in range(2):
                            _store_runs(m_ref, runs, m_new[s], s)
                            _store_runs(l_ref, runs, l_new[s], s)
                        _store_runs(acc_ref, runs, pv)
            return carry

        def softmax_pv(sub, bi=bi, first_branch=first_branch):
            soft = []
            for runs, krow, old, sc2 in sub:
                m_new, l_new, alpha, ps = [], [], [], []
                for s in range(2):
                    sc = sc2[s * SPAN:(s + 1) * SPAN, :]
                    halves = (sc[:, 0:SPAN], sc[:, SPAN:2 * SPAN])
                    mb = jnp.max(jnp.maximum(halves[0], halves[1]), axis=-1, keepdims=True)
                    if first_branch:
                        m = jnp.broadcast_to(mb, (SPAN, LANES))
                    else:
                        m = jnp.maximum(old[0][s], mb)
                        alpha.append(jnp.exp2(old[0][s] - m))
                    p = [jnp.exp2(h - m) for h in halves]
                    rs = jnp.sum(p[0] + p[1], axis=-1, keepdims=True)
                    ps.append(jnp.concatenate([p[0].astype(BF16), p[1].astype(BF16)], axis=1))
                    m_new.append(m)
                    l_new.append(jnp.broadcast_to(rs, (SPAN, LANES)) if first_branch
                                 else alpha[s] * old[1][s] + rs)
                soft.append((m_new, l_new, alpha, ps))
            out = []
            for (runs, krow, old, _), (m_new, l_new, alpha, ps) in zip(sub, soft):
                pv2 = _dot(jnp.concatenate(ps, axis=0), vph_ref[bi, pl.ds(krow, 2 * SPAN), :])
                pv = jnp.where(first_head, pv2[0:SPAN, :], pv2[SPAN:2 * SPAN, :])
                if not first_branch:
                    pv = jnp.where(first_head, alpha[0], alpha[1]) * old[2] + pv
                out.append((runs, m_new, l_new, pv))
            return out

        lax.fori_loop(0, nblk // ATT_UNROLL, group, 0)

    for r16 in range(d16):
        tmp_ref[pl.ds((r16 % d4) * l4 + r16 // d4, l16, stride=d4), :] = acc_ref[r16 * l16:(r16 + 1) * l16, :]
    for r4 in range(d4):
        o_ref[pl.ds(r4, l4, stride=d4), :] = tmp_ref[r4 * l4:(r4 + 1) * l4, :]

    for bi, d in enumerate(DILATIONS):
        ln = ATT_T // d
        ph = ln + SPAN
        for r in range(d):
            kph_ref[bi, r * ph:r * ph + SPAN, :] = kph_ref[bi, (r + 1) * ph - SPAN:(r + 1) * ph, :]
            vph_ref[bi, r * ph:r * ph + SPAN, :] = vph_ref[bi, (r + 1) * ph - SPAN:(r + 1) * ph, :]


def _attn_prompt(qkv, bias, batch):
    m = qkv.shape[0]
    nt = m // batch // ATT_T
    npair = N_HEADS_A // 2
    nd = len(DILATIONS)
    blk = lambda off: pl.BlockSpec((ATT_T, LANES), lambda b, p, t: (b * nt + t, off + p))
    halo_rows = ATT_T + DILATIONS[-1] * SPAN
    return pl.pallas_call(
        _attn_kernel,
        grid=(batch, npair, nt),
        in_specs=[blk(0), blk(npair), blk(2 * npair),
                  pl.BlockSpec((1, nd, 2, 2 * SPAN, 2 * SPAN), lambda b, p, t: (p, 0, 0, 0, 0))],
        out_specs=blk(0),
        out_shape=jax.ShapeDtypeStruct((m, D_ATTN), F32),
        scratch_shapes=[pltpu.VMEM((ATT_T, LANES), F32),
                        pltpu.VMEM((ATT_T, LANES), F32),
                        pltpu.VMEM((nd, halo_rows, LANES), BF16),
                        pltpu.VMEM((nd, halo_rows, LANES), BF16),
                        pltpu.VMEM((ATT_T, LANES), F32),
                        pltpu.VMEM((2, ATT_T, LANES), F32),
                        pltpu.VMEM((2, ATT_T, LANES), F32)],
        compiler_params=pltpu.CompilerParams(
            dimension_semantics=("arbitrary", "arbitrary", "arbitrary"), vmem_limit_bytes=VMEM_LIMIT),
        name="attn_prompt",
    )(qkv, qkv, qkv, bias)


SAMPLE_ROWS = 4


def _sample_tables():
    pos = np.arange(MAX_WINDOW)[None, :]
    t = (np.arange(8) % SAMPLE_ROWS)[:, None]
    dist = MAX_WINDOW + t - pos
    mult = np.zeros(dist.shape, np.float32)
    for d in DILATIONS:
        mult += ((dist % d == 0) & (dist <= SPAN * d)).astype(np.float32)
    bucket = np.where(mult > 0, _t5_bucket_np(dist), -1).astype(np.int32)
    return bucket, np.concatenate([mult, mult], axis=0)


def _sample_bias_kernel(tab_ref, rb_ref, rbv_ref, bmain_ref, bnew_ref, *, buckets):
    pair = pl.program_id(0)
    tab = tab_ref[...]
    acc = [jnp.full(tab.shape, NEG, F32) for _ in range(2)]
    for bkt in buckets:
        hit = tab == bkt
        for s in range(2):
            acc[s] = jnp.where(hit, rb_ref[bkt, 2 * pair + s], acc[s])
    for s in range(2):
        bmain_ref[0, s * 8:(s + 1) * 8, :] = acc[s]
    t = lax.broadcasted_iota(jnp.int32, (8, 1), 0) % SAMPLE_ROWS
    for tp in range(SAMPLE_ROWS):
        b = jnp.full((8, LANES), NEG, F32)
        for dist in range(SAMPLE_ROWS):
            b = jnp.where(t - tp == dist, rbv_ref[dist:dist + 1, :], b)
        bnew_ref[tp] = b


def _sample_bias_tables(rel_bias):
    bucket, mult = _sample_tables()
    buckets = sorted(int(b) for b in np.unique(bucket) if b >= 0)
    npair = N_HEADS_A // 2
    rb_lanes = jnp.pad(rel_bias, ((0, 0), (0, LANES - N_HEADS_A)))
    bmain, bnew = pl.pallas_call(
        functools.partial(_sample_bias_kernel, buckets=buckets),
        grid=(npair,),
        in_specs=[_resident((8, MAX_WINDOW)), pl.BlockSpec(memory_space=pltpu.SMEM),
                  _resident((N_BUCKETS, LANES))],
        out_specs=[pl.BlockSpec((1, 16, MAX_WINDOW), lambda p: (p, 0, 0)),
                   pl.BlockSpec((SAMPLE_ROWS, 8, LANES), lambda p: (0, 0, 0))],
        out_shape=[jax.ShapeDtypeStruct((npair, 16, MAX_WINDOW), F32),
                   jax.ShapeDtypeStruct((SAMPLE_ROWS, 8, LANES), F32)],
        compiler_params=pltpu.CompilerParams(dimension_semantics=("arbitrary",)),
        name="sample_bias_tables",
    )(jnp.asarray(bucket), rel_bias, rb_lanes)
    return bmain, bnew, jnp.asarray(mult)


def _attn_sample_body(qkv_ref, kt_ref, vt_ref, bmain_ref, mult_ref, bnew_ref, e_ref, om_ref):
    rows = SAMPLE_ROWS
    npair = N_HEADS_A // 2
    second = pl.program_id(0) % 2 == 1
    lower = lax.broadcasted_iota(jnp.int32, (2 * rows, 1), 0) < rows
    mine = jnp.logical_xor(lower, second)
    blk = qkv_ref[...]
    q8_all = jnp.where(mine, blk, pltpu.roll(blk, rows, 0))
    q8 = q8_all[:, 0:D_ATTN] * ATTN_SCALE
    kn = q8_all[:, D_ATTN:2 * D_ATTN]
    vn = q8_all[:, 2 * D_ATTN:3 * D_ATTN]
    lane = lax.broadcasted_iota(jnp.int32, (1, LANES), 1)
    first_head = lane < HEAD_DIM
    e = e_ref[...]

    s_new = []
    m_tot = jnp.full((8, LANES), NEG, F32)
    for tp in range(rows):
        hi, lo, lo2 = _split3(q8 * kn[tp:tp + 1, :])
        s = _dot_nt(hi, e) + _dot_nt(lo, e) + _dot_nt(lo2, e) + bnew_ref[tp]
        s_new.append(s)
        m_tot = jnp.maximum(m_tot, s)

    mult = mult_ref[...]
    l_tot = jnp.zeros((8, LANES), F32)

    def window_logits(p):
        qp = q8[:, p * LANES:(p + 1) * LANES]
        qbd = jnp.concatenate([jnp.where(first_head, qp, 0.0), jnp.where(first_head, 0.0, qp)],
                              axis=0).astype(BF16)
        return _dot(qbd, kt_ref[0, p].astype(BF16)) + bmain_ref[p]

    s_next = window_logits(0)
    for p in range(npair):
        s16 = s_next
        if p + 1 < npair:
            s_next = window_logits(p + 1)
        m_new =jnp.concatenate([m_tot[:, 2 * p:2 * p + 1], m_tot[:, 2 * p + 1:2 * p + 2]], axis=0)
        m = jnp.maximum(jnp.max(s16, axis=-1, keepdims=True), m_new)
        pr = mult * jnp.exp(s16 - m)
        ls = jnp.sum(pr, axis=-1, keepdims=True)
        o16 = _dot_nt(pr.astype(BF16), vt_ref[0, p].astype(BF16))
        om_ref[:, p * LANES:(p + 1) * LANES] = jnp.where(first_head, o16[0:8], o16[8:16])
        m_tot = jnp.where(lane == 2 * p, m[0:8], jnp.where(lane == 2 * p + 1, m[8:16], m_tot))
        l_tot = jnp.where(lane == 2 * p, ls[0:8], jnp.where(lane == 2 * p + 1, ls[8:16], l_tot))
        yield

    t = lax.broadcasted_iota(jnp.int32, (8, 1), 0) % rows
    o = om_ref[...]
    for tp in range(rows):
        c = jnp.where(t == tp, float(len(DILATIONS)), jnp.where(t > tp, 1.0, 0.0))
        pn = c * jnp.exp(s_new[tp] - m_tot)
        l_tot = l_tot + pn
        o = o + _expand_heads(pn, e_ref) * vn[tp:tp + 1, :]
    o = o / _expand_heads(l_tot, e_ref)
    return o, second, lower


def _interleave(*gens):
    results = [None] * len(gens)
    live = list(range(len(gens)))
    while live:
        for k in list(live):
            try:
                next(gens[k])
            except StopIteration as stop:
                results[k] = stop.value
                live.remove(k)
    return results


def _sample_attn_prompt_ssd_kernel(qkv_ref, kt_ref, vt_ref, bmain_ref, mult_ref, bnew_ref, e_ref,
                                   z_ref, xbc_ref, dt_ref, conv0_ref, h0_ref, convw_ref, convb_ref, dtb_ref,
                                   alog_ref, dskip_ref, gnorm_ref, o_ref, y_ref, hout_ref,
                                   om_ref, xpad_ref, state_ref, *, chunks):
    c = pl.program_id(0) % chunks

    @pl.when(c == 0)
    def _():
        state_ref[...] = h0_ref[0]
        xpad_ref[8 - (CONV_WIDTH - 1):8, :] = conv0_ref[0]

    attn = _attn_sample_body(qkv_ref, kt_ref, vt_ref, bmain_ref, mult_ref, bnew_ref, e_ref, om_ref)
    scan = _ssd_chunk(z_ref.at[0], xbc_ref.at[0], dt_ref.at[0], y_ref.at[0], xpad_ref, state_ref,
                      convw_ref, convb_ref, dtb_ref, alog_ref, dskip_ref, gnorm_ref, e_ref)
    o, second, lower = _interleave(attn, scan)[0]

    @pl.when(jnp.logical_not(second))
    def _():
        o_ref[...] = o

    @pl.when(second)
    def _():
        o_ref[...] = jnp.where(lower, o_ref[...], o)

    @pl.when(c == chunks - 1)
    def _():
        hout_ref[0] = state_ref[...]


def _sample_attn_prompt_ssd(qkv, kt, vt, bmain, mult, bnew, expand, z, xbc, dt_raw, conv0, h0, ssd_params):
    rows = SAMPLE_ROWS
    batch = qkv.shape[0] // rows
    nseq, seq = z.shape[0], z.shape[1]
    chunks = seq // SSD_CHUNK
    assert batch == nseq * chunks, "one prompt chunk per sample sequence"
    npair = N_HEADS_A // 2
    cache = pl.BlockSpec((1, npair, LANES, MAX_WINDOW), lambda i: (i, 0, 0, 0))
    chunk = lambda w: pl.BlockSpec((1, SSD_CHUNK, w), lambda i: (i // chunks, i % chunks, 0))
    per_seq = lambda r, w: pl.BlockSpec((1, r, w), lambda i: (i // chunks, 0, 0))
    return pl.pallas_call(
        functools.partial(_sample_attn_prompt_ssd_kernel, chunks=chunks),
        grid=(batch,),
        in_specs=[pl.BlockSpec((2 * rows, 3 * D_ATTN), lambda i: (i // 2, 0)), cache, cache,
                  _resident((npair, 16, MAX_WINDOW)), _resident((16, MAX_WINDOW)),
                  _resident((rows, 8, LANES)), _resident((LANES, D_ATTN)),
                  chunk(D_SSM), chunk(CONV_DIM), chunk(LANES), per_seq(CONV_WIDTH - 1, CONV_DIM),
                  per_seq(D_SSM, D_STATE)] + _ssd_param_specs()[:-1],
        out_specs=[pl.BlockSpec((2 * rows, D_ATTN), lambda i: (i // 2, 0)), chunk(D_SSM),
                   per_seq(D_SSM, D_STATE)],
        out_shape=[jax.ShapeDtypeStruct((batch * rows, D_ATTN), F32),
                   jax.ShapeDtypeStruct((nseq, seq, D_SSM), F32),
                   jax.ShapeDtypeStruct((nseq, D_SSM, D_STATE), F32)],
        scratch_shapes=[pltpu.VMEM((8, D_ATTN), F32), pltpu.VMEM((8 + SSD_CHUNK, CONV_DIM), F32),
                        pltpu.VMEM((D_SSM, D_STATE), F32)],
        compiler_params=pltpu.CompilerParams(
            dimension_semantics=("arbitrary",), vmem_limit_bytes=VMEM_LIMIT),
        name="sample_attn_prompt_ssd",
    )(qkv, kt, vt, bmain, mult, bnew, expand, z, xbc, dt_raw, conv0, h0, *ssd_params[:-1])


def _head_expand_matrix():
    e = np.zeros((LANES, D_SSM), np.float32)
    for h in range(N_HEADS_S):
        e[h, h * SSM_HEAD_DIM:(h + 1) * SSM_HEAD_DIM] = 1.0
    return jnp.asarray(e, BF16)


def _prep_params(norm_mix, w_in, q_norm, k_norm, conv_w, conv_b, dt_bias, a_log, d_skip, ssm_norm,
                 w_out, norm_ffn, w_up, w_down):
    n_main = 3 * D_ATTN + D_SSM + CONV_DIM
    pad_heads = lambda v: jnp.pad(v.reshape(1, N_HEADS_S), ((0, 0), (0, LANES - N_HEADS_S)))
    return dict(
        gmix=norm_mix.reshape(1, D_MODEL),
        w_main=w_in.astype(BF16),
        w_dt=jnp.pad(w_in[:, n_main:], ((0, 0), (0, LANES - N_HEADS_S))).astype(BF16),
        gq=jnp.tile(q_norm, N_HEADS_A).reshape(1, D_ATTN),
        gk=jnp.tile(k_norm, N_HEADS_A).reshape(1, D_ATTN),
        conv_w=conv_w,
        conv_b=conv_b.reshape(1, CONV_DIM),
        dt_bias=pad_heads(dt_bias),
        a_log=pad_heads(a_log),
        d_skip_e=jnp.repeat(d_skip, SSM_HEAD_DIM).reshape(1, D_SSM),
        gnorm=ssm_norm.reshape(1, D_SSM),
        expand=_head_expand_matrix(),
        w_out=w_out.astype(BF16),
        gffn=norm_ffn.reshape(1, D_MODEL),
        w_up=w_up.astype(BF16),
        w_down=w_down.astype(BF16),
    )


def _ssd_param_list(p):
    return (p["conv_w"], p["conv_b"], p["dt_bias"], p["a_log"], p["d_skip_e"], p["gnorm"], p["expand"])


def kernel(x_prompt, x_sample, cache_attn_k, cache_attn_v, state_conv, state_ssm, norm_mix, w_in, q_norm,
           k_norm, rel_bias, conv_w, conv_b, dt_bias, a_log, d_skip, ssm_norm, w_out, norm_ffn, w_up, w_down):
    depth = w_in.shape[0]
    assert depth == 1, "single-layer decoder"
    l = 0
    p = _prep_params(norm_mix[l], w_in[l], q_norm[l], k_norm[l], conv_w[l], conv_b[l], dt_bias[l], a_log[l],
                     d_skip[l], ssm_norm[l], w_out[l], norm_ffn[l], w_up[l], w_down[l])
    ssd_params = _ssd_param_list(p)
    tm = 512

    bp, sp = x_prompt.shape[:2]
    keep = min(MAX_WINDOW, sp)
    xp = x_prompt.reshape(bp * sp, D_MODEL)
    qkv, z, xbc, dt_raw = _inproj(xp, p["gmix"], p["w_main"], p["w_dt"], p["gq"], p["gk"], tm)
    attn = _attn_prompt(qkv, _bias_tables(rel_bias), bp)
    qkv3 = qkv.reshape(bp, sp, 3 * D_ATTN)
    new_k_prompt = qkv3[:, sp - keep:, D_ATTN:2 * D_ATTN].reshape(depth, bp, keep, N_HEADS_A, HEAD_DIM)
    new_v_prompt = qkv3[:, sp - keep:, 2 * D_ATTN:].reshape(depth, bp, keep, N_HEADS_A, HEAD_DIM)
    new_conv_prompt = xbc.reshape(bp, sp, CONV_DIM)[:, sp - (CONV_WIDTH - 1):].reshape(
        depth, bp, CONV_WIDTH - 1, CONV_DIM)

    bs, ts = x_sample.shape[:2]
    assert cache_attn_k.shape[2] == MAX_WINDOW and ts == SAMPLE_ROWS
    xs = x_sample.reshape(bs * ts, D_MODEL)
    qkv_s, z_s, xbc_s, dt_s = _inproj(xs, p["gmix"], p["w_main"], p["w_dt"], p["gq"], p["gk"], tm)
    npair = N_HEADS_A // 2
    kt = jnp.transpose(cache_attn_k[l], (0, 2, 3, 1)).reshape(bs, npair, LANES, MAX_WINDOW)
    vt = jnp.transpose(cache_attn_v[l], (0, 2, 3, 1)).reshape(bs, npair, LANES, MAX_WINDOW)
    bmain, bnew, mult = _sample_bias_tables(rel_bias)
    conv0 = jnp.zeros((bp, CONV_WIDTH - 1, CONV_DIM), F32)
    h0 = jnp.zeros((bp, D_SSM, D_STATE), F32)
    attn_s, ssm_o, hst = _sample_attn_prompt_ssd(
        qkv_s, kt, vt, bmain, mult, bnew, p["expand"], z.reshape(bp, sp, D_SSM), xbc.reshape(bp, sp, CONV_DIM),
        dt_raw.reshape(bp, sp, LANES), conv0, h0, ssd_params)
    y_prompt = _outffn(xp, attn, ssm_o.reshape(bp * sp, D_SSM), p["w_out"], p["gffn"], p["w_up"], p["w_down"], tm)
    new_ssm_prompt = hst.reshape(depth, bp, N_HEADS_S, SSM_HEAD_DIM, D_STATE)
    ssm_os, hsts = _ssd_sample(z_s, xbc_s, dt_s, jnp.transpose(state_conv[l], (1, 0, 2)),
                               state_ssm[l].reshape(bs, D_SSM, D_STATE), ssd_params, ts)
    y_sample = _outffn(xs, attn_s, ssm_os, p["w_out"], p["gffn"], p["w_up"], p["w_down"], tm)
    new_k_sample = qkv_s[:, D_ATTN:2 * D_ATTN].reshape(depth, bs, ts, N_HEADS_A, HEAD_DIM)
    new_v_sample = qkv_s[:, 2 * D_ATTN:].reshape(depth, bs, ts, N_HEADS_A, HEAD_DIM)
    new_conv_sample = xbc_s.reshape(bs, ts, CONV_DIM)[:, ts - (CONV_WIDTH - 1):].reshape(
        depth, bs, CONV_WIDTH - 1, CONV_DIM)
    new_ssm_sample = hsts.reshape(depth, bs, N_HEADS_S, SSM_HEAD_DIM, D_STATE)

    return (y_prompt.reshape(bp, sp, D_MODEL), y_sample.reshape(bs, ts, D_MODEL),
            new_k_prompt, new_v_prompt, new_conv_prompt, new_ssm_prompt,
            new_k_sample, new_v_sample, new_conv_sample, new_ssm_sample)
```

```python
import functools
import math

import jax
import jax.numpy as jnp
import numpy as np
from jax import lax
from jax.experimental import pallas as pl
from jax.experimental.pallas import tpu as pltpu

F32 = jnp.float32
BF16 = jnp.bfloat16

D_MODEL = 1024
HEAD_DIM = 64
N_HEADS_A = 16
D_ATTN = 1024
SPAN = 128
DILATIONS = (1, 4, 16)
MAX_WINDOW = 2048
N_BUCKETS = 32
MAX_EXACT = 16
SSM_HEAD_DIM = 64
N_HEADS_S = 16
D_SSM = 1024
SSM_GROUPS = 2
D_STATE = 128
CONV_WIDTH = 4
CONV_DIM = D_SSM + 2 * SSM_GROUPS * D_STATE
SSD_CHUNK = 128
D_FF = 4096
EPS = 1e-6
ATTN_SCALE = HEAD_DIM ** -0.5
LOG2E = 1.4426950408889634

LANES = 128
NEG = -1e30
VMEM_LIMIT = 56 * 1024 * 1024


def _resident(shape):
    nd = len(shape)
    return pl.BlockSpec(shape, lambda *_: (0,) * nd, pipeline_mode=pl.Buffered(1))


def _split3(x):
    hi = x.astype(BF16)
    r1 = x - hi.astype(F32)
    lo = r1.astype(BF16)
    lo2 = (r1 - lo.astype(F32)).astype(BF16)
    return hi, lo, lo2


def _dot(a, b):
    return jnp.dot(a, b, preferred_element_type=F32)


def _dot_nt(a, b):
    return lax.dot_general(a, b, (((1,), (1,)), ((), ())), preferred_element_type=F32)


def _sigmoid(x):
    return 0.5 * jnp.tanh(0.5 * x) + 0.5


def _softplus(x):
    return jnp.maximum(x, 0.0) + jnp.log1p(jnp.exp(-jnp.abs(x)))


IN_TN = 512


def _inproj_kernel(x_ref, gmix_ref, w_ref, wdt_ref, gq_ref, gk_ref,
                   qkv_ref, z_ref, xbc_ref, dt_ref, *tail_refs, with_tail):
    x = x_ref[...]
    ms = jnp.mean(x * x, axis=-1, keepdims=True)
    xn = ((x * lax.rsqrt(ms + EPS)) * gmix_ref[...]).astype(BF16)
    first_head = lax.broadcasted_iota(jnp.int32, (1, LANES), 1) < HEAD_DIM

    n_main = 3 * D_ATTN + D_SSM + CONV_DIM
    for j in range(n_main // IN_TN):
        c0 = j * IN_TN
        t = _dot(xn, w_ref[:, c0:c0 + IN_TN])
        if c0 < 2 * D_ATTN:
            g_ref = gq_ref if c0 < D_ATTN else gk_ref
            g0 = c0 % D_ATTN
            for c in range(IN_TN // LANES):
                tc = t[:, c * LANES:(c + 1) * LANES]
                s = tc * tc
                sa = jnp.sum(jnp.where(first_head, s, 0.0), axis=-1, keepdims=True)
                sb = jnp.sum(jnp.where(first_head, 0.0, s), axis=-1, keepdims=True)
                inv = lax.rsqrt(jnp.where(first_head, sa, sb) * (1.0 / HEAD_DIM) + EPS)
                out = (tc * inv) * g_ref[:, g0 + c * LANES:g0 + (c + 1) * LANES]
                qkv_ref[:, c0 + c * LANES:c0 + (c + 1) * LANES] = out
                if with_tail and c0 >= D_ATTN:
                    tail_refs[0][:, g0 + c * LANES:g0 + (c + 1) * LANES] = out
        elif c0 < 3 * D_ATTN:
            qkv_ref[:, c0:c0 + IN_TN] = t
            if with_tail:
                tail_refs[1][:, c0 - 2 * D_ATTN:c0 - 2 * D_ATTN + IN_TN] = t
        elif c0 < 3 * D_ATTN + D_SSM:
            z_ref[:, c0 - 3 * D_ATTN:c0 - 3 * D_ATTN + IN_TN] = t
        else:
            o0 = c0 - 3 * D_ATTN - D_SSM
            xbc_ref[:, o0:o0 + IN_TN] = t
    dt_ref[...] = _dot(xn, wdt_ref[...])


def _inproj(x2d, gmix, w_main, w_dt, gq, gk, tm, tail=None):
    m = x2d.shape[0]
    row = lambda w: pl.BlockSpec((tm, w), lambda i: (i, 0))
    out_specs = [row(3 * D_ATTN), row(D_SSM), row(CONV_DIM), row(LANES)]
    out_shape = [jax.ShapeDtypeStruct((m, 3 * D_ATTN), F32), jax.ShapeDtypeStruct((m, D_SSM), F32),
                 jax.ShapeDtypeStruct((m, CONV_DIM), F32), jax.ShapeDtypeStruct((m, LANES), F32)]
    if tail is not None:
        per_seq, kept = tail[0] // tm, tail[1] // tm
        kept_block = lambda i: ((i // per_seq) * kept + jnp.maximum(i % per_seq - (per_seq - kept), 0), 0)
        out_specs += [pl.BlockSpec((tm, D_ATTN), kept_block)] * 2
        out_shape += [jax.ShapeDtypeStruct((m // per_seq // tm * kept * tm, D_ATTN), F32)] * 2
    return pl.pallas_call(
        functools.partial(_inproj_kernel, with_tail=tail is not None),
        grid=(m // tm,),
        in_specs=[row(D_MODEL), _resident((1, D_MODEL)), _resident(w_main.shape),
                  _resident((D_MODEL, LANES)), _resident((1, D_ATTN)), _resident((1, D_ATTN))],
        out_specs=out_specs,
        out_shape=out_shape,
        compiler_params=pltpu.CompilerParams(
            dimension_semantics=("arbitrary",), vmem_limit_bytes=VMEM_LIMIT),
        name="inproj",
    )(x2d, gmix, w_main, w_dt, gq, gk)


FFN_TF = 1024


def _outffn_kernel(x_ref, attn_ref, ssm_ref, wout_ref, gffn_ref, wup_ref, wdown_ref, y_ref):
    a = attn_ref[...].astype(BF16)
    s = ssm_ref[...].astype(BF16)
    h = x_ref[...] + _dot(a, wout_ref[0:D_ATTN, :]) + _dot(s, wout_ref[D_ATTN:D_ATTN + D_SSM, :])
    ms = jnp.mean(h * h, axis=-1, keepdims=True)
    hn = ((h * lax.rsqrt(ms + EPS)) * gffn_ref[...]).astype(BF16)
    acc = None
    for f in range(D_FF // FFN_TF):
        u = _dot(hn, wup_ref[:, f * FFN_TF:(f + 1) * FFN_TF])
        u = jnp.maximum(u, 0.0)
        u = (u * u).astype(BF16)
        d = _dot(u, wdown_ref[f * FFN_TF:(f + 1) * FFN_TF, :])
        acc = d if acc is None else acc + d
    y_ref[...] = h + acc


def _outffn(x2d, attn, ssm_o, w_out, gffn, w_up, w_down, tm):
    m = x2d.shape[0]
    row = pl.BlockSpec((tm, D_MODEL), lambda i: (i, 0))
    return pl.pallas_call(
        _outffn_kernel,
        grid=(m // tm,),
        in_specs=[row, row, row, _resident((D_ATTN + D_SSM, D_MODEL)), _resident((1, D_MODEL)),
                  _resident((D_MODEL, D_FF)), _resident((D_FF, D_MODEL))],
        out_specs=row,
        out_shape=jax.ShapeDtypeStruct((m, D_MODEL), F32),
        compiler_params=pltpu.CompilerParams(
            dimension_semantics=("arbitrary",), vmem_limit_bytes=VMEM_LIMIT),
        name="outffn",
    )(x2d, attn, ssm_o, w_out, gffn, w_up, w_down)


def _expand_heads(v, e_ref):
    hi, lo, lo2 = _split3(v)
    e = e_ref[...]
    return _dot(hi, e) + _dot(lo, e) + _dot(lo2, e)


def _ssd_chunk(z_ref, xbc_ref, dt_ref, y_ref, xpad_ref, state_ref,
               convw_ref, convb_ref, dtb_ref, alog_ref, dskip_ref, gnorm_ref, e_ref):
    q = SSD_CHUNK
    pad = 8
    tail = CONV_WIDTH - 1

    xbc = xbc_ref[...]
    xpad_ref[pad:pad + q, :] = xbc
    conv = convb_ref[...] + convw_ref[tail:tail + 1, :] * xbc
    for i in range(tail):
        conv = conv + convw_ref[i:i + 1, :] * xpad_ref[pad - tail + i:pad - tail + i + q, :]
    xpad_ref[pad - tail:pad, :] = xbc[q - tail:q, :]
    xc = conv * _sigmoid(conv)
    xs = xc[:, 0:D_SSM]

    lane = lax.broadcasted_iota(jnp.int32, (1, LANES), 1)
    dt = jnp.where(lane < N_HEADS_S, _softplus(dt_ref[...] + dtb_ref[...]), 0.0)
    da = dt * (-jnp.exp(alog_ref[...]))

    ii = lax.broadcasted_iota(jnp.int32, (q, q), 0)
    jj = lax.broadcasted_iota(jnp.int32, (q, q), 1)
    causal = ii >= jj
    tri = jnp.where(causal, 1.0, 0.0).astype(BF16)
    hi, lo, lo2 = _split3(da)
    cum = _dot(tri, hi) + _dot(tri, lo) + _dot(tri, lo2)
    cum_t = cum.T

    cum_e = _expand_heads(cum, e_ref)
    last_e = cum_e[q - 1:q, :]
    xdt = xs * _expand_heads(dt, e_ref)
    xw_t = (xdt * jnp.exp(last_e - cum_e)).T.astype(BF16)
    state_decay = jnp.exp(jnp.broadcast_to(last_e, (q, D_SSM)).T)
    off_scale = jnp.exp(cum_e)

    first_head = lane < SSM_HEAD_DIM
    heads_per_group = N_HEADS_S // SSM_GROUPS
    gw = heads_per_group * SSM_HEAD_DIM
    state = state_ref[...]
    state_bf = state.astype(BF16)
    yield
    for g in range(SSM_GROUPS):
        bg = xc[:, D_SSM + g * D_STATE:D_SSM + (g + 1) * D_STATE].astype(BF16)
        cg = xc[:, D_SSM + (SSM_GROUPS + g) * D_STATE:D_SSM + (SSM_GROUPS + g + 1) * D_STATE].astype(BF16)
        cb = _dot_nt(cg, bg)
        y_off = _dot_nt(cg, state_bf[g * gw:(g + 1) * gw, :]) * off_scale[:, g * gw:(g + 1) * gw]
        y_ref[:, g * gw:(g + 1) * gw] = y_off
        for hp in range(heads_per_group // 2):
            pair = g * (heads_per_group // 2) + hp
            xp = xdt[:, pair * LANES:(pair + 1) * LANES]
            yp = None
            for s in range(2):
                h = 2 * pair + s
                seg = cum[:, h:h + 1] - cum_t[h:h + 1, :]
                dec = jnp.exp(jnp.where(causal, seg, NEG))
                m = (cb * dec).astype(BF16)
                xm = jnp.where(first_head if s == 0 else jnp.logical_not(first_head), xp, 0.0).astype(BF16)
                d = _dot(m, xm)
                yp = d if yp is None else yp + d
            y_ref[:, pair * LANES:(pair + 1) * LANES] += yp
            yield
        state_ref[g * gw:(g + 1) * gw, :] = (
            state_decay[g * gw:(g + 1) * gw, :] * state[g * gw:(g + 1) * gw, :]
            + _dot(xw_t[g * gw:(g + 1) * gw, :], bg))

    z = z_ref[...]
    y = (y_ref[...] + xs * dskip_ref[...]) * (z * _sigmoid(z))
    for g in range(SSM_GROUPS):
        yg = y[:, g * gw:(g + 1) * gw]
        ms = jnp.mean(yg * yg, axis=-1, keepdims=True)
        y_ref[:, g * gw:(g + 1) * gw] = (yg * lax.rsqrt(ms + EPS)) * gnorm_ref[:, g * gw:(g + 1) * gw]


def _ssd_sample_kernel(z_ref, xbc_ref, dt_ref, conv0_ref, h0_ref, convw_ref, convb_ref, dtb_ref,
                       alog_ref, dskip_ref, gnorm_ref, e_ref, y_ref, hout_ref, xs_ref, z8_ref, dt8_ref,
                       *, rows):
    for b in range(h0_ref.shape[0]):
        tok = pl.ds(b * rows, rows)
        _ssd_sample_one(z_ref.at[tok], xbc_ref.at[tok], dt_ref.at[tok], conv0_ref.at[:, b], h0_ref.at[b],
                        convw_ref, convb_ref, dtb_ref, alog_ref, dskip_ref, gnorm_ref, e_ref, y_ref.at[tok],
                        hout_ref.at[b], xs_ref.at[b], z8_ref.at[b], dt8_ref.at[b], rows)


def _ssd_sample_one(z_ref, xbc_ref, dt_ref, conv0_ref, h0_ref, convw_ref, convb_ref, dtb_ref,
                    alog_ref, dskip_ref, gnorm_ref, e_ref, y_ref, hout_ref, xs_ref, z8_ref, dt8_ref, rows):
    tail = CONV_WIDTH - 1
    r8 = 8
    row = lax.broadcasted_iota(jnp.int32, (r8, 1), 0)
    lane = lax.broadcasted_iota(jnp.int32, (1, LANES), 1)

    xs_ref[r8:2 * r8, :] = jnp.zeros((r8, CONV_DIM), F32)
    xs_ref[r8:r8 + rows, :] = xbc_ref[...]
    xs_ref[r8 - tail:r8, :] = conv0_ref[...]
    z8_ref[...] = jnp.zeros((r8, D_SSM), F32)
    z8_ref[0:rows, :] = z_ref[...]
    dt8_ref[...] = jnp.zeros((r8, LANES), F32)
    dt8_ref[0:rows, :] = dt_ref[...]

    conv = convb_ref[...] + convw_ref[tail:tail + 1, :] * xs_ref[r8:2 * r8, :]
    for i in range(tail):
        conv = conv + convw_ref[i:i + 1, :] * xs_ref[r8 - tail + i:2 * r8 - tail + i, :]
    xc = conv * _sigmoid(conv)
    xs = xc[:, 0:D_SSM]

    live = jnp.logical_and(lane < N_HEADS_S, row < rows)
    dt = jnp.where(live, _softplus(dt8_ref[...] + dtb_ref[...]), 0.0)
    da = dt * (-jnp.exp(alog_ref[...]))
    cum = da
    for k in range(1, rows):
        cum = cum + pltpu.roll(da, k, 0)
    last = rows - 1

    heads_per_group = N_HEADS_S // SSM_GROUPS
    gw = heads_per_group * SSM_HEAD_DIM
    bg = [xc[:, D_SSM + g * D_STATE:D_SSM + (g + 1) * D_STATE] for g in range(SSM_GROUPS)]
    cg = [xc[:, D_SSM + (SSM_GROUPS + g) * D_STATE:D_SSM + (SSM_GROUPS + g + 1) * D_STATE]
          for g in range(SSM_GROUPS)]
    group0 = lane < heads_per_group
    w = []
    for j in range(rows):
        cb = [jnp.sum(cg[g] * bg[g][j:j + 1, :], axis=-1, keepdims=True) for g in range(SSM_GROUPS)]
        dec = jnp.exp(jnp.where(row >= j, cum - cum[j:j + 1, :], NEG))
        w.append(jnp.where(group0, cb[0], cb[1]) * dec)

    stack = jnp.concatenate([cum, dt] + w, axis=0)
    hi, lo, lo2 = _split3(stack)
    ex = _dot(jnp.concatenate([hi, lo, lo2], axis=0), e_ref[...])
    ns = stack.shape[0]
    ex = ex[0:ns] + ex[ns:2 * ns] + ex[2 * ns:3 * ns]
    cum_e = ex[0:r8]
    xdt = xs * ex[r8:2 * r8]
    y = None
    for j in range(rows):
        term = ex[(2 + j) * r8:(3 + j) * r8] * xdt[j:j + 1, :]
        y = term if y is None else y + term
    last_e = cum_e[last:last + 1, :]

    state = h0_ref[...]
    state_bf = state.astype(BF16)
    zeros8 = jnp.zeros((r8, D_STATE), F32)
    y_off = [_dot_nt(jnp.concatenate([cg[g], zeros8], axis=0).astype(BF16),
                     state_bf[g * gw:(g + 1) * gw, :])[0:r8, :] for g in range(SSM_GROUPS)]
    y = y + jnp.concatenate(y_off, axis=1) * jnp.exp(cum_e)

    z = z8_ref[...]
    y = (y + xs * dskip_ref[...]) * (z * _sigmoid(z))
    outs = []
    for g in range(SSM_GROUPS):
        yg = y[:, g * gw:(g + 1) * gw]
        ms = jnp.mean(yg * yg, axis=-1, keepdims=True)
        outs.append((yg * lax.rsqrt(ms + EPS)) * gnorm_ref[:, g * gw:(g + 1) * gw])
    y_ref[...] = jnp.concatenate(outs, axis=1)[0:rows, :]

    pad = jnp.zeros((SSD_CHUNK - r8, D_SSM), F32)
    xw_t = jnp.concatenate([xdt * jnp.exp(last_e - cum_e), pad], axis=0).T.astype(BF16)
    state_decay = jnp.exp(jnp.broadcast_to(last_e, (SSD_CHUNK, D_SSM)).T)
    bpad = jnp.zeros((SSD_CHUNK - r8, D_STATE), F32)
    for g in range(SSM_GROUPS):
        b128 = jnp.concatenate([bg[g], bpad], axis=0).astype(BF16)
        hout_ref[g * gw:(g + 1) * gw, :] = (
            state_decay[g * gw:(g + 1) * gw, :] * state[g * gw:(g + 1) * gw, :]
            + _dot(xw_t[g * gw:(g + 1) * gw, :], b128))


def _ssd_param_specs():
    return [_resident((CONV_WIDTH, CONV_DIM)), _resident((1, CONV_DIM)), _resident((1, LANES)),
            _resident((1, LANES)), _resident((1, D_SSM)), _resident((1, D_SSM)),
            _resident((LANES, D_SSM))]


SSD_SAMPLE_SEQS = 8


def _ssd_sample(z, xbc, dt_raw, conv0, h0, ssd_params, rows):
    batch = h0.shape[0]
    nb = SSD_SAMPLE_SEQS
    tok = lambda w: pl.BlockSpec((nb * rows, w), lambda i: (i, 0))
    state = pl.BlockSpec((nb, D_SSM, D_STATE), lambda i: (i, 0, 0))
    return pl.pallas_call(
        functools.partial(_ssd_sample_kernel, rows=rows),
        grid=(batch // nb,),
        in_specs=[tok(D_SSM), tok(CONV_DIM), tok(LANES),
                  pl.BlockSpec((CONV_WIDTH - 1, nb, CONV_DIM), lambda i: (0, i, 0)), state] + _ssd_param_specs(),
        out_specs=[tok(D_SSM), state],
        out_shape=[jax.ShapeDtypeStruct((batch * rows, D_SSM), F32),
                   jax.ShapeDtypeStruct((batch, D_SSM, D_STATE), F32)],
        scratch_shapes=[pltpu.VMEM((nb, 16, CONV_DIM), F32), pltpu.VMEM((nb, 8, D_SSM), F32),
                        pltpu.VMEM((nb, 8, LANES), F32)],
        compiler_params=pltpu.CompilerParams(
            dimension_semantics=("arbitrary",), vmem_limit_bytes=VMEM_LIMIT),
        name="ssd_sample",
    )(z, xbc, dt_raw, conv0, h0, *ssd_params)


def _t5_bucket_np(dist):
    dist = np.asarray(dist, np.int64)
    df = np.maximum(dist, 1).astype(np.float32)
    large = MAX_EXACT + (np.log(df / np.float32(MAX_EXACT)) / np.float32(math.log(MAX_WINDOW / MAX_EXACT))
                         * np.float32(N_BUCKETS - MAX_EXACT)).astype(np.int32)
    return np.where(dist < MAX_EXACT, dist, np.minimum(large, N_BUCKETS - 1)).astype(np.int32)


def _band_bucket_tables():
    ki = np.arange(SPAN)[None, :]
    out = np.empty((len(DILATIONS), 2, SPAN, SPAN), np.int32)
    for bi, d in enumerate(DILATIONS):
        nrun = DILATIONS[-1] // d
        row = np.arange(SPAN)
        qi = (nrun * (row % (SPAN // nrun)) + row // (SPAN // nrun))[:, None]
        rel_prev = qi + SPAN - ki
        rel_cur = qi - ki
        out[bi, 0] = np.where(rel_prev <= SPAN, _t5_bucket_np(np.clip(rel_prev, 0, SPAN) * d), -1)
        out[bi, 1] = np.where(rel_cur >= 0, _t5_bucket_np(np.clip(rel_cur, 0, SPAN) * d), -1)
    return out


def _bias_table_kernel(tab_ref, rb_ref, out_ref, *, buckets):
    pair = pl.program_id(0)
    for bi in range(len(DILATIONS)):
        for half in range(2):
            tab = tab_ref[bi, half]
            acc = [jnp.full((SPAN, SPAN), NEG, F32) for _ in range(2)]
            for bkt in buckets[bi][half]:
                hit = tab == bkt
                for s in range(2):
                    acc[s] = jnp.where(hit, rb_ref[bkt, 2 * pair + s] * LOG2E, acc[s])
            for s in range(2):
                rows = slice(s * SPAN, (s + 1) * SPAN)
                if half == 0:
                    out_ref[0, bi, 0, rows, 0:SPAN] = jnp.full((SPAN, SPAN), NEG, F32)
                    out_ref[0, bi, 1, rows, 0:SPAN] = acc[s]
                else:
                    out_ref[0, bi, 0, rows, SPAN:2 * SPAN] = acc[s]
                    out_ref[0, bi, 1, rows, SPAN:2 * SPAN] = acc[s]


def _bias_tables(rel_bias):
    tabs = _band_bucket_tables()
    buckets = [[sorted(int(b) for b in np.unique(tabs[bi, half]) if b >= 0) for half in range(2)]
               for bi in range(len(DILATIONS))]
    nd = len(DILATIONS)
    return pl.pallas_call(
        functools.partial(_bias_table_kernel, buckets=buckets),
        grid=(N_HEADS_A // 2,),
        in_specs=[_resident((nd, 2, SPAN, SPAN)),
                  pl.BlockSpec(memory_space=pltpu.SMEM)],
        out_specs=pl.BlockSpec((1, nd, 2, 2 * SPAN, 2 * SPAN), lambda p: (p, 0, 0, 0, 0)),
        out_shape=jax.ShapeDtypeStruct((N_HEADS_A // 2, nd, 2, 2 * SPAN, 2 * SPAN), F32),
        compiler_params=pltpu.CompilerParams(dimension_semantics=("arbitrary",)),
        name="bias_tables",
    )(jnp.asarray(tabs), rel_bias)


ATT_T = SPAN * DILATIONS[-1]
ATT_UNROLL = 16
ATT_SUB = 2


def _query_runs(d, r, n):
    run = SPAN * d // DILATIONS[-1]
    return [pl.ds(pl.multiple_of((r + d * c) * SPAN + run * n, 8), run) for c in range(DILATIONS[-1] // d)]


def _load_runs(ref, runs, *lead):
    parts = [ref[(*lead, rows, slice(None))] for rows in runs]
    return parts[0] if len(parts) == 1 else jnp.concatenate(parts, axis=0)


def _store_runs(ref, runs, value, *lead):
    run = value.shape[0] // len(runs)
    for c, rows in enumerate(runs):
        ref[(*lead, rows, slice(None))] = value[c * run:(c + 1) * run, :]


def _attn_kernel(q_ref, k_ref, v_ref, bias_ref, o_ref, qp_ref, tmp_ref, kph_ref, vph_ref, acc_ref, m_ref,
                 l_ref):
    t = pl.program_id(2)
    lane = lax.broadcasted_iota(jnp.int32, (1, LANES), 1)
    first_head = lane < HEAD_DIM
    head_mask = (first_head, jnp.logical_not(first_head))
    nblk = ATT_T // SPAN
    dmax = DILATIONS[-1]

    @pl.when(t == 0)
    def _():
        for bi, d in enumerate(DILATIONS):
            ph = ATT_T // d + SPAN
            for r in range(d):
                kph_ref[bi, r * ph:r * ph + SPAN, :] = jnp.zeros((SPAN, LANES), BF16)
                vph_ref[bi, r * ph:r * ph + SPAN, :] = jnp.zeros((SPAN, LANES), BF16)

    d4, d16 = DILATIONS[1], DILATIONS[2]
    assert DILATIONS == (1, d4, d4 * d4)
    l4, l16 = ATT_T // d4, ATT_T // d16
    for r4 in range(d4):
        tmp_ref[r4 * l4:(r4 + 1) * l4, :] = q_ref[pl.ds(r4, l4, stride=d4), :] * (ATTN_SCALE * LOG2E)
    for r16 in range(d16):
        qp_ref[r16 * l16:(r16 + 1) * l16, :] = tmp_ref[pl.ds((r16 % d4) * l4 + r16 // d4, l16, stride=d4), :]
    for src_ref, dst_ref in ((k_ref, kph_ref), (v_ref, vph_ref)):
        dst_ref[0, SPAN:SPAN + ATT_T, :] = src_ref[...].astype(BF16)
        ph4, ph16 = l4 + SPAN, l16 + SPAN
        for r4 in range(d4):
            x4 = src_ref[pl.ds(r4, l4, stride=d4), :]
            tmp_ref[r4 * l4:(r4 + 1) * l4, :] = x4
            dst_ref[1, r4 * ph4 + SPAN:(r4 + 1) * ph4, :] = x4.astype(BF16)
        for r16 in range(d16):
            x16 = tmp_ref[pl.ds((r16 % d4) * l4 + r16 // d4, l16, stride=d4), :]
            dst_ref[2, r16 * ph16 + SPAN:(r16 + 1) * ph16, :] = x16.astype(BF16)

    order = tuple(reversed(range(len(DILATIONS))))
    for bi in order:
        d = DILATIONS[bi]
        ln = ATT_T // d
        ph = ln + SPAN
        nb = ln // SPAN
        first_branch = bi == order[0]
        last_branch = bi == order[-1]

        def group(it, carry, bi=bi, d=d, ph=ph, nb=nb, first_branch=first_branch, last_branch=last_branch):
            def fetch(j):
                sub = []
                for g in range(j * ATT_SUB, (j + 1) * ATT_SUB):
                    blk = it * ATT_UNROLL + g
                    r = blk // nb
                    n = blk % nb
                    runs = _query_runs(d, r, n)
                    krow = pl.multiple_of(r * ph + n * SPAN, SPAN)
                    variant = jnp.logical_or(t > 0, n > 0).astype(jnp.int32)
                    qf = _load_runs(qp_ref, runs)
                    qs = jnp.concatenate([jnp.where(head_mask[s], qf, 0.0).astype(BF16) for s in range(2)],
                                         axis=0)
                    logits = _dot_nt(qs, kph_ref[bi, pl.ds(krow, 2 * SPAN), :]) + bias_ref[0, bi, variant]
                    old = None
                    if not first_branch:
                        old = ([_load_runs(m_ref, runs, s) for s in range(2)],
                               [_load_runs(l_ref, runs, s) for s in range(2)], _load_runs(acc_ref, runs))
                    sub.append((runs, krow, old, logits))
                return sub

            nsub = ATT_UNROLL // ATT_SUB
            nxt = fetch(0)
            for j in range(nsub):
                cur = nxt
                if j + 1 < nsub:
                    nxt = fetch(j + 1)
                for runs, m_new, l_new, pv in softmax_pv(cur):
                    if last_branch:
                        _store_runs(acc_ref, runs, pv / jnp.where(first_head, l_new[0], l_new[1]))
                    else:
                        for s in range(2):
                            _store_runs(m_ref, runs, m_new[s], s)
                            _store_runs(l_ref, runs, l_new[s], s)
                        _store_runs(acc_ref, runs, pv)
            return carry

        def softmax_pv(sub, bi=bi, first_branch=first_branch):
            soft = []
            for runs, krow, old, sc2 in sub:
                m_new, l_new, alpha, ps = [], [], [], []
                for s in range(2):
                    sc = sc2[s * SPAN:(s + 1) * SPAN, :]
                    halves = (sc[:, 0:SPAN], sc[:, SPAN:2 * SPAN])
                    mb = jnp.max(jnp.maximum(halves[0], halves[1]), axis=-1, keepdims=True)
                    if first_branch:
                        m = jnp.broadcast_to(mb, (SPAN, LANES))
                    else:
                        m = jnp.maximum(old[0][s], mb)
                        alpha.append(jnp.exp2(old[0][s] - m))
                    p = [jnp.exp2(h - m) for h in halves]
                    rs = jnp.sum(p[0] + p[1], axis=-1, keepdims=True)
                    ps.append(jnp.concatenate([p[0].astype(BF16), p[1].astype(BF16)], axis=1))
                    m_new.append(m)
                    l_new.append(jnp.broadcast_to(rs, (SPAN, LANES)) if first_branch
                                 else alpha[s] * old[1][s] + rs)
                soft.append((m_new, l_new, alpha, ps))
            out = []
            for (runs, krow, old, _), (m_new, l_new, alpha, ps) in zip(sub, soft):
                pv2 = _dot(jnp.concatenate(ps, axis=0), vph_ref[bi, pl.ds(krow, 2 * SPAN), :])
                pv = jnp.where(first_head, pv2[0:SPAN, :], pv2[SPAN:2 * SPAN, :])
                if not first_branch:
                    pv = jnp.where(first_head, alpha[0], alpha[1]) * old[2] + pv
                out.append((runs, m_new, l_new, pv))
            return out

        lax.fori_loop(0, nblk // ATT_UNROLL, group, 0)

    for r16 in range(d16):
        tmp_ref[pl.ds((r16 % d4) * l4 + r16 // d4, l16, stride=d4), :] = acc_ref[r16 * l16:(r16 + 1) * l16, :]
    for r4 in range(d4):
        o_ref[pl.ds(r4, l4, stride=d4), :] = tmp_ref[r4 * l4:(r4 + 1) * l4, :]

    for bi, d in enumerate(DILATIONS):
        ln = ATT_T // d
        ph = ln + SPAN
        for r in range(d):
            kph_ref[bi, r * ph:r * ph + SPAN, :] = kph_ref[bi, (r + 1) * ph - SPAN:(r + 1) * ph, :]
            vph_ref[bi, r * ph:r * ph + SPAN, :] = vph_ref[bi, (r + 1) * ph - SPAN:(r + 1) * ph, :]


def _attn_prompt(qkv, bias, batch):
    m = qkv.shape[0]
    nt = m // batch // ATT_T
    npair = N_HEADS_A // 2
    nd = len(DILATIONS)
    blk = lambda off: pl.BlockSpec((ATT_T, LANES), lambda b, p, t: (b * nt + t, off + p))
    halo_rows = ATT_T + DILATIONS[-1] * SPAN
    return pl.pallas_call(
        _attn_kernel,
        grid=(batch, npair, nt),
        in_specs=[blk(0), blk(npair), blk(2 * npair),
                  pl.BlockSpec((1, nd, 2, 2 * SPAN, 2 * SPAN), lambda b, p, t: (p, 0, 0, 0, 0))],
        out_specs=blk(0),
        out_shape=jax.ShapeDtypeStruct((m, D_ATTN), F32),
        scratch_shapes=[pltpu.VMEM((ATT_T, LANES), F32),
                        pltpu.VMEM((ATT_T, LANES), F32),
                        pltpu.VMEM((nd, halo_rows, LANES), BF16),
                        pltpu.VMEM((nd, halo_rows, LANES), BF16),
                        pltpu.VMEM((ATT_T, LANES), F32),
                        pltpu.VMEM((2, ATT_T, LANES), F32),
                        pltpu.VMEM((2, ATT_T, LANES), F32)],
        compiler_params=pltpu.CompilerParams(
            dimension_semantics=("arbitrary", "arbitrary", "arbitrary"), vmem_limit_bytes=VMEM_LIMIT),
        name="attn_prompt",
    )(qkv, qkv, qkv, bias)


SAMPLE_ROWS = 4


def _sample_tables():
    pos = np.arange(MAX_WINDOW)[None, :]
    t = (np.arange(8) % SAMPLE_ROWS)[:, None]
    dist = MAX_WINDOW + t - pos
    mult = np.zeros(dist.shape, np.float32)
    for d in DILATIONS:
        mult += ((dist % d == 0) & (dist <= SPAN * d)).astype(np.float32)
    bucket = np.where(mult > 0, _t5_bucket_np(dist), -1).astype(np.int32)
    return bucket, np.concatenate([mult, mult], axis=0)


def _sample_bias_kernel(tab_ref, rb_ref, rbv_ref, bmain_ref, bnew_ref, *, buckets):
    pair = pl.program_id(0)
    tab = tab_ref[...]
    acc = [jnp.full(tab.shape, NEG, F32) for _ in range(2)]
    for bkt in buckets:
        hit = tab == bkt
        for s in range(2):
            acc[s] = jnp.where(hit, rb_ref[bkt, 2 * pair + s], acc[s])
    for s in range(2):
        bmain_ref[0, s * 8:(s + 1) * 8, :] = acc[s]
    t = lax.broadcasted_iota(jnp.int32, (8, 1), 0) % SAMPLE_ROWS
    for tp in range(SAMPLE_ROWS):
        b = jnp.full((8, LANES), NEG, F32)
        for dist in range(SAMPLE_ROWS):
            b = jnp.where(t - tp == dist, rbv_ref[dist:dist + 1, :], b)
        bnew_ref[tp] = b


def _sample_bias_tables(rel_bias):
    bucket, mult = _sample_tables()
    buckets = sorted(int(b) for b in np.unique(bucket) if b >= 0)
    npair = N_HEADS_A // 2
    rb_lanes = jnp.pad(rel_bias, ((0, 0), (0, LANES - N_HEADS_A)))
    bmain, bnew = pl.pallas_call(
        functools.partial(_sample_bias_kernel, buckets=buckets),
        grid=(npair,),
        in_specs=[_resident((8, MAX_WINDOW)), pl.BlockSpec(memory_space=pltpu.SMEM),
                  _resident((N_BUCKETS, LANES))],
        out_specs=[pl.BlockSpec((1, 16, MAX_WINDOW), lambda p: (p, 0, 0)),
                   pl.BlockSpec((SAMPLE_ROWS, 8, LANES), lambda p: (0, 0, 0))],
        out_shape=[jax.ShapeDtypeStruct((npair, 16, MAX_WINDOW), F32),
                   jax.ShapeDtypeStruct((SAMPLE_ROWS, 8, LANES), F32)],
        compiler_params=pltpu.CompilerParams(dimension_semantics=("arbitrary",)),
        name="sample_bias_tables",
    )(jnp.asarray(bucket), rel_bias, rb_lanes)
    return bmain, bnew, jnp.asarray(mult)


def _attn_sample_body(qkv_ref, kt_ref, vt_ref, bmain_ref, mult_ref, bnew_ref, e_ref, om_ref):
    rows = SAMPLE_ROWS
    npair = N_HEADS_A // 2
    second = pl.program_id(0) % 2 == 1
    lower = lax.broadcasted_iota(jnp.int32, (2 * rows, 1), 0) < rows
    mine = jnp.logical_xor(lower, second)
    blk = qkv_ref[...]
    q8_all = jnp.where(mine, blk, pltpu.roll(blk, rows, 0))
    q8 = q8_all[:, 0:D_ATTN] * ATTN_SCALE
    kn = q8_all[:, D_ATTN:2 * D_ATTN]
    vn = q8_all[:, 2 * D_ATTN:3 * D_ATTN]
    lane = lax.broadcasted_iota(jnp.int32, (1, LANES), 1)
    first_head = lane < HEAD_DIM
    e = e_ref[...]

    s_new = []
    m_tot = jnp.full((8, LANES), NEG, F32)
    for tp in range(rows):
        hi, lo, lo2 = _split3(q8 * kn[tp:tp + 1, :])
        s = _dot_nt(hi, e) + _dot_nt(lo, e) + _dot_nt(lo2, e) + bnew_ref[tp]
        s_new.append(s)
        m_tot = jnp.maximum(m_tot, s)

    mult = mult_ref[...]
    l_tot = jnp.zeros((8, LANES), F32)

    def window_logits(p):
        qp = q8[:, p * LANES:(p + 1) * LANES]
        qbd = jnp.concatenate([jnp.where(first_head, qp, 0.0), jnp.where(first_head, 0.0, qp)],
                              axis=0).astype(BF16)
        return _dot(qbd, kt_ref[0, p].astype(BF16)) + bmain_ref[p]

    s_next = window_logits(0)
    for p in range(npair):
        s16 = s_next
        if p + 1 < npair:
            s_next = window_logits(p + 1)
        m_new =jnp.concatenate([m_tot[:, 2 * p:2 * p + 1], m_tot[:, 2 * p + 1:2 * p + 2]], axis=0)
        m = jnp.maximum(jnp.max(s16, axis=-1, keepdims=True), m_new)
        pr = mult * jnp.exp(s16 - m)
        ls = jnp.sum(pr, axis=-1, keepdims=True)
        o16 = _dot_nt(pr.astype(BF16), vt_ref[0, p].astype(BF16))
        om_ref[:, p * LANES:(p + 1) * LANES] = jnp.where(first_head, o16[0:8], o16[8:16])
        m_tot = jnp.where(lane == 2 * p, m[0:8], jnp.where(lane == 2 * p + 1, m[8:16], m_tot))
        l_tot = jnp.where(lane == 2 * p, ls[0:8], jnp.where(lane == 2 * p + 1, ls[8:16], l_tot))
        yield

    t = lax.broadcasted_iota(jnp.int32, (8, 1), 0) % rows
    o = om_ref[...]
    for tp in range(rows):
        c = jnp.where(t == tp, float(len(DILATIONS)), jnp.where(t > tp, 1.0, 0.0))
        pn = c * jnp.exp(s_new[tp] - m_tot)
        l_tot = l_tot + pn
        o = o + _expand_heads(pn, e_ref) * vn[tp:tp + 1, :]
    o = o / _expand_heads(l_tot, e_ref)
    return o, second, lower


def _interleave(*gens):
    results = [None] * len(gens)
    live = list(range(len(gens)))
    while live:
        for k in list(live):
            try:
                next(gens[k])
            except StopIteration as stop:
                results[k] = stop.value
                live.remove(k)
    return results


def _sample_attn_prompt_ssd_kernel(qkv_ref, kt_ref, vt_ref, bmain_ref, mult_ref, bnew_ref, e_ref,
                                   z_ref, xbc_ref, dt_ref, conv0_ref, h0_ref, convw_ref, convb_ref, dtb_ref,
                                   alog_ref, dskip_ref, gnorm_ref, o_ref, y_ref, hout_ref,
                                   om_ref, xpad_ref, state_ref, *, chunks):
    c = pl.program_id(0) % chunks

    @pl.when(c == 0)
    def _():
        state_ref[...] = h0_ref[0]
        xpad_ref[8 - (CONV_WIDTH - 1):8, :] = conv0_ref[0]

    attn = _attn_sample_body(qkv_ref, kt_ref, vt_ref, bmain_ref, mult_ref, bnew_ref, e_ref, om_ref)
    scan = _ssd_chunk(z_ref.at[0], xbc_ref.at[0], dt_ref.at[0], y_ref.at[0], xpad_ref, state_ref,
                      convw_ref, convb_ref, dtb_ref, alog_ref, dskip_ref, gnorm_ref, e_ref)
    o, second, lower = _interleave(attn, scan)[0]

    @pl.when(jnp.logical_not(second))
    def _():
        o_ref[...] = o

    @pl.when(second)
    def _():
        o_ref[...] = jnp.where(lower, o_ref[...], o)

    @pl.when(c == chunks - 1)
    def _():
        hout_ref[0] = state_ref[...]


def _sample_attn_prompt_ssd(qkv, kt, vt, bmain, mult, bnew, expand, z, xbc, dt_raw, conv0, h0, ssd_params):
    rows = SAMPLE_ROWS
    batch = qkv.shape[0] // rows
    nseq, seq = z.shape[0], z.shape[1]
    chunks = seq // SSD_CHUNK
    assert batch == nseq * chunks, "one prompt chunk per sample sequence"
    npair = N_HEADS_A // 2
    cache = pl.BlockSpec((1, npair, LANES, MAX_WINDOW), lambda i: (i, 0, 0, 0))
    chunk = lambda w: pl.BlockSpec((1, SSD_CHUNK, w), lambda i: (i // chunks, i % chunks, 0))
    per_seq = lambda r, w: pl.BlockSpec((1, r, w), lambda i: (i // chunks, 0, 0))
    return pl.pallas_call(
        functools.partial(_sample_attn_prompt_ssd_kernel, chunks=chunks),
        grid=(batch,),
        in_specs=[pl.BlockSpec((2 * rows, 3 * D_ATTN), lambda i: (i // 2, 0)), cache, cache,
                  _resident((npair, 16, MAX_WINDOW)), _resident((16, MAX_WINDOW)),
                  _resident((rows, 8, LANES)), _resident((LANES, D_ATTN)),
                  chunk(D_SSM), chunk(CONV_DIM), chunk(LANES), per_seq(CONV_WIDTH - 1, CONV_DIM),
                  per_seq(D_SSM, D_STATE)] + _ssd_param_specs()[:-1],
        out_specs=[pl.BlockSpec((2 * rows, D_ATTN), lambda i: (i // 2, 0)), chunk(D_SSM),
                   per_seq(D_SSM, D_STATE)],
        out_shape=[jax.ShapeDtypeStruct((batch * rows, D_ATTN), F32),
                   jax.ShapeDtypeStruct((nseq, seq, D_SSM), F32),
                   jax.ShapeDtypeStruct((nseq, D_SSM, D_STATE), F32)],
        scratch_shapes=[pltpu.VMEM((8, D_ATTN), F32), pltpu.VMEM((8 + SSD_CHUNK, CONV_DIM), F32),
                        pltpu.VMEM((D_SSM, D_STATE), F32)],
        compiler_params=pltpu.CompilerParams(
            dimension_semantics=("arbitrary",), vmem_limit_bytes=VMEM_LIMIT),
        name="sample_attn_prompt_ssd",
    )(qkv, kt, vt, bmain, mult, bnew, expand, z, xbc, dt_raw, conv0, h0, *ssd_params[:-1])


def _head_expand_matrix():
    e = np.zeros((LANES, D_SSM), np.float32)
    for h in range(N_HEADS_S):
        e[h, h * SSM_HEAD_DIM:(h + 1) * SSM_HEAD_DIM] = 1.0
    return jnp.asarray(e, BF16)


def _prep_params(norm_mix, w_in, q_norm, k_norm, conv_w, conv_b, dt_bias, a_log, d_skip, ssm_norm,
                 w_out, norm_ffn, w_up, w_down):
    n_main = 3 * D_ATTN + D_SSM + CONV_DIM
    pad_heads = lambda v: jnp.pad(v.reshape(1, N_HEADS_S), ((0, 0), (0, LANES - N_HEADS_S)))
    return dict(
        gmix=norm_mix.reshape(1, D_MODEL),
        w_main=w_in.astype(BF16),
        w_dt=jnp.pad(w_in[:, n_main:], ((0, 0), (0, LANES - N_HEADS_S))).astype(BF16),
        gq=jnp.tile(q_norm, N_HEADS_A).reshape(1, D_ATTN),
        gk=jnp.tile(k_norm, N_HEADS_A).reshape(1, D_ATTN),
        conv_w=conv_w,
        conv_b=conv_b.reshape(1, CONV_DIM),
        dt_bias=pad_heads(dt_bias),
        a_log=pad_heads(a_log),
        d_skip_e=jnp.repeat(d_skip, SSM_HEAD_DIM).reshape(1, D_SSM),
        gnorm=ssm_norm.reshape(1, D_SSM),
        expand=_head_expand_matrix(),
        w_out=w_out.astype(BF16),
        gffn=norm_ffn.reshape(1, D_MODEL),
        w_up=w_up.astype(BF16),
        w_down=w_down.astype(BF16),
    )


def _ssd_param_list(p):
    return (p["conv_w"], p["conv_b"], p["dt_bias"], p["a_log"], p["d_skip_e"], p["gnorm"], p["expand"])


def kernel(x_prompt, x_sample, cache_attn_k, cache_attn_v, state_conv, state_ssm, norm_mix, w_in, q_norm,
           k_norm, rel_bias, conv_w, conv_b, dt_bias, a_log, d_skip, ssm_norm, w_out, norm_ffn, w_up, w_down):
    depth = w_in.shape[0]
    assert depth == 1, "single-layer decoder"
    l = 0
    p = _prep_params(norm_mix[l], w_in[l], q_norm[l], k_norm[l], conv_w[l], conv_b[l], dt_bias[l], a_log[l],
                     d_skip[l], ssm_norm[l], w_out[l], norm_ffn[l], w_up[l], w_down[l])
    ssd_params = _ssd_param_list(p)
    tm = 512

    bp, sp = x_prompt.shape[:2]
    keep = min(MAX_WINDOW, sp)
    xp = x_prompt.reshape(bp * sp, D_MODEL)
    qkv, z, xbc, dt_raw, ktail, vtail = _inproj(xp, p["gmix"], p["w_main"], p["w_dt"], p["gq"], p["gk"], tm,
                                                tail=(sp, keep))
    attn = _attn_prompt(qkv, _bias_tables(rel_bias), bp)
    new_k_prompt = ktail.reshape(depth, bp, keep, N_HEADS_A, HEAD_DIM)
    new_v_prompt = vtail.reshape(depth, bp, keep, N_HEADS_A, HEAD_DIM)
    new_conv_prompt = xbc.reshape(bp, sp, CONV_DIM)[:, sp - (CONV_WIDTH - 1):].reshape(
        depth, bp, CONV_WIDTH - 1, CONV_DIM)

    bs, ts = x_sample.shape[:2]
    assert cache_attn_k.shape[2] == MAX_WINDOW and ts == SAMPLE_ROWS
    xs = x_sample.reshape(bs * ts, D_MODEL)
    qkv_s, z_s, xbc_s, dt_s = _inproj(xs, p["gmix"], p["w_main"], p["w_dt"], p["gq"], p["gk"], tm)
    npair = N_HEADS_A // 2
    kt = jnp.transpose(cache_attn_k[l], (0, 2, 3, 1)).reshape(bs, npair, LANES, MAX_WINDOW)
    vt = jnp.transpose(cache_attn_v[l], (0, 2, 3, 1)).reshape(bs, npair, LANES, MAX_WINDOW)
    bmain, bnew, mult = _sample_bias_tables(rel_bias)
    conv0 = jnp.zeros((bp, CONV_WIDTH - 1, CONV_DIM), F32)
    h0 = jnp.zeros((bp, D_SSM, D_STATE), F32)
    attn_s, ssm_o, hst = _sample_attn_prompt_ssd(
        qkv_s, kt, vt, bmain, mult, bnew, p["expand"], z.reshape(bp, sp, D_SSM), xbc.reshape(bp, sp, CONV_DIM),
        dt_raw.reshape(bp, sp, LANES), conv0, h0, ssd_params)
    y_prompt = _outffn(xp, attn, ssm_o.reshape(bp * sp, D_SSM), p["w_out"], p["gffn"], p["w_up"], p["w_down"], tm)
    new_ssm_prompt = hst.reshape(depth, bp, N_HEADS_S, SSM_HEAD_DIM, D_STATE)
    ssm_os, hsts = _ssd_sample(z_s, xbc_s, dt_s, jnp.transpose(state_conv[l], (1, 0, 2)),
                               state_ssm[l].reshape(bs, D_SSM, D_STATE), ssd_params, ts)
    y_sample = _outffn(xs, attn_s, ssm_os, p["w_out"], p["gffn"], p["w_up"], p["w_down"], tm)
    new_k_sample = qkv_s[:, D_ATTN:2 * D_ATTN].reshape(depth, bs, ts, N_HEADS_A, HEAD_DIM)
    new_v_sample = qkv_s[:, 2 * D_ATTN:].reshape(depth, bs, ts, N_HEADS_A, HEAD_DIM)
    new_conv_sample = xbc_s.reshape(bs, ts, CONV_DIM)[:, ts - (CONV_WIDTH - 1):].reshape(
        depth, bs, CONV_WIDTH - 1, CONV_DIM)
    new_ssm_sample = hsts.reshape(depth, bs, N_HEADS_S, SSM_HEAD_DIM, D_STATE)

    return (y_prompt.reshape(bp, sp, D_MODEL), y_sample.reshape(bs, ts, D_MODEL),
            new_k_prompt, new_v_prompt, new_conv_prompt, new_ssm_prompt,
            new_k_sample, new_v_sample, new_conv_sample, new_ssm_sample)
```

```python
import functools
import math

import jax
import jax.numpy as jnp
import numpy as np
from jax import lax
from jax.experimental import pallas as pl
from jax.experimental.pallas import tpu as pltpu

F32 = jnp.float32
BF16 = jnp.bfloat16

D_MODEL = 1024
HEAD_DIM = 64
N_HEADS_A = 16
D_ATTN = 1024
SPAN = 128
DILATIONS = (1, 4, 16)
MAX_WINDOW = 2048
N_BUCKETS = 32
MAX_EXACT = 16
SSM_HEAD_DIM = 64
N_HEADS_S = 16
D_SSM = 1024
SSM_GROUPS = 2
D_STATE = 128
CONV_WIDTH = 4
CONV_DIM = D_SSM + 2 * SSM_GROUPS * D_STATE
SSD_CHUNK = 128
D_FF = 4096
EPS = 1e-6
ATTN_SCALE = HEAD_DIM ** -0.5
LOG2E = 1.4426950408889634

LANES = 128
NEG = -1e30
VMEM_LIMIT = 56 * 1024 * 1024


def _resident(shape):
    nd = len(shape)
    return pl.BlockSpec(shape, lambda *_: (0,) * nd, pipeline_mode=pl.Buffered(1))


def _split3(x):
    hi = x.astype(BF16)
    r1 = x - hi.astype(F32)
    lo = r1.astype(BF16)
    lo2 = (r1 - lo.astype(F32)).astype(BF16)
    return hi, lo, lo2


def _dot(a, b):
    return jnp.dot(a, b, preferred_element_type=F32)


def _dot_nt(a, b):
    return lax.dot_general(a, b, (((1,), (1,)), ((), ())), preferred_element_type=F32)


def _sigmoid(x):
    return 0.5 * jnp.tanh(0.5 * x) + 0.5


def _softplus(x):
    return jnp.maximum(x, 0.0) + jnp.log1p(jnp.exp(-jnp.abs(x)))


IN_TN = 512


def _inproj_kernel(x_ref, gmix_ref, w_ref, wdt_ref, gq_ref, gk_ref,
                   qkv_ref, z_ref, xbc_ref, dt_ref, *tail_refs, with_tail):
    x = x_ref[...]
    ms = jnp.mean(x * x, axis=-1, keepdims=True)
    xn = ((x * lax.rsqrt(ms + EPS)) * gmix_ref[...]).astype(BF16)
    first_head = lax.broadcasted_iota(jnp.int32, (1, LANES), 1) < HEAD_DIM

    n_main = 3 * D_ATTN + D_SSM + CONV_DIM
    for j in range(n_main // IN_TN):
        c0 = j * IN_TN
        t = _dot(xn, w_ref[:, c0:c0 + IN_TN])
        if c0 < 2 * D_ATTN:
            g_ref = gq_ref if c0 < D_ATTN else gk_ref
            g0 = c0 % D_ATTN
            for c in range(IN_TN // LANES):
                tc = t[:, c * LANES:(c + 1) * LANES]
                s = tc * tc
                sa = jnp.sum(jnp.where(first_head, s, 0.0), axis=-1, keepdims=True)
                sb = jnp.sum(jnp.where(first_head, 0.0, s), axis=-1, keepdims=True)
                inv = lax.rsqrt(jnp.where(first_head, sa, sb) * (1.0 / HEAD_DIM) + EPS)
                out = (tc * inv) * g_ref[:, g0 + c * LANES:g0 + (c + 1) * LANES]
                qkv_ref[:, c0 + c * LANES:c0 + (c + 1) * LANES] = out
                if with_tail and c0 >= D_ATTN:
                    tail_refs[0][0, :, g0 + c * LANES:g0 + (c + 1) * LANES] = out
        elif c0 < 3 * D_ATTN:
            qkv_ref[:, c0:c0 + IN_TN] = t
            if with_tail:
                tail_refs[1][0, :, c0 - 2 * D_ATTN:c0 - 2 * D_ATTN + IN_TN] = t
        elif c0 < 3 * D_ATTN + D_SSM:
            z_ref[:, c0 - 3 * D_ATTN:c0 - 3 * D_ATTN + IN_TN] = t
        else:
            o0 = c0 - 3 * D_ATTN - D_SSM
            xbc_ref[:, o0:o0 + IN_TN] = t
    dt_ref[...] = _dot(xn, wdt_ref[...])


def _inproj(x2d, gmix, w_main, w_dt, gq, gk, tm, tail=None):
    m = x2d.shape[0]
    row = lambda w: pl.BlockSpec((tm, w), lambda i: (i, 0))
    out_specs = [row(3 * D_ATTN), row(D_SSM), row(CONV_DIM), row(LANES)]
    out_shape = [jax.ShapeDtypeStruct((m, 3 * D_ATTN), F32), jax.ShapeDtypeStruct((m, D_SSM), F32),
                 jax.ShapeDtypeStruct((m, CONV_DIM), F32), jax.ShapeDtypeStruct((m, LANES), F32)]
    if tail is not None:
        per_seq, kept = tail[0] // tm, tail[1] // tm
        kept_block = lambda i: (i // per_seq, jnp.maximum(i % per_seq - (per_seq - kept), 0), 0)
        out_specs += [pl.BlockSpec((1, tm, D_ATTN), kept_block)] * 2
        out_shape += [jax.ShapeDtypeStruct((m // tail[0], tail[1], D_ATTN), F32)] * 2
    return pl.pallas_call(
        functools.partial(_inproj_kernel, with_tail=tail is not None),
        grid=(m // tm,),
        in_specs=[row(D_MODEL), _resident((1, D_MODEL)), _resident(w_main.shape),
                  _resident((D_MODEL, LANES)), _resident((1, D_ATTN)), _resident((1, D_ATTN))],
        out_specs=out_specs,
        out_shape=out_shape,
        compiler_params=pltpu.CompilerParams(
            dimension_semantics=("arbitrary",), vmem_limit_bytes=VMEM_LIMIT),
        name="inproj",
    )(x2d, gmix, w_main, w_dt, gq, gk)


FFN_TF = 1024


def _outffn_kernel(x_ref, attn_ref, ssm_ref, wout_ref, gffn_ref, wup_ref, wdown_ref, y_ref):
    a = attn_ref[...].astype(BF16)
    s = ssm_ref[...].astype(BF16)
    h = x_ref[...] + _dot(a, wout_ref[0:D_ATTN, :]) + _dot(s, wout_ref[D_ATTN:D_ATTN + D_SSM, :])
    ms = jnp.mean(h * h, axis=-1, keepdims=True)
    hn = ((h * lax.rsqrt(ms + EPS)) * gffn_ref[...]).astype(BF16)
    acc = None
    for f in range(D_FF // FFN_TF):
        u = _dot(hn, wup_ref[:, f * FFN_TF:(f + 1) * FFN_TF])
        u = jnp.maximum(u, 0.0)
        u = (u * u).astype(BF16)
        d = _dot(u, wdown_ref[f * FFN_TF:(f + 1) * FFN_TF, :])
        acc = d if acc is None else acc + d
    y_ref[...] = h + acc


def _outffn(x2d, attn, ssm_o, w_out, gffn, w_up, w_down, tm):
    m = x2d.shape[0]
    row = pl.BlockSpec((tm, D_MODEL), lambda i: (i, 0))
    return pl.pallas_call(
        _outffn_kernel,
        grid=(m // tm,),
        in_specs=[row, row, row, _resident((D_ATTN + D_SSM, D_MODEL)), _resident((1, D_MODEL)),
                  _resident((D_MODEL, D_FF)), _resident((D_FF, D_MODEL))],
        out_specs=row,
        out_shape=jax.ShapeDtypeStruct((m, D_MODEL), F32),
        compiler_params=pltpu.CompilerParams(
            dimension_semantics=("arbitrary",), vmem_limit_bytes=VMEM_LIMIT),
        name="outffn",
    )(x2d, attn, ssm_o, w_out, gffn, w_up, w_down)


def _expand_heads(v, e_ref):
    hi, lo, lo2 = _split3(v)
    e = e_ref[...]
    return _dot(hi, e) + _dot(lo, e) + _dot(lo2, e)


def _ssd_chunk(z_ref, xbc_ref, dt_ref, y_ref, xpad_ref, state_ref,
               convw_ref, convb_ref, dtb_ref, alog_ref, dskip_ref, gnorm_ref, e_ref):
    q = SSD_CHUNK
    pad = 8
    tail = CONV_WIDTH - 1

    xbc = xbc_ref[...]
    xpad_ref[pad:pad + q, :] = xbc
    conv = convb_ref[...] + convw_ref[tail:tail + 1, :] * xbc
    for i in range(tail):
        conv = conv + convw_ref[i:i + 1, :] * xpad_ref[pad - tail + i:pad - tail + i + q, :]
    xpad_ref[pad - tail:pad, :] = xbc[q - tail:q, :]
    xc = conv * _sigmoid(conv)
    xs = xc[:, 0:D_SSM]

    lane = lax.broadcasted_iota(jnp.int32, (1, LANES), 1)
    dt = jnp.where(lane < N_HEADS_S, _softplus(dt_ref[...] + dtb_ref[...]), 0.0)
    da = dt * (-jnp.exp(alog_ref[...]))

    ii = lax.broadcasted_iota(jnp.int32, (q, q), 0)
    jj = lax.broadcasted_iota(jnp.int32, (q, q), 1)
    causal = ii >= jj
    tri = jnp.where(causal, 1.0, 0.0).astype(BF16)
    hi, lo, lo2 = _split3(da)
    cum = _dot(tri, hi) + _dot(tri, lo) + _dot(tri, lo2)
    cum_t = cum.T

    cum_e = _expand_heads(cum, e_ref)
    last_e = cum_e[q - 1:q, :]
    xdt = xs * _expand_heads(dt, e_ref)
    xw_t = (xdt * jnp.exp(last_e - cum_e)).T.astype(BF16)
    state_decay = jnp.exp(jnp.broadcast_to(last_e, (q, D_SSM)).T)
    off_scale = jnp.exp(cum_e)

    first_head = lane < SSM_HEAD_DIM
    heads_per_group = N_HEADS_S // SSM_GROUPS
    gw = heads_per_group * SSM_HEAD_DIM
    state = state_ref[...]
    state_bf = state.astype(BF16)
    yield
    for g in range(SSM_GROUPS):
        bg = xc[:, D_SSM + g * D_STATE:D_SSM + (g + 1) * D_STATE].astype(BF16)
        cg = xc[:, D_SSM + (SSM_GROUPS + g) * D_STATE:D_SSM + (SSM_GROUPS + g + 1) * D_STATE].astype(BF16)
        cb = _dot_nt(cg, bg)
        y_off = _dot_nt(cg, state_bf[g * gw:(g + 1) * gw, :]) * off_scale[:, g * gw:(g + 1) * gw]
        y_ref[:, g * gw:(g + 1) * gw] = y_off
        for hp in range(heads_per_group // 2):
            pair = g * (heads_per_group // 2) + hp
            xp = xdt[:, pair * LANES:(pair + 1) * LANES]
            yp = None
            for s in range(2):
                h = 2 * pair + s
                seg = cum[:, h:h + 1] - cum_t[h:h + 1, :]
                dec = jnp.exp(jnp.where(causal, seg, NEG))
                m = (cb * dec).astype(BF16)
                xm = jnp.where(first_head if s == 0 else jnp.logical_not(first_head), xp, 0.0).astype(BF16)
                d = _dot(m, xm)
                yp = d if yp is None else yp + d
            y_ref[:, pair * LANES:(pair + 1) * LANES] += yp
            yield
        state_ref[g * gw:(g + 1) * gw, :] = (
            state_decay[g * gw:(g + 1) * gw, :] * state[g * gw:(g + 1) * gw, :]
            + _dot(xw_t[g * gw:(g + 1) * gw, :], bg))

    z = z_ref[...]
    y = (y_ref[...] + xs * dskip_ref[...]) * (z * _sigmoid(z))
    for g in range(SSM_GROUPS):
        yg = y[:, g * gw:(g + 1) * gw]
        ms = jnp.mean(yg * yg, axis=-1, keepdims=True)
        y_ref[:, g * gw:(g + 1) * gw] = (yg * lax.rsqrt(ms + EPS)) * gnorm_ref[:, g * gw:(g + 1) * gw]


def _ssd_sample_kernel(z_ref, xbc_ref, dt_ref, conv0_ref, h0_ref, convw_ref, convb_ref, dtb_ref,
                       alog_ref, dskip_ref, gnorm_ref, e_ref, y_ref, hout_ref, xs_ref, z8_ref, dt8_ref,
                       *, rows):
    for b in range(h0_ref.shape[0]):
        tok = pl.ds(b * rows, rows)
        _ssd_sample_one(z_ref.at[tok], xbc_ref.at[tok], dt_ref.at[tok], conv0_ref.at[:, b], h0_ref.at[b],
                        convw_ref, convb_ref, dtb_ref, alog_ref, dskip_ref, gnorm_ref, e_ref, y_ref.at[tok],
                        hout_ref.at[b], xs_ref.at[b], z8_ref.at[b], dt8_ref.at[b], rows)


def _ssd_sample_one(z_ref, xbc_ref, dt_ref, conv0_ref, h0_ref, convw_ref, convb_ref, dtb_ref,
                    alog_ref, dskip_ref, gnorm_ref, e_ref, y_ref, hout_ref, xs_ref, z8_ref, dt8_ref, rows):
    tail = CONV_WIDTH - 1
    r8 = 8
    row = lax.broadcasted_iota(jnp.int32, (r8, 1), 0)
    lane = lax.broadcasted_iota(jnp.int32, (1, LANES), 1)

    xs_ref[r8:2 * r8, :] = jnp.zeros((r8, CONV_DIM), F32)
    xs_ref[r8:r8 + rows, :] = xbc_ref[...]
    xs_ref[r8 - tail:r8, :] = conv0_ref[...]
    z8_ref[...] = jnp.zeros((r8, D_SSM), F32)
    z8_ref[0:rows, :] = z_ref[...]
    dt8_ref[...] = jnp.zeros((r8, LANES), F32)
    dt8_ref[0:rows, :] = dt_ref[...]

    conv = convb_ref[...] + convw_ref[tail:tail + 1, :] * xs_ref[r8:2 * r8, :]
    for i in range(tail):
        conv = conv + convw_ref[i:i + 1, :] * xs_ref[r8 - tail + i:2 * r8 - tail + i, :]
    xc = conv * _sigmoid(conv)
    xs = xc[:, 0:D_SSM]

    live = jnp.logical_and(lane < N_HEADS_S, row < rows)
    dt = jnp.where(live, _softplus(dt8_ref[...] + dtb_ref[...]), 0.0)
    da = dt * (-jnp.exp(alog_ref[...]))
    cum = da
    for k in range(1, rows):
        cum = cum + pltpu.roll(da, k, 0)
    last = rows - 1

    heads_per_group = N_HEADS_S // SSM_GROUPS
    gw = heads_per_group * SSM_HEAD_DIM
    bg = [xc[:, D_SSM + g * D_STATE:D_SSM + (g + 1) * D_STATE] for g in range(SSM_GROUPS)]
    cg = [xc[:, D_SSM + (SSM_GROUPS + g) * D_STATE:D_SSM + (SSM_GROUPS + g + 1) * D_STATE]
          for g in range(SSM_GROUPS)]
    group0 = lane < heads_per_group
    w = []
    for j in range(rows):
        cb = [jnp.sum(cg[g] * bg[g][j:j + 1, :], axis=-1, keepdims=True) for g in range(SSM_GROUPS)]
        dec = jnp.exp(jnp.where(row >= j, cum - cum[j:j + 1, :], NEG))
        w.append(jnp.where(group0, cb[0], cb[1]) * dec)

    stack = jnp.concatenate([cum, dt] + w, axis=0)
    hi, lo, lo2 = _split3(stack)
    ex = _dot(jnp.concatenate([hi, lo, lo2], axis=0), e_ref[...])
    ns = stack.shape[0]
    ex = ex[0:ns] + ex[ns:2 * ns] + ex[2 * ns:3 * ns]
    cum_e = ex[0:r8]
    xdt = xs * ex[r8:2 * r8]
    y = None
    for j in range(rows):
        term = ex[(2 + j) * r8:(3 + j) * r8] * xdt[j:j + 1, :]
        y = term if y is None else y + term
    last_e = cum_e[last:last + 1, :]

    state = h0_ref[...]
    state_bf = state.astype(BF16)
    zeros8 = jnp.zeros((r8, D_STATE), F32)
    y_off = [_dot_nt(jnp.concatenate([cg[g], zeros8], axis=0).astype(BF16),
                     state_bf[g * gw:(g + 1) * gw, :])[0:r8, :] for g in range(SSM_GROUPS)]
    y = y + jnp.concatenate(y_off, axis=1) * jnp.exp(cum_e)

    z = z8_ref[...]
    y = (y + xs * dskip_ref[...]) * (z * _sigmoid(z))
    outs = []
    for g in range(SSM_GROUPS):
        yg = y[:, g * gw:(g + 1) * gw]
        ms = jnp.mean(yg * yg, axis=-1, keepdims=True)
        outs.append((yg * lax.rsqrt(ms + EPS)) * gnorm_ref[:, g * gw:(g + 1) * gw])
    y_ref[...] = jnp.concatenate(outs, axis=1)[0:rows, :]

    pad = jnp.zeros((SSD_CHUNK - r8, D_SSM), F32)
    xw_t = jnp.concatenate([xdt * jnp.exp(last_e - cum_e), pad], axis=0).T.astype(BF16)
    state_decay = jnp.exp(jnp.broadcast_to(last_e, (SSD_CHUNK, D_SSM)).T)
    bpad = jnp.zeros((SSD_CHUNK - r8, D_STATE), F32)
    for g in range(SSM_GROUPS):
        b128 = jnp.concatenate([bg[g], bpad], axis=0).astype(BF16)
        hout_ref[g * gw:(g + 1) * gw, :] = (
            state_decay[g * gw:(g + 1) * gw, :] * state[g * gw:(g + 1) * gw, :]
            + _dot(xw_t[g * gw:(g + 1) * gw, :], b128))


def _ssd_param_specs():
    return [_resident((CONV_WIDTH, CONV_DIM)), _resident((1, CONV_DIM)), _resident((1, LANES)),
            _resident((1, LANES)), _resident((1, D_SSM)), _resident((1, D_SSM)),
            _resident((LANES, D_SSM))]


SSD_SAMPLE_SEQS = 8


def _ssd_sample(z, xbc, dt_raw, conv0, h0, ssd_params, rows):
    batch = h0.shape[0]
    nb = SSD_SAMPLE_SEQS
    tok = lambda w: pl.BlockSpec((nb * rows, w), lambda i: (i, 0))
    state = pl.BlockSpec((nb, D_SSM, D_STATE), lambda i: (i, 0, 0))
    return pl.pallas_call(
        functools.partial(_ssd_sample_kernel, rows=rows),
        grid=(batch // nb,),
        in_specs=[tok(D_SSM), tok(CONV_DIM), tok(LANES),
                  pl.BlockSpec((CONV_WIDTH - 1, nb, CONV_DIM), lambda i: (0, i, 0)), state] + _ssd_param_specs(),
        out_specs=[tok(D_SSM), state],
        out_shape=[jax.ShapeDtypeStruct((batch * rows, D_SSM), F32),
                   jax.ShapeDtypeStruct((batch, D_SSM, D_STATE), F32)],
        scratch_shapes=[pltpu.VMEM((nb, 16, CONV_DIM), F32), pltpu.VMEM((nb, 8, D_SSM), F32),
                        pltpu.VMEM((nb, 8, LANES), F32)],
        compiler_params=pltpu.CompilerParams(
            dimension_semantics=("arbitrary",), vmem_limit_bytes=VMEM_LIMIT),
        name="ssd_sample",
    )(z, xbc, dt_raw, conv0, h0, *ssd_params)


def _t5_bucket_np(dist):
    dist = np.asarray(dist, np.int64)
    df = np.maximum(dist, 1).astype(np.float32)
    large = MAX_EXACT + (np.log(df / np.float32(MAX_EXACT)) / np.float32(math.log(MAX_WINDOW / MAX_EXACT))
                         * np.float32(N_BUCKETS - MAX_EXACT)).astype(np.int32)
    return np.where(dist < MAX_EXACT, dist, np.minimum(large, N_BUCKETS - 1)).astype(np.int32)


def _band_bucket_tables():
    ki = np.arange(SPAN)[None, :]
    out = np.empty((len(DILATIONS), 2, SPAN, SPAN), np.int32)
    for bi, d in enumerate(DILATIONS):
        nrun = DILATIONS[-1] // d
        row = np.arange(SPAN)
        qi = (nrun * (row % (SPAN // nrun)) + row // (SPAN // nrun))[:, None]
        rel_prev = qi + SPAN - ki
        rel_cur = qi - ki
        out[bi, 0] = np.where(rel_prev <= SPAN, _t5_bucket_np(np.clip(rel_prev, 0, SPAN) * d), -1)
        out[bi, 1] = np.where(rel_cur >= 0, _t5_bucket_np(np.clip(rel_cur, 0, SPAN) * d), -1)
    return out


def _bias_table_kernel(tab_ref, rb_ref, out_ref, *, buckets):
    pair = pl.program_id(0)
    for bi in range(len(DILATIONS)):
        for half in range(2):
            tab = tab_ref[bi, half]
            acc = [jnp.full((SPAN, SPAN), NEG, F32) for _ in range(2)]
            for bkt in buckets[bi][half]:
                hit = tab == bkt
                for s in range(2):
                    acc[s] = jnp.where(hit, rb_ref[bkt, 2 * pair + s] * LOG2E, acc[s])
            for s in range(2):
                rows = slice(s * SPAN, (s + 1) * SPAN)
                if half == 0:
                    out_ref[0, bi, 0, rows, 0:SPAN] = jnp.full((SPAN, SPAN), NEG, F32)
                    out_ref[0, bi, 1, rows, 0:SPAN] = acc[s]
                else:
                    out_ref[0, bi, 0, rows, SPAN:2 * SPAN] = acc[s]
                    out_ref[0, bi, 1, rows, SPAN:2 * SPAN] = acc[s]


def _bias_tables(rel_bias):
    tabs = _band_bucket_tables()
    buckets = [[sorted(int(b) for b in np.unique(tabs[bi, half]) if b >= 0) for half in range(2)]
               for bi in range(len(DILATIONS))]
    nd = len(DILATIONS)
    return pl.pallas_call(
        functools.partial(_bias_table_kernel, buckets=buckets),
        grid=(N_HEADS_A // 2,),
        in_specs=[_resident((nd, 2, SPAN, SPAN)),
                  pl.BlockSpec(memory_space=pltpu.SMEM)],
        out_specs=pl.BlockSpec((1, nd, 2, 2 * SPAN, 2 * SPAN), lambda p: (p, 0, 0, 0, 0)),
        out_shape=jax.ShapeDtypeStruct((N_HEADS_A // 2, nd, 2, 2 * SPAN, 2 * SPAN), F32),
        compiler_params=pltpu.CompilerParams(dimension_semantics=("arbitrary",)),
        name="bias_tables",
    )(jnp.asarray(tabs), rel_bias)


ATT_T = SPAN * DILATIONS[-1]
ATT_UNROLL = 16
ATT_SUB = 2


def _query_runs(d, r, n):
    run = SPAN * d // DILATIONS[-1]
    return [pl.ds(pl.multiple_of((r + d * c) * SPAN + run * n, 8), run) for c in range(DILATIONS[-1] // d)]


def _load_runs(ref, runs, *lead):
    parts = [ref[(*lead, rows, slice(None))] for rows in runs]
    return parts[0] if len(parts) == 1 else jnp.concatenate(parts, axis=0)


def _store_runs(ref, runs, value, *lead):
    run = value.shape[0] // len(runs)
    for c, rows in enumerate(runs):
        ref[(*lead, rows, slice(None))] = value[c * run:(c + 1) * run, :]


def _attn_kernel(q_ref, k_ref, v_ref, bias_ref, o_ref, qp_ref, tmp_ref, kph_ref, vph_ref, acc_ref, m_ref,
                 l_ref):
    t = pl.program_id(2)
    lane = lax.broadcasted_iota(jnp.int32, (1, LANES), 1)
    first_head = lane < HEAD_DIM
    head_mask = (first_head, jnp.logical_not(first_head))
    nblk = ATT_T // SPAN
    dmax = DILATIONS[-1]

    @pl.when(t == 0)
    def _():
        for bi, d in enumerate(DILATIONS):
            ph = ATT_T // d + SPAN
            for r in range(d):
                kph_ref[bi, r * ph:r * ph + SPAN, :] = jnp.zeros((SPAN, LANES), BF16)
                vph_ref[bi, r * ph:r * ph + SPAN, :] = jnp.zeros((SPAN, LANES), BF16)

    d4, d16 = DILATIONS[1], DILATIONS[2]
    assert DILATIONS == (1, d4, d4 * d4)
    l4, l16 = ATT_T // d4, ATT_T // d16
    for r4 in range(d4):
        tmp_ref[r4 * l4:(r4 + 1) * l4, :] = q_ref[pl.ds(r4, l4, stride=d4), :] * (ATTN_SCALE * LOG2E)
    for r16 in range(d16):
        qp_ref[r16 * l16:(r16 + 1) * l16, :] = tmp_ref[pl.ds((r16 % d4) * l4 + r16 // d4, l16, stride=d4), :]
    for src_ref, dst_ref in ((k_ref, kph_ref), (v_ref, vph_ref)):
        dst_ref[0, SPAN:SPAN + ATT_T, :] = src_ref[...].astype(BF16)
        ph4, ph16 = l4 + SPAN, l16 + SPAN
        for r4 in range(d4):
            x4 = src_ref[pl.ds(r4, l4, stride=d4), :]
            tmp_ref[r4 * l4:(r4 + 1) * l4, :] = x4
            dst_ref[1, r4 * ph4 + SPAN:(r4 + 1) * ph4, :] = x4.astype(BF16)
        for r16 in range(d16):
            x16 = tmp_ref[pl.ds((r16 % d4) * l4 + r16 // d4, l16, stride=d4), :]
            dst_ref[2, r16 * ph16 + SPAN:(r16 + 1) * ph16, :] = x16.astype(BF16)

    order = tuple(reversed(range(len(DILATIONS))))
    for bi in order:
        d = DILATIONS[bi]
        ln = ATT_T // d
        ph = ln + SPAN
        nb = ln // SPAN
        first_branch = bi == order[0]
        last_branch = bi == order[-1]

        def group(it, carry, bi=bi, d=d, ph=ph, nb=nb, first_branch=first_branch, last_branch=last_branch):
            def fetch(j):
                sub = []
                for g in range(j * ATT_SUB, (j + 1) * ATT_SUB):
                    blk = it * ATT_UNROLL + g
                    r = blk // nb
                    n = blk % nb
                    runs = _query_runs(d, r, n)
                    krow = pl.multiple_of(r * ph + n * SPAN, SPAN)
                    variant = jnp.logical_or(t > 0, n > 0).astype(jnp.int32)
                    qf = _load_runs(qp_ref, runs)
                    qs = jnp.concatenate([jnp.where(head_mask[s], qf, 0.0).astype(BF16) for s in range(2)],
                                         axis=0)
                    logits = _dot_nt(qs, kph_ref[bi, pl.ds(krow, 2 * SPAN), :]) + bias_ref[0, bi, variant]
                    old = None
                    if not first_branch:
                        old = ([_load_runs(m_ref, runs, s) for s in range(2)],
                               [_load_runs(l_ref, runs, s) for s in range(2)], _load_runs(acc_ref, runs))
                    sub.append((runs, krow, old, logits))
                return sub

            nsub = ATT_UNROLL // ATT_SUB
            nxt = fetch(0)
            for j in range(nsub):
                cur = nxt
                if j + 1 < nsub:
                    nxt = fetch(j + 1)
                for runs, m_new, l_new, pv in softmax_pv(cur):
                    if last_branch:
                        _store_runs(acc_ref, runs, pv / jnp.where(first_head, l_new[0], l_new[1]))
                    else:
                        for s in range(2):
                            _store_runs(m_ref, runs, m_new[s], s)
                            _store_runs(l_ref, runs, l_new[s], s)
                        _store_runs(acc_ref, runs, pv)
            return carry

        def softmax_pv(sub, bi=bi, first_branch=first_branch):
            soft = []
            for runs, krow, old, sc2 in sub:
                m_new, l_new, alpha, ps = [], [], [], []
                for s in range(2):
                    sc = sc2[s * SPAN:(s + 1) * SPAN, :]
                    halves = (sc[:, 0:SPAN], sc[:, SPAN:2 * SPAN])
                    mb = jnp.max(jnp.maximum(halves[0], halves[1]), axis=-1, keepdims=True)
                    if first_branch:
                        m = jnp.broadcast_to(mb, (SPAN, LANES))
                    else:
                        m = jnp.maximum(old[0][s], mb)
                        alpha.append(jnp.exp2(old[0][s] - m))
                    p = [jnp.exp2(h - m) for h in halves]
                    rs = jnp.sum(p[0] + p[1], axis=-1, keepdims=True)
                    ps.append(jnp.concatenate([p[0].astype(BF16), p[1].astype(BF16)], axis=1))
                    m_new.append(m)
                    l_new.append(jnp.broadcast_to(rs, (SPAN, LANES)) if first_branch
                                 else alpha[s] * old[1][s] + rs)
                soft.append((m_new, l_new, alpha, ps))
            out = []
            for (runs, krow, old, _), (m_new, l_new, alpha, ps) in zip(sub, soft):
                pv2 = _dot(jnp.concatenate(ps, axis=0), vph_ref[bi, pl.ds(krow, 2 * SPAN), :])
                pv = jnp.where(first_head, pv2[0:SPAN, :], pv2[SPAN:2 * SPAN, :])
                if not first_branch:
                    pv = jnp.where(first_head, alpha[0], alpha[1]) * old[2] + pv
                out.append((runs, m_new, l_new, pv))
            return out

        lax.fori_loop(0, nblk // ATT_UNROLL, group, 0)

    for r16 in range(d16):
        tmp_ref[pl.ds((r16 % d4) * l4 + r16 // d4, l16, stride=d4), :] = acc_ref[r16 * l16:(r16 + 1) * l16, :]
    for r4 in range(d4):
        o_ref[pl.ds(r4, l4, stride=d4), :] = tmp_ref[r4 * l4:(r4 + 1) * l4, :]

    for bi, d in enumerate(DILATIONS):
        ln = ATT_T // d
        ph = ln + SPAN
        for r in range(d):
            kph_ref[bi, r * ph:r * ph + SPAN, :] = kph_ref[bi, (r + 1) * ph - SPAN:(r + 1) * ph, :]
            vph_ref[bi, r * ph:r * ph + SPAN, :] = vph_ref[bi, (r + 1) * ph - SPAN:(r + 1) * ph, :]


def _attn_prompt(qkv, bias, batch):
    m = qkv.shape[0]
    nt = m // batch // ATT_T
    npair = N_HEADS_A // 2
    nd = len(DILATIONS)
    blk = lambda off: pl.BlockSpec((ATT_T, LANES), lambda b, p, t: (b * nt + t, off + p))
    halo_rows = ATT_T + DILATIONS[-1] * SPAN
    return pl.pallas_call(
        _attn_kernel,
        grid=(batch, npair, nt),
        in_specs=[blk(0), blk(npair), blk(2 * npair),
                  pl.BlockSpec((1, nd, 2, 2 * SPAN, 2 * SPAN), lambda b, p, t: (p, 0, 0, 0, 0))],
        out_specs=blk(0),
        out_shape=jax.ShapeDtypeStruct((m, D_ATTN), F32),
        scratch_shapes=[pltpu.VMEM((ATT_T, LANES), F32),
                        pltpu.VMEM((ATT_T, LANES), F32),
                        pltpu.VMEM((nd, halo_rows, LANES), BF16),
                        pltpu.VMEM((nd, halo_rows, LANES), BF16),
                        pltpu.VMEM((ATT_T, LANES), F32),
                        pltpu.VMEM((2, ATT_T, LANES), F32),
                        pltpu.VMEM((2, ATT_T, LANES), F32)],
        compiler_params=pltpu.CompilerParams(
            dimension_semantics=("arbitrary", "arbitrary", "arbitrary"), vmem_limit_bytes=VMEM_LIMIT),
        name="attn_prompt",
    )(qkv, qkv, qkv, bias)


SAMPLE_ROWS = 4


def _sample_tables():
    pos = np.arange(MAX_WINDOW)[None, :]
    t = (np.arange(8) % SAMPLE_ROWS)[:, None]
    dist = MAX_WINDOW + t - pos
    mult = np.zeros(dist.shape, np.float32)
    for d in DILATIONS:
        mult += ((dist % d == 0) & (dist <= SPAN * d)).astype(np.float32)
    bucket = np.where(mult > 0, _t5_bucket_np(dist), -1).astype(np.int32)
    return bucket, np.concatenate([mult, mult], axis=0)


def _sample_bias_kernel(tab_ref, rb_ref, rbv_ref, bmain_ref, bnew_ref, *, buckets):
    pair = pl.program_id(0)
    tab = tab_ref[...]
    acc = [jnp.full(tab.shape, NEG, F32) for _ in range(2)]
    for bkt in buckets:
        hit = tab == bkt
        for s in range(2):
            acc[s] = jnp.where(hit, rb_ref[bkt, 2 * pair + s], acc[s])
    for s in range(2):
        bmain_ref[0, s * 8:(s + 1) * 8, :] = acc[s]
    t = lax.broadcasted_iota(jnp.int32, (8, 1), 0) % SAMPLE_ROWS
    for tp in range(SAMPLE_ROWS):
        b = jnp.full((8, LANES), NEG, F32)
        for dist in range(SAMPLE_ROWS):
            b = jnp.where(t - tp == dist, rbv_ref[dist:dist + 1, :], b)
        bnew_ref[tp] = b


def _sample_bias_tables(rel_bias):
    bucket, mult = _sample_tables()
    buckets = sorted(int(b) for b in np.unique(bucket) if b >= 0)
    npair = N_HEADS_A // 2
    rb_lanes = jnp.pad(rel_bias, ((0, 0), (0, LANES - N_HEADS_A)))
    bmain, bnew = pl.pallas_call(
        functools.partial(_sample_bias_kernel, buckets=buckets),
        grid=(npair,),
        in_specs=[_resident((8, MAX_WINDOW)), pl.BlockSpec(memory_space=pltpu.SMEM),
                  _resident((N_BUCKETS, LANES))],
        out_specs=[pl.BlockSpec((1, 16, MAX_WINDOW), lambda p: (p, 0, 0)),
                   pl.BlockSpec((SAMPLE_ROWS, 8, LANES), lambda p: (0, 0, 0))],
        out_shape=[jax.ShapeDtypeStruct((npair, 16, MAX_WINDOW), F32),
                   jax.ShapeDtypeStruct((SAMPLE_ROWS, 8, LANES), F32)],
        compiler_params=pltpu.CompilerParams(dimension_semantics=("arbitrary",)),
        name="sample_bias_tables",
    )(jnp.asarray(bucket), rel_bias, rb_lanes)
    return bmain, bnew, jnp.asarray(mult)


def _attn_sample_body(qkv_ref, kt_ref, vt_ref, bmain_ref, mult_ref, bnew_ref, e_ref, om_ref):
    rows = SAMPLE_ROWS
    npair = N_HEADS_A // 2
    second = pl.program_id(0) % 2 == 1
    lower = lax.broadcasted_iota(jnp.int32, (2 * rows, 1), 0) < rows
    mine = jnp.logical_xor(lower, second)
    blk = qkv_ref[...]
    q8_all = jnp.where(mine, blk, pltpu.roll(blk, rows, 0))
    q8 = q8_all[:, 0:D_ATTN] * ATTN_SCALE
    kn = q8_all[:, D_ATTN:2 * D_ATTN]
    vn = q8_all[:, 2 * D_ATTN:3 * D_ATTN]
    lane = lax.broadcasted_iota(jnp.int32, (1, LANES), 1)
    first_head = lane < HEAD_DIM
    e = e_ref[...]

    s_new = []
    m_tot = jnp.full((8, LANES), NEG, F32)
    for tp in range(rows):
        hi, lo, lo2 = _split3(q8 * kn[tp:tp + 1, :])
        s = _dot_nt(hi, e) + _dot_nt(lo, e) + _dot_nt(lo2, e) + bnew_ref[tp]
        s_new.append(s)
        m_tot = jnp.maximum(m_tot, s)

    mult = mult_ref[...]
    l_tot = jnp.zeros((8, LANES), F32)

    def window_logits(p):
        qp = q8[:, p * LANES:(p + 1) * LANES]
        qbd = jnp.concatenate([jnp.where(first_head, qp, 0.0), jnp.where(first_head, 0.0, qp)],
                              axis=0).astype(BF16)
        return _dot(qbd, kt_ref[0, p].astype(BF16)) + bmain_ref[p]

    s_next = window_logits(0)
    for p in range(npair):
        s16 = s_next
        if p + 1 < npair:
            s_next = window_logits(p + 1)
        m_new =jnp.concatenate([m_tot[:, 2 * p:2 * p + 1], m_tot[:, 2 * p + 1:2 * p + 2]], axis=0)
        m = jnp.maximum(jnp.max(s16, axis=-1, keepdims=True), m_new)
        pr = mult * jnp.exp(s16 - m)
        ls = jnp.sum(pr, axis=-1, keepdims=True)
        o16 = _dot_nt(pr.astype(BF16), vt_ref[0, p].astype(BF16))
        om_ref[:, p * LANES:(p + 1) * LANES] = jnp.where(first_head, o16[0:8], o16[8:16])
        m_tot = jnp.where(lane == 2 * p, m[0:8], jnp.where(lane == 2 * p + 1, m[8:16], m_tot))
        l_tot = jnp.where(lane == 2 * p, ls[0:8], jnp.where(lane == 2 * p + 1, ls[8:16], l_tot))
        yield

    t = lax.broadcasted_iota(jnp.int32, (8, 1), 0) % rows
    o = om_ref[...]
    for tp in range(rows):
        c = jnp.where(t == tp, float(len(DILATIONS)), jnp.where(t > tp, 1.0, 0.0))
        pn = c * jnp.exp(s_new[tp] - m_tot)
        l_tot = l_tot + pn
        o = o + _expand_heads(pn, e_ref) * vn[tp:tp + 1, :]
    o = o / _expand_heads(l_tot, e_ref)
    return o, second, lower


def _interleave(*gens):
    results = [None] * len(gens)
    live = list(range(len(gens)))
    while live:
        for k in list(live):
            try:
                next(gens[k])
            except StopIteration as stop:
                results[k] = stop.value
                live.remove(k)
    return results


def _sample_attn_prompt_ssd_kernel(qkv_ref, kt_ref, vt_ref, bmain_ref, mult_ref, bnew_ref, e_ref,
                                   z_ref, xbc_ref, dt_ref, conv0_ref, h0_ref, convw_ref, convb_ref, dtb_ref,
                                   alog_ref, dskip_ref, gnorm_ref, o_ref, y_ref, hout_ref,
                                   om_ref, xpad_ref, state_ref, *, chunks):
    c = pl.program_id(0) % chunks

    @pl.when(c == 0)
    def _():
        state_ref[...] = h0_ref[0]
        xpad_ref[8 - (CONV_WIDTH - 1):8, :] = conv0_ref[0]

    attn = _attn_sample_body(qkv_ref, kt_ref, vt_ref, bmain_ref, mult_ref, bnew_ref, e_ref, om_ref)
    scan = _ssd_chunk(z_ref.at[0], xbc_ref.at[0], dt_ref.at[0], y_ref.at[0], xpad_ref, state_ref,
                      convw_ref, convb_ref, dtb_ref, alog_ref, dskip_ref, gnorm_ref, e_ref)
    o, second, lower = _interleave(attn, scan)[0]

    @pl.when(jnp.logical_not(second))
    def _():
        o_ref[...] = o

    @pl.when(second)
    def _():
        o_ref[...] = jnp.where(lower, o_ref[...], o)

    @pl.when(c == chunks - 1)
    def _():
        hout_ref[0] = state_ref[...]


def _sample_attn_prompt_ssd(qkv, kt, vt, bmain, mult, bnew, expand, z, xbc, dt_raw, conv0, h0, ssd_params):
    rows = SAMPLE_ROWS
    batch = qkv.shape[0] // rows
    nseq, seq = z.shape[0], z.shape[1]
    chunks = seq // SSD_CHUNK
    assert batch == nseq * chunks, "one prompt chunk per sample sequence"
    npair = N_HEADS_A // 2
    cache = pl.BlockSpec((1, npair, LANES, MAX_WINDOW), lambda i: (i, 0, 0, 0))
    chunk = lambda w: pl.BlockSpec((1, SSD_CHUNK, w), lambda i: (i // chunks, i % chunks, 0))
    per_seq = lambda r, w: pl.BlockSpec((1, r, w), lambda i: (i // chunks, 0, 0))
    return pl.pallas_call(
        functools.partial(_sample_attn_prompt_ssd_kernel, chunks=chunks),
        grid=(batch,),
        in_specs=[pl.BlockSpec((2 * rows, 3 * D_ATTN), lambda i: (i // 2, 0)), cache, cache,
                  _resident((npair, 16, MAX_WINDOW)), _resident((16, MAX_WINDOW)),
                  _resident((rows, 8, LANES)), _resident((LANES, D_ATTN)),
                  chunk(D_SSM), chunk(CONV_DIM), chunk(LANES), per_seq(CONV_WIDTH - 1, CONV_DIM),
                  per_seq(D_SSM, D_STATE)] + _ssd_param_specs()[:-1],
        out_specs=[pl.BlockSpec((2 * rows, D_ATTN), lambda i: (i // 2, 0)), chunk(D_SSM),
                   per_seq(D_SSM, D_STATE)],
        out_shape=[jax.ShapeDtypeStruct((batch * rows, D_ATTN), F32),
                   jax.ShapeDtypeStruct((nseq, seq, D_SSM), F32),
                   jax.ShapeDtypeStruct((nseq, D_SSM, D_STATE), F32)],
        scratch_shapes=[pltpu.VMEM((8, D_ATTN), F32), pltpu.VMEM((8 + SSD_CHUNK, CONV_DIM), F32),
                        pltpu.VMEM((D_SSM, D_STATE), F32)],
        compiler_params=pltpu.CompilerParams(
            dimension_semantics=("arbitrary",), vmem_limit_bytes=VMEM_LIMIT),
        name="sample_attn_prompt_ssd",
    )(qkv, kt, vt, bmain, mult, bnew, expand, z, xbc, dt_raw, conv0, h0, *ssd_params[:-1])


def _head_expand_matrix():
    e = np.zeros((LANES, D_SSM), np.float32)
    for h in range(N_HEADS_S):
        e[h, h * SSM_HEAD_DIM:(h + 1) * SSM_HEAD_DIM] = 1.0
    return jnp.asarray(e, BF16)


def _prep_params(norm_mix, w_in, q_norm, k_norm, conv_w, conv_b, dt_bias, a_log, d_skip, ssm_norm,
                 w_out, norm_ffn, w_up, w_down):
    n_main = 3 * D_ATTN + D_SSM + CONV_DIM
    pad_heads = lambda v: jnp.pad(v.reshape(1, N_HEADS_S), ((0, 0), (0, LANES - N_HEADS_S)))
    return dict(
        gmix=norm_mix.reshape(1, D_MODEL),
        w_main=w_in.astype(BF16),
        w_dt=jnp.pad(w_in[:, n_main:], ((0, 0), (0, LANES - N_HEADS_S))).astype(BF16),
        gq=jnp.tile(q_norm, N_HEADS_A).reshape(1, D_ATTN),
        gk=jnp.tile(k_norm, N_HEADS_A).reshape(1, D_ATTN),
        conv_w=conv_w,
        conv_b=conv_b.reshape(1, CONV_DIM),
        dt_bias=pad_heads(dt_bias),
        a_log=pad_heads(a_log),
        d_skip_e=jnp.repeat(d_skip, SSM_HEAD_DIM).reshape(1, D_SSM),
        gnorm=ssm_norm.reshape(1, D_SSM),
        expand=_head_expand_matrix(),
        w_out=w_out.astype(BF16),
        gffn=norm_ffn.reshape(1, D_MODEL),
        w_up=w_up.astype(BF16),
        w_down=w_down.astype(BF16),
    )


def _ssd_param_list(p):
    return (p["conv_w"], p["conv_b"], p["dt_bias"], p["a_log"], p["d_skip_e"], p["gnorm"], p["expand"])


def kernel(x_prompt, x_sample, cache_attn_k, cache_attn_v, state_conv, state_ssm, norm_mix, w_in, q_norm,
           k_norm, rel_bias, conv_w, conv_b, dt_bias, a_log, d_skip, ssm_norm, w_out, norm_ffn, w_up, w_down):
    depth = w_in.shape[0]
    assert depth == 1, "single-layer decoder"
    l = 0
    p = _prep_params(norm_mix[l], w_in[l], q_norm[l], k_norm[l], conv_w[l], conv_b[l], dt_bias[l], a_log[l],
                     d_skip[l], ssm_norm[l], w_out[l], norm_ffn[l], w_up[l], w_down[l])
    ssd_params = _ssd_param_list(p)
    tm = 512

    bp, sp = x_prompt.shape[:2]
    keep = min(MAX_WINDOW, sp)
    xp = x_prompt.reshape(bp * sp, D_MODEL)
    qkv, z, xbc, dt_raw, ktail, vtail = _inproj(xp, p["gmix"], p["w_main"], p["w_dt"], p["gq"], p["gk"], tm,
                                                tail=(sp, keep))
    attn = _attn_prompt(qkv, _bias_tables(rel_bias), bp)
    new_k_prompt = ktail.reshape(depth, bp, keep, N_HEADS_A, HEAD_DIM)
    new_v_prompt = vtail.reshape(depth, bp, keep, N_HEADS_A, HEAD_DIM)
    new_conv_prompt = xbc.reshape(bp, sp, CONV_DIM)[:, sp - (CONV_WIDTH - 1):].reshape(
        depth, bp, CONV_WIDTH - 1, CONV_DIM)

    bs, ts = x_sample.shape[:2]
    assert cache_attn_k.shape[2] == MAX_WINDOW and ts == SAMPLE_ROWS
    xs = x_sample.reshape(bs * ts, D_MODEL)
    qkv_s, z_s, xbc_s, dt_s = _inproj(xs, p["gmix"], p["w_main"], p["w_dt"], p["gq"], p["gk"], tm)
    npair = N_HEADS_A // 2
    kt = jnp.transpose(cache_attn_k[l], (0, 2, 3, 1)).reshape(bs, npair, LANES, MAX_WINDOW)
    vt = jnp.transpose(cache_attn_v[l], (0, 2, 3, 1)).reshape(bs, npair, LANES, MAX_WINDOW)
    bmain, bnew, mult = _sample_bias_tables(rel_bias)
    conv0 = jnp.zeros((bp, CONV_WIDTH - 1, CONV_DIM), F32)
    h0 = jnp.zeros((bp, D_SSM, D_STATE), F32)
    attn_s, ssm_o, hst = _sample_attn_prompt_ssd(
        qkv_s, kt, vt, bmain, mult, bnew, p["expand"], z.reshape(bp, sp, D_SSM), xbc.reshape(bp, sp, CONV_DIM),
        dt_raw.reshape(bp, sp, LANES), conv0, h0, ssd_params)
    y_prompt = _outffn(xp, attn, ssm_o.reshape(bp * sp, D_SSM), p["w_out"], p["gffn"], p["w_up"], p["w_down"], tm)
    new_ssm_prompt = hst.reshape(depth, bp, N_HEADS_S, SSM_HEAD_DIM, D_STATE)
    ssm_os, hsts = _ssd_sample(z_s, xbc_s, dt_s, jnp.transpose(state_conv[l], (1, 0, 2)),
                               state_ssm[l].reshape(bs, D_SSM, D_STATE), ssd_params, ts)
    y_sample = _outffn(xs, attn_s, ssm_os, p["w_out"], p["gffn"], p["w_up"], p["w_down"], tm)
    new_k_sample = qkv_s[:, D_ATTN:2 * D_ATTN].reshape(depth, bs, ts, N_HEADS_A, HEAD_DIM)
    new_v_sample = qkv_s[:, 2 * D_ATTN:].reshape(depth, bs, ts, N_HEADS_A, HEAD_DIM)
    new_conv_sample = xbc_s.reshape(bs, ts, CONV_DIM)[:, ts - (CONV_WIDTH - 1):].reshape(
        depth, bs, CONV_WIDTH - 1, CONV_DIM)
    new_ssm_sample = hsts.reshape(depth, bs, N_HEADS_S, SSM_HEAD_DIM, D_STATE)

    return (y_prompt.reshape(bp, sp, D_MODEL), y_sample.reshape(bs, ts, D_MODEL),
            new_k_prompt, new_v_prompt, new_conv_prompt, new_ssm_prompt,
            new_k_sample, new_v_sample, new_conv_sample, new_ssm_sample)
```

```python
import functools
import math

import jax
import jax.numpy as jnp
import numpy as np
from jax import lax
from jax.experimental import pallas as pl
from jax.experimental.pallas import tpu as pltpu

F32 = jnp.float32
BF16 = jnp.bfloat16

D_MODEL = 1024
HEAD_DIM = 64
N_HEADS_A = 16
D_ATTN = 1024
SPAN = 128
DILATIONS = (1, 4, 16)
MAX_WINDOW = 2048
N_BUCKETS = 32
MAX_EXACT = 16
SSM_HEAD_DIM = 64
N_HEADS_S = 16
D_SSM = 1024
SSM_GROUPS = 2
D_STATE = 128
CONV_WIDTH = 4
CONV_DIM = D_SSM + 2 * SSM_GROUPS * D_STATE
SSD_CHUNK = 128
D_FF = 4096
EPS = 1e-6
ATTN_SCALE = HEAD_DIM ** -0.5
LOG2E = 1.4426950408889634

LANES = 128
NEG = -1e30
VMEM_LIMIT = 56 * 1024 * 1024


def _resident(shape):
    nd = len(shape)
    return pl.BlockSpec(shape, lambda *_: (0,) * nd, pipeline_mode=pl.Buffered(1))


def _split3(x):
    hi = x.astype(BF16)
    r1 = x - hi.astype(F32)
    lo = r1.astype(BF16)
    lo2 = (r1 - lo.astype(F32)).astype(BF16)
    return hi, lo, lo2


def _dot(a, b):
    return jnp.dot(a, b, preferred_element_type=F32)


def _dot_nt(a, b):
    return lax.dot_general(a, b, (((1,), (1,)), ((), ())), preferred_element_type=F32)


def _sigmoid(x):
    return 0.5 * jnp.tanh(0.5 * x) + 0.5


def _softplus(x):
    return jnp.maximum(x, 0.0) + jnp.log1p(jnp.exp(-jnp.abs(x)))


IN_TN = 512


def _inproj_kernel(x_ref, xs_ref, gmix_ref, w_ref, wdt_ref, gq_ref, gk_ref,
                   qkv_ref, z_ref, xbc_ref, dt_ref, *tail_refs, with_tail):
    x = jnp.where(pl.program_id(0) == 0, xs_ref[...], x_ref[...])
    ms = jnp.mean(x * x, axis=-1, keepdims=True)
    xn = ((x * lax.rsqrt(ms + EPS)) * gmix_ref[...]).astype(BF16)
    first_head = lax.broadcasted_iota(jnp.int32, (1, LANES), 1) < HEAD_DIM

    n_main = 3 * D_ATTN + D_SSM + CONV_DIM
    for j in range(n_main // IN_TN):
        c0 = j * IN_TN
        t = _dot(xn, w_ref[:, c0:c0 + IN_TN])
        if c0 < 2 * D_ATTN:
            g_ref = gq_ref if c0 < D_ATTN else gk_ref
            g0 = c0 % D_ATTN
            for c in range(IN_TN // LANES):
                tc = t[:, c * LANES:(c + 1) * LANES]
                s = tc * tc
                sa = jnp.sum(jnp.where(first_head, s, 0.0), axis=-1, keepdims=True)
                sb = jnp.sum(jnp.where(first_head, 0.0, s), axis=-1, keepdims=True)
                inv = lax.rsqrt(jnp.where(first_head, sa, sb) * (1.0 / HEAD_DIM) + EPS)
                out = (tc * inv) * g_ref[:, g0 + c * LANES:g0 + (c + 1) * LANES]
                qkv_ref[:, c0 + c * LANES:c0 + (c + 1) * LANES] = out
                if with_tail and c0 >= D_ATTN:
                    tail_refs[0][0, :, g0 + c * LANES:g0 + (c + 1) * LANES] = out
        elif c0 < 3 * D_ATTN:
            qkv_ref[:, c0:c0 + IN_TN] = t
            if with_tail:
                tail_refs[1][0, :, c0 - 2 * D_ATTN:c0 - 2 * D_ATTN + IN_TN] = t
        elif c0 < 3 * D_ATTN + D_SSM:
            z_ref[:, c0 - 3 * D_ATTN:c0 - 3 * D_ATTN + IN_TN] = t
        else:
            o0 = c0 - 3 * D_ATTN - D_SSM
            xbc_ref[:, o0:o0 + IN_TN] = t
    dt_ref[...] = _dot(xn, wdt_ref[...])


def _inproj(x2d, xs, gmix, w_main, w_dt, gq, gk, tm, tail):
    m = x2d.shape[0]
    assert xs.shape[0] == tm
    nblk = m // tm
    tile = lambda i: jnp.maximum(i - 1, 0)
    row = lambda w: pl.BlockSpec((tm, w), lambda i: (jnp.where(i == 0, nblk, i - 1), 0))
    out_specs = [row(3 * D_ATTN), row(D_SSM), row(CONV_DIM), row(LANES)]
    out_shape = [jax.ShapeDtypeStruct((m + tm, w), F32) for w in (3 * D_ATTN, D_SSM, CONV_DIM, LANES)]
    per_seq, kept = tail[0] // tm, tail[1] // tm
    kept_block = lambda i: (tile(i) // per_seq, jnp.maximum(tile(i) % per_seq - (per_seq - kept), 0), 0)
    out_specs += [pl.BlockSpec((1, tm, D_ATTN), kept_block)] * 2
    out_shape += [jax.ShapeDtypeStruct((m // tail[0], tail[1], D_ATTN), F32)] * 2
    return pl.pallas_call(
        functools.partial(_inproj_kernel, with_tail=True),
        grid=(nblk + 1,),
        in_specs=[pl.BlockSpec((tm, D_MODEL), lambda i: (tile(i), 0)), _resident((tm, D_MODEL)),
                  _resident((1, D_MODEL)), _resident(w_main.shape),
                  _resident((D_MODEL, LANES)), _resident((1, D_ATTN)), _resident((1, D_ATTN))],
        out_specs=out_specs,
        out_shape=out_shape,
        compiler_params=pltpu.CompilerParams(
            dimension_semantics=("arbitrary",), vmem_limit_bytes=VMEM_LIMIT),
        name="inproj",
    )(x2d, xs, gmix, w_main, w_dt, gq, gk)


FFN_TF = 1024


def _outffn_kernel(x_ref, attn_ref, ssm_ref, wout_ref, gffn_ref, wup_ref, wdown_ref, y_ref):
    a = attn_ref[...].astype(BF16)
    s = ssm_ref[...].astype(BF16)
    h = x_ref[...] + _dot(a, wout_ref[0:D_ATTN, :]) + _dot(s, wout_ref[D_ATTN:D_ATTN + D_SSM, :])
    ms = jnp.mean(h * h, axis=-1, keepdims=True)
    hn = ((h * lax.rsqrt(ms + EPS)) * gffn_ref[...]).astype(BF16)
    acc = None
    for f in range(D_FF // FFN_TF):
        u = _dot(hn, wup_ref[:, f * FFN_TF:(f + 1) * FFN_TF])
        u = jnp.maximum(u, 0.0)
        u = (u * u).astype(BF16)
        d = _dot(u, wdown_ref[f * FFN_TF:(f + 1) * FFN_TF, :])
        acc = d if acc is None else acc + d
    y_ref[...] = h + acc


def _outffn(x2d, attn, ssm_o, w_out, gffn, w_up, w_down, tm):
    m = x2d.shape[0]
    row = pl.BlockSpec((tm, D_MODEL), lambda i: (i, 0))
    return pl.pallas_call(
        _outffn_kernel,
        grid=(m // tm,),
        in_specs=[row, row, row, _resident((D_ATTN + D_SSM, D_MODEL)), _resident((1, D_MODEL)),
                  _resident((D_MODEL, D_FF)), _resident((D_FF, D_MODEL))],
        out_specs=row,
        out_shape=jax.ShapeDtypeStruct((m, D_MODEL), F32),
        compiler_params=pltpu.CompilerParams(
            dimension_semantics=("arbitrary",), vmem_limit_bytes=VMEM_LIMIT),
        name="outffn",
    )(x2d, attn, ssm_o, w_out, gffn, w_up, w_down)


def _expand_heads(v, e_ref):
    hi, lo, lo2 = _split3(v)
    e = e_ref[...]
    return _dot(hi, e) + _dot(lo, e) + _dot(lo2, e)


def _ssd_chunk(z_ref, xbc_ref, dt_ref, y_ref, xpad_ref, state_ref,
               convw_ref, convb_ref, dtb_ref, alog_ref, dskip_ref, gnorm_ref, e_ref):
    q = SSD_CHUNK
    pad = 8
    tail = CONV_WIDTH - 1

    xbc = xbc_ref[...]
    xpad_ref[pad:pad + q, :] = xbc
    conv = convb_ref[...] + convw_ref[tail:tail + 1, :] * xbc
    for i in range(tail):
        conv = conv + convw_ref[i:i + 1, :] * xpad_ref[pad - tail + i:pad - tail + i + q, :]
    xpad_ref[pad - tail:pad, :] = xbc[q - tail:q, :]
    xc = conv * _sigmoid(conv)
    xs = xc[:, 0:D_SSM]

    lane = lax.broadcasted_iota(jnp.int32, (1, LANES), 1)
    dt = jnp.where(lane < N_HEADS_S, _softplus(dt_ref[...] + dtb_ref[...]), 0.0)
    da = dt * (-jnp.exp(alog_ref[...]))

    ii = lax.broadcasted_iota(jnp.int32, (q, q), 0)
    jj = lax.broadcasted_iota(jnp.int32, (q, q), 1)
    causal = ii >= jj
    tri = jnp.where(causal, 1.0, 0.0).astype(BF16)
    hi, lo, lo2 = _split3(da)
    cum = _dot(tri, hi) + _dot(tri, lo) + _dot(tri, lo2)
    cum_t = cum.T

    cum_e = _expand_heads(cum, e_ref)
    last_e = cum_e[q - 1:q, :]
    xdt = xs * _expand_heads(dt, e_ref)
    xw_t = (xdt * jnp.exp(last_e - cum_e)).T.astype(BF16)
    state_decay = jnp.exp(jnp.broadcast_to(last_e, (q, D_SSM)).T)
    off_scale = jnp.exp(cum_e)

    first_head = lane < SSM_HEAD_DIM
    heads_per_group = N_HEADS_S // SSM_GROUPS
    gw = heads_per_group * SSM_HEAD_DIM
    state = state_ref[...]
    state_bf = state.astype(BF16)
    yield
    for g in range(SSM_GROUPS):
        bg = xc[:, D_SSM + g * D_STATE:D_SSM + (g + 1) * D_STATE].astype(BF16)
        cg = xc[:, D_SSM + (SSM_GROUPS + g) * D_STATE:D_SSM + (SSM_GROUPS + g + 1) * D_STATE].astype(BF16)
        cb = _dot_nt(cg, bg)
        y_off = _dot_nt(cg, state_bf[g * gw:(g + 1) * gw, :]) * off_scale[:, g * gw:(g + 1) * gw]
        y_ref[:, g * gw:(g + 1) * gw] = y_off
        for hp in range(heads_per_group // 2):
            pair = g * (heads_per_group // 2) + hp
            xp = xdt[:, pair * LANES:(pair + 1) * LANES]
            yp = None
            for s in range(2):
                h = 2 * pair + s
                seg = cum[:, h:h + 1] - cum_t[h:h + 1, :]
                dec = jnp.exp(jnp.where(causal, seg, NEG))
                m = (cb * dec).astype(BF16)
                xm = jnp.where(first_head if s == 0 else jnp.logical_not(first_head), xp, 0.0).astype(BF16)
                d = _dot(m, xm)
                yp = d if yp is None else yp + d
            y_ref[:, pair * LANES:(pair + 1) * LANES] += yp
            yield
        state_ref[g * gw:(g + 1) * gw, :] = (
            state_decay[g * gw:(g + 1) * gw, :] * state[g * gw:(g + 1) * gw, :]
            + _dot(xw_t[g * gw:(g + 1) * gw, :], bg))

    z = z_ref[...]
    y = (y_ref[...] + xs * dskip_ref[...]) * (z * _sigmoid(z))
    for g in range(SSM_GROUPS):
        yg = y[:, g * gw:(g + 1) * gw]
        ms = jnp.mean(yg * yg, axis=-1, keepdims=True)
        y_ref[:, g * gw:(g + 1) * gw] = (yg * lax.rsqrt(ms + EPS)) * gnorm_ref[:, g * gw:(g + 1) * gw]


def _ssd_sample_kernel(z_ref, xbc_ref, dt_ref, conv0_ref, h0_ref, convw_ref, convb_ref, dtb_ref,
                       alog_ref, dskip_ref, gnorm_ref, e_ref, y_ref, hout_ref, xs_ref, z8_ref, dt8_ref,
                       *, rows):
    for b in range(h0_ref.shape[0]):
        tok = pl.ds(b * rows, rows)
        _ssd_sample_one(z_ref.at[tok], xbc_ref.at[tok], dt_ref.at[tok], conv0_ref.at[:, b], h0_ref.at[b],
                        convw_ref, convb_ref, dtb_ref, alog_ref, dskip_ref, gnorm_ref, e_ref, y_ref.at[tok],
                        hout_ref.at[b], xs_ref.at[b], z8_ref.at[b], dt8_ref.at[b], rows)


def _ssd_sample_one(z_ref, xbc_ref, dt_ref, conv0_ref, h0_ref, convw_ref, convb_ref, dtb_ref,
                    alog_ref, dskip_ref, gnorm_ref, e_ref, y_ref, hout_ref, xs_ref, z8_ref, dt8_ref, rows):
    tail = CONV_WIDTH - 1
    r8 = 8
    row = lax.broadcasted_iota(jnp.int32, (r8, 1), 0)
    lane = lax.broadcasted_iota(jnp.int32, (1, LANES), 1)

    xs_ref[r8:2 * r8, :] = jnp.zeros((r8, CONV_DIM), F32)
    xs_ref[r8:r8 + rows, :] = xbc_ref[...]
    xs_ref[r8 - tail:r8, :] = conv0_ref[...]
    z8_ref[...] = jnp.zeros((r8, D_SSM), F32)
    z8_ref[0:rows, :] = z_ref[...]
    dt8_ref[...] = jnp.zeros((r8, LANES), F32)
    dt8_ref[0:rows, :] = dt_ref[...]

    conv = convb_ref[...] + convw_ref[tail:tail + 1, :] * xs_ref[r8:2 * r8, :]
    for i in range(tail):
        conv = conv + convw_ref[i:i + 1, :] * xs_ref[r8 - tail + i:2 * r8 - tail + i, :]
    xc = conv * _sigmoid(conv)
    xs = xc[:, 0:D_SSM]

    live = jnp.logical_and(lane < N_HEADS_S, row < rows)
    dt = jnp.where(live, _softplus(dt8_ref[...] + dtb_ref[...]), 0.0)
    da = dt * (-jnp.exp(alog_ref[...]))
    cum = da
    for k in range(1, rows):
        cum = cum + pltpu.roll(da, k, 0)
    last = rows - 1

    heads_per_group = N_HEADS_S // SSM_GROUPS
    gw = heads_per_group * SSM_HEAD_DIM
    bg = [xc[:, D_SSM + g * D_STATE:D_SSM + (g + 1) * D_STATE] for g in range(SSM_GROUPS)]
    cg = [xc[:, D_SSM + (SSM_GROUPS + g) * D_STATE:D_SSM + (SSM_GROUPS + g + 1) * D_STATE]
          for g in range(SSM_GROUPS)]
    group0 = lane < heads_per_group
    w = []
    for j in range(rows):
        cb = [jnp.sum(cg[g] * bg[g][j:j + 1, :], axis=-1, keepdims=True) for g in range(SSM_GROUPS)]
        dec = jnp.exp(jnp.where(row >= j, cum - cum[j:j + 1, :], NEG))
        w.append(jnp.where(group0, cb[0], cb[1]) * dec)

    stack = jnp.concatenate([cum, dt] + w, axis=0)
    hi, lo, lo2 = _split3(stack)
    ex = _dot(jnp.concatenate([hi, lo, lo2], axis=0), e_ref[...])
    ns = stack.shape[0]
    ex = ex[0:ns] + ex[ns:2 * ns] + ex[2 * ns:3 * ns]
    cum_e = ex[0:r8]
    xdt = xs * ex[r8:2 * r8]
    y = None
    for j in range(rows):
        term = ex[(2 + j) * r8:(3 + j) * r8] * xdt[j:j + 1, :]
        y = term if y is None else y + term
    last_e = cum_e[last:last + 1, :]

    state = h0_ref[...]
    state_bf = state.astype(BF16)
    zeros8 = jnp.zeros((r8, D_STATE), F32)
    y_off = [_dot_nt(jnp.concatenate([cg[g], zeros8], axis=0).astype(BF16),
                     state_bf[g * gw:(g + 1) * gw, :])[0:r8, :] for g in range(SSM_GROUPS)]
    y = y + jnp.concatenate(y_off, axis=1) * jnp.exp(cum_e)

    z = z8_ref[...]
    y = (y + xs * dskip_ref[...]) * (z * _sigmoid(z))
    outs = []
    for g in range(SSM_GROUPS):
        yg = y[:, g * gw:(g + 1) * gw]
        ms = jnp.mean(yg * yg, axis=-1, keepdims=True)
        outs.append((yg * lax.rsqrt(ms + EPS)) * gnorm_ref[:, g * gw:(g + 1) * gw])
    y_ref[...] = jnp.concatenate(outs, axis=1)[0:rows, :]

    pad = jnp.zeros((SSD_CHUNK - r8, D_SSM), F32)
    xw_t = jnp.concatenate([xdt * jnp.exp(last_e - cum_e), pad], axis=0).T.astype(BF16)
    state_decay = jnp.exp(jnp.broadcast_to(last_e, (SSD_CHUNK, D_SSM)).T)
    bpad = jnp.zeros((SSD_CHUNK - r8, D_STATE), F32)
    for g in range(SSM_GROUPS):
        b128 = jnp.concatenate([bg[g], bpad], axis=0).astype(BF16)
        hout_ref[g * gw:(g + 1) * gw, :] = (
            state_decay[g * gw:(g + 1) * gw, :] * state[g * gw:(g + 1) * gw, :]
            + _dot(xw_t[g * gw:(g + 1) * gw, :], b128))


def _ssd_param_specs():
    return [_resident((CONV_WIDTH, CONV_DIM)), _resident((1, CONV_DIM)), _resident((1, LANES)),
            _resident((1, LANES)), _resident((1, D_SSM)), _resident((1, D_SSM)),
            _resident((LANES, D_SSM))]


SSD_SAMPLE_SEQS = 8


def _ssd_sample(z, xbc, dt_raw, conv0, h0, ssd_params, rows, row0):
    batch = h0.shape[0]
    nb = SSD_SAMPLE_SEQS
    blk0 = row0 // (nb * rows)
    assert blk0 * nb * rows == row0
    tok = lambda w: pl.BlockSpec((nb * rows, w), lambda i: (i, 0))
    tok_in = lambda w: pl.BlockSpec((nb * rows, w), lambda i: (blk0 + i, 0))
    state = pl.BlockSpec((nb, D_SSM, D_STATE), lambda i: (i, 0, 0))
    return pl.pallas_call(
        functools.partial(_ssd_sample_kernel, rows=rows),
        grid=(batch // nb,),
        in_specs=[tok_in(D_SSM), tok_in(CONV_DIM), tok_in(LANES),
                  pl.BlockSpec((CONV_WIDTH - 1, nb, CONV_DIM), lambda i: (0, i, 0)), state] + _ssd_param_specs(),
        out_specs=[tok(D_SSM), state],
        out_shape=[jax.ShapeDtypeStruct((batch * rows, D_SSM), F32),
                   jax.ShapeDtypeStruct((batch, D_SSM, D_STATE), F32)],
        scratch_shapes=[pltpu.VMEM((nb, 16, CONV_DIM), F32), pltpu.VMEM((nb, 8, D_SSM), F32),
                        pltpu.VMEM((nb, 8, LANES), F32)],
        compiler_params=pltpu.CompilerParams(
            dimension_semantics=("arbitrary",), vmem_limit_bytes=VMEM_LIMIT),
        name="ssd_sample",
    )(z, xbc, dt_raw, conv0, h0, *ssd_params)


def _t5_bucket_np(dist):
    dist = np.asarray(dist, np.int64)
    df = np.maximum(dist, 1).astype(np.float32)
    large = MAX_EXACT + (np.log(df / np.float32(MAX_EXACT)) / np.float32(math.log(MAX_WINDOW / MAX_EXACT))
                         * np.float32(N_BUCKETS - MAX_EXACT)).astype(np.int32)
    return np.where(dist < MAX_EXACT, dist, np.minimum(large, N_BUCKETS - 1)).astype(np.int32)


def _band_bucket_tables():
    ki = np.arange(SPAN)[None, :]
    out = np.empty((len(DILATIONS), 2, SPAN, SPAN), np.int32)
    for bi, d in enumerate(DILATIONS):
        nrun = DILATIONS[-1] // d
        row = np.arange(SPAN)
        qi = (nrun * (row % (SPAN // nrun)) + row // (SPAN // nrun))[:, None]
        rel_prev = qi + SPAN - ki
        rel_cur = qi - ki
        out[bi, 0] = np.where(rel_prev <= SPAN, _t5_bucket_np(np.clip(rel_prev, 0, SPAN) * d), -1)
        out[bi, 1] = np.where(rel_cur >= 0, _t5_bucket_np(np.clip(rel_cur, 0, SPAN) * d), -1)
    return out


def _bias_table_kernel(tab_ref, rb_ref, out_ref, *, buckets):
    pair = pl.program_id(0)
    for bi in range(len(DILATIONS)):
        for half in range(2):
            tab = tab_ref[bi, half]
            acc = [jnp.full((SPAN, SPAN), NEG, F32) for _ in range(2)]
            for bkt in buckets[bi][half]:
                hit = tab == bkt
                for s in range(2):
                    acc[s] = jnp.where(hit, rb_ref[bkt, 2 * pair + s] * LOG2E, acc[s])
            for s in range(2):
                rows = slice(s * SPAN, (s + 1) * SPAN)
                if half == 0:
                    out_ref[0, bi, 0, rows, 0:SPAN] = jnp.full((SPAN, SPAN), NEG, F32)
                    out_ref[0, bi, 1, rows, 0:SPAN] = acc[s]
                else:
                    out_ref[0, bi, 0, rows, SPAN:2 * SPAN] = acc[s]
                    out_ref[0, bi, 1, rows, SPAN:2 * SPAN] = acc[s]


def _bias_tables(rel_bias):
    tabs = _band_bucket_tables()
    buckets = [[sorted(int(b) for b in np.unique(tabs[bi, half]) if b >= 0) for half in range(2)]
               for bi in range(len(DILATIONS))]
    nd = len(DILATIONS)
    return pl.pallas_call(
        functools.partial(_bias_table_kernel, buckets=buckets),
        grid=(N_HEADS_A // 2,),
        in_specs=[_resident((nd, 2, SPAN, SPAN)),
                  pl.BlockSpec(memory_space=pltpu.SMEM)],
        out_specs=pl.BlockSpec((1, nd, 2, 2 * SPAN, 2 * SPAN), lambda p: (p, 0, 0, 0, 0)),
        out_shape=jax.ShapeDtypeStruct((N_HEADS_A // 2, nd, 2, 2 * SPAN, 2 * SPAN), F32),
        compiler_params=pltpu.CompilerParams(dimension_semantics=("arbitrary",)),
        name="bias_tables",
    )(jnp.asarray(tabs), rel_bias)


ATT_T = SPAN * DILATIONS[-1]
ATT_UNROLL = 16
ATT_SUB = 2


def _query_runs(d, r, n):
    run = SPAN * d // DILATIONS[-1]
    return [pl.ds(pl.multiple_of((r + d * c) * SPAN + run * n, 8), run) for c in range(DILATIONS[-1] // d)]


def _load_runs(ref, runs, *lead):
    parts = [ref[(*lead, rows, slice(None))] for rows in runs]
    return parts[0] if len(parts) == 1 else jnp.concatenate(parts, axis=0)


def _store_runs(ref, runs, value, *lead):
    run = value.shape[0] // len(runs)
    for c, rows in enumerate(runs):
        ref[(*lead, rows, slice(None))] = value[c * run:(c + 1) * run, :]


def _attn_kernel(q_ref, k_ref, v_ref, bias_ref, o_ref, qp_ref, tmp_ref, kph_ref, vph_ref, acc_ref, m_ref,
                 l_ref):
    t = pl.program_id(2)
    lane = lax.broadcasted_iota(jnp.int32, (1, LANES), 1)
    first_head = lane < HEAD_DIM
    head_mask = (first_head, jnp.logical_not(first_head))
    nblk = ATT_T // SPAN
    dmax = DILATIONS[-1]

    @pl.when(t == 0)
    def _():
        for bi, d in enumerate(DILATIONS):
            ph = ATT_T // d + SPAN
            for r in range(d):
                kph_ref[bi, r * ph:r * ph + SPAN, :] = jnp.zeros((SPAN, LANES), BF16)
                vph_ref[bi, r * ph:r * ph + SPAN, :] = jnp.zeros((SPAN, LANES), BF16)

    d4, d16 = DILATIONS[1], DILATIONS[2]
    assert DILATIONS == (1, d4, d4 * d4)
    l4, l16 = ATT_T // d4, ATT_T // d16
    for r4 in range(d4):
        tmp_ref[r4 * l4:(r4 + 1) * l4, :] = q_ref[pl.ds(r4, l4, stride=d4), :] * (ATTN_SCALE * LOG2E)
    for r16 in range(d16):
        qp_ref[r16 * l16:(r16 + 1) * l16, :] = tmp_ref[pl.ds((r16 % d4) * l4 + r16 // d4, l16, stride=d4), :]
    for src_ref, dst_ref in ((k_ref, kph_ref), (v_ref, vph_ref)):
        dst_ref[0, SPAN:SPAN + ATT_T, :] = src_ref[...].astype(BF16)
        ph4, ph16 = l4 + SPAN, l16 + SPAN
        for r4 in range(d4):
            x4 = src_ref[pl.ds(r4, l4, stride=d4), :]
            tmp_ref[r4 * l4:(r4 + 1) * l4, :] = x4
            dst_ref[1, r4 * ph4 + SPAN:(r4 + 1) * ph4, :] = x4.astype(BF16)
        for r16 in range(d16):
            x16 = tmp_ref[pl.ds((r16 % d4) * l4 + r16 // d4, l16, stride=d4), :]
            dst_ref[2, r16 * ph16 + SPAN:(r16 + 1) * ph16, :] = x16.astype(BF16)

    order = tuple(reversed(range(len(DILATIONS))))
    for bi in order:
        d = DILATIONS[bi]
        ln = ATT_T // d
        ph = ln + SPAN
        nb = ln // SPAN
        first_branch = bi == order[0]
        last_branch = bi == order[-1]

        def group(it, carry, bi=bi, d=d, ph=ph, nb=nb, first_branch=first_branch, last_branch=last_branch):
            def fetch(j):
                sub = []
                for g in range(j * ATT_SUB, (j + 1) * ATT_SUB):
                    blk = it * ATT_UNROLL + g
                    r = blk // nb
                    n = blk % nb
                    runs = _query_runs(d, r, n)
                    krow = pl.multiple_of(r * ph + n * SPAN, SPAN)
                    variant = jnp.logical_or(t > 0, n > 0).astype(jnp.int32)
                    qf = _load_runs(qp_ref, runs)
                    qs = jnp.concatenate([jnp.where(head_mask[s], qf, 0.0).astype(BF16) for s in range(2)],
                                         axis=0)
                    logits = _dot_nt(qs, kph_ref[bi, pl.ds(krow, 2 * SPAN), :]) + bias_ref[0, bi, variant]
                    old = None
                    if not first_branch:
                        old = ([_load_runs(m_ref, runs, s) for s in range(2)],
                               [_load_runs(l_ref, runs, s) for s in range(2)], _load_runs(acc_ref, runs))
                    sub.append((runs, krow, old, logits))
                return sub

            nsub = ATT_UNROLL // ATT_SUB
            nxt = fetch(0)
            for j in range(nsub):
                cur = nxt
                if j + 1 < nsub:
                    nxt = fetch(j + 1)
                for runs, m_new, l_new, pv in softmax_pv(cur):
                    if last_branch:
                        _store_runs(acc_ref, runs, pv / jnp.where(first_head, l_new[0], l_new[1]))
                    else:
                        for s in range(2):
                            _store_runs(m_ref, runs, m_new[s], s)
                            _store_runs(l_ref, runs, l_new[s], s)
                        _store_runs(acc_ref, runs, pv)
            return carry

        def softmax_pv(sub, bi=bi, first_branch=first_branch):
            soft = []
            for runs, krow, old, sc2 in sub:
                m_new, l_new, alpha, ps = [], [], [], []
                for s in range(2):
                    sc = sc2[s * SPAN:(s + 1) * SPAN, :]
                    halves = (sc[:, 0:SPAN], sc[:, SPAN:2 * SPAN])
                    mb = jnp.max(jnp.maximum(halves[0], halves[1]), axis=-1, keepdims=True)
                    if first_branch:
                        m = jnp.broadcast_to(mb, (SPAN, LANES))
                    else:
                        m = jnp.maximum(old[0][s], mb)
                        alpha.append(jnp.exp2(old[0][s] - m))
                    p = [jnp.exp2(h - m) for h in halves]
                    rs = jnp.sum(p[0] + p[1], axis=-1, keepdims=True)
                    ps.append(jnp.concatenate([p[0].astype(BF16), p[1].astype(BF16)], axis=1))
                    m_new.append(m)
                    l_new.append(jnp.broadcast_to(rs, (SPAN, LANES)) if first_branch
                                 else alpha[s] * old[1][s] + rs)
                soft.append((m_new, l_new, alpha, ps))
            out = []
            for (runs, krow, old, _), (m_new, l_new, alpha, ps) in zip(sub, soft):
                pv2 = _dot(jnp.concatenate(ps, axis=0), vph_ref[bi, pl.ds(krow, 2 * SPAN), :])
                pv = jnp.where(first_head, pv2[0:SPAN, :], pv2[SPAN:2 * SPAN, :])
                if not first_branch:
                    pv = jnp.where(first_head, alpha[0], alpha[1]) * old[2] + pv
                out.append((runs, m_new, l_new, pv))
            return out

        lax.fori_loop(0, nblk // ATT_UNROLL, group, 0)

    for r16 in range(d16):
        tmp_ref[pl.ds((r16 % d4) * l4 + r16 // d4, l16, stride=d4), :] = acc_ref[r16 * l16:(r16 + 1) * l16, :]
    for r4 in range(d4):
        o_ref[pl.ds(r4, l4, stride=d4), :] = tmp_ref[r4 * l4:(r4 + 1) * l4, :]

    for bi, d in enumerate(DILATIONS):
        ln = ATT_T // d
        ph = ln + SPAN
        for r in range(d):
            kph_ref[bi, r * ph:r * ph + SPAN, :] = kph_ref[bi, (r + 1) * ph - SPAN:(r + 1) * ph, :]
            vph_ref[bi, r * ph:r * ph + SPAN, :] = vph_ref[bi, (r + 1) * ph - SPAN:(r + 1) * ph, :]


def _attn_prompt(qkv, bias, batch, seq):
    m = batch * seq
    nt = seq // ATT_T
    npair = N_HEADS_A // 2
    nd = len(DILATIONS)
    blk = lambda off: pl.BlockSpec((ATT_T, LANES), lambda b, p, t: (b * nt + t, off + p))
    halo_rows = ATT_T + DILATIONS[-1] * SPAN
    return pl.pallas_call(
        _attn_kernel,
        grid=(batch, npair, nt),
        in_specs=[blk(0), blk(npair), blk(2 * npair),
                  pl.BlockSpec((1, nd, 2, 2 * SPAN, 2 * SPAN), lambda b, p, t: (p, 0, 0, 0, 0))],
        out_specs=blk(0),
        out_shape=jax.ShapeDtypeStruct((m, D_ATTN), F32),
        scratch_shapes=[pltpu.VMEM((ATT_T, LANES), F32),
                        pltpu.VMEM((ATT_T, LANES), F32),
                        pltpu.VMEM((nd, halo_rows, LANES), BF16),
                        pltpu.VMEM((nd, halo_rows, LANES), BF16),
                        pltpu.VMEM((ATT_T, LANES), F32),
                        pltpu.VMEM((2, ATT_T, LANES), F32),
                        pltpu.VMEM((2, ATT_T, LANES), F32)],
        compiler_params=pltpu.CompilerParams(
            dimension_semantics=("arbitrary", "arbitrary", "arbitrary"), vmem_limit_bytes=VMEM_LIMIT),
        name="attn_prompt",
    )(qkv, qkv, qkv, bias)


SAMPLE_ROWS = 4


def _sample_tables():
    pos = np.arange(MAX_WINDOW)[None, :]
    t = (np.arange(8) % SAMPLE_ROWS)[:, None]
    dist = MAX_WINDOW + t - pos
    mult = np.zeros(dist.shape, np.float32)
    for d in DILATIONS:
        mult += ((dist % d == 0) & (dist <= SPAN * d)).astype(np.float32)
    bucket = np.where(mult > 0, _t5_bucket_np(dist), -1).astype(np.int32)
    return bucket, np.concatenate([mult, mult], axis=0)


def _sample_bias_kernel(tab_ref, rb_ref, rbv_ref, bmain_ref, bnew_ref, *, buckets):
    pair = pl.program_id(0)
    tab = tab_ref[...]
    acc = [jnp.full(tab.shape, NEG, F32) for _ in range(2)]
    for bkt in buckets:
        hit = tab == bkt
        for s in range(2):
            acc[s] = jnp.where(hit, rb_ref[bkt, 2 * pair + s], acc[s])
    for s in range(2):
        bmain_ref[0, s * 8:(s + 1) * 8, :] = acc[s]
    t = lax.broadcasted_iota(jnp.int32, (8, 1), 0) % SAMPLE_ROWS
    for tp in range(SAMPLE_ROWS):
        b = jnp.full((8, LANES), NEG, F32)
        for dist in range(SAMPLE_ROWS):
            b = jnp.where(t - tp == dist, rbv_ref[dist:dist + 1, :], b)
        bnew_ref[tp] = b


def _sample_bias_tables(rel_bias):
    bucket, mult = _sample_tables()
    buckets = sorted(int(b) for b in np.unique(bucket) if b >= 0)
    npair = N_HEADS_A // 2
    rb_lanes = jnp.pad(rel_bias, ((0, 0), (0, LANES - N_HEADS_A)))
    bmain, bnew = pl.pallas_call(
        functools.partial(_sample_bias_kernel, buckets=buckets),
        grid=(npair,),
        in_specs=[_resident((8, MAX_WINDOW)), pl.BlockSpec(memory_space=pltpu.SMEM),
                  _resident((N_BUCKETS, LANES))],
        out_specs=[pl.BlockSpec((1, 16, MAX_WINDOW), lambda p: (p, 0, 0)),
                   pl.BlockSpec((SAMPLE_ROWS, 8, LANES), lambda p: (0, 0, 0))],
        out_shape=[jax.ShapeDtypeStruct((npair, 16, MAX_WINDOW), F32),
                   jax.ShapeDtypeStruct((SAMPLE_ROWS, 8, LANES), F32)],
        compiler_params=pltpu.CompilerParams(dimension_semantics=("arbitrary",)),
        name="sample_bias_tables",
    )(jnp.asarray(bucket), rel_bias, rb_lanes)
    return bmain, bnew, jnp.asarray(mult)


def _attn_sample_body(qkv_ref, kt_ref, vt_ref, bmain_ref, mult_ref, bnew_ref, e_ref, om_ref):
    rows = SAMPLE_ROWS
    npair = N_HEADS_A // 2
    second = pl.program_id(0) % 2 == 1
    lower = lax.broadcasted_iota(jnp.int32, (2 * rows, 1), 0) < rows
    mine = jnp.logical_xor(lower, second)
    blk = qkv_ref[...]
    q8_all = jnp.where(mine, blk, pltpu.roll(blk, rows, 0))
    q8 = q8_all[:, 0:D_ATTN] * ATTN_SCALE
    kn = q8_all[:, D_ATTN:2 * D_ATTN]
    vn = q8_all[:, 2 * D_ATTN:3 * D_ATTN]
    lane = lax.broadcasted_iota(jnp.int32, (1, LANES), 1)
    first_head = lane < HEAD_DIM
    e = e_ref[...]

    s_new = []
    m_tot = jnp.full((8, LANES), NEG, F32)
    for tp in range(rows):
        hi, lo, lo2 = _split3(q8 * kn[tp:tp + 1, :])
        s = _dot_nt(hi, e) + _dot_nt(lo, e) + _dot_nt(lo2, e) + bnew_ref[tp]
        s_new.append(s)
        m_tot = jnp.maximum(m_tot, s)

    mult = mult_ref[...]
    l_tot = jnp.zeros((8, LANES), F32)

    def window_logits(p):
        qp = q8[:, p * LANES:(p + 1) * LANES]
        qbd = jnp.concatenate([jnp.where(first_head, qp, 0.0), jnp.where(first_head, 0.0, qp)],
                              axis=0).astype(BF16)
        return _dot(qbd, kt_ref[0, p].astype(BF16)) + bmain_ref[p]

    s_next = window_logits(0)
    for p in range(npair):
        s16 = s_next
        if p + 1 < npair:
            s_next = window_logits(p + 1)
        m_new =jnp.concatenate([m_tot[:, 2 * p:2 * p + 1], m_tot[:, 2 * p + 1:2 * p + 2]], axis=0)
        m = jnp.maximum(jnp.max(s16, axis=-1, keepdims=True), m_new)
        pr = mult * jnp.exp(s16 - m)
        ls = jnp.sum(pr, axis=-1, keepdims=True)
        o16 = _dot_nt(pr.astype(BF16), vt_ref[0, p].astype(BF16))
        om_ref[:, p * LANES:(p + 1) * LANES] = jnp.where(first_head, o16[0:8], o16[8:16])
        m_tot = jnp.where(lane == 2 * p, m[0:8], jnp.where(lane == 2 * p + 1, m[8:16], m_tot))
        l_tot = jnp.where(lane == 2 * p, ls[0:8], jnp.where(lane == 2 * p + 1, ls[8:16], l_tot))
        yield

    t = lax.broadcasted_iota(jnp.int32, (8, 1), 0) % rows
    o = om_ref[...]
    for tp in range(rows):
        c = jnp.where(t == tp, float(len(DILATIONS)), jnp.where(t > tp, 1.0, 0.0))
        pn = c * jnp.exp(s_new[tp] - m_tot)
        l_tot = l_tot + pn
        o = o + _expand_heads(pn, e_ref) * vn[tp:tp + 1, :]
    o = o / _expand_heads(l_tot, e_ref)
    return o, second, lower


def _interleave(*gens):
    results = [None] * len(gens)
    live = list(range(len(gens)))
    while live:
        for k in list(live):
            try:
                next(gens[k])
            except StopIteration as stop:
                results[k] = stop.value
                live.remove(k)
    return results


def _sample_attn_prompt_ssd_kernel(qkv_ref, kt_ref, vt_ref, bmain_ref, mult_ref, bnew_ref, e_ref,
                                   z_ref, xbc_ref, dt_ref, conv0_ref, h0_ref, convw_ref, convb_ref, dtb_ref,
                                   alog_ref, dskip_ref, gnorm_ref, o_ref, y_ref, hout_ref,
                                   om_ref, xpad_ref, state_ref, *, chunks):
    c = pl.program_id(0) % chunks

    @pl.when(c == 0)
    def _():
        state_ref[...] = h0_ref[0]
        xpad_ref[8 - (CONV_WIDTH - 1):8, :] = conv0_ref[0]

    attn = _attn_sample_body(qkv_ref, kt_ref, vt_ref, bmain_ref, mult_ref, bnew_ref, e_ref, om_ref)
    scan = _ssd_chunk(z_ref, xbc_ref, dt_ref, y_ref.at[0], xpad_ref, state_ref,
                      convw_ref, convb_ref, dtb_ref, alog_ref, dskip_ref, gnorm_ref, e_ref)
    o, second, lower = _interleave(attn, scan)[0]

    @pl.when(jnp.logical_not(second))
    def _():
        o_ref[...] = o

    @pl.when(second)
    def _():
        o_ref[...] = jnp.where(lower, o_ref[...], o)

    @pl.when(c == chunks - 1)
    def _():
        hout_ref[0] = state_ref[...]


def _sample_attn_prompt_ssd(qkv, kt, vt, bmain, mult, bnew, expand, z, xbc, dt_raw, conv0, h0, ssd_params,
                            nseq, seq):
    rows = SAMPLE_ROWS
    batch = kt.shape[0]
    chunks = seq // SSD_CHUNK
    assert batch == nseq * chunks, "one prompt chunk per sample sequence"
    assert qkv.shape[0] == nseq * seq + batch * rows
    sample_blk0 = nseq * seq // (2 * rows)
    npair = N_HEADS_A // 2
    cache = pl.BlockSpec((1, npair, LANES, MAX_WINDOW), lambda i: (i, 0, 0, 0))
    chunk_in = lambda w: pl.BlockSpec((SSD_CHUNK, w), lambda i: (i, 0))
    chunk = lambda w: pl.BlockSpec((1, SSD_CHUNK, w), lambda i: (i // chunks, i % chunks, 0))
    per_seq = lambda r, w: pl.BlockSpec((1, r, w), lambda i: (i // chunks, 0, 0))
    return pl.pallas_call(
        functools.partial(_sample_attn_prompt_ssd_kernel, chunks=chunks),
        grid=(batch,),
        in_specs=[pl.BlockSpec((2 * rows, 3 * D_ATTN), lambda i: (sample_blk0 + i // 2, 0)), cache, cache,
                  _resident((npair, 16, MAX_WINDOW)), _resident((16, MAX_WINDOW)),
                  _resident((rows, 8, LANES)), _resident((LANES, D_ATTN)),
                  chunk_in(D_SSM), chunk_in(CONV_DIM), chunk_in(LANES), per_seq(CONV_WIDTH - 1, CONV_DIM),
                  per_seq(D_SSM, D_STATE)] + _ssd_param_specs()[:-1],
        out_specs=[pl.BlockSpec((2 * rows, D_ATTN), lambda i: (i // 2, 0)), chunk(D_SSM),
                   per_seq(D_SSM, D_STATE)],
        out_shape=[jax.ShapeDtypeStruct((batch * rows, D_ATTN), F32),
                   jax.ShapeDtypeStruct((nseq, seq, D_SSM), F32),
                   jax.ShapeDtypeStruct((nseq, D_SSM, D_STATE), F32)],
        scratch_shapes=[pltpu.VMEM((8, D_ATTN), F32), pltpu.VMEM((8 + SSD_CHUNK, CONV_DIM), F32),
                        pltpu.VMEM((D_SSM, D_STATE), F32)],
        compiler_params=pltpu.CompilerParams(
            dimension_semantics=("arbitrary",), vmem_limit_bytes=VMEM_LIMIT),
        name="sample_attn_prompt_ssd",
    )(qkv, kt, vt, bmain, mult, bnew, expand, z, xbc, dt_raw, conv0, h0, *ssd_params[:-1])


def _head_expand_matrix():
    e = np.zeros((LANES, D_SSM), np.float32)
    for h in range(N_HEADS_S):
        e[h, h * SSM_HEAD_DIM:(h + 1) * SSM_HEAD_DIM] = 1.0
    return jnp.asarray(e, BF16)


def _prep_params(norm_mix, w_in, q_norm, k_norm, conv_w, conv_b, dt_bias, a_log, d_skip, ssm_norm,
                 w_out, norm_ffn, w_up, w_down):
    n_main = 3 * D_ATTN + D_SSM + CONV_DIM
    pad_heads = lambda v: jnp.pad(v.reshape(1, N_HEADS_S), ((0, 0), (0, LANES - N_HEADS_S)))
    return dict(
        gmix=norm_mix.reshape(1, D_MODEL),
        w_main=w_in.astype(BF16),
        w_dt=jnp.pad(w_in[:, n_main:], ((0, 0), (0, LANES - N_HEADS_S))).astype(BF16),
        gq=jnp.tile(q_norm, N_HEADS_A).reshape(1, D_ATTN),
        gk=jnp.tile(k_norm, N_HEADS_A).reshape(1, D_ATTN),
        conv_w=conv_w,
        conv_b=conv_b.reshape(1, CONV_DIM),
        dt_bias=pad_heads(dt_bias),
        a_log=pad_heads(a_log),
        d_skip_e=jnp.repeat(d_skip, SSM_HEAD_DIM).reshape(1, D_SSM),
        gnorm=ssm_norm.reshape(1, D_SSM),
        expand=_head_expand_matrix(),
        w_out=w_out.astype(BF16),
        gffn=norm_ffn.reshape(1, D_MODEL),
        w_up=w_up.astype(BF16),
        w_down=w_down.astype(BF16),
    )


def _ssd_param_list(p):
    return (p["conv_w"], p["conv_b"], p["dt_bias"], p["a_log"], p["d_skip_e"], p["gnorm"], p["expand"])


def kernel(x_prompt, x_sample, cache_attn_k, cache_attn_v, state_conv, state_ssm, norm_mix, w_in, q_norm,
           k_norm, rel_bias, conv_w, conv_b, dt_bias, a_log, d_skip, ssm_norm, w_out, norm_ffn, w_up, w_down):
    depth = w_in.shape[0]
    assert depth == 1, "single-layer decoder"
    l = 0
    p = _prep_params(norm_mix[l], w_in[l], q_norm[l], k_norm[l], conv_w[l], conv_b[l], dt_bias[l], a_log[l],
                     d_skip[l], ssm_norm[l], w_out[l], norm_ffn[l], w_up[l], w_down[l])
    ssd_params = _ssd_param_list(p)
    tm = 512

    bp, sp = x_prompt.shape[:2]
    keep = min(MAX_WINDOW, sp)
    xp = x_prompt.reshape(bp * sp, D_MODEL)
    bs, ts = x_sample.shape[:2]
    assert cache_attn_k.shape[2] == MAX_WINDOW and ts == SAMPLE_ROWS
    xs = x_sample.reshape(bs * ts, D_MODEL)
    mp = bp * sp
    qkv, z, xbc, dt_raw, ktail, vtail = _inproj(xp, xs, p["gmix"], p["w_main"], p["w_dt"], p["gq"], p["gk"], tm,
                                                tail=(sp, keep))
    attn = _attn_prompt(qkv, _bias_tables(rel_bias), bp, sp)
    new_k_prompt = ktail.reshape(depth, bp, keep, N_HEADS_A, HEAD_DIM)
    new_v_prompt = vtail.reshape(depth, bp, keep, N_HEADS_A, HEAD_DIM)
    new_conv_prompt = jnp.stack([xbc[(b + 1) * sp - (CONV_WIDTH - 1):(b + 1) * sp] for b in range(bp)]).reshape(
        depth, bp, CONV_WIDTH - 1, CONV_DIM)

    npair = N_HEADS_A // 2
    kt = jnp.transpose(cache_attn_k[l], (0, 2, 3, 1)).reshape(bs, npair, LANES, MAX_WINDOW)
    vt = jnp.transpose(cache_attn_v[l], (0, 2, 3, 1)).reshape(bs, npair, LANES, MAX_WINDOW)
    bmain, bnew, mult = _sample_bias_tables(rel_bias)
    conv0 = jnp.zeros((bp, CONV_WIDTH - 1, CONV_DIM), F32)
    h0 = jnp.zeros((bp, D_SSM, D_STATE), F32)
    attn_s, ssm_o, hst = _sample_attn_prompt_ssd(
        qkv, kt, vt, bmain, mult, bnew, p["expand"], z, xbc, dt_raw, conv0, h0, ssd_params, bp, sp)
    y_prompt = _outffn(xp, attn, ssm_o.reshape(bp * sp, D_SSM), p["w_out"], p["gffn"], p["w_up"], p["w_down"], tm)
    new_ssm_prompt = hst.reshape(depth, bp, N_HEADS_S, SSM_HEAD_DIM, D_STATE)
    ssm_os, hsts = _ssd_sample(z, xbc, dt_raw, jnp.transpose(state_conv[l], (1, 0, 2)),
                               state_ssm[l].reshape(bs, D_SSM, D_STATE), ssd_params, ts, mp)
    y_sample = _outffn(xs, attn_s, ssm_os, p["w_out"], p["gffn"], p["w_up"], p["w_down"], tm)
    new_k_sample = qkv[mp:, D_ATTN:2 * D_ATTN].reshape(depth, bs, ts, N_HEADS_A, HEAD_DIM)
    new_v_sample = qkv[mp:, 2 * D_ATTN:].reshape(depth, bs, ts, N_HEADS_A, HEAD_DIM)
    new_conv_sample = xbc[mp:].reshape(bs, ts, CONV_DIM)[:, ts - (CONV_WIDTH - 1):].reshape(
        depth, bs, CONV_WIDTH - 1, CONV_DIM)
    new_ssm_sample = hsts.reshape(depth, bs, N_HEADS_S, SSM_HEAD_DIM, D_STATE)

    return (y_prompt.reshape(bp, sp, D_MODEL), y_sample.reshape(bs, ts, D_MODEL),
            new_k_prompt, new_v_prompt, new_conv_prompt, new_ssm_prompt,
            new_k_sample, new_v_sample, new_conv_sample, new_ssm_sample)
```

```python
import functools
import math

import jax
import jax.numpy as jnp
import numpy as np
from jax import lax
from jax.experimental import pallas as pl
from jax.experimental.pallas import tpu as pltpu

F32 = jnp.float32
BF16 = jnp.bfloat16

D_MODEL = 1024
HEAD_DIM = 64
N_HEADS_A = 16
D_ATTN = 1024
SPAN = 128
DILATIONS = (1, 4, 16)
MAX_WINDOW = 2048
N_BUCKETS = 32
MAX_EXACT = 16
SSM_HEAD_DIM = 64
N_HEADS_S = 16
D_SSM = 1024
SSM_GROUPS = 2
D_STATE = 128
CONV_WIDTH = 4
CONV_DIM = D_SSM + 2 * SSM_GROUPS * D_STATE
SSD_CHUNK = 128
D_FF = 4096
EPS = 1e-6
ATTN_SCALE = HEAD_DIM ** -0.5
LOG2E = 1.4426950408889634

LANES = 128
NEG = -1e30
VMEM_LIMIT = 56 * 1024 * 1024


def _resident(shape):
    nd = len(shape)
    return pl.BlockSpec(shape, lambda *_: (0,) * nd, pipeline_mode=pl.Buffered(1))


def _split3(x):
    hi = x.astype(BF16)
    r1 = x - hi.astype(F32)
    lo = r1.astype(BF16)
    lo2 = (r1 - lo.astype(F32)).astype(BF16)
    return hi, lo, lo2


def _dot(a, b):
    return jnp.dot(a, b, preferred_element_type=F32)


def _dot_nt(a, b):
    return lax.dot_general(a, b, (((1,), (1,)), ((), ())), preferred_element_type=F32)


def _sigmoid(x):
    return 0.5 * jnp.tanh(0.5 * x) + 0.5


def _softplus(x):
    return jnp.maximum(x, 0.0) + jnp.log1p(jnp.exp(-jnp.abs(x)))


IN_TN = 512


def _inproj_kernel(x_ref, gmix_ref, w_ref, wdt_ref, gq_ref, gk_ref,
                   qkv_ref, z_ref, xbc_ref, dt_ref, *tail_refs, with_tail):
    x = x_ref[...]
    ms = jnp.mean(x * x, axis=-1, keepdims=True)
    xn = ((x * lax.rsqrt(ms + EPS)) * gmix_ref[...]).astype(BF16)
    first_head = lax.broadcasted_iota(jnp.int32, (1, LANES), 1) < HEAD_DIM

    n_main = 3 * D_ATTN + D_SSM + CONV_DIM
    for j in range(n_main // IN_TN):
        c0 = j * IN_TN
        t = _dot(xn, w_ref[:, c0:c0 + IN_TN])
        if c0 < 2 * D_ATTN:
            g_ref = gq_ref if c0 < D_ATTN else gk_ref
            g0 = c0 % D_ATTN
            for c in range(IN_TN // LANES):
                tc = t[:, c * LANES:(c + 1) * LANES]
                s = tc * tc
                sa = jnp.sum(jnp.where(first_head, s, 0.0), axis=-1, keepdims=True)
                sb = jnp.sum(jnp.where(first_head, 0.0, s), axis=-1, keepdims=True)
                inv = lax.rsqrt(jnp.where(first_head, sa, sb) * (1.0 / HEAD_DIM) + EPS)
                out = (tc * inv) * g_ref[:, g0 + c * LANES:g0 + (c + 1) * LANES]
                qkv_ref[:, c0 + c * LANES:c0 + (c + 1) * LANES] = out
                if with_tail and c0 >= D_ATTN:
                    tail_refs[0][0, g0 + c * LANES:g0 + (c + 1) * LANES, :] = out.T
        elif c0 < 3 * D_ATTN:
            qkv_ref[:, c0:c0 + IN_TN] = t
            if with_tail:
                tail_refs[1][0, c0 - 2 * D_ATTN:c0 - 2 * D_ATTN + IN_TN, :] = t.T
        elif c0 < 3 * D_ATTN + D_SSM:
            z_ref[:, c0 - 3 * D_ATTN:c0 - 3 * D_ATTN + IN_TN] = t
        else:
            o0 = c0 - 3 * D_ATTN - D_SSM
            xbc_ref[:, o0:o0 + IN_TN] = t
    dt_ref[...] = _dot(xn, wdt_ref[...])


def _inproj(x2d, gmix, w_main, w_dt, gq, gk, tm, tail=None):
    m = x2d.shape[0]
    row = lambda w: pl.BlockSpec((tm, w), lambda i: (i, 0))
    out_specs = [row(3 * D_ATTN), row(D_SSM), row(CONV_DIM), row(LANES)]
    out_shape = [jax.ShapeDtypeStruct((m, 3 * D_ATTN), F32), jax.ShapeDtypeStruct((m, D_SSM), F32),
                 jax.ShapeDtypeStruct((m, CONV_DIM), F32), jax.ShapeDtypeStruct((m, LANES), F32)]
    if tail is not None:
        per_seq, kept = tail[0] // tm, tail[1] // tm
        kept_block = lambda i: (i // per_seq, 0, jnp.maximum(i % per_seq - (per_seq - kept), 0))
        out_specs += [pl.BlockSpec((1, D_ATTN, tm), kept_block)] * 2
        out_shape += [jax.ShapeDtypeStruct((m // tail[0], D_ATTN, tail[1]), F32)] * 2
    return pl.pallas_call(
        functools.partial(_inproj_kernel, with_tail=tail is not None),
        grid=(m // tm,),
        in_specs=[row(D_MODEL), _resident((1, D_MODEL)), _resident(w_main.shape),
                  _resident((D_MODEL, LANES)), _resident((1, D_ATTN)), _resident((1, D_ATTN))],
        out_specs=out_specs,
        out_shape=out_shape,
        compiler_params=pltpu.CompilerParams(
            dimension_semantics=("arbitrary",), vmem_limit_bytes=VMEM_LIMIT),
        name="inproj",
    )(x2d, gmix, w_main, w_dt, gq, gk)


FFN_TF = 1024


def _outffn_kernel(x_ref, attn_ref, ssm_ref, wout_ref, gffn_ref, wup_ref, wdown_ref, y_ref):
    a = attn_ref[...].astype(BF16)
    s = ssm_ref[...].astype(BF16)
    h = x_ref[...] + _dot(a, wout_ref[0:D_ATTN, :]) + _dot(s, wout_ref[D_ATTN:D_ATTN + D_SSM, :])
    ms = jnp.mean(h * h, axis=-1, keepdims=True)
    hn = ((h * lax.rsqrt(ms + EPS)) * gffn_ref[...]).astype(BF16)
    acc = None
    for f in range(D_FF // FFN_TF):
        u = _dot(hn, wup_ref[:, f * FFN_TF:(f + 1) * FFN_TF])
        u = jnp.maximum(u, 0.0)
        u = (u * u).astype(BF16)
        d = _dot(u, wdown_ref[f * FFN_TF:(f + 1) * FFN_TF, :])
        acc = d if acc is None else acc + d
    y_ref[...] = h + acc


def _outffn(x2d, attn, ssm_o, w_out, gffn, w_up, w_down, tm):
    m = x2d.shape[0]
    row = pl.BlockSpec((tm, D_MODEL), lambda i: (i, 0))
    return pl.pallas_call(
        _outffn_kernel,
        grid=(m // tm,),
        in_specs=[row, row, row, _resident((D_ATTN + D_SSM, D_MODEL)), _resident((1, D_MODEL)),
                  _resident((D_MODEL, D_FF)), _resident((D_FF, D_MODEL))],
        out_specs=row,
        out_shape=jax.ShapeDtypeStruct((m, D_MODEL), F32),
        compiler_params=pltpu.CompilerParams(
            dimension_semantics=("arbitrary",), vmem_limit_bytes=VMEM_LIMIT),
        name="outffn",
    )(x2d, attn, ssm_o, w_out, gffn, w_up, w_down)


def _expand_heads(v, e_ref):
    hi, lo, lo2 = _split3(v)
    e = e_ref[...]
    return _dot(hi, e) + _dot(lo, e) + _dot(lo2, e)


def _ssd_chunk(z_ref, xbc_ref, dt_ref, y_ref, xpad_ref, state_ref,
               convw_ref, convb_ref, dtb_ref, alog_ref, dskip_ref, gnorm_ref, e_ref):
    q = SSD_CHUNK
    pad = 8
    tail = CONV_WIDTH - 1

    xbc = xbc_ref[...]
    xpad_ref[pad:pad + q, :] = xbc
    conv = convb_ref[...] + convw_ref[tail:tail + 1, :] * xbc
    for i in range(tail):
        conv = conv + convw_ref[i:i + 1, :] * xpad_ref[pad - tail + i:pad - tail + i + q, :]
    xpad_ref[pad - tail:pad, :] = xbc[q - tail:q, :]
    xc = conv * _sigmoid(conv)
    xs = xc[:, 0:D_SSM]

    lane = lax.broadcasted_iota(jnp.int32, (1, LANES), 1)
    dt = jnp.where(lane < N_HEADS_S, _softplus(dt_ref[...] + dtb_ref[...]), 0.0)
    da = dt * (-jnp.exp(alog_ref[...]))

    ii = lax.broadcasted_iota(jnp.int32, (q, q), 0)
    jj = lax.broadcasted_iota(jnp.int32, (q, q), 1)
    causal = ii >= jj
    tri = jnp.where(causal, 1.0, 0.0).astype(BF16)
    hi, lo, lo2 = _split3(da)
    cum = _dot(tri, hi) + _dot(tri, lo) + _dot(tri, lo2)
    cum_t = cum.T

    cum_e = _expand_heads(cum, e_ref)
    last_e = cum_e[q - 1:q, :]
    xdt = xs * _expand_heads(dt, e_ref)
    xw_t = (xdt * jnp.exp(last_e - cum_e)).T.astype(BF16)
    state_decay = jnp.exp(jnp.broadcast_to(last_e, (q, D_SSM)).T)
    off_scale = jnp.exp(cum_e)

    first_head = lane < SSM_HEAD_DIM
    heads_per_group = N_HEADS_S // SSM_GROUPS
    gw = heads_per_group * SSM_HEAD_DIM
    state = state_ref[...]
    state_bf = state.astype(BF16)
    yield
    for g in range(SSM_GROUPS):
        bg = xc[:, D_SSM + g * D_STATE:D_SSM + (g + 1) * D_STATE].astype(BF16)
        cg = xc[:, D_SSM + (SSM_GROUPS + g) * D_STATE:D_SSM + (SSM_GROUPS + g + 1) * D_STATE].astype(BF16)
        cb = _dot_nt(cg, bg)
        y_off = _dot_nt(cg, state_bf[g * gw:(g + 1) * gw, :]) * off_scale[:, g * gw:(g + 1) * gw]
        y_ref[:, g * gw:(g + 1) * gw] = y_off
        for hp in range(heads_per_group // 2):
            pair = g * (heads_per_group // 2) + hp
            xp = xdt[:, pair * LANES:(pair + 1) * LANES]
            yp = None
            for s in range(2):
                h = 2 * pair + s
                seg = cum[:, h:h + 1] - cum_t[h:h + 1, :]
                dec = jnp.exp(jnp.where(causal, seg, NEG))
                m = (cb * dec).astype(BF16)
                xm = jnp.where(first_head if s == 0 else jnp.logical_not(first_head), xp, 0.0).astype(BF16)
                d = _dot(m, xm)
                yp = d if yp is None else yp + d
            y_ref[:, pair * LANES:(pair + 1) * LANES] += yp
            yield
        state_ref[g * gw:(g + 1) * gw, :] = (
            state_decay[g * gw:(g + 1) * gw, :] * state[g * gw:(g + 1) * gw, :]
            + _dot(xw_t[g * gw:(g + 1) * gw, :], bg))

    z = z_ref[...]
    y = (y_ref[...] + xs * dskip_ref[...]) * (z * _sigmoid(z))
    for g in range(SSM_GROUPS):
        yg = y[:, g * gw:(g + 1) * gw]
        ms = jnp.mean(yg * yg, axis=-1, keepdims=True)
        y_ref[:, g * gw:(g + 1) * gw] = (yg * lax.rsqrt(ms + EPS)) * gnorm_ref[:, g * gw:(g + 1) * gw]


def _ssd_sample_kernel(z_ref, xbc_ref, dt_ref, conv0_ref, h0_ref, convw_ref, convb_ref, dtb_ref,
                       alog_ref, dskip_ref, gnorm_ref, e_ref, y_ref, hout_ref, xs_ref, z8_ref, dt8_ref,
                       *, rows):
    for b in range(h0_ref.shape[0]):
        tok = pl.ds(b * rows, rows)
        _ssd_sample_one(z_ref.at[tok], xbc_ref.at[tok], dt_ref.at[tok], conv0_ref.at[:, b], h0_ref.at[b],
                        convw_ref, convb_ref, dtb_ref, alog_ref, dskip_ref, gnorm_ref, e_ref, y_ref.at[tok],
                        hout_ref.at[b], xs_ref.at[b], z8_ref.at[b], dt8_ref.at[b], rows)


def _ssd_sample_one(z_ref, xbc_ref, dt_ref, conv0_ref, h0_ref, convw_ref, convb_ref, dtb_ref,
                    alog_ref, dskip_ref, gnorm_ref, e_ref, y_ref, hout_ref, xs_ref, z8_ref, dt8_ref, rows):
    tail = CONV_WIDTH - 1
    r8 = 8
    row = lax.broadcasted_iota(jnp.int32, (r8, 1), 0)
    lane = lax.broadcasted_iota(jnp.int32, (1, LANES), 1)

    xs_ref[r8:2 * r8, :] = jnp.zeros((r8, CONV_DIM), F32)
    xs_ref[r8:r8 + rows, :] = xbc_ref[...]
    xs_ref[r8 - tail:r8, :] = conv0_ref[...]
    z8_ref[...] = jnp.zeros((r8, D_SSM), F32)
    z8_ref[0:rows, :] = z_ref[...]
    dt8_ref[...] = jnp.zeros((r8, LANES), F32)
    dt8_ref[0:rows, :] = dt_ref[...]

    conv = convb_ref[...] + convw_ref[tail:tail + 1, :] * xs_ref[r8:2 * r8, :]
    for i in range(tail):
        conv = conv + convw_ref[i:i + 1, :] * xs_ref[r8 - tail + i:2 * r8 - tail + i, :]
    xc = conv * _sigmoid(conv)
    xs = xc[:, 0:D_SSM]

    live = jnp.logical_and(lane < N_HEADS_S, row < rows)
    dt = jnp.where(live, _softplus(dt8_ref[...] + dtb_ref[...]), 0.0)
    da = dt * (-jnp.exp(alog_ref[...]))
    cum = da
    for k in range(1, rows):
        cum = cum + pltpu.roll(da, k, 0)
    last = rows - 1

    heads_per_group = N_HEADS_S // SSM_GROUPS
    gw = heads_per_group * SSM_HEAD_DIM
    bg = [xc[:, D_SSM + g * D_STATE:D_SSM + (g + 1) * D_STATE] for g in range(SSM_GROUPS)]
    cg = [xc[:, D_SSM + (SSM_GROUPS + g) * D_STATE:D_SSM + (SSM_GROUPS + g + 1) * D_STATE]
          for g in range(SSM_GROUPS)]
    group0 = lane < heads_per_group
    w = []
    for j in range(rows):
        cb = [jnp.sum(cg[g] * bg[g][j:j + 1, :], axis=-1, keepdims=True) for g in range(SSM_GROUPS)]
        dec = jnp.exp(jnp.where(row >= j, cum - cum[j:j + 1, :], NEG))
        w.append(jnp.where(group0, cb[0], cb[1]) * dec)

    stack = jnp.concatenate([cum, dt] + w, axis=0)
    hi, lo, lo2 = _split3(stack)
    ex = _dot(jnp.concatenate([hi, lo, lo2], axis=0), e_ref[...])
    ns = stack.shape[0]
    ex = ex[0:ns] + ex[ns:2 * ns] + ex[2 * ns:3 * ns]
    cum_e = ex[0:r8]
    xdt = xs * ex[r8:2 * r8]
    y = None
    for j in range(rows):
        term = ex[(2 + j) * r8:(3 + j) * r8] * xdt[j:j + 1, :]
        y = term if y is None else y + term
    last_e = cum_e[last:last + 1, :]

    state = h0_ref[...]
    state_bf = state.astype(BF16)
    zeros8 = jnp.zeros((r8, D_STATE), F32)
    y_off = [_dot_nt(jnp.concatenate([cg[g], zeros8], axis=0).astype(BF16),
                     state_bf[g * gw:(g + 1) * gw, :])[0:r8, :] for g in range(SSM_GROUPS)]
    y = y + jnp.concatenate(y_off, axis=1) * jnp.exp(cum_e)

    z = z8_ref[...]
    y = (y + xs * dskip_ref[...]) * (z * _sigmoid(z))
    outs = []
    for g in range(SSM_GROUPS):
        yg = y[:, g * gw:(g + 1) * gw]
        ms = jnp.mean(yg * yg, axis=-1, keepdims=True)
        outs.append((yg * lax.rsqrt(ms + EPS)) * gnorm_ref[:, g * gw:(g + 1) * gw])
    y_ref[...] = jnp.concatenate(outs, axis=1)[0:rows, :]

    pad = jnp.zeros((SSD_CHUNK - r8, D_SSM), F32)
    xw_t = jnp.concatenate([xdt * jnp.exp(last_e - cum_e), pad], axis=0).T.astype(BF16)
    state_decay = jnp.exp(jnp.broadcast_to(last_e, (SSD_CHUNK, D_SSM)).T)
    bpad = jnp.zeros((SSD_CHUNK - r8, D_STATE), F32)
    for g in range(SSM_GROUPS):
        b128 = jnp.concatenate([bg[g], bpad], axis=0).astype(BF16)
        hout_ref[g * gw:(g + 1) * gw, :] = (
            state_decay[g * gw:(g + 1) * gw, :] * state[g * gw:(g + 1) * gw, :]
            + _dot(xw_t[g * gw:(g + 1) * gw, :], b128))


def _ssd_param_specs():
    return [_resident((CONV_WIDTH, CONV_DIM)), _resident((1, CONV_DIM)), _resident((1, LANES)),
            _resident((1, LANES)), _resident((1, D_SSM)), _resident((1, D_SSM)),
            _resident((LANES, D_SSM))]


SSD_SAMPLE_SEQS = 8


def _ssd_sample(z, xbc, dt_raw, conv0, h0, ssd_params, rows):
    batch = h0.shape[0]
    nb = SSD_SAMPLE_SEQS
    tok = lambda w: pl.BlockSpec((nb * rows, w), lambda i: (i, 0))
    state = pl.BlockSpec((nb, D_SSM, D_STATE), lambda i: (i, 0, 0))
    return pl.pallas_call(
        functools.partial(_ssd_sample_kernel, rows=rows),
        grid=(batch // nb,),
        in_specs=[tok(D_SSM), tok(CONV_DIM), tok(LANES),
                  pl.BlockSpec((CONV_WIDTH - 1, nb, CONV_DIM), lambda i: (0, i, 0)), state] + _ssd_param_specs(),
        out_specs=[tok(D_SSM), state],
        out_shape=[jax.ShapeDtypeStruct((batch * rows, D_SSM), F32),
                   jax.ShapeDtypeStruct((batch, D_SSM, D_STATE), F32)],
        scratch_shapes=[pltpu.VMEM((nb, 16, CONV_DIM), F32), pltpu.VMEM((nb, 8, D_SSM), F32),
                        pltpu.VMEM((nb, 8, LANES), F32)],
        compiler_params=pltpu.CompilerParams(
            dimension_semantics=("arbitrary",), vmem_limit_bytes=VMEM_LIMIT),
        name="ssd_sample",
    )(z, xbc, dt_raw, conv0, h0, *ssd_params)


def _t5_bucket_np(dist):
    dist = np.asarray(dist, np.int64)
    df = np.maximum(dist, 1).astype(np.float32)
    large = MAX_EXACT + (np.log(df / np.float32(MAX_EXACT)) / np.float32(math.log(MAX_WINDOW / MAX_EXACT))
                         * np.float32(N_BUCKETS - MAX_EXACT)).astype(np.int32)
    return np.where(dist < MAX_EXACT, dist, np.minimum(large, N_BUCKETS - 1)).astype(np.int32)


def _band_bucket_tables():
    ki = np.arange(SPAN)[None, :]
    out = np.empty((len(DILATIONS), 2, SPAN, SPAN), np.int32)
    for bi, d in enumerate(DILATIONS):
        nrun = DILATIONS[-1] // d
        row = np.arange(SPAN)
        qi = (nrun * (row % (SPAN // nrun)) + row // (SPAN // nrun))[:, None]
        rel_prev = qi + SPAN - ki
        rel_cur = qi - ki
        out[bi, 0] = np.where(rel_prev <= SPAN, _t5_bucket_np(np.clip(rel_prev, 0, SPAN) * d), -1)
        out[bi, 1] = np.where(rel_cur >= 0, _t5_bucket_np(np.clip(rel_cur, 0, SPAN) * d), -1)
    return out


def _bias_table_kernel(tab_ref, rb_ref, out_ref, *, buckets):
    pair = pl.program_id(0)
    for bi in range(len(DILATIONS)):
        for half in range(2):
            tab = tab_ref[bi, half]
            acc = [jnp.full((SPAN, SPAN), NEG, F32) for _ in range(2)]
            for bkt in buckets[bi][half]:
                hit = tab == bkt
                for s in range(2):
                    acc[s] = jnp.where(hit, rb_ref[bkt, 2 * pair + s] * LOG2E, acc[s])
            for s in range(2):
                rows = slice(s * SPAN, (s + 1) * SPAN)
                if half == 0:
                    out_ref[0, bi, 0, rows, 0:SPAN] = jnp.full((SPAN, SPAN), NEG, F32)
                    out_ref[0, bi, 1, rows, 0:SPAN] = acc[s]
                else:
                    out_ref[0, bi, 0, rows, SPAN:2 * SPAN] = acc[s]
                    out_ref[0, bi, 1, rows, SPAN:2 * SPAN] = acc[s]


def _bias_tables(rel_bias):
    tabs = _band_bucket_tables()
    buckets = [[sorted(int(b) for b in np.unique(tabs[bi, half]) if b >= 0) for half in range(2)]
               for bi in range(len(DILATIONS))]
    nd = len(DILATIONS)
    return pl.pallas_call(
        functools.partial(_bias_table_kernel, buckets=buckets),
        grid=(N_HEADS_A // 2,),
        in_specs=[_resident((nd, 2, SPAN, SPAN)),
                  pl.BlockSpec(memory_space=pltpu.SMEM)],
        out_specs=pl.BlockSpec((1, nd, 2, 2 * SPAN, 2 * SPAN), lambda p: (p, 0, 0, 0, 0)),
        out_shape=jax.ShapeDtypeStruct((N_HEADS_A // 2, nd, 2, 2 * SPAN, 2 * SPAN), F32),
        compiler_params=pltpu.CompilerParams(dimension_semantics=("arbitrary",)),
        name="bias_tables",
    )(jnp.asarray(tabs), rel_bias)


ATT_T = SPAN * DILATIONS[-1]
ATT_UNROLL = 16
ATT_SUB = 2


def _query_runs(d, r, n):
    run = SPAN * d // DILATIONS[-1]
    return [pl.ds(pl.multiple_of((r + d * c) * SPAN + run * n, 8), run) for c in range(DILATIONS[-1] // d)]


def _load_runs(ref, runs, *lead):
    parts = [ref[(*lead, rows, slice(None))] for rows in runs]
    return parts[0] if len(parts) == 1 else jnp.concatenate(parts, axis=0)


def _store_runs(ref, runs, value, *lead):
    run = value.shape[0] // len(runs)
    for c, rows in enumerate(runs):
        ref[(*lead, rows, slice(None))] = value[c * run:(c + 1) * run, :]


def _attn_kernel(q_ref, k_ref, v_ref, bias_ref, o_ref, qp_ref, tmp_ref, kph_ref, vph_ref, acc_ref, m_ref,
                 l_ref):
    t = pl.program_id(2)
    lane = lax.broadcasted_iota(jnp.int32, (1, LANES), 1)
    first_head = lane < HEAD_DIM
    head_mask = (first_head, jnp.logical_not(first_head))
    nblk = ATT_T // SPAN
    dmax = DILATIONS[-1]

    @pl.when(t == 0)
    def _():
        for bi, d in enumerate(DILATIONS):
            ph = ATT_T // d + SPAN
            for r in range(d):
                kph_ref[bi, r * ph:r * ph + SPAN, :] = jnp.zeros((SPAN, LANES), BF16)
                vph_ref[bi, r * ph:r * ph + SPAN, :] = jnp.zeros((SPAN, LANES), BF16)

    d4, d16 = DILATIONS[1], DILATIONS[2]
    assert DILATIONS == (1, d4, d4 * d4)
    l4, l16 = ATT_T // d4, ATT_T // d16
    for r4 in range(d4):
        tmp_ref[r4 * l4:(r4 + 1) * l4, :] = q_ref[pl.ds(r4, l4, stride=d4), :] * (ATTN_SCALE * LOG2E)
    for r16 in range(d16):
        qp_ref[r16 * l16:(r16 + 1) * l16, :] = tmp_ref[pl.ds((r16 % d4) * l4 + r16 // d4, l16, stride=d4), :]
    for src_ref, dst_ref in ((k_ref, kph_ref), (v_ref, vph_ref)):
        dst_ref[0, SPAN:SPAN + ATT_T, :] = src_ref[...].astype(BF16)
        ph4, ph16 = l4 + SPAN, l16 + SPAN
        for r4 in range(d4):
            x4 = src_ref[pl.ds(r4, l4, stride=d4), :]
            tmp_ref[r4 * l4:(r4 + 1) * l4, :] = x4
            dst_ref[1, r4 * ph4 + SPAN:(r4 + 1) * ph4, :] = x4.astype(BF16)
        for r16 in range(d16):
            x16 = tmp_ref[pl.ds((r16 % d4) * l4 + r16 // d4, l16, stride=d4), :]
            dst_ref[2, r16 * ph16 + SPAN:(r16 + 1) * ph16, :] = x16.astype(BF16)

    order = tuple(reversed(range(len(DILATIONS))))
    for bi in order:
        d = DILATIONS[bi]
        ln = ATT_T // d
        ph = ln + SPAN
        nb = ln // SPAN
        first_branch = bi == order[0]
        last_branch = bi == order[-1]

        def group(it, carry, bi=bi, d=d, ph=ph, nb=nb, first_branch=first_branch, last_branch=last_branch):
            def fetch(j):
                sub = []
                for g in range(j * ATT_SUB, (j + 1) * ATT_SUB):
                    blk = it * ATT_UNROLL + g
                    r = blk // nb
                    n = blk % nb
                    runs = _query_runs(d, r, n)
                    krow = pl.multiple_of(r * ph + n * SPAN, SPAN)
                    variant = jnp.logical_or(t > 0, n > 0).astype(jnp.int32)
                    qf = _load_runs(qp_ref, runs)
                    qs = jnp.concatenate([jnp.where(head_mask[s], qf, 0.0).astype(BF16) for s in range(2)],
                                         axis=0)
                    logits = _dot_nt(qs, kph_ref[bi, pl.ds(krow, 2 * SPAN), :]) + bias_ref[0, bi, variant]
                    old = None
                    if not first_branch:
                        old = ([_load_runs(m_ref, runs, s) for s in range(2)],
                               [_load_runs(l_ref, runs, s) for s in range(2)], _load_runs(acc_ref, runs))
                    sub.append((runs, krow, old, logits))
                return sub

            nsub = ATT_UNROLL // ATT_SUB
            nxt = fetch(0)
            for j in range(nsub):
                cur = nxt
                if j + 1 < nsub:
                    nxt = fetch(j + 1)
                for runs, m_new, l_new, pv in softmax_pv(cur):
                    if last_branch:
                        _store_runs(acc_ref, runs, pv / jnp.where(first_head, l_new[0], l_new[1]))
                    else:
                        for s in range(2):
                            _store_runs(m_ref, runs, m_new[s], s)
                            _store_runs(l_ref, runs, l_new[s], s)
                        _store_runs(acc_ref, runs, pv)
            return carry

        def softmax_pv(sub, bi=bi, first_branch=first_branch):
            soft = []
            for runs, krow, old, sc2 in sub:
                m_new, l_new, alpha, ps = [], [], [], []
                for s in range(2):
                    sc = sc2[s * SPAN:(s + 1) * SPAN, :]
                    halves = (sc[:, 0:SPAN], sc[:, SPAN:2 * SPAN])
                    mb = jnp.max(jnp.maximum(halves[0], halves[1]), axis=-1, keepdims=True)
                    if first_branch:
                        m = jnp.broadcast_to(mb, (SPAN, LANES))
                    else:
                        m = jnp.maximum(old[0][s], mb)
                        alpha.append(jnp.exp2(old[0][s] - m))
                    p = [jnp.exp2(h - m) for h in halves]
                    rs = jnp.sum(p[0] + p[1], axis=-1, keepdims=True)
                    ps.append(jnp.concatenate([p[0].astype(BF16), p[1].astype(BF16)], axis=1))
                    m_new.append(m)
                    l_new.append(jnp.broadcast_to(rs, (SPAN, LANES)) if first_branch
                                 else alpha[s] * old[1][s] + rs)
                soft.append((m_new, l_new, alpha, ps))
            out = []
            for (runs, krow, old, _), (m_new, l_new, alpha, ps) in zip(sub, soft):
                pv2 = _dot(jnp.concatenate(ps, axis=0), vph_ref[bi, pl.ds(krow, 2 * SPAN), :])
                pv = jnp.where(first_head, pv2[0:SPAN, :], pv2[SPAN:2 * SPAN, :])
                if not first_branch:
                    pv = jnp.where(first_head, alpha[0], alpha[1]) * old[2] + pv
                out.append((runs, m_new, l_new, pv))
            return out

        lax.fori_loop(0, nblk // ATT_UNROLL, group, 0)

    for r16 in range(d16):
        tmp_ref[pl.ds((r16 % d4) * l4 + r16 // d4, l16, stride=d4), :] = acc_ref[r16 * l16:(r16 + 1) * l16, :]
    for r4 in range(d4):
        o_ref[pl.ds(r4, l4, stride=d4), :] = tmp_ref[r4 * l4:(r4 + 1) * l4, :]

    for bi, d in enumerate(DILATIONS):
        ln = ATT_T // d
        ph = ln + SPAN
        for r in range(d):
            kph_ref[bi, r * ph:r * ph + SPAN, :] = kph_ref[bi, (r + 1) * ph - SPAN:(r + 1) * ph, :]
            vph_ref[bi, r * ph:r * ph + SPAN, :] = vph_ref[bi, (r + 1) * ph - SPAN:(r + 1) * ph, :]


def _attn_prompt(qkv, bias, batch):
    m = qkv.shape[0]
    nt = m // batch // ATT_T
    npair = N_HEADS_A // 2
    nd = len(DILATIONS)
    blk = lambda off: pl.BlockSpec((ATT_T, LANES), lambda b, p, t: (b * nt + t, off + p))
    halo_rows = ATT_T + DILATIONS[-1] * SPAN
    return pl.pallas_call(
        _attn_kernel,
        grid=(batch, npair, nt),
        in_specs=[blk(0), blk(npair), blk(2 * npair),
                  pl.BlockSpec((1, nd, 2, 2 * SPAN, 2 * SPAN), lambda b, p, t: (p, 0, 0, 0, 0))],
        out_specs=blk(0),
        out_shape=jax.ShapeDtypeStruct((m, D_ATTN), F32),
        scratch_shapes=[pltpu.VMEM((ATT_T, LANES), F32),
                        pltpu.VMEM((ATT_T, LANES), F32),
                        pltpu.VMEM((nd, halo_rows, LANES), BF16),
                        pltpu.VMEM((nd, halo_rows, LANES), BF16),
                        pltpu.VMEM((ATT_T, LANES), F32),
                        pltpu.VMEM((2, ATT_T, LANES), F32),
                        pltpu.VMEM((2, ATT_T, LANES), F32)],
        compiler_params=pltpu.CompilerParams(
            dimension_semantics=("arbitrary", "arbitrary", "arbitrary"), vmem_limit_bytes=VMEM_LIMIT),
        name="attn_prompt",
    )(qkv, qkv, qkv, bias)


SAMPLE_ROWS = 4


def _sample_tables():
    pos = np.arange(MAX_WINDOW)[None, :]
    t = (np.arange(8) % SAMPLE_ROWS)[:, None]
    dist = MAX_WINDOW + t - pos
    mult = np.zeros(dist.shape, np.float32)
    for d in DILATIONS:
        mult += ((dist % d == 0) & (dist <= SPAN * d)).astype(np.float32)
    bucket = np.where(mult > 0, _t5_bucket_np(dist), -1).astype(np.int32)
    return bucket, np.concatenate([mult, mult], axis=0)


def _sample_bias_kernel(tab_ref, rb_ref, rbv_ref, bmain_ref, bnew_ref, *, buckets):
    pair = pl.program_id(0)
    tab = tab_ref[...]
    acc = [jnp.full(tab.shape, NEG, F32) for _ in range(2)]
    for bkt in buckets:
        hit = tab == bkt
        for s in range(2):
            acc[s] = jnp.where(hit, rb_ref[bkt, 2 * pair + s], acc[s])
    for s in range(2):
        bmain_ref[0, s * 8:(s + 1) * 8, :] = acc[s]
    t = lax.broadcasted_iota(jnp.int32, (8, 1), 0) % SAMPLE_ROWS
    for tp in range(SAMPLE_ROWS):
        b = jnp.full((8, LANES), NEG, F32)
        for dist in range(SAMPLE_ROWS):
            b = jnp.where(t - tp == dist, rbv_ref[dist:dist + 1, :], b)
        bnew_ref[tp] = b


def _sample_bias_tables(rel_bias):
    bucket, mult = _sample_tables()
    buckets = sorted(int(b) for b in np.unique(bucket) if b >= 0)
    npair = N_HEADS_A // 2
    rb_lanes = jnp.pad(rel_bias, ((0, 0), (0, LANES - N_HEADS_A)))
    bmain, bnew = pl.pallas_call(
        functools.partial(_sample_bias_kernel, buckets=buckets),
        grid=(npair,),
        in_specs=[_resident((8, MAX_WINDOW)), pl.BlockSpec(memory_space=pltpu.SMEM),
                  _resident((N_BUCKETS, LANES))],
        out_specs=[pl.BlockSpec((1, 16, MAX_WINDOW), lambda p: (p, 0, 0)),
                   pl.BlockSpec((SAMPLE_ROWS, 8, LANES), lambda p: (0, 0, 0))],
        out_shape=[jax.ShapeDtypeStruct((npair, 16, MAX_WINDOW), F32),
                   jax.ShapeDtypeStruct((SAMPLE_ROWS, 8, LANES), F32)],
        compiler_params=pltpu.CompilerParams(dimension_semantics=("arbitrary",)),
        name="sample_bias_tables",
    )(jnp.asarray(bucket), rel_bias, rb_lanes)
    return bmain, bnew, jnp.asarray(mult)


def _attn_sample_body(qkv_ref, kt_ref, vt_ref, bmain_ref, mult_ref, bnew_ref, e_ref, om_ref):
    rows = SAMPLE_ROWS
    npair = N_HEADS_A // 2
    second = pl.program_id(0) % 2 == 1
    lower = lax.broadcasted_iota(jnp.int32, (2 * rows, 1), 0) < rows
    mine = jnp.logical_xor(lower, second)
    blk = qkv_ref[...]
    q8_all = jnp.where(mine, blk, pltpu.roll(blk, rows, 0))
    q8 = q8_all[:, 0:D_ATTN] * ATTN_SCALE
    kn = q8_all[:, D_ATTN:2 * D_ATTN]
    vn = q8_all[:, 2 * D_ATTN:3 * D_ATTN]
    lane = lax.broadcasted_iota(jnp.int32, (1, LANES), 1)
    first_head = lane < HEAD_DIM
    e = e_ref[...]

    s_new = []
    m_tot = jnp.full((8, LANES), NEG, F32)
    for tp in range(rows):
        hi, lo, lo2 = _split3(q8 * kn[tp:tp + 1, :])
        s = _dot_nt(hi, e) + _dot_nt(lo, e) + _dot_nt(lo2, e) + bnew_ref[tp]
        s_new.append(s)
        m_tot = jnp.maximum(m_tot, s)

    mult = mult_ref[...]
    l_tot = jnp.zeros((8, LANES), F32)

    def window_logits(p):
        qp = q8[:, p * LANES:(p + 1) * LANES]
        qbd = jnp.concatenate([jnp.where(first_head, qp, 0.0), jnp.where(first_head, 0.0, qp)],
                              axis=0).astype(BF16)
        return _dot(qbd, kt_ref[0, p].astype(BF16)) + bmain_ref[p]

    s_next = window_logits(0)
    for p in range(npair):
        s16 = s_next
        if p + 1 < npair:
            s_next = window_logits(p + 1)
        m_new =jnp.concatenate([m_tot[:, 2 * p:2 * p + 1], m_tot[:, 2 * p + 1:2 * p + 2]], axis=0)
        m = jnp.maximum(jnp.max(s16, axis=-1, keepdims=True), m_new)
        pr = mult * jnp.exp(s16 - m)
        ls = jnp.sum(pr, axis=-1, keepdims=True)
        o16 = _dot_nt(pr.astype(BF16), vt_ref[0, p].astype(BF16))
        om_ref[:, p * LANES:(p + 1) * LANES] = jnp.where(first_head, o16[0:8], o16[8:16])
        m_tot = jnp.where(lane == 2 * p, m[0:8], jnp.where(lane == 2 * p + 1, m[8:16], m_tot))
        l_tot = jnp.where(lane == 2 * p, ls[0:8], jnp.where(lane == 2 * p + 1, ls[8:16], l_tot))
        yield

    t = lax.broadcasted_iota(jnp.int32, (8, 1), 0) % rows
    o = om_ref[...]
    for tp in range(rows):
        c = jnp.where(t == tp, float(len(DILATIONS)), jnp.where(t > tp, 1.0, 0.0))
        pn = c * jnp.exp(s_new[tp] - m_tot)
        l_tot = l_tot + pn
        o = o + _expand_heads(pn, e_ref) * vn[tp:tp + 1, :]
    o = o / _expand_heads(l_tot, e_ref)
    return o, second, lower


def _interleave(*gens):
    results = [None] * len(gens)
    live = list(range(len(gens)))
    while live:
        for k in list(live):
            try:
                next(gens[k])
            except StopIteration as stop:
                results[k] = stop.value
                live.remove(k)
    return results


def _sample_attn_prompt_ssd_kernel(qkv_ref, kt_ref, vt_ref, bmain_ref, mult_ref, bnew_ref, e_ref,
                                   z_ref, xbc_ref, dt_ref, conv0_ref, h0_ref, convw_ref, convb_ref, dtb_ref,
                                   alog_ref, dskip_ref, gnorm_ref, o_ref, y_ref, hout_ref,
                                   om_ref, xpad_ref, state_ref, *, chunks):
    c = pl.program_id(0) % chunks

    @pl.when(c == 0)
    def _():
        state_ref[...] = h0_ref[0]
        xpad_ref[8 - (CONV_WIDTH - 1):8, :] = conv0_ref[0]

    attn = _attn_sample_body(qkv_ref, kt_ref, vt_ref, bmain_ref, mult_ref, bnew_ref, e_ref, om_ref)
    scan = _ssd_chunk(z_ref.at[0], xbc_ref.at[0], dt_ref.at[0], y_ref.at[0], xpad_ref, state_ref,
                      convw_ref, convb_ref, dtb_ref, alog_ref, dskip_ref, gnorm_ref, e_ref)
    o, second, lower = _interleave(attn, scan)[0]

    @pl.when(jnp.logical_not(second))
    def _():
        o_ref[...] = o

    @pl.when(second)
    def _():
        o_ref[...] = jnp.where(lower, o_ref[...], o)

    @pl.when(c == chunks - 1)
    def _():
        hout_ref[0] = state_ref[...]


def _sample_attn_prompt_ssd(qkv, kt, vt, bmain, mult, bnew, expand, z, xbc, dt_raw, conv0, h0, ssd_params):
    rows = SAMPLE_ROWS
    batch = qkv.shape[0] // rows
    nseq, seq = z.shape[0], z.shape[1]
    chunks = seq // SSD_CHUNK
    assert batch == nseq * chunks, "one prompt chunk per sample sequence"
    npair = N_HEADS_A // 2
    cache = pl.BlockSpec((1, npair, LANES, MAX_WINDOW), lambda i: (i, 0, 0, 0))
    chunk = lambda w: pl.BlockSpec((1, SSD_CHUNK, w), lambda i: (i // chunks, i % chunks, 0))
    per_seq = lambda r, w: pl.BlockSpec((1, r, w), lambda i: (i // chunks, 0, 0))
    return pl.pallas_call(
        functools.partial(_sample_attn_prompt_ssd_kernel, chunks=chunks),
        grid=(batch,),
        in_specs=[pl.BlockSpec((2 * rows, 3 * D_ATTN), lambda i: (i // 2, 0)), cache, cache,
                  _resident((npair, 16, MAX_WINDOW)), _resident((16, MAX_WINDOW)),
                  _resident((rows, 8, LANES)), _resident((LANES, D_ATTN)),
                  chunk(D_SSM), chunk(CONV_DIM), chunk(LANES), per_seq(CONV_WIDTH - 1, CONV_DIM),
                  per_seq(D_SSM, D_STATE)] + _ssd_param_specs()[:-1],
        out_specs=[pl.BlockSpec((2 * rows, D_ATTN), lambda i: (i // 2, 0)), chunk(D_SSM),
                   per_seq(D_SSM, D_STATE)],
        out_shape=[jax.ShapeDtypeStruct((batch * rows, D_ATTN), F32),
                   jax.ShapeDtypeStruct((nseq, seq, D_SSM), F32),
                   jax.ShapeDtypeStruct((nseq, D_SSM, D_STATE), F32)],
        scratch_shapes=[pltpu.VMEM((8, D_ATTN), F32), pltpu.VMEM((8 + SSD_CHUNK, CONV_DIM), F32),
                        pltpu.VMEM((D_SSM, D_STATE), F32)],
        compiler_params=pltpu.CompilerParams(
            dimension_semantics=("arbitrary",), vmem_limit_bytes=VMEM_LIMIT),
        name="sample_attn_prompt_ssd",
    )(qkv, kt, vt, bmain, mult, bnew, expand, z, xbc, dt_raw, conv0, h0, *ssd_params[:-1])


def _head_expand_matrix():
    e = np.zeros((LANES, D_SSM), np.float32)
    for h in range(N_HEADS_S):
        e[h, h * SSM_HEAD_DIM:(h + 1) * SSM_HEAD_DIM] = 1.0
    return jnp.asarray(e, BF16)


def _prep_params(norm_mix, w_in, q_norm, k_norm, conv_w, conv_b, dt_bias, a_log, d_skip, ssm_norm,
                 w_out, norm_ffn, w_up, w_down):
    n_main = 3 * D_ATTN + D_SSM + CONV_DIM
    pad_heads = lambda v: jnp.pad(v.reshape(1, N_HEADS_S), ((0, 0), (0, LANES - N_HEADS_S)))
    return dict(
        gmix=norm_mix.reshape(1, D_MODEL),
        w_main=w_in.astype(BF16),
        w_dt=jnp.pad(w_in[:, n_main:], ((0, 0), (0, LANES - N_HEADS_S))).astype(BF16),
        gq=jnp.tile(q_norm, N_HEADS_A).reshape(1, D_ATTN),
        gk=jnp.tile(k_norm, N_HEADS_A).reshape(1, D_ATTN),
        conv_w=conv_w,
        conv_b=conv_b.reshape(1, CONV_DIM),
        dt_bias=pad_heads(dt_bias),
        a_log=pad_heads(a_log),
        d_skip_e=jnp.repeat(d_skip, SSM_HEAD_DIM).reshape(1, D_SSM),
        gnorm=ssm_norm.reshape(1, D_SSM),
        expand=_head_expand_matrix(),
        w_out=w_out.astype(BF16),
        gffn=norm_ffn.reshape(1, D_MODEL),
        w_up=w_up.astype(BF16),
        w_down=w_down.astype(BF16),
    )


def _ssd_param_list(p):
    return (p["conv_w"], p["conv_b"], p["dt_bias"], p["a_log"], p["d_skip_e"], p["gnorm"], p["expand"])


def kernel(x_prompt, x_sample, cache_attn_k, cache_attn_v, state_conv, state_ssm, norm_mix, w_in, q_norm,
           k_norm, rel_bias, conv_w, conv_b, dt_bias, a_log, d_skip, ssm_norm, w_out, norm_ffn, w_up, w_down):
    depth = w_in.shape[0]
    assert depth == 1, "single-layer decoder"
    l = 0
    p = _prep_params(norm_mix[l], w_in[l], q_norm[l], k_norm[l], conv_w[l], conv_b[l], dt_bias[l], a_log[l],
                     d_skip[l], ssm_norm[l], w_out[l], norm_ffn[l], w_up[l], w_down[l])
    ssd_params = _ssd_param_list(p)
    tm = 512

    bp, sp = x_prompt.shape[:2]
    keep = min(MAX_WINDOW, sp)
    xp = x_prompt.reshape(bp * sp, D_MODEL)
    qkv, z, xbc, dt_raw, ktail, vtail = _inproj(xp, p["gmix"], p["w_main"], p["w_dt"], p["gq"], p["gk"], tm,
                                                tail=(sp, keep))
    attn = _attn_prompt(qkv, _bias_tables(rel_bias), bp)
    to_rows = lambda a: jnp.transpose(a.reshape(bp, N_HEADS_A, HEAD_DIM, keep), (0, 3, 1, 2)).reshape(
        depth, bp, keep, N_HEADS_A, HEAD_DIM)
    new_k_prompt = to_rows(ktail)
    new_v_prompt = to_rows(vtail)
    new_conv_prompt = xbc.reshape(bp, sp, CONV_DIM)[:, sp - (CONV_WIDTH - 1):].reshape(
        depth, bp, CONV_WIDTH - 1, CONV_DIM)

    bs, ts = x_sample.shape[:2]
    assert cache_attn_k.shape[2] == MAX_WINDOW and ts == SAMPLE_ROWS
    xs = x_sample.reshape(bs * ts, D_MODEL)
    qkv_s, z_s, xbc_s, dt_s = _inproj(xs, p["gmix"], p["w_main"], p["w_dt"], p["gq"], p["gk"], tm)
    npair = N_HEADS_A // 2
    kt = jnp.transpose(cache_attn_k[l], (0, 2, 3, 1)).reshape(bs, npair, LANES, MAX_WINDOW)
    vt = jnp.transpose(cache_attn_v[l], (0, 2, 3, 1)).reshape(bs, npair, LANES, MAX_WINDOW)
    bmain, bnew, mult = _sample_bias_tables(rel_bias)
    conv0 = jnp.zeros((bp, CONV_WIDTH - 1, CONV_DIM), F32)
    h0 = jnp.zeros((bp, D_SSM, D_STATE), F32)
    attn_s, ssm_o, hst = _sample_attn_prompt_ssd(
        qkv_s, kt, vt, bmain, mult, bnew, p["expand"], z.reshape(bp, sp, D_SSM), xbc.reshape(bp, sp, CONV_DIM),
        dt_raw.reshape(bp, sp, LANES), conv0, h0, ssd_params)
    y_prompt = _outffn(xp, attn, ssm_o.reshape(bp * sp, D_SSM), p["w_out"], p["gffn"], p["w_up"], p["w_down"], tm)
    new_ssm_prompt = hst.reshape(depth, bp, N_HEADS_S, SSM_HEAD_DIM, D_STATE)
    ssm_os, hsts = _ssd_sample(z_s, xbc_s, dt_s, jnp.transpose(state_conv[l], (1, 0, 2)),
                               state_ssm[l].reshape(bs, D_SSM, D_STATE), ssd_params, ts)
    y_sample = _outffn(xs, attn_s, ssm_os, p["w_out"], p["gffn"], p["w_up"], p["w_down"], tm)
    new_k_sample = qkv_s[:, D_ATTN:2 * D_ATTN].reshape(depth, bs, ts, N_HEADS_A, HEAD_DIM)
    new_v_sample = qkv_s[:, 2 * D_ATTN:].reshape(depth, bs, ts, N_HEADS_A, HEAD_DIM)
    new_conv_sample = xbc_s.reshape(bs, ts, CONV_DIM)[:, ts - (CONV_WIDTH - 1):].reshape(
        depth, bs, CONV_WIDTH - 1, CONV_DIM)
    new_ssm_sample = hsts.reshape(depth, bs, N_HEADS_S, SSM_HEAD_DIM, D_STATE)

    return (y_prompt.reshape(bp, sp, D_MODEL), y_sample.reshape(bs, ts, D_MODEL),
            new_k_prompt, new_v_prompt, new_conv_prompt, new_ssm_prompt,
            new_k_sample, new_v_sample, new_conv_sample, new_ssm_sample)
```

```python
import functools
import math

import jax
import jax.numpy as jnp
import numpy as np
from jax import lax
from jax.experimental import pallas as pl
from jax.experimental.pallas import tpu as pltpu

F32 = jnp.float32
BF16 = jnp.bfloat16

D_MODEL = 1024
HEAD_DIM = 64
N_HEADS_A = 16
D_ATTN = 1024
SPAN = 128
DILATIONS = (1, 4, 16)
MAX_WINDOW = 2048
N_BUCKETS = 32
MAX_EXACT = 16
SSM_HEAD_DIM = 64
N_HEADS_S = 16
D_SSM = 1024
SSM_GROUPS = 2
D_STATE = 128
CONV_WIDTH = 4
CONV_DIM = D_SSM + 2 * SSM_GROUPS * D_STATE
SSD_CHUNK = 128
D_FF = 4096
EPS = 1e-6
ATTN_SCALE = HEAD_DIM ** -0.5
LOG2E = 1.4426950408889634

LANES = 128
NEG = -1e30
VMEM_LIMIT = 56 * 1024 * 1024


def _resident(shape):
    nd = len(shape)
    return pl.BlockSpec(shape, lambda *_: (0,) * nd, pipeline_mode=pl.Buffered(1))


def _split3(x):
    hi = x.astype(BF16)
    r1 = x - hi.astype(F32)
    lo = r1.astype(BF16)
    lo2 = (r1 - lo.astype(F32)).astype(BF16)
    return hi, lo, lo2


def _dot(a, b):
    return jnp.dot(a, b, preferred_element_type=F32)


def _dot_nt(a, b):
    return lax.dot_general(a, b, (((1,), (1,)), ((), ())), preferred_element_type=F32)


def _sigmoid(x):
    return 0.5 * jnp.tanh(0.5 * x) + 0.5


def _softplus(x):
    return jnp.maximum(x, 0.0) + jnp.log1p(jnp.exp(-jnp.abs(x)))


IN_TN = 512


def _inproj_kernel(x_ref, gmix_ref, w_ref, wdt_ref, gq_ref, gk_ref,
                   qkv_ref, z_ref, xbc_ref, dt_ref, *tail_refs, with_tail):
    x = x_ref[...]
    ms = jnp.mean(x * x, axis=-1, keepdims=True)
    xn = ((x * lax.rsqrt(ms + EPS)) * gmix_ref[...]).astype(BF16)
    first_head = lax.broadcasted_iota(jnp.int32, (1, LANES), 1) < HEAD_DIM

    n_main = 3 * D_ATTN + D_SSM + CONV_DIM
    for j in range(n_main // IN_TN):
        c0 = j * IN_TN
        t = _dot(xn, w_ref[:, c0:c0 + IN_TN])
        if c0 < 2 * D_ATTN:
            g_ref = gq_ref if c0 < D_ATTN else gk_ref
            g0 = c0 % D_ATTN
            for c in range(IN_TN // LANES):
                tc = t[:, c * LANES:(c + 1) * LANES]
                s = tc * tc
                sa = jnp.sum(jnp.where(first_head, s, 0.0), axis=-1, keepdims=True)
                sb = jnp.sum(jnp.where(first_head, 0.0, s), axis=-1, keepdims=True)
                inv = lax.rsqrt(jnp.where(first_head, sa, sb) * (1.0 / HEAD_DIM) + EPS)
                out = (tc * inv) * g_ref[:, g0 + c * LANES:g0 + (c + 1) * LANES]
                qkv_ref[:, c0 + c * LANES:c0 + (c + 1) * LANES] = out
                if with_tail and c0 >= D_ATTN:
                    tail_refs[0][0, g0 + c * LANES:g0 + (c + 1) * LANES, :] = out.T
        elif c0 < 3 * D_ATTN:
            qkv_ref[:, c0:c0 + IN_TN] = t
            if with_tail:
                tail_refs[1][0, c0 - 2 * D_ATTN:c0 - 2 * D_ATTN + IN_TN, :] = t.T
        elif c0 < 3 * D_ATTN + D_SSM:
            z_ref[:, c0 - 3 * D_ATTN:c0 - 3 * D_ATTN + IN_TN] = t
        else:
            o0 = c0 - 3 * D_ATTN - D_SSM
            xbc_ref[:, o0:o0 + IN_TN] = t
    dt_ref[...] = _dot(xn, wdt_ref[...])


def _inproj(x2d, gmix, w_main, w_dt, gq, gk, tm, tail=None):
    m = x2d.shape[0]
    row = lambda w: pl.BlockSpec((tm, w), lambda i: (i, 0))
    out_specs = [row(3 * D_ATTN), row(D_SSM), row(CONV_DIM), row(LANES)]
    out_shape = [jax.ShapeDtypeStruct((m, 3 * D_ATTN), F32), jax.ShapeDtypeStruct((m, D_SSM), F32),
                 jax.ShapeDtypeStruct((m, CONV_DIM), F32), jax.ShapeDtypeStruct((m, LANES), F32)]
    if tail is not None:
        per_seq, kept = tail[0] // tm, tail[1] // tm
        kept_block = lambda i: (i // per_seq, 0, jnp.maximum(i % per_seq - (per_seq - kept), 0))
        out_specs += [pl.BlockSpec((1, D_ATTN, tm), kept_block)] * 2
        out_shape += [jax.ShapeDtypeStruct((m // tail[0], D_ATTN, tail[1]), F32)] * 2
    return pl.pallas_call(
        functools.partial(_inproj_kernel, with_tail=tail is not None),
        grid=(m // tm,),
        in_specs=[row(D_MODEL), _resident((1, D_MODEL)), _resident(w_main.shape),
                  _resident((D_MODEL, LANES)), _resident((1, D_ATTN)), _resident((1, D_ATTN))],
        out_specs=out_specs,
        out_shape=out_shape,
        compiler_params=pltpu.CompilerParams(
            dimension_semantics=("arbitrary",), vmem_limit_bytes=VMEM_LIMIT),
        name="inproj",
    )(x2d, gmix, w_main, w_dt, gq, gk)


FFN_TF = 1024


def _outffn_kernel(x_ref, attn_ref, ssm_ref, wout_ref, gffn_ref, wup_ref, wdown_ref, y_ref):
    a = attn_ref[...].astype(BF16)
    s = ssm_ref[...].astype(BF16)
    h = x_ref[...] + _dot(a, wout_ref[0:D_ATTN, :]) + _dot(s, wout_ref[D_ATTN:D_ATTN + D_SSM, :])
    ms = jnp.mean(h * h, axis=-1, keepdims=True)
    hn = ((h * lax.rsqrt(ms + EPS)) * gffn_ref[...]).astype(BF16)
    acc = None
    for f in range(D_FF // FFN_TF):
        u = _dot(hn, wup_ref[:, f * FFN_TF:(f + 1) * FFN_TF])
        u = jnp.maximum(u, 0.0)
        u = (u * u).astype(BF16)
        d = _dot(u, wdown_ref[f * FFN_TF:(f + 1) * FFN_TF, :])
        acc = d if acc is None else acc + d
    y_ref[...] = h + acc


def _outffn(x2d, attn, ssm_o, w_out, gffn, w_up, w_down, tm):
    m = x2d.shape[0]
    row = pl.BlockSpec((tm, D_MODEL), lambda i: (i, 0))
    return pl.pallas_call(
        _outffn_kernel,
        grid=(m // tm,),
        in_specs=[row, row, row, _resident((D_ATTN + D_SSM, D_MODEL)), _resident((1, D_MODEL)),
                  _resident((D_MODEL, D_FF)), _resident((D_FF, D_MODEL))],
        out_specs=row,
        out_shape=jax.ShapeDtypeStruct((m, D_MODEL), F32),
        compiler_params=pltpu.CompilerParams(
            dimension_semantics=("arbitrary",), vmem_limit_bytes=VMEM_LIMIT),
        name="outffn",
    )(x2d, attn, ssm_o, w_out, gffn, w_up, w_down)


def _expand_heads(v, e_ref):
    hi, lo, lo2 = _split3(v)
    e = e_ref[...]
    return _dot(hi, e) + _dot(lo, e) + _dot(lo2, e)


def _ssd_chunk(z_ref, xbc_ref, dt_ref, y_ref, xpad_ref, state_ref,
               convw_ref, convb_ref, dtb_ref, alog_ref, dskip_ref, gnorm_ref, e_ref):
    q = SSD_CHUNK
    pad = 8
    tail = CONV_WIDTH - 1

    xbc = xbc_ref[...]
    xpad_ref[pad:pad + q, :] = xbc
    conv = convb_ref[...] + convw_ref[tail:tail + 1, :] * xbc
    for i in range(tail):
        conv = conv + convw_ref[i:i + 1, :] * xpad_ref[pad - tail + i:pad - tail + i + q, :]
    xpad_ref[pad - tail:pad, :] = xbc[q - tail:q, :]
    xc = conv * _sigmoid(conv)
    xs = xc[:, 0:D_SSM]

    lane = lax.broadcasted_iota(jnp.int32, (1, LANES), 1)
    dt = jnp.where(lane < N_HEADS_S, _softplus(dt_ref[...] + dtb_ref[...]), 0.0)
    da = dt * (-jnp.exp(alog_ref[...]))

    ii = lax.broadcasted_iota(jnp.int32, (q, q), 0)
    jj = lax.broadcasted_iota(jnp.int32, (q, q), 1)
    causal = ii >= jj
    tri = jnp.where(causal, 1.0, 0.0).astype(BF16)
    hi, lo, lo2 = _split3(da)
    cum = _dot(tri, hi) + _dot(tri, lo) + _dot(tri, lo2)
    cum_t = cum.T

    cum_e = _expand_heads(cum, e_ref)
    last_e = cum_e[q - 1:q, :]
    xdt = xs * _expand_heads(dt, e_ref)
    xw_t = (xdt * jnp.exp(last_e - cum_e)).T.astype(BF16)
    state_decay = jnp.exp(jnp.broadcast_to(last_e, (q, D_SSM)).T)
    off_scale = jnp.exp(cum_e)

    first_head = lane < SSM_HEAD_DIM
    heads_per_group = N_HEADS_S // SSM_GROUPS
    gw = heads_per_group * SSM_HEAD_DIM
    state = state_ref[...]
    state_bf = state.astype(BF16)
    yield
    for g in range(SSM_GROUPS):
        bg = xc[:, D_SSM + g * D_STATE:D_SSM + (g + 1) * D_STATE].astype(BF16)
        cg = xc[:, D_SSM + (SSM_GROUPS + g) * D_STATE:D_SSM + (SSM_GROUPS + g + 1) * D_STATE].astype(BF16)
        cb = _dot_nt(cg, bg)
        y_off = _dot_nt(cg, state_bf[g * gw:(g + 1) * gw, :]) * off_scale[:, g * gw:(g + 1) * gw]
        y_ref[:, g * gw:(g + 1) * gw] = y_off
        for hp in range(heads_per_group // 2):
            pair = g * (heads_per_group // 2) + hp
            xp = xdt[:, pair * LANES:(pair + 1) * LANES]
            yp = None
            for s in range(2):
                h = 2 * pair + s
                seg = cum[:, h:h + 1] - cum_t[h:h + 1, :]
                dec = jnp.exp(jnp.where(causal, seg, NEG))
                m = (cb * dec).astype(BF16)
                xm = jnp.where(first_head if s == 0 else jnp.logical_not(first_head), xp, 0.0).astype(BF16)
                d = _dot(m, xm)
                yp = d if yp is None else yp + d
            y_ref[:, pair * LANES:(pair + 1) * LANES] += yp
            yield
        state_ref[g * gw:(g + 1) * gw, :] = (
            state_decay[g * gw:(g + 1) * gw, :] * state[g * gw:(g + 1) * gw, :]
            + _dot(xw_t[g * gw:(g + 1) * gw, :], bg))

    z = z_ref[...]
    y = (y_ref[...] + xs * dskip_ref[...]) * (z * _sigmoid(z))
    for g in range(SSM_GROUPS):
        yg = y[:, g * gw:(g + 1) * gw]
        ms = jnp.mean(yg * yg, axis=-1, keepdims=True)
        y_ref[:, g * gw:(g + 1) * gw] = (yg * lax.rsqrt(ms + EPS)) * gnorm_ref[:, g * gw:(g + 1) * gw]


def _ssd_sample_kernel(z_ref, xbc_ref, dt_ref, conv0_ref, h0_ref, convw_ref, convb_ref, dtb_ref,
                       alog_ref, dskip_ref, gnorm_ref, e_ref, y_ref, hout_ref, xs_ref, z8_ref, dt8_ref,
                       *, rows):
    for b in range(h0_ref.shape[0]):
        tok = pl.ds(b * rows, rows)
        _ssd_sample_one(z_ref.at[tok], xbc_ref.at[tok], dt_ref.at[tok], conv0_ref.at[:, b], h0_ref.at[b],
                        convw_ref, convb_ref, dtb_ref, alog_ref, dskip_ref, gnorm_ref, e_ref, y_ref.at[tok],
                        hout_ref.at[b], xs_ref.at[b], z8_ref.at[b], dt8_ref.at[b], rows)


def _ssd_sample_one(z_ref, xbc_ref, dt_ref, conv0_ref, h0_ref, convw_ref, convb_ref, dtb_ref,
                    alog_ref, dskip_ref, gnorm_ref, e_ref, y_ref, hout_ref, xs_ref, z8_ref, dt8_ref, rows):
    tail = CONV_WIDTH - 1
    r8 = 8
    row = lax.broadcasted_iota(jnp.int32, (r8, 1), 0)
    lane = lax.broadcasted_iota(jnp.int32, (1, LANES), 1)

    xs_ref[r8:2 * r8, :] = jnp.zeros((r8, CONV_DIM), F32)
    xs_ref[r8:r8 + rows, :] = xbc_ref[...]
    xs_ref[r8 - tail:r8, :] = conv0_ref[...]
    z8_ref[...] = jnp.zeros((r8, D_SSM), F32)
    z8_ref[0:rows, :] = z_ref[...]
    dt8_ref[...] = jnp.zeros((r8, LANES), F32)
    dt8_ref[0:rows, :] = dt_ref[...]

    conv = convb_ref[...] + convw_ref[tail:tail + 1, :] * xs_ref[r8:2 * r8, :]
    for i in range(tail):
        conv = conv + convw_ref[i:i + 1, :] * xs_ref[r8 - tail + i:2 * r8 - tail + i, :]
    xc = conv * _sigmoid(conv)
    xs = xc[:, 0:D_SSM]

    live = jnp.logical_and(lane < N_HEADS_S, row < rows)
    dt = jnp.where(live, _softplus(dt8_ref[...] + dtb_ref[...]), 0.0)
    da = dt * (-jnp.exp(alog_ref[...]))
    cum = da
    for k in range(1, rows):
        cum = cum + pltpu.roll(da, k, 0)
    last = rows - 1

    heads_per_group = N_HEADS_S // SSM_GROUPS
    gw = heads_per_group * SSM_HEAD_DIM
    bg = [xc[:, D_SSM + g * D_STATE:D_SSM + (g + 1) * D_STATE] for g in range(SSM_GROUPS)]
    cg = [xc[:, D_SSM + (SSM_GROUPS + g) * D_STATE:D_SSM + (SSM_GROUPS + g + 1) * D_STATE]
          for g in range(SSM_GROUPS)]
    group0 = lane < heads_per_group
    w = []
    for j in range(rows):
        cb = [jnp.sum(cg[g] * bg[g][j:j + 1, :], axis=-1, keepdims=True) for g in range(SSM_GROUPS)]
        dec = jnp.exp(jnp.where(row >= j, cum - cum[j:j + 1, :], NEG))
        w.append(jnp.where(group0, cb[0], cb[1]) * dec)

    stack = jnp.concatenate([cum, dt] + w, axis=0)
    hi, lo, lo2 = _split3(stack)
    ex = _dot(jnp.concatenate([hi, lo, lo2], axis=0), e_ref[...])
    ns = stack.shape[0]
    ex = ex[0:ns] + ex[ns:2 * ns] + ex[2 * ns:3 * ns]
    cum_e = ex[0:r8]
    xdt = xs * ex[r8:2 * r8]
    y = None
    for j in range(rows):
        term = ex[(2 + j) * r8:(3 + j) * r8] * xdt[j:j + 1, :]
        y = term if y is None else y + term
    last_e = cum_e[last:last + 1, :]

    state = h0_ref[...]
    state_bf = state.astype(BF16)
    zeros8 = jnp.zeros((r8, D_STATE), F32)
    y_off = [_dot_nt(jnp.concatenate([cg[g], zeros8], axis=0).astype(BF16),
                     state_bf[g * gw:(g + 1) * gw, :])[0:r8, :] for g in range(SSM_GROUPS)]
    y = y + jnp.concatenate(y_off, axis=1) * jnp.exp(cum_e)

    z = z8_ref[...]
    y = (y + xs * dskip_ref[...]) * (z * _sigmoid(z))
    outs = []
    for g in range(SSM_GROUPS):
        yg = y[:, g * gw:(g + 1) * gw]
        ms = jnp.mean(yg * yg, axis=-1, keepdims=True)
        outs.append((yg * lax.rsqrt(ms + EPS)) * gnorm_ref[:, g * gw:(g + 1) * gw])
    y_ref[...] = jnp.concatenate(outs, axis=1)[0:rows, :]

    pad = jnp.zeros((SSD_CHUNK - r8, D_SSM), F32)
    xw_t = jnp.concatenate([xdt * jnp.exp(last_e - cum_e), pad], axis=0).T.astype(BF16)
    state_decay = jnp.exp(jnp.broadcast_to(last_e, (SSD_CHUNK, D_SSM)).T)
    bpad = jnp.zeros((SSD_CHUNK - r8, D_STATE), F32)
    for g in range(SSM_GROUPS):
        b128 = jnp.concatenate([bg[g], bpad], axis=0).astype(BF16)
        hout_ref[g * gw:(g + 1) * gw, :] = (
            state_decay[g * gw:(g + 1) * gw, :] * state[g * gw:(g + 1) * gw, :]
            + _dot(xw_t[g * gw:(g + 1) * gw, :], b128))


def _ssd_param_specs():
    return [_resident((CONV_WIDTH, CONV_DIM)), _resident((1, CONV_DIM)), _resident((1, LANES)),
            _resident((1, LANES)), _resident((1, D_SSM)), _resident((1, D_SSM)),
            _resident((LANES, D_SSM))]


SSD_SAMPLE_SEQS = 8


def _ssd_sample(z, xbc, dt_raw, conv0, h0, ssd_params, rows):
    batch = h0.shape[0]
    nb = SSD_SAMPLE_SEQS
    tok = lambda w: pl.BlockSpec((nb * rows, w), lambda i: (i, 0))
    state = pl.BlockSpec((nb, D_SSM, D_STATE), lambda i: (i, 0, 0))
    return pl.pallas_call(
        functools.partial(_ssd_sample_kernel, rows=rows),
        grid=(batch // nb,),
        in_specs=[tok(D_SSM), tok(CONV_DIM), tok(LANES),
                  pl.BlockSpec((CONV_WIDTH - 1, nb, CONV_DIM), lambda i: (0, i, 0)), state] + _ssd_param_specs(),
        out_specs=[tok(D_SSM), state],
        out_shape=[jax.ShapeDtypeStruct((batch * rows, D_SSM), F32),
                   jax.ShapeDtypeStruct((batch, D_SSM, D_STATE), F32)],
        scratch_shapes=[pltpu.VMEM((nb, 16, CONV_DIM), F32), pltpu.VMEM((nb, 8, D_SSM), F32),
                        pltpu.VMEM((nb, 8, LANES), F32)],
        compiler_params=pltpu.CompilerParams(
            dimension_semantics=("arbitrary",), vmem_limit_bytes=VMEM_LIMIT),
        name="ssd_sample",
    )(z, xbc, dt_raw, conv0, h0, *ssd_params)


def _t5_bucket_np(dist):
    dist = np.asarray(dist, np.int64)
    df = np.maximum(dist, 1).astype(np.float32)
    large = MAX_EXACT + (np.log(df / np.float32(MAX_EXACT)) / np.float32(math.log(MAX_WINDOW / MAX_EXACT))
                         * np.float32(N_BUCKETS - MAX_EXACT)).astype(np.int32)
    return np.where(dist < MAX_EXACT, dist, np.minimum(large, N_BUCKETS - 1)).astype(np.int32)


def _band_bucket_tables():
    ki = np.arange(SPAN)[None, :]
    out = np.empty((len(DILATIONS), 2, SPAN, SPAN), np.int32)
    for bi, d in enumerate(DILATIONS):
        nrun = DILATIONS[-1] // d
        row = np.arange(SPAN)
        qi = (nrun * (row % (SPAN // nrun)) + row // (SPAN // nrun))[:, None]
        rel_prev = qi + SPAN - ki
        rel_cur = qi - ki
        out[bi, 0] = np.where(rel_prev <= SPAN, _t5_bucket_np(np.clip(rel_prev, 0, SPAN) * d), -1)
        out[bi, 1] = np.where(rel_cur >= 0, _t5_bucket_np(np.clip(rel_cur, 0, SPAN) * d), -1)
    return out


def _bias_table_kernel(tab_ref, rb_ref, out_ref, *, buckets):
    pair = pl.program_id(0)
    for bi in range(len(DILATIONS)):
        for half in range(2):
            tab = tab_ref[bi, half]
            acc = [jnp.full((SPAN, SPAN), NEG, F32) for _ in range(2)]
            for bkt in buckets[bi][half]:
                hit = tab == bkt
                for s in range(2):
                    acc[s] = jnp.where(hit, rb_ref[bkt, 2 * pair + s] * LOG2E, acc[s])
            for s in range(2):
                rows = slice(s * SPAN, (s + 1) * SPAN)
                if half == 0:
                    out_ref[0, bi, 0, rows, 0:SPAN] = jnp.full((SPAN, SPAN), NEG, F32)
                    out_ref[0, bi, 1, rows, 0:SPAN] = acc[s]
                else:
                    out_ref[0, bi, 0, rows, SPAN:2 * SPAN] = acc[s]
                    out_ref[0, bi, 1, rows, SPAN:2 * SPAN] = acc[s]


def _bias_tables(rel_bias):
    tabs = _band_bucket_tables()
    buckets = [[sorted(int(b) for b in np.unique(tabs[bi, half]) if b >= 0) for half in range(2)]
               for bi in range(len(DILATIONS))]
    nd = len(DILATIONS)
    return pl.pallas_call(
        functools.partial(_bias_table_kernel, buckets=buckets),
        grid=(N_HEADS_A // 2,),
        in_specs=[_resident((nd, 2, SPAN, SPAN)),
                  pl.BlockSpec(memory_space=pltpu.SMEM)],
        out_specs=pl.BlockSpec((1, nd, 2, 2 * SPAN, 2 * SPAN), lambda p: (p, 0, 0, 0, 0)),
        out_shape=jax.ShapeDtypeStruct((N_HEADS_A // 2, nd, 2, 2 * SPAN, 2 * SPAN), F32),
        compiler_params=pltpu.CompilerParams(dimension_semantics=("arbitrary",)),
        name="bias_tables",
    )(jnp.asarray(tabs), rel_bias)


ATT_T = SPAN * DILATIONS[-1]
ATT_UNROLL = 16
ATT_SUB = 2


def _query_runs(d, r, n):
    run = SPAN * d // DILATIONS[-1]
    return [pl.ds(pl.multiple_of((r + d * c) * SPAN + run * n, 8), run) for c in range(DILATIONS[-1] // d)]


def _load_runs(ref, runs, *lead):
    parts = [ref[(*lead, rows, slice(None))] for rows in runs]
    return parts[0] if len(parts) == 1 else jnp.concatenate(parts, axis=0)


def _store_runs(ref, runs, value, *lead):
    run = value.shape[0] // len(runs)
    for c, rows in enumerate(runs):
        ref[(*lead, rows, slice(None))] = value[c * run:(c + 1) * run, :]


def _attn_kernel(q_ref, k_ref, v_ref, bias_ref, o_ref, qp_ref, tmp_ref, kph_ref, vph_ref, acc_ref, m_ref,
                 l_ref):
    t = pl.program_id(2)
    lane = lax.broadcasted_iota(jnp.int32, (1, LANES), 1)
    first_head = lane < HEAD_DIM
    head_mask = (first_head, jnp.logical_not(first_head))
    nblk = ATT_T // SPAN
    dmax = DILATIONS[-1]

    @pl.when(t == 0)
    def _():
        for bi, d in enumerate(DILATIONS):
            ph = ATT_T // d + SPAN
            for r in range(d):
                kph_ref[bi, r * ph:r * ph + SPAN, :] = jnp.zeros((SPAN, LANES), BF16)
                vph_ref[bi, r * ph:r * ph + SPAN, :] = jnp.zeros((SPAN, LANES), BF16)

    d4, d16 = DILATIONS[1], DILATIONS[2]
    assert DILATIONS == (1, d4, d4 * d4)
    l4, l16 = ATT_T // d4, ATT_T // d16
    for r4 in range(d4):
        tmp_ref[r4 * l4:(r4 + 1) * l4, :] = q_ref[pl.ds(r4, l4, stride=d4), :] * (ATTN_SCALE * LOG2E)
    for r16 in range(d16):
        qp_ref[r16 * l16:(r16 + 1) * l16, :] = tmp_ref[pl.ds((r16 % d4) * l4 + r16 // d4, l16, stride=d4), :]
    for src_ref, dst_ref in ((k_ref, kph_ref), (v_ref, vph_ref)):
        dst_ref[0, SPAN:SPAN + ATT_T, :] = src_ref[...].astype(BF16)
        ph4, ph16 = l4 + SPAN, l16 + SPAN
        for r4 in range(d4):
            x4 = src_ref[pl.ds(r4, l4, stride=d4), :]
            tmp_ref[r4 * l4:(r4 + 1) * l4, :] = x4
            dst_ref[1, r4 * ph4 + SPAN:(r4 + 1) * ph4, :] = x4.astype(BF16)
        for r16 in range(d16):
            x16 = tmp_ref[pl.ds((r16 % d4) * l4 + r16 // d4, l16, stride=d4), :]
            dst_ref[2, r16 * ph16 + SPAN:(r16 + 1) * ph16, :] = x16.astype(BF16)

    order = tuple(reversed(range(len(DILATIONS))))
    for bi in order:
        d = DILATIONS[bi]
        ln = ATT_T // d
        ph = ln + SPAN
        nb = ln // SPAN
        first_branch = bi == order[0]
        last_branch = bi == order[-1]

        def group(it, carry, bi=bi, d=d, ph=ph, nb=nb, first_branch=first_branch, last_branch=last_branch):
            def fetch(j):
                sub = []
                for g in range(j * ATT_SUB, (j + 1) * ATT_SUB):
                    blk = it * ATT_UNROLL + g
                    r = blk // nb
                    n = blk % nb
                    runs = _query_runs(d, r, n)
                    krow = pl.multiple_of(r * ph + n * SPAN, SPAN)
                    variant = jnp.logical_or(t > 0, n > 0).astype(jnp.int32)
                    qf = _load_runs(qp_ref, runs)
                    qs = jnp.concatenate([jnp.where(head_mask[s], qf, 0.0).astype(BF16) for s in range(2)],
                                         axis=0)
                    logits = _dot_nt(qs, kph_ref[bi, pl.ds(krow, 2 * SPAN), :]) + bias_ref[0, bi, variant]
                    old = None
                    if not first_branch:
                        old = ([_load_runs(m_ref, runs, s) for s in range(2)],
                               [_load_runs(l_ref, runs, s) for s in range(2)], _load_runs(acc_ref, runs))
                    sub.append((runs, krow, old, logits))
                return sub

            nsub = ATT_UNROLL // ATT_SUB
            nxt = fetch(0)
            for j in range(nsub):
                cur = nxt
                if j + 1 < nsub:
                    nxt = fetch(j + 1)
                for runs, m_new, l_new, pv in softmax_pv(cur):
                    if last_branch:
                        _store_runs(acc_ref, runs, pv / jnp.where(first_head, l_new[0], l_new[1]))
                    else:
                        for s in range(2):
                            _store_runs(m_ref, runs, m_new[s], s)
                            _store_runs(l_ref, runs, l_new[s], s)
                        _store_runs(acc_ref, runs, pv)
            return carry

        def softmax_pv(sub, bi=bi, first_branch=first_branch):
            soft = []
            for runs, krow, old, sc2 in sub:
                m_new, l_new, alpha, ps = [], [], [], []
                for s in range(2):
                    sc = sc2[s * SPAN:(s + 1) * SPAN, :]
                    halves = (sc[:, 0:SPAN], sc[:, SPAN:2 * SPAN])
                    mb = jnp.max(jnp.maximum(halves[0], halves[1]), axis=-1, keepdims=True)
                    if first_branch:
                        m = jnp.broadcast_to(mb, (SPAN, LANES))
                    else:
                        m = jnp.maximum(old[0][s], mb)
                        alpha.append(jnp.exp2(old[0][s] - m))
                    p = [jnp.exp2(h - m) for h in halves]
                    rs = jnp.sum(p[0] + p[1], axis=-1, keepdims=True)
                    ps.append(jnp.concatenate([p[0].astype(BF16), p[1].astype(BF16)], axis=1))
                    m_new.append(m)
                    l_new.append(jnp.broadcast_to(rs, (SPAN, LANES)) if first_branch
                                 else alpha[s] * old[1][s] + rs)
                soft.append((m_new, l_new, alpha, ps))
            out = []
            for (runs, krow, old, _), (m_new, l_new, alpha, ps) in zip(sub, soft):
                pv2 = _dot(jnp.concatenate(ps, axis=0), vph_ref[bi, pl.ds(krow, 2 * SPAN), :])
                pv = jnp.where(first_head, pv2[0:SPAN, :], pv2[SPAN:2 * SPAN, :])
                if not first_branch:
                    pv = jnp.where(first_head, alpha[0], alpha[1]) * old[2] + pv
                out.append((runs, m_new, l_new, pv))
            return out

        lax.fori_loop(0, nblk // ATT_UNROLL, group, 0)

    for r16 in range(d16):
        tmp_ref[pl.ds((r16 % d4) * l4 + r16 // d4, l16, stride=d4), :] = acc_ref[r16 * l16:(r16 + 1) * l16, :]
    for r4 in range(d4):
        o_ref[pl.ds(r4, l4, stride=d4), :] = tmp_ref[r4 * l4:(r4 + 1) * l4, :]

    for bi, d in enumerate(DILATIONS):
        ln = ATT_T // d
        ph = ln + SPAN
        for r in range(d):
            kph_ref[bi, r * ph:r * ph + SPAN, :] = kph_ref[bi, (r + 1) * ph - SPAN:(r + 1) * ph, :]
            vph_ref[bi, r * ph:r * ph + SPAN, :] = vph_ref[bi, (r + 1) * ph - SPAN:(r + 1) * ph, :]


def _attn_prompt(qkv, bias, batch):
    m = qkv.shape[0]
    nt = m // batch // ATT_T
    npair = N_HEADS_A // 2
    nd = len(DILATIONS)
    blk = lambda off: pl.BlockSpec((ATT_T, LANES), lambda b, p, t: (b * nt + t, off + p))
    halo_rows = ATT_T + DILATIONS[-1] * SPAN
    return pl.pallas_call(
        _attn_kernel,
        grid=(batch, npair, nt),
        in_specs=[blk(0), blk(npair), blk(2 * npair),
                  pl.BlockSpec((1, nd, 2, 2 * SPAN, 2 * SPAN), lambda b, p, t: (p, 0, 0, 0, 0))],
        out_specs=blk(0),
        out_shape=jax.ShapeDtypeStruct((m, D_ATTN), F32),
        scratch_shapes=[pltpu.VMEM((ATT_T, LANES), F32),
                        pltpu.VMEM((ATT_T, LANES), F32),
                        pltpu.VMEM((nd, halo_rows, LANES), BF16),
                        pltpu.VMEM((nd, halo_rows, LANES), BF16),
                        pltpu.VMEM((ATT_T, LANES), F32),
                        pltpu.VMEM((2, ATT_T, LANES), F32),
                        pltpu.VMEM((2, ATT_T, LANES), F32)],
        compiler_params=pltpu.CompilerParams(
            dimension_semantics=("arbitrary", "arbitrary", "arbitrary"), vmem_limit_bytes=VMEM_LIMIT),
        name="attn_prompt",
    )(qkv, qkv, qkv, bias)


SAMPLE_ROWS = 4


def _sample_tables():
    pos = np.arange(MAX_WINDOW)[None, :]
    t = (np.arange(8) % SAMPLE_ROWS)[:, None]
    dist = MAX_WINDOW + t - pos
    mult = np.zeros(dist.shape, np.float32)
    for d in DILATIONS:
        mult += ((dist % d == 0) & (dist <= SPAN * d)).astype(np.float32)
    bucket = np.where(mult > 0, _t5_bucket_np(dist), -1).astype(np.int32)
    return bucket, np.concatenate([mult, mult], axis=0)


def _sample_bias_kernel(tab_ref, rb_ref, rbv_ref, bmain_ref, bnew_ref, *, buckets):
    pair = pl.program_id(0)
    tab = tab_ref[...]
    acc = [jnp.full(tab.shape, NEG, F32) for _ in range(2)]
    for bkt in buckets:
        hit = tab == bkt
        for s in range(2):
            acc[s] = jnp.where(hit, rb_ref[bkt, 2 * pair + s], acc[s])
    for s in range(2):
        bmain_ref[0, s * 8:(s + 1) * 8, :] = acc[s]
    t = lax.broadcasted_iota(jnp.int32, (8, 1), 0) % SAMPLE_ROWS
    for tp in range(SAMPLE_ROWS):
        b = jnp.full((8, LANES), NEG, F32)
        for dist in range(SAMPLE_ROWS):
            b = jnp.where(t - tp == dist, rbv_ref[dist:dist + 1, :], b)
        bnew_ref[tp] = b


def _sample_bias_tables(rel_bias):
    bucket, mult = _sample_tables()
    buckets = sorted(int(b) for b in np.unique(bucket) if b >= 0)
    npair = N_HEADS_A // 2
    rb_lanes = jnp.pad(rel_bias, ((0, 0), (0, LANES - N_HEADS_A)))
    bmain, bnew = pl.pallas_call(
        functools.partial(_sample_bias_kernel, buckets=buckets),
        grid=(npair,),
        in_specs=[_resident((8, MAX_WINDOW)), pl.BlockSpec(memory_space=pltpu.SMEM),
                  _resident((N_BUCKETS, LANES))],
        out_specs=[pl.BlockSpec((1, 16, MAX_WINDOW), lambda p: (p, 0, 0)),
                   pl.BlockSpec((SAMPLE_ROWS, 8, LANES), lambda p: (0, 0, 0))],
        out_shape=[jax.ShapeDtypeStruct((npair, 16, MAX_WINDOW), F32),
                   jax.ShapeDtypeStruct((SAMPLE_ROWS, 8, LANES), F32)],
        compiler_params=pltpu.CompilerParams(dimension_semantics=("arbitrary",)),
        name="sample_bias_tables",
    )(jnp.asarray(bucket), rel_bias, rb_lanes)
    return bmain, bnew, jnp.asarray(mult)


def _attn_sample_body(qkv_ref, kt_ref, vt_ref, bmain_ref, mult_ref, bnew_ref, e_ref, om_ref):
    rows = SAMPLE_ROWS
    npair = N_HEADS_A // 2
    second = pl.program_id(0) % 2 == 1
    lower = lax.broadcasted_iota(jnp.int32, (2 * rows, 1), 0) < rows
    mine = jnp.logical_xor(lower, second)
    blk = qkv_ref[...]
    q8_all = jnp.where(mine, blk, pltpu.roll(blk, rows, 0))
    q8 = q8_all[:, 0:D_ATTN] * ATTN_SCALE
    kn = q8_all[:, D_ATTN:2 * D_ATTN]
    vn = q8_all[:, 2 * D_ATTN:3 * D_ATTN]
    lane = lax.broadcasted_iota(jnp.int32, (1, LANES), 1)
    first_head = lane < HEAD_DIM
    e = e_ref[...]

    s_new = []
    m_tot = jnp.full((8, LANES), NEG, F32)
    for tp in range(rows):
        hi, lo, lo2 = _split3(q8 * kn[tp:tp + 1, :])
        s = _dot_nt(hi, e) + _dot_nt(lo, e) + _dot_nt(lo2, e) + bnew_ref[tp]
        s_new.append(s)
        m_tot = jnp.maximum(m_tot, s)

    mult = mult_ref[...]
    l_tot = jnp.zeros((8, LANES), F32)

    def window_logits(p):
        qp = q8[:, p * LANES:(p + 1) * LANES]
        qbd = jnp.concatenate([jnp.where(first_head, qp, 0.0), jnp.where(first_head, 0.0, qp)],
                              axis=0).astype(BF16)
        return _dot(qbd, kt_ref[0, p].astype(BF16)) + bmain_ref[p]

    s_next = window_logits(0)
    for p in range(npair):
        s16 = s_next
        if p + 1 < npair:
            s_next = window_logits(p + 1)
        m_new =jnp.concatenate([m_tot[:, 2 * p:2 * p + 1], m_tot[:, 2 * p + 1:2 * p + 2]], axis=0)
        m = jnp.maximum(jnp.max(s16, axis=-1, keepdims=True), m_new)
        pr = mult * jnp.exp(s16 - m)
        ls = jnp.sum(pr, axis=-1, keepdims=True)
        o16 = _dot_nt(pr.astype(BF16), vt_ref[0, p].astype(BF16))
        om_ref[:, p * LANES:(p + 1) * LANES] = jnp.where(first_head, o16[0:8], o16[8:16])
        m_tot = jnp.where(lane == 2 * p, m[0:8], jnp.where(lane == 2 * p + 1, m[8:16], m_tot))
        l_tot = jnp.where(lane == 2 * p, ls[0:8], jnp.where(lane == 2 * p + 1, ls[8:16], l_tot))
        yield

    t = lax.broadcasted_iota(jnp.int32, (8, 1), 0) % rows
    o = om_ref[...]
    for tp in range(rows):
        c = jnp.where(t == tp, float(len(DILATIONS)), jnp.where(t > tp, 1.0, 0.0))
        pn = c * jnp.exp(s_new[tp] - m_tot)
        l_tot = l_tot + pn
        o = o + _expand_heads(pn, e_ref) * vn[tp:tp + 1, :]
    o = o / _expand_heads(l_tot, e_ref)
    return o, second, lower


def _interleave(*gens):
    results = [None] * len(gens)
    live = list(range(len(gens)))
    while live:
        for k in list(live):
            try:
                next(gens[k])
            except StopIteration as stop:
                results[k] = stop.value
                live.remove(k)
    return results


def _sample_attn_prompt_ssd_kernel(qkv_ref, kt_ref, vt_ref, bmain_ref, mult_ref, bnew_ref, e_ref,
                                   z_ref, xbc_ref, dt_ref, conv0_ref, h0_ref, convw_ref, convb_ref, dtb_ref,
                                   alog_ref, dskip_ref, gnorm_ref, o_ref, y_ref, hout_ref,
                                   om_ref, xpad_ref, state_ref, kbuf_ref, ksem, *, chunks):
    step = pl.program_id(0)
    nstep = pl.num_programs(0)
    c = step % chunks
    k_copy = lambda seq, slot: pltpu.make_async_copy(kt_ref.at[seq], kbuf_ref.at[slot], ksem.at[slot])

    @pl.when(step == 0)
    def _():
        k_copy(0, 0).start()
        k_copy(1, 1).start()

    @pl.when(step + 2 < nstep)
    def _():
        k_copy(step + 2, (step + 2) % 3).start()

    k_copy(step, step % 3).wait()
    kwin_ref = kbuf_ref.at[pl.ds(step % 3, 1)]

    @pl.when(c == 0)
    def _():
        state_ref[...] = h0_ref[0]
        xpad_ref[8 - (CONV_WIDTH - 1):8, :] = conv0_ref[0]

    attn = _attn_sample_body(qkv_ref, kwin_ref, vt_ref, bmain_ref, mult_ref, bnew_ref, e_ref, om_ref)
    scan = _ssd_chunk(z_ref.at[0], xbc_ref.at[0], dt_ref.at[0], y_ref.at[0], xpad_ref, state_ref,
                      convw_ref, convb_ref, dtb_ref, alog_ref, dskip_ref, gnorm_ref, e_ref)
    o, second, lower = _interleave(attn, scan)[0]

    @pl.when(jnp.logical_not(second))
    def _():
        o_ref[...] = o

    @pl.when(second)
    def _():
        o_ref[...] = jnp.where(lower, o_ref[...], o)

    @pl.when(c == chunks - 1)
    def _():
        hout_ref[0] = state_ref[...]


def _sample_attn_prompt_ssd(qkv, kt, vt, bmain, mult, bnew, expand, z, xbc, dt_raw, conv0, h0, ssd_params):
    rows = SAMPLE_ROWS
    batch = qkv.shape[0] // rows
    nseq, seq = z.shape[0], z.shape[1]
    chunks = seq // SSD_CHUNK
    assert batch == nseq * chunks, "one prompt chunk per sample sequence"
    npair = N_HEADS_A // 2
    cache = pl.BlockSpec((1, npair, LANES, MAX_WINDOW), lambda i: (i, 0, 0, 0))
    chunk = lambda w: pl.BlockSpec((1, SSD_CHUNK, w), lambda i: (i // chunks, i % chunks, 0))
    per_seq = lambda r, w: pl.BlockSpec((1, r, w), lambda i: (i // chunks, 0, 0))
    return pl.pallas_call(
        functools.partial(_sample_attn_prompt_ssd_kernel, chunks=chunks),
        grid=(batch,),
        in_specs=[pl.BlockSpec((2 * rows, 3 * D_ATTN), lambda i: (i // 2, 0)),
                  pl.BlockSpec(memory_space=pl.ANY), cache,
                  _resident((npair, 16, MAX_WINDOW)), _resident((16, MAX_WINDOW)),
                  _resident((rows, 8, LANES)), _resident((LANES, D_ATTN)),
                  chunk(D_SSM), chunk(CONV_DIM), chunk(LANES), per_seq(CONV_WIDTH - 1, CONV_DIM),
                  per_seq(D_SSM, D_STATE)] + _ssd_param_specs()[:-1],
        out_specs=[pl.BlockSpec((2 * rows, D_ATTN), lambda i: (i // 2, 0)), chunk(D_SSM),
                   per_seq(D_SSM, D_STATE)],
        out_shape=[jax.ShapeDtypeStruct((batch * rows, D_ATTN), F32),
                   jax.ShapeDtypeStruct((nseq, seq, D_SSM), F32),
                   jax.ShapeDtypeStruct((nseq, D_SSM, D_STATE), F32)],
        scratch_shapes=[pltpu.VMEM((8, D_ATTN), F32), pltpu.VMEM((8 + SSD_CHUNK, CONV_DIM), F32),
                        pltpu.VMEM((D_SSM, D_STATE), F32),
                        pltpu.VMEM((3, npair, LANES, MAX_WINDOW), F32), pltpu.SemaphoreType.DMA((3,))],
        compiler_params=pltpu.CompilerParams(
            dimension_semantics=("arbitrary",), vmem_limit_bytes=VMEM_LIMIT),
        name="sample_attn_prompt_ssd",
    )(qkv, kt, vt, bmain, mult, bnew, expand, z, xbc, dt_raw, conv0, h0, *ssd_params[:-1])


def _head_expand_matrix():
    e = np.zeros((LANES, D_SSM), np.float32)
    for h in range(N_HEADS_S):
        e[h, h * SSM_HEAD_DIM:(h + 1) * SSM_HEAD_DIM] = 1.0
    return jnp.asarray(e, BF16)


def _prep_params(norm_mix, w_in, q_norm, k_norm, conv_w, conv_b, dt_bias, a_log, d_skip, ssm_norm,
                 w_out, norm_ffn, w_up, w_down):
    n_main = 3 * D_ATTN + D_SSM + CONV_DIM
    pad_heads = lambda v: jnp.pad(v.reshape(1, N_HEADS_S), ((0, 0), (0, LANES - N_HEADS_S)))
    return dict(
        gmix=norm_mix.reshape(1, D_MODEL),
        w_main=w_in.astype(BF16),
        w_dt=jnp.pad(w_in[:, n_main:], ((0, 0), (0, LANES - N_HEADS_S))).astype(BF16),
        gq=jnp.tile(q_norm, N_HEADS_A).reshape(1, D_ATTN),
        gk=jnp.tile(k_norm, N_HEADS_A).reshape(1, D_ATTN),
        conv_w=conv_w,
        conv_b=conv_b.reshape(1, CONV_DIM),
        dt_bias=pad_heads(dt_bias),
        a_log=pad_heads(a_log),
        d_skip_e=jnp.repeat(d_skip, SSM_HEAD_DIM).reshape(1, D_SSM),
        gnorm=ssm_norm.reshape(1, D_SSM),
        expand=_head_expand_matrix(),
        w_out=w_out.astype(BF16),
        gffn=norm_ffn.reshape(1, D_MODEL),
        w_up=w_up.astype(BF16),
        w_down=w_down.astype(BF16),
    )


def _ssd_param_list(p):
    return (p["conv_w"], p["conv_b"], p["dt_bias"], p["a_log"], p["d_skip_e"], p["gnorm"], p["expand"])


def kernel(x_prompt, x_sample, cache_attn_k, cache_attn_v, state_conv, state_ssm, norm_mix, w_in, q_norm,
           k_norm, rel_bias, conv_w, conv_b, dt_bias, a_log, d_skip, ssm_norm, w_out, norm_ffn, w_up, w_down):
    depth = w_in.shape[0]
    assert depth == 1, "single-layer decoder"
    l = 0
    p = _prep_params(norm_mix[l], w_in[l], q_norm[l], k_norm[l], conv_w[l], conv_b[l], dt_bias[l], a_log[l],
                     d_skip[l], ssm_norm[l], w_out[l], norm_ffn[l], w_up[l], w_down[l])
    ssd_params = _ssd_param_list(p)
    tm = 512

    bp, sp = x_prompt.shape[:2]
    keep = min(MAX_WINDOW, sp)
    xp = x_prompt.reshape(bp * sp, D_MODEL)
    qkv, z, xbc, dt_raw, ktail, vtail = _inproj(xp, p["gmix"], p["w_main"], p["w_dt"], p["gq"], p["gk"], tm,
                                                tail=(sp, keep))
    attn = _attn_prompt(qkv, _bias_tables(rel_bias), bp)
    to_rows = lambda a: jnp.transpose(a.reshape(bp, N_HEADS_A, HEAD_DIM, keep), (0, 3, 1, 2)).reshape(
        depth, bp, keep, N_HEADS_A, HEAD_DIM)
    new_k_prompt = to_rows(ktail)
    new_v_prompt = to_rows(vtail)
    new_conv_prompt = xbc.reshape(bp, sp, CONV_DIM)[:, sp - (CONV_WIDTH - 1):].reshape(
        depth, bp, CONV_WIDTH - 1, CONV_DIM)

    bs, ts = x_sample.shape[:2]
    assert cache_attn_k.shape[2] == MAX_WINDOW and ts == SAMPLE_ROWS
    xs = x_sample.reshape(bs * ts, D_MODEL)
    qkv_s, z_s, xbc_s, dt_s = _inproj(xs, p["gmix"], p["w_main"], p["w_dt"], p["gq"], p["gk"], tm)
    npair = N_HEADS_A // 2
    kt = jnp.transpose(cache_attn_k[l], (0, 2, 3, 1)).reshape(bs, npair, LANES, MAX_WINDOW)
    vt = jnp.transpose(cache_attn_v[l], (0, 2, 3, 1)).reshape(bs, npair, LANES, MAX_WINDOW)
    bmain, bnew, mult = _sample_bias_tables(rel_bias)
    conv0 = jnp.zeros((bp, CONV_WIDTH - 1, CONV_DIM), F32)
    h0 = jnp.zeros((bp, D_SSM, D_STATE), F32)
    attn_s, ssm_o, hst = _sample_attn_prompt_ssd(
        qkv_s, kt, vt, bmain, mult, bnew, p["expand"], z.reshape(bp, sp, D_SSM), xbc.reshape(bp, sp, CONV_DIM),
        dt_raw.reshape(bp, sp, LANES), conv0, h0, ssd_params)
    y_prompt = _outffn(xp, attn, ssm_o.reshape(bp * sp, D_SSM), p["w_out"], p["gffn"], p["w_up"], p["w_down"], tm)
    new_ssm_prompt = hst.reshape(depth, bp, N_HEADS_S, SSM_HEAD_DIM, D_STATE)
    ssm_os, hsts = _ssd_sample(z_s, xbc_s, dt_s, jnp.transpose(state_conv[l], (1, 0, 2)),
                               state_ssm[l].reshape(bs, D_SSM, D_STATE), ssd_params, ts)
    y_sample = _outffn(xs, attn_s, ssm_os, p["w_out"], p["gffn"], p["w_up"], p["w_down"], tm)
    new_k_sample = qkv_s[:, D_ATTN:2 * D_ATTN].reshape(depth, bs, ts, N_HEADS_A, HEAD_DIM)
    new_v_sample = qkv_s[:, 2 * D_ATTN:].reshape(depth, bs, ts, N_HEADS_A, HEAD_DIM)
    new_conv_sample = xbc_s.reshape(bs, ts, CONV_DIM)[:, ts - (CONV_WIDTH - 1):].reshape(
        depth, bs, CONV_WIDTH - 1, CONV_DIM)
    new_ssm_sample = hsts.reshape(depth, bs, N_HEADS_S, SSM_HEAD_DIM, D_STATE)

    return (y_prompt.reshape(bp, sp, D_MODEL), y_sample.reshape(bs, ts, D_MODEL),
            new_k_prompt, new_v_prompt, new_conv_prompt, new_ssm_prompt,
            new_k_sample, new_v_sample, new_conv_sample, new_ssm_sample)
```
